```python
import jax, jax.numpy as jnp
from jax import lax
import numpy as np

D_MODEL = 4096
BATCH = 8
SEQ = 4096
DEPTH = 1

N_Q_HEADS = 64
N_KV_HEADS = 8
HEAD_DIM = 64
Q_PER_KV = N_Q_HEADS // N_KV_HEADS
ATTN_WIDTH = N_Q_HEADS * HEAD_DIM
KV_WIDTH = N_KV_HEADS * HEAD_DIM
WINDOW = 128
BLOCK = 128
ROPE_THETA = 500000.0
ROPE_DIM = HEAD_DIM // 4

GMLP_WIDTH = D_MODEL
GMLP_GROUPS = 8
GMLP_GROUP_DIM = GMLP_WIDTH // GMLP_GROUPS
GMLP_CHUNK = 128

NORM_EPS = 1e-5
LN_EPS = 1e-5

PROJ_SIZES = (ATTN_WIDTH, KV_WIDTH, KV_WIDTH, ATTN_WIDTH,
              GMLP_WIDTH, GMLP_WIDTH, GMLP_WIDTH, D_MODEL, D_MODEL)
PROJ_WIDTH = int(sum(PROJ_SIZES))
SPLIT_POINTS = tuple(int(s) for s in np.cumsum(PROJ_SIZES)[:-1])

kernel_name = "hybrid_swa_sink_gmlp_gated_merge"


def rms_norm(x, g):
    x32 = x.astype(jnp.float32)
    y = x32 * lax.rsqrt(jnp.mean(x32 * x32, axis=-1, keepdims=True) + NORM_EPS)
    return (y * g.astype(jnp.float32)).astype(x.dtype)


def layer_norm(x, g, b):
    x32 = x.astype(jnp.float32)
    mu = jnp.mean(x32, axis=-1, keepdims=True)
    xc = x32 - mu
    y = xc * lax.rsqrt(jnp.mean(xc * xc, axis=-1, keepdims=True) + LN_EPS)
    return (y * g.astype(jnp.float32) + b.astype(jnp.float32)).astype(x.dtype)


def rope_tables(positions, dtype):
    half = ROPE_DIM // 2
    inv_freq = ROPE_THETA ** (-jnp.arange(half, dtype=jnp.float32) * 2.0 / ROPE_DIM)
    ang = positions.astype(jnp.float32)[..., None] * inv_freq
    return jnp.cos(ang)[:, :, None, :].astype(dtype), jnp.sin(ang)[:, :, None, :].astype(dtype)


def partial_rope(t, cos, sin):
    half = ROPE_DIM // 2
    t1 = t[..., :half]
    t2 = t[..., half:ROPE_DIM]
    return jnp.concatenate([t1 * cos - t2 * sin, t2 * cos + t1 * sin, t[..., ROPE_DIM:]], axis=-1)


def sliding_window_sink_attention(q, k, v, sink):
    B, S = q.shape[0], q.shape[1]
    nb = S // BLOCK
    qb = q.reshape(B, nb, BLOCK, N_KV_HEADS, Q_PER_KV, HEAD_DIM)
    kb = k.reshape(B, nb, BLOCK, N_KV_HEADS, HEAD_DIM)
    vb = v.reshape(B, nb, BLOCK, N_KV_HEADS, HEAD_DIM)

    def with_prev(t):
        prev = jnp.concatenate([jnp.zeros_like(t[:, :1]), t[:, :-1]], axis=1)
        return jnp.concatenate([prev, t], axis=2)

    kband, vband = with_prev(kb), with_prev(vb)
    sink_g = sink.astype(jnp.float32).reshape(1, N_KV_HEADS, Q_PER_KV, 1, 1)
    qi = jnp.arange(BLOCK)[:, None]
    si = jnp.arange(2 * BLOCK)[None, :]
    band = (si <= qi + BLOCK) & (si > qi + BLOCK - WINDOW)
    scale = HEAD_DIM ** -0.5

    def one_block(args):
        idx, qx, kx, vx = args
        s = jnp.einsum('bqhgd,bshd->bhgqs', qx, kx,
                       preferred_element_type=jnp.float32) * scale
        mask = band & ((idx > 0) | (si >= BLOCK))
        s = jnp.where(mask, s, -jnp.inf)
        m = jnp.maximum(jnp.max(s, axis=-1, keepdims=True), sink_g)
        p = jnp.exp(s - m)
        denom = jnp.sum(p, axis=-1, keepdims=True) + jnp.exp(sink_g - m)
        return jnp.einsum('bhgqs,bshd->bqhgd', (p / denom).astype(vx.dtype), vx)

    xs = (jnp.arange(nb), jnp.moveaxis(qb, 1, 0), jnp.moveaxis(kband, 1, 0), jnp.moveaxis(vband, 1, 0))
    out = lax.map(one_block, xs)
    return jnp.moveaxis(out, 0, 1).reshape(B, S, ATTN_WIDTH)


def chunked_spatial_gating(u, v, w_s, b_s, ln_g, ln_b):
    B, S, W = v.shape
    nc = S // GMLP_CHUNK
    vn = layer_norm(v, ln_g, ln_b)
    vc = vn.reshape(B, nc, GMLP_CHUNK, GMLP_GROUPS, GMLP_GROUP_DIM)
    causal = jnp.tril(jnp.ones((GMLP_CHUNK, GMLP_CHUNK), dtype=bool))
    w = jnp.where(causal[None], w_s, jnp.zeros_like(w_s)).astype(v.dtype)
    mixed = jnp.einsum('gts,bnsgc->bntgc', w, vc) + b_s.T[:, :, None].astype(v.dtype)
    return u * mixed.reshape(B, S, W)


def _fwd_setup_inputs(seed: int = 0) -> dict:
    key = jax.random.key(seed)
    ks = jax.random.split(key, 16)
    f32 = jnp.float32
    x = jax.random.normal(ks[0], (BATCH, SEQ, D_MODEL), f32)
    offsets = jax.random.randint(ks[1], (BATCH, 1), 0, 4096, dtype=jnp.int32)
    positions = offsets + jnp.arange(SEQ, dtype=jnp.int32)[None, :]
    norm_g = 1.0 + 0.02 * jax.random.normal(ks[2], (DEPTH, D_MODEL), f32)
    w_in = jax.random.normal(ks[3], (DEPTH, D_MODEL, PROJ_WIDTH), f32) * D_MODEL ** -0.5
    attn_sink = 0.5 * jax.random.normal(ks[4], (DEPTH, N_Q_HEADS), f32)
    gmlp_ln_g = 1.0 + 0.02 * jax.random.normal(ks[5], (DEPTH, GMLP_WIDTH), f32)
    gmlp_ln_b = 0.02 * jax.random.normal(ks[6], (DEPTH, GMLP_WIDTH), f32)
    w_spatial = jax.random.normal(ks[7], (DEPTH, GMLP_GROUPS, GMLP_CHUNK, GMLP_CHUNK), f32) * GMLP_CHUNK ** -0.5
    b_spatial = 1.0 + 0.1 * jax.random.normal(ks[8], (DEPTH, GMLP_GROUPS, GMLP_CHUNK), f32)
    w_up_attn = jax.random.normal(ks[9], (DEPTH, ATTN_WIDTH, D_MODEL), f32) * ATTN_WIDTH ** -0.5
    w_up_gmlp = jax.random.normal(ks[10], (DEPTH, GMLP_WIDTH, D_MODEL), f32) * GMLP_WIDTH ** -0.5
    w_out = jax.random.normal(ks[11], (DEPTH, D_MODEL, D_MODEL), f32) * D_MODEL ** -0.5
    final_norm_g = 1.0 + 0.02 * jax.random.normal(ks[12], (D_MODEL,), f32)
    return {"x": x, "positions": positions, "norm_g": norm_g, "w_in": w_in,
            "attn_sink": attn_sink, "gmlp_ln_g": gmlp_ln_g, "gmlp_ln_b": gmlp_ln_b,
            "w_spatial": w_spatial, "b_spatial": b_spatial, "w_up_attn": w_up_attn,
            "w_up_gmlp": w_up_gmlp, "w_out": w_out, "final_norm_g": final_norm_g}


def _fwd_reference(x, positions, norm_g, w_in, attn_sink, gmlp_ln_g, gmlp_ln_b, w_spatial,
              b_spatial, w_up_attn, w_up_gmlp, w_out, final_norm_g):
    B, S = x.shape[0], x.shape[1]
    cos, sin = rope_tables(positions, x.dtype)
    for l in range(DEPTH):
        h = rms_norm(x, norm_g[l])
        proj = jnp.einsum('bsd,dp->bsp', h, w_in[l])
        q, k, v, gate_a, u, vg, gate_b, mg_a, mg_b = jnp.split(proj, SPLIT_POINTS, axis=-1)
        q = partial_rope(q.reshape(B, S, N_Q_HEADS, HEAD_DIM), cos, sin)
        k = partial_rope(k.reshape(B, S, N_KV_HEADS, HEAD_DIM), cos, sin)
        v = v.reshape(B, S, N_KV_HEADS, HEAD_DIM)
        attn = sliding_window_sink_attention(q, k, v, attn_sink[l])
        y_a = jnp.einsum('bsw,wd->bsd', attn * jax.nn.silu(gate_a), w_up_attn[l])
        sg = chunked_spatial_gating(jax.nn.gelu(u), jax.nn.gelu(vg), w_spatial[l], b_spatial[l],
                                    gmlp_ln_g[l], gmlp_ln_b[l])
        y_b = jnp.einsum('bsw,wd->bsd', sg * jax.nn.silu(gate_b), w_up_gmlp[l])
        merged = jax.nn.sigmoid(mg_a) * y_a + jax.nn.sigmoid(mg_b) * y_b
        x = x + jnp.einsum('bsd,de->bse', merged, w_out[l])
    return rms_norm(x, final_norm_g)


import jax as _jax
import jax.numpy as _jnp

TWIN_FORMAT = 'train_step'
FWD_PARAMS = ['x', 'positions', 'norm_g', 'w_in', 'attn_sink', 'gmlp_ln_g', 'gmlp_ln_b', 'w_spatial', 'b_spatial', 'w_up_attn', 'w_up_gmlp', 'w_out', 'final_norm_g']
TWIN_WEIGHTS = ['norm_g', 'w_in', 'attn_sink', 'gmlp_ln_g', 'gmlp_ln_b', 'w_spatial', 'b_spatial', 'w_up_attn', 'w_up_gmlp', 'w_out', 'final_norm_g']
TWIN_DIFF_INPUT = 'x'
TWIN_INPUTS = ['x', 'positions', 'norm_g', 'w_in', 'attn_sink', 'gmlp_ln_g', 'gmlp_ln_b', 'w_spatial', 'b_spatial', 'w_up_attn', 'w_up_gmlp', 'w_out', 'final_norm_g', 'loss_target', 'm_norm_g', 'm_w_in', 'm_attn_sink', 'm_gmlp_ln_g', 'm_gmlp_ln_b', 'm_w_spatial', 'm_b_spatial', 'm_w_up_attn', 'm_w_up_gmlp', 'm_w_out', 'm_final_norm_g', 'v_norm_g', 'v_w_in', 'v_attn_sink', 'v_gmlp_ln_g', 'v_gmlp_ln_b', 'v_w_spatial', 'v_b_spatial', 'v_w_up_attn', 'v_w_up_gmlp', 'v_w_out', 'v_final_norm_g']
TWIN_OUTPUTS = ['loss', 'grad_x', 'grad_norm_g', 'grad_w_in', 'grad_attn_sink', 'grad_gmlp_ln_g', 'grad_gmlp_ln_b', 'grad_w_spatial', 'grad_b_spatial', 'grad_w_up_attn', 'grad_w_up_gmlp', 'grad_w_out', 'grad_final_norm_g', 'delta_norm_g', 'delta_w_in', 'delta_attn_sink', 'delta_gmlp_ln_g', 'delta_gmlp_ln_b', 'delta_w_spatial', 'delta_b_spatial', 'delta_w_up_attn', 'delta_w_up_gmlp', 'delta_w_out', 'delta_final_norm_g', 'new_m_norm_g', 'new_m_w_in', 'new_m_attn_sink', 'new_m_gmlp_ln_g', 'new_m_gmlp_ln_b', 'new_m_w_spatial', 'new_m_b_spatial', 'new_m_w_up_attn', 'new_m_w_up_gmlp', 'new_m_w_out', 'new_m_final_norm_g', 'new_v_norm_g', 'new_v_w_in', 'new_v_attn_sink', 'new_v_gmlp_ln_g', 'new_v_gmlp_ln_b', 'new_v_w_spatial', 'new_v_b_spatial', 'new_v_w_up_attn', 'new_v_w_up_gmlp', 'new_v_w_out', 'new_v_final_norm_g']
TWIN_LEAF_KINDS = {'loss': 'loss', 'grad_x': 'grad_x', 'grad_norm_g': 'grad_w', 'grad_w_in': 'grad_w', 'grad_attn_sink': 'grad_w', 'grad_gmlp_ln_g': 'grad_w', 'grad_gmlp_ln_b': 'grad_w', 'grad_w_spatial': 'grad_w', 'grad_b_spatial': 'grad_w', 'grad_w_up_attn': 'grad_w', 'grad_w_up_gmlp': 'grad_w', 'grad_w_out': 'grad_w', 'grad_final_norm_g': 'grad_w', 'delta_norm_g': 'delta_w', 'delta_w_in': 'delta_w', 'delta_attn_sink': 'delta_w', 'delta_gmlp_ln_g': 'delta_w', 'delta_gmlp_ln_b': 'delta_w', 'delta_w_spatial': 'delta_w', 'delta_b_spatial': 'delta_w', 'delta_w_up_attn': 'delta_w', 'delta_w_up_gmlp': 'delta_w', 'delta_w_out': 'delta_w', 'delta_final_norm_g': 'delta_w', 'new_m_norm_g': 'new_m', 'new_m_w_in': 'new_m', 'new_m_attn_sink': 'new_m', 'new_m_gmlp_ln_g': 'new_m', 'new_m_gmlp_ln_b': 'new_m', 'new_m_w_spatial': 'new_m', 'new_m_b_spatial': 'new_m', 'new_m_w_up_attn': 'new_m', 'new_m_w_up_gmlp': 'new_m', 'new_m_w_out': 'new_m', 'new_m_final_norm_g': 'new_m', 'new_v_norm_g': 'new_v', 'new_v_w_in': 'new_v', 'new_v_attn_sink': 'new_v', 'new_v_gmlp_ln_g': 'new_v', 'new_v_gmlp_ln_b': 'new_v', 'new_v_w_spatial': 'new_v', 'new_v_b_spatial': 'new_v', 'new_v_w_up_attn': 'new_v', 'new_v_w_up_gmlp': 'new_v', 'new_v_w_out': 'new_v', 'new_v_final_norm_g': 'new_v'}


def _forward(args):
    return _fwd_reference(*[args[k] for k in FWD_PARAMS])


def _output_shape():
    out = _jax.eval_shape(lambda: _forward(_fwd_setup_inputs(0)))
    return out.shape, out.dtype

N_MICROBATCH = 1
ADAM_LR = 0.001
ADAM_B1 = 0.9
ADAM_B2 = 0.999
ADAM_EPS = 1e-08
ADAM_WD = 0.01
ADAM_STEP = 10
PER_EXAMPLE_BATCH_AXIS = {'x': 0, 'positions': 0, 'loss_target': 0}
SHARED_INPUTS = []
_WEIGHT_DTYPES = {'norm_g': _jnp.float32, 'w_in': _jnp.float32, 'attn_sink': _jnp.float32, 'gmlp_ln_g': _jnp.float32, 'gmlp_ln_b': _jnp.float32, 'w_spatial': _jnp.float32, 'b_spatial': _jnp.float32, 'w_up_attn': _jnp.float32, 'w_up_gmlp': _jnp.float32, 'w_out': _jnp.float32, 'final_norm_g': _jnp.float32}
MOMENT_SCALE = {'norm_g': 1.969857e-02, 'w_in': 7.114819e-03, 'attn_sink': 2.168175e-03, 'gmlp_ln_g': 6.594765e-03, 'gmlp_ln_b': 6.533583e-03, 'w_spatial': 1.295371e-02, 'b_spatial': 1.798195e-02, 'w_up_attn': 2.829642e-03, 'w_up_gmlp': 1.117456e-02, 'w_out': 1.145762e-02, 'final_norm_g': 7.986477e+00}


def _to_microbatches(a, axis):
    t = _jnp.moveaxis(a, axis, 0)
    t = t.reshape((N_MICROBATCH, t.shape[0] // N_MICROBATCH) + t.shape[1:])
    return _jnp.moveaxis(t, 1, axis + 1)


def setup_inputs(seed: int = 0) -> dict:
    inp = _fwd_setup_inputs(seed)
    key = _jax.random.fold_in(_jax.random.key(seed), 7919)
    shape, _ = _output_shape()
    out = dict(inp)
    out["loss_target"] = _jax.random.normal(_jax.random.fold_in(key, 0), shape, _jnp.float32)
    for i, name in enumerate(TWIN_WEIGHTS):
        w = inp[name].astype(_jnp.float32)
        if MOMENT_SCALE is None:
            s = _jnp.sqrt(_jnp.mean(_jnp.square(w)) + 1e-30)
        else:
            s = MOMENT_SCALE[name]
        km, kv = _jax.random.split(_jax.random.fold_in(key, i + 1))
        out[name] = w
        out["m_" + name] = s * _jax.random.normal(km, w.shape, _jnp.float32)
        out["v_" + name] = (s * s) * _jax.random.uniform(kv, w.shape, _jnp.float32, 0.5, 1.5)
    if N_MICROBATCH > 1:
        for name, axis in PER_EXAMPLE_BATCH_AXIS.items():
            out[name] = _to_microbatches(out[name], axis)
    return {'x': out['x'], 'positions': out['positions'], 'norm_g': out['norm_g'], 'w_in': out['w_in'], 'attn_sink': out['attn_sink'], 'gmlp_ln_g': out['gmlp_ln_g'], 'gmlp_ln_b': out['gmlp_ln_b'], 'w_spatial': out['w_spatial'], 'b_spatial': out['b_spatial'], 'w_up_attn': out['w_up_attn'], 'w_up_gmlp': out['w_up_gmlp'], 'w_out': out['w_out'], 'final_norm_g': out['final_norm_g'], 'loss_target': out['loss_target'], 'm_norm_g': out['m_norm_g'], 'm_w_in': out['m_w_in'], 'm_attn_sink': out['m_attn_sink'], 'm_gmlp_ln_g': out['m_gmlp_ln_g'], 'm_gmlp_ln_b': out['m_gmlp_ln_b'], 'm_w_spatial': out['m_w_spatial'], 'm_b_spatial': out['m_b_spatial'], 'm_w_up_attn': out['m_w_up_attn'], 'm_w_up_gmlp': out['m_w_up_gmlp'], 'm_w_out': out['m_w_out'], 'm_final_norm_g': out['m_final_norm_g'], 'v_norm_g': out['v_norm_g'], 'v_w_in': out['v_w_in'], 'v_attn_sink': out['v_attn_sink'], 'v_gmlp_ln_g': out['v_gmlp_ln_g'], 'v_gmlp_ln_b': out['v_gmlp_ln_b'], 'v_w_spatial': out['v_w_spatial'], 'v_b_spatial': out['v_b_spatial'], 'v_w_up_attn': out['v_w_up_attn'], 'v_w_up_gmlp': out['v_w_up_gmlp'], 'v_w_out': out['v_w_out'], 'v_final_norm_g': out['v_final_norm_g']}


def _loss(weights, diff, rest, loss_target):
    with _jax.named_scope("forward"):
        args = {**rest, TWIN_DIFF_INPUT: diff, **{k: w.astype(_WEIGHT_DTYPES[k]) for k, w in weights.items()}}
        y = _forward(args)
    with _jax.named_scope("loss_head"):
        err = _jnp.square(y.astype(_jnp.float32) - loss_target)
        return 0.5 * _jnp.sum(_jnp.mean(err, axis=-1)) if err.ndim else 0.5 * err


def _adamw(w, g, m, v):
    m = ADAM_B1 * m + (1.0 - ADAM_B1) * g
    v = ADAM_B2 * v + (1.0 - ADAM_B2) * _jnp.square(g)
    m_hat = m / (1.0 - ADAM_B1 ** ADAM_STEP)
    v_hat = v / (1.0 - ADAM_B2 ** ADAM_STEP)
    delta = -ADAM_LR * (m_hat / (_jnp.sqrt(v_hat) + ADAM_EPS) + ADAM_WD * w)
    return delta, m, v


def reference(x, positions, norm_g, w_in, attn_sink, gmlp_ln_g, gmlp_ln_b, w_spatial, b_spatial, w_up_attn, w_up_gmlp, w_out, final_norm_g, loss_target, m_norm_g, m_w_in, m_attn_sink, m_gmlp_ln_g, m_gmlp_ln_b, m_w_spatial, m_b_spatial, m_w_up_attn, m_w_up_gmlp, m_w_out, m_final_norm_g, v_norm_g, v_w_in, v_attn_sink, v_gmlp_ln_g, v_gmlp_ln_b, v_w_spatial, v_b_spatial, v_w_up_attn, v_w_up_gmlp, v_w_out, v_final_norm_g):
    given = dict(x=x, positions=positions, norm_g=norm_g, w_in=w_in, attn_sink=attn_sink, gmlp_ln_g=gmlp_ln_g, gmlp_ln_b=gmlp_ln_b, w_spatial=w_spatial, b_spatial=b_spatial, w_up_attn=w_up_attn, w_up_gmlp=w_up_gmlp, w_out=w_out, final_norm_g=final_norm_g, loss_target=loss_target, m_norm_g=m_norm_g, m_w_in=m_w_in, m_attn_sink=m_attn_sink, m_gmlp_ln_g=m_gmlp_ln_g, m_gmlp_ln_b=m_gmlp_ln_b, m_w_spatial=m_w_spatial, m_b_spatial=m_b_spatial, m_w_up_attn=m_w_up_attn, m_w_up_gmlp=m_w_up_gmlp, m_w_out=m_w_out, m_final_norm_g=m_final_norm_g, v_norm_g=v_norm_g, v_w_in=v_w_in, v_attn_sink=v_attn_sink, v_gmlp_ln_g=v_gmlp_ln_g, v_gmlp_ln_b=v_gmlp_ln_b, v_w_spatial=v_w_spatial, v_b_spatial=v_b_spatial, v_w_up_attn=v_w_up_attn, v_w_up_gmlp=v_w_up_gmlp, v_w_out=v_w_out, v_final_norm_g=v_final_norm_g)
    weights = {n: given[n] for n in TWIN_WEIGHTS}
    shared = {n: given[n] for n in SHARED_INPUTS}
    per_example = {n: given[n] for n in ['x', 'positions']}
    grad_fn = _jax.value_and_grad(_loss, argnums=(0, 1))

    def one_microbatch(ex, loss_target):
        ex = dict(ex)
        diff = ex.pop(TWIN_DIFF_INPUT)
        return grad_fn(weights, diff, {**shared, **ex}, loss_target)

    if N_MICROBATCH == 1:
        loss, (grad_w, grad_x) = one_microbatch(per_example, given["loss_target"])
    else:
        def body(carry, xs):
            loss_sum, grad_sum = carry
            l_k, (gw_k, gx_k) = one_microbatch(xs[0], xs[1])
            with _jax.named_scope("update"):
                return (loss_sum + l_k, _jax.tree.map(_jnp.add, grad_sum, gw_k)), gx_k

        init = (_jnp.zeros((), _jnp.float32), _jax.tree.map(_jnp.zeros_like, weights))
        (loss, grad_w), grad_x = _jax.lax.scan(body, init, (per_example, given["loss_target"]))
    with _jax.named_scope("update"):
        delta_w, new_m, new_v = {}, {}, {}
        for n in TWIN_WEIGHTS:
            delta_w[n], new_m[n], new_v[n] = _adamw(weights[n], grad_w[n], given["m_" + n], given["v_" + n])
    return (loss, grad_x, *[grad_w[n] for n in TWIN_WEIGHTS], *[delta_w[n] for n in TWIN_WEIGHTS],
            *[new_m[n] for n in TWIN_WEIGHTS], *[new_v[n] for n in TWIN_WEIGHTS])
```

```python
import functools

import jax
import jax.numpy as jnp
from jax import lax
from jax.experimental import pallas as pl
from jax.experimental.pallas import tpu as pltpu

F32 = jnp.float32
BF16 = jnp.bfloat16

D_MODEL = 4096
N_Q_HEADS = 64
N_KV_HEADS = 8
HEAD_DIM = 64
Q_PER_KV = N_Q_HEADS // N_KV_HEADS
ATTN_WIDTH = N_Q_HEADS * HEAD_DIM
KV_WIDTH = N_KV_HEADS * HEAD_DIM
WINDOW = 128
BLOCK = 128
ROPE_THETA = 500000.0
ROPE_DIM = HEAD_DIM // 4
ROPE_HALF = ROPE_DIM // 2
GMLP_WIDTH = D_MODEL
GMLP_GROUPS = 8
GMLP_GROUP_DIM = GMLP_WIDTH // GMLP_GROUPS
GMLP_CHUNK = 128
NORM_EPS = 1e-5
LN_EPS = 1e-5

PROJ_SIZES = (ATTN_WIDTH, KV_WIDTH, KV_WIDTH, ATTN_WIDTH, GMLP_WIDTH, GMLP_WIDTH, GMLP_WIDTH, D_MODEL, D_MODEL)
PROJ_WIDTH = sum(PROJ_SIZES)
OFF_Q, OFF_K, OFF_V, OFF_GA, OFF_U, OFF_VG, OFF_GB, OFF_MA, OFF_MB = (
    sum(PROJ_SIZES[:i]) for i in range(len(PROJ_SIZES)))

ADAM_LR = 0.001
ADAM_B1 = 0.9
ADAM_B2 = 0.999
ADAM_EPS = 1e-08
ADAM_WD = 0.01
ADAM_STEP = 10

N_CHIPS = 4
SLAB = PROJ_WIDTH // N_CHIPS
UP_ROWS = D_MODEL // N_CHIPS
LANES = 128
COL_BLK = 1024
HEADS_PER_STEP = 2 * Q_PER_KV
VMEM_LIMIT = 56 * 1024 * 1024

MESH = pl.DeviceIdType.MESH
HBM_SPEC = pl.BlockSpec(memory_space=pltpu.HBM)
SMEM_SPEC = pl.BlockSpec(memory_space=pltpu.SMEM)


def _params(*sem):
    return pltpu.CompilerParams(dimension_semantics=sem, vmem_limit_bytes=VMEM_LIMIT)


def _sigmoid(x):
    return jax.nn.sigmoid(x)


def _gelu(x):
    return jax.nn.gelu(x, approximate=True)


def _gelu_grad(x):
    c = 0.7978845608028654
    inner = c * (x + 0.044715 * x * x * x)
    t = jnp.tanh(inner)
    return 0.5 * (1.0 + t) + 0.5 * x * (1.0 - t * t) * c * (1.0 + 3 * 0.044715 * x * x)


def _matmul(a, b, *, mode, out_dtype, name, residual=None, tm=1024, tn=1024, tk=1024):
    if mode == "nn":
        (M, K), N = a.shape, b.shape[1]
    elif mode == "nt":
        (M, K), N = a.shape, b.shape[0]
    else:
        (K, M), N = a.shape, b.shape[1]
    tm, tn, tk = min(tm, M), min(tn, N), min(tk, K)
    assert M % tm == 0 and N % tn == 0 and K % tk == 0
    nk = K // tk
    if mode == "nn":
        a_spec = pl.BlockSpec((tm, tk), lambda i, j, k: (i, k))
        b_spec = pl.BlockSpec((tk, tn), lambda i, j, k: (k, j))
        dims = (((1,), (0,)), ((), ()))
    elif mode == "nt":
        a_spec = pl.BlockSpec((tm, tk), lambda i, j, k: (i, k))
        b_spec = pl.BlockSpec((tn, tk), lambda i, j, k: (j, k))
        dims = (((1,), (1,)), ((), ()))
    else:
        a_spec = pl.BlockSpec((tk, tm), lambda i, j, k: (k, i))
        b_spec = pl.BlockSpec((tk, tn), lambda i, j, k: (k, j))
        dims = (((0,), (0,)), ((), ()))
    o_spec = pl.BlockSpec((tm, tn), lambda i, j, k: (i, j))
    in_specs = [a_spec, b_spec] + ([o_spec] if residual is not None else [])

    def body(*refs):
        if residual is not None:
            a_ref, b_ref, r_ref, o_ref, acc = refs
        else:
            a_ref, b_ref, o_ref, acc = refs
        k = pl.program_id(2)
        part = lax.dot_general(a_ref[...], b_ref[...], dims, preferred_element_type=F32)

        @pl.when(k == 0)
        def _():
            acc[...] = part

        @pl.when(k > 0)
        def _():
            acc[...] += part

        @pl.when(k == nk - 1)
        def _():
            r = acc[...]
            if residual is not None:
                r = r + r_ref[...]
            o_ref[...] = r.astype(out_dtype)

    args = (a, b) + ((residual,) if residual is not None else ())
    return pl.pallas_call(
        body, name=name, out_shape=jax.ShapeDtypeStruct((M, N), out_dtype),
        grid=(M // tm, N // tn, nk), in_specs=in_specs, out_specs=o_spec,
        scratch_shapes=[pltpu.VMEM((tm, tn), F32)],
        compiler_params=_params("parallel", "parallel", "arbitrary"),
    )(*args)


def _cast_bf16(w, name, rows=64):
    R, C = w.shape
    rows = min(rows, R)

    def body(w_ref, o_ref):
        o_ref[...] = w_ref[...].astype(BF16)

    spec = pl.BlockSpec((rows, C), lambda i: (i, 0))
    return pl.pallas_call(body, name=name, out_shape=jax.ShapeDtypeStruct((R, C), BF16), grid=(R // rows,),
                          in_specs=[spec], out_specs=spec, compiler_params=_params("parallel"))(w)


def _rms_fwd(x, g, rows=256):
    S, D = x.shape
    rows = min(rows, S)

    def body(x_ref, g_ref, h_ref):
        xv = x_ref[...]
        ms = jnp.mean(xv * xv, axis=-1, keepdims=True)
        h_ref[...] = (xv * lax.rsqrt(ms + NORM_EPS) * g_ref[...]).astype(BF16)

    spec = pl.BlockSpec((rows, D), lambda i: (i, 0))
    return pl.pallas_call(body, name="rms_fwd", out_shape=jax.ShapeDtypeStruct((S, D), BF16), grid=(S // rows,),
                          in_specs=[spec, pl.BlockSpec((1, D), lambda i: (0, 0))], out_specs=spec,
                          compiler_params=_params("parallel"))(x, g)


def _rope_tables(positions):
    inv_freq = ROPE_THETA ** (-jnp.arange(ROPE_HALF, dtype=F32) * 2.0 / ROPE_DIM)
    ang = positions.astype(F32)[:, None] * inv_freq
    cos, sin = jnp.cos(ang), jnp.sin(ang)
    S = positions.shape[0]
    rest = HEAD_DIM - ROPE_DIM
    zeros_h, zeros_r = jnp.zeros((S, ROPE_HALF), F32), jnp.zeros((S, rest), F32)
    c = jnp.concatenate([cos, cos, jnp.ones((S, rest), F32)], axis=1)
    s1 = jnp.concatenate([-sin, zeros_h, zeros_r], axis=1)
    s2 = jnp.concatenate([zeros_h, sin, zeros_r], axis=1)
    reps = LANES // HEAD_DIM
    return jnp.tile(c, (1, reps)), jnp.tile(s1, (1, reps)), jnp.tile(s2, (1, reps))


def _rope(t, c, s1, s2, sign):
    n = t.shape[1]
    reps = n // c.shape[1]
    if reps > 1:
        c, s1, s2 = jnp.tile(c, (1, reps)), jnp.tile(s1, (1, reps)), jnp.tile(s2, (1, reps))
    up = pltpu.roll(t, n - ROPE_HALF, 1)
    down = pltpu.roll(t, ROPE_HALF, 1)
    return t * c + sign * (up * s1 + down * s2)


def _attn_specs(nb):
    last = nb - 1

    def cur(i):
        return jnp.minimum(i, last)

    def prev(i):
        return jnp.maximum(jnp.minimum(i, last) - 1, 0)

    kq, kk, kv, kg = OFF_Q // COL_BLK, OFF_K // LANES, OFF_V // LANES, OFF_GA // COL_BLK
    wide = lambda off: pl.BlockSpec((BLOCK, COL_BLK), lambda p, i: (cur(i), off + p))
    kv_cur = lambda off: pl.BlockSpec((BLOCK, LANES), lambda p, i: (cur(i), off + p))
    kv_prev = lambda off: pl.BlockSpec((BLOCK, LANES), lambda p, i: (prev(i), off + p))
    tab_cur = pl.BlockSpec((BLOCK, LANES), lambda p, i: (cur(i), 0))
    tab_prev = pl.BlockSpec((BLOCK, LANES), lambda p, i: (prev(i), 0))
    proj_specs = [wide(kq), kv_cur(kk), kv_prev(kk), kv_cur(kv), kv_prev(kv), wide(kg)]
    table_specs = [tab_cur] * 3 + [tab_prev] * 3
    return proj_specs, table_specs, cur, prev


def _softmax_band(q2, k2, sink_col, first_block):
    s = lax.dot_general(q2, k2, (((1,), (1,)), ((), ())), preferred_element_type=F32) * (HEAD_DIM ** -0.5)
    a = lax.broadcasted_iota(jnp.int32, s.shape, 0) & (BLOCK - 1)
    si = lax.broadcasted_iota(jnp.int32, s.shape, 1)
    mask = (si <= a + BLOCK) & (si > a + BLOCK - WINDOW) & (jnp.logical_not(first_block) | (si >= BLOCK))
    s = jnp.where(mask, s, -1e30)
    m = jnp.maximum(jnp.max(s, axis=1, keepdims=True), sink_col)
    e = jnp.exp(s - m)
    es = jnp.exp(sink_col - m)
    inv = 1.0 / (jnp.sum(e, axis=1, keepdims=True) + es)
    return e * inv, es * inv


def _sink_col(sink_ref, pair, kvh):
    return jnp.concatenate(
        [jnp.full((BLOCK, 1), sink_ref[(pair * 2 + kvh) * Q_PER_KV + g], F32) for g in range(Q_PER_KV)], axis=0)


def _stack_heads(ref, kvh):
    base = kvh * Q_PER_KV * HEAD_DIM
    return jnp.concatenate([ref[:, base + g * HEAD_DIM: base + (g + 1) * HEAD_DIM] for g in range(Q_PER_KV)], axis=0)


def _unstack_heads(ref, kvh, val):
    base = kvh * Q_PER_KV * HEAD_DIM
    for g in range(Q_PER_KV):
        ref[:, base + g * HEAD_DIM: base + (g + 1) * HEAD_DIM] = val[g * BLOCK:(g + 1) * BLOCK, :].astype(ref.dtype)


def _roped_band(k_ref, kp_ref, v_ref, vp_ref, tabs):
    c, s1, s2, cp, s1p, s2p = (t[...] for t in tabs)
    k_cur = _rope(k_ref[...].astype(F32), c, s1, s2, 1.0)
    k_prev = _rope(kp_ref[...].astype(F32), cp, s1p, s2p, 1.0)
    kband = jnp.concatenate([k_prev, k_cur], axis=0).astype(BF16)
    vband = jnp.concatenate([vp_ref[...], v_ref[...]], axis=0)
    return kband, vband


def _attn_fwd(proj, tables, sink):
    S = proj.shape[0]
    nb = S // BLOCK
    proj_specs, table_specs, _, _ = _attn_specs(nb)

    def body(q_ref, k_ref, kp_ref, v_ref, vp_ref, ga_ref, c_ref, s1_ref, s2_ref, cp_ref, s1p_ref, s2p_ref, sink_ref,
             attn_ref, ain_ref, qr_scr):
        pair, i = pl.program_id(0), pl.program_id(1)
        tabs = (c_ref, s1_ref, s2_ref, cp_ref, s1p_ref, s2p_ref)
        qr_scr[...] = _rope(q_ref[...].astype(F32), c_ref[...], s1_ref[...], s2_ref[...], 1.0)
        kband, vband = _roped_band(k_ref, kp_ref, v_ref, vp_ref, tabs)
        for kvh in range(2):
            q2 = _stack_heads(qr_scr, kvh).astype(BF16)
            k2 = kband[:, kvh * HEAD_DIM:(kvh + 1) * HEAD_DIM]
            v2 = vband[:, kvh * HEAD_DIM:(kvh + 1) * HEAD_DIM]
            p, _ = _softmax_band(q2, k2, _sink_col(sink_ref, pair, kvh), i == 0)
            o = jnp.dot(p.astype(BF16), v2, preferred_element_type=F32)
            _unstack_heads(attn_ref, kvh, o)
        ga = ga_ref[...].astype(F32)
        ain_ref[...] = (attn_ref[...] * (ga * _sigmoid(ga))).astype(BF16)

    out_spec = pl.BlockSpec((BLOCK, COL_BLK), lambda p, i: (i, p))
    return pl.pallas_call(
        body, name="attn_fwd",
        out_shape=(jax.ShapeDtypeStruct((S, ATTN_WIDTH), F32), jax.ShapeDtypeStruct((S, ATTN_WIDTH), BF16)),
        grid=(N_KV_HEADS // 2, nb), in_specs=proj_specs + table_specs + [SMEM_SPEC], out_specs=(out_spec, out_spec),
        scratch_shapes=[pltpu.VMEM((BLOCK, COL_BLK), F32)],
        compiler_params=_params("arbitrary", "arbitrary"),
    )(*([proj] * 6), *tables, *tables, sink)


def _attn_bwd(proj, tables, sink, attn, da_in):
    S = proj.shape[0]
    nb = S // BLOCK
    proj_specs, table_specs, cur, prev = _attn_specs(nb)
    wide_cur = pl.BlockSpec((BLOCK, COL_BLK), lambda p, i: (cur(i), p))
    kv_out = pl.BlockSpec((BLOCK, LANES), lambda p, i: (jnp.maximum(i - 1, 0), p))

    def body(q_ref, k_ref, kp_ref, v_ref, vp_ref, ga_ref, c_ref, s1_ref, s2_ref, cp_ref, s1p_ref, s2p_ref, sink_ref,
             attn_ref, da_ref, dq_ref, dk_ref, dv_ref, dga_ref, dsink_ref,
             qr_scr, do_scr, dq_scr, dk_scr, dv_scr, carry_k, carry_v, sink_acc):
        pair, i = pl.program_id(0), pl.program_id(1)
        tabs = (c_ref, s1_ref, s2_ref, cp_ref, s1p_ref, s2p_ref)

        @pl.when(i < nb)
        def _():
            ga = ga_ref[...].astype(F32)
            sg = _sigmoid(ga)
            da = da_ref[...]
            dga_ref[...] = (da * attn_ref[...] * (sg * (1.0 + ga * (1.0 - sg)))).astype(BF16)
            do_scr[...] = da * (ga * sg)
            qr_scr[...] = _rope(q_ref[...].astype(F32), c_ref[...], s1_ref[...], s2_ref[...], 1.0)
            kband, vband = _roped_band(k_ref, kp_ref, v_ref, vp_ref, tabs)
            for kvh in range(2):
                q2 = _stack_heads(qr_scr, kvh).astype(BF16)
                do2 = _stack_heads(do_scr, kvh)
                o2 = _stack_heads(attn_ref, kvh)
                k2 = kband[:, kvh * HEAD_DIM:(kvh + 1) * HEAD_DIM]
                v2 = vband[:, kvh * HEAD_DIM:(kvh + 1) * HEAD_DIM]
                p, p_sink = _softmax_band(q2, k2, _sink_col(sink_ref, pair, kvh), i == 0)
                do2b = do2.astype(BF16)
                dp = lax.dot_general(do2b, v2, (((1,), (1,)), ((), ())), preferred_element_type=F32)
                delta = jnp.sum(do2 * o2, axis=1, keepdims=True)
                ds = (p * (dp - delta) * (HEAD_DIM ** -0.5)).astype(BF16)
                dq2 = jnp.dot(ds, k2, preferred_element_type=F32)
                dk2 = lax.dot_general(ds, q2, (((0,), (0,)), ((), ())), preferred_element_type=F32)
                dv2 = lax.dot_general(p.astype(BF16), do2b, (((0,), (0,)), ((), ())), preferred_element_type=F32)
                _unstack_heads(dq_scr, kvh, dq2)
                dk_scr[:, kvh * HEAD_DIM:(kvh + 1) * HEAD_DIM] = dk2
                dv_scr[:, kvh * HEAD_DIM:(kvh + 1) * HEAD_DIM] = dv2
                contrib = -p_sink * delta

                @pl.when(i == 0)
                def _():
                    sink_acc[kvh] = contrib

                @pl.when(i > 0)
                def _():
                    sink_acc[kvh] += contrib

            dq_ref[...] = _rope(dq_scr[...], c_ref[...], s1_ref[...], s2_ref[...], -1.0).astype(BF16)
            dk_prev = _rope(dk_scr[0:BLOCK, :], cp_ref[...], s1p_ref[...], s2p_ref[...], -1.0)
            dk_cur = _rope(dk_scr[BLOCK:2 * BLOCK, :], c_ref[...], s1_ref[...], s2_ref[...], -1.0)
            dv_prev, dv_cur = dv_scr[0:BLOCK, :], dv_scr[BLOCK:2 * BLOCK, :]

            @pl.when(i > 0)
            def _():
                dk_ref[...] = (carry_k[...] + dk_prev).astype(BF16)
                dv_ref[...] = (carry_v[...] + dv_prev).astype(BF16)

            carry_k[...] = dk_cur
            carry_v[...] = dv_cur

        @pl.when(i == nb)
        def _():
            dk_ref[...] = carry_k[...].astype(BF16)
            dv_ref[...] = carry_v[...].astype(BF16)
            lane = lax.broadcasted_iota(jnp.int32, (8, LANES), 1)
            acc = jnp.zeros((8, LANES), F32)
            for kvh in range(2):
                for g in range(Q_PER_KV):
                    val = jnp.sum(sink_acc[kvh, g * BLOCK:(g + 1) * BLOCK, :], axis=0, keepdims=True)
                    acc = jnp.where(lane == kvh * Q_PER_KV + g, val, acc)
            dsink_ref[0] = acc

    return pl.pallas_call(
        body, name="attn_bwd",
        out_shape=(jax.ShapeDtypeStruct((S, ATTN_WIDTH), BF16), jax.ShapeDtypeStruct((S, KV_WIDTH), BF16),
                   jax.ShapeDtypeStruct((S, KV_WIDTH), BF16), jax.ShapeDtypeStruct((S, ATTN_WIDTH), BF16),
                   jax.ShapeDtypeStruct((N_KV_HEADS // 2, 8, LANES), F32)),
        grid=(N_KV_HEADS // 2, nb + 1),
        in_specs=proj_specs + table_specs + [SMEM_SPEC, wide_cur, wide_cur],
        out_specs=(wide_cur, kv_out, kv_out, wide_cur, pl.BlockSpec((1, 8, LANES), lambda p, i: (p, 0, 0))),
        scratch_shapes=[pltpu.VMEM((BLOCK, COL_BLK), F32), pltpu.VMEM((BLOCK, COL_BLK), F32),
                        pltpu.VMEM((BLOCK, COL_BLK), F32), pltpu.VMEM((2 * BLOCK, LANES), F32),
                        pltpu.VMEM((2 * BLOCK, LANES), F32), pltpu.VMEM((BLOCK, LANES), F32),
                        pltpu.VMEM((BLOCK, LANES), F32), pltpu.VMEM((2, Q_PER_KV * BLOCK, 1), F32)],
        compiler_params=_params("arbitrary", "arbitrary"),
    )(*([proj] * 6), *tables, *tables, sink, attn, da_in)


N_PARTS = GMLP_WIDTH // COL_BLK
GROUPS_PER_PART = COL_BLK // GMLP_GROUP_DIM


def _part_specs(off):
    return [pl.BlockSpec((GMLP_CHUNK, COL_BLK), functools.partial(lambda j, i: (i, j), off // COL_BLK + k))
            for k in range(N_PARTS)]


def _group(refs, g):
    lo = (g % GROUPS_PER_PART) * GMLP_GROUP_DIM
    return refs[g // GROUPS_PER_PART][:, lo:lo + GMLP_GROUP_DIM].astype(F32)


def _gmlp_norm_stats(vg_refs, gv_scr):
    total = jnp.zeros((GMLP_CHUNK, 1), F32)
    for k in range(N_PARTS):
        gv = _gelu(vg_refs[k][...].astype(F32))
        gv_scr[:, k * COL_BLK:(k + 1) * COL_BLK] = gv
        total = total + jnp.sum(gv, axis=1, keepdims=True)
    mu = total / GMLP_WIDTH
    xc = gv_scr[...] - mu
    var = jnp.sum(xc * xc, axis=1, keepdims=True) / GMLP_WIDTH
    return mu, lax.rsqrt(var + LN_EPS)


def _tril_bf16(ws_ref, g):
    t = lax.broadcasted_iota(jnp.int32, (GMLP_CHUNK, GMLP_CHUNK), 0)
    s = lax.broadcasted_iota(jnp.int32, (GMLP_CHUNK, GMLP_CHUNK), 1)
    return jnp.where(s <= t, ws_ref[g], 0.0).astype(BF16), s <= t


def _gmlp_fwd(proj, ws, bs_t, lg, lb):
    S = proj.shape[0]
    nb = S // GMLP_CHUNK

    def body(*refs):
        u_refs, vg_refs, gb_refs = refs[0:4], refs[4:8], refs[8:12]
        ws_ref, bst_ref, lg_ref, lb_ref, out_ref, gv_scr = refs[12:]
        mu, rstd = _gmlp_norm_stats(vg_refs, gv_scr)
        for g in range(GMLP_GROUPS):
            cols = slice(g * GMLP_GROUP_DIM, (g + 1) * GMLP_GROUP_DIM)
            vn = (gv_scr[:, cols] - mu) * rstd * lg_ref[:, cols] + lb_ref[:, cols]
            w, _ = _tril_bf16(ws_ref, g)
            mixed = jnp.dot(w, vn.astype(BF16), preferred_element_type=F32) + bst_ref[:, g:g + 1]
            gb = _group(gb_refs, g)
            out_ref[:, cols] = (_gelu(_group(u_refs, g)) * mixed * (gb * _sigmoid(gb))).astype(BF16)

    full = lambda shape: pl.BlockSpec(shape, lambda i: tuple(0 for _ in shape))
    return pl.pallas_call(
        body, name="gmlp_fwd", out_shape=jax.ShapeDtypeStruct((S, GMLP_WIDTH), BF16), grid=(nb,),
        in_specs=_part_specs(OFF_U) + _part_specs(OFF_VG) + _part_specs(OFF_GB)
        + [full(ws.shape), full(bs_t.shape), full(lg.shape), full(lb.shape)],
        out_specs=pl.BlockSpec((GMLP_CHUNK, GMLP_WIDTH), lambda i: (i, 0)),
        scratch_shapes=[pltpu.VMEM((GMLP_CHUNK, GMLP_WIDTH), F32)],
        compiler_params=_params("parallel"),
    )(*([proj] * 12), ws, bs_t, lg, lb)


def _gmlp_bwd(proj, ws, bs_t, lg, lb, db_in):
    S = proj.shape[0]
    nb = S // GMLP_CHUNK
    W = GMLP_WIDTH

    def body(*refs):
        u_refs, vg_refs, gb_refs = refs[0:4], refs[4:8], refs[8:12]
        ws_ref, bst_ref, lg_ref, lb_ref, db_ref = refs[12:17]
        out_ref, dws_ref, dbs_ref, dlg_ref, dlb_ref = refs[17:22]
        gv_scr, dvh_scr = refs[22:]
        i = pl.program_id(0)

        @pl.when(i == 0)
        def _():
            dws_ref[...] = jnp.zeros_like(dws_ref)
            dbs_ref[...] = jnp.zeros_like(dbs_ref)
            dlg_ref[...] = jnp.zeros_like(dlg_ref)
            dlb_ref[...] = jnp.zeros_like(dlb_ref)

        mu, rstd = _gmlp_norm_stats(vg_refs, gv_scr)
        sum_dvh = jnp.zeros((GMLP_CHUNK, 1), F32)
        sum_dvh_vh = jnp.zeros((GMLP_CHUNK, 1), F32)
        for g in range(GMLP_GROUPS):
            cols = slice(g * GMLP_GROUP_DIM, (g + 1) * GMLP_GROUP_DIM)
            vhat = (gv_scr[:, cols] - mu) * rstd
            vn = (vhat * lg_ref[:, cols] + lb_ref[:, cols]).astype(BF16)
            w, tril = _tril_bf16(ws_ref, g)
            mixed = jnp.dot(w, vn, preferred_element_type=F32) + bst_ref[:, g:g + 1]
            u, gb, db = _group(u_refs, g), _group(gb_refs, g), db_ref[:, cols]
            gu, sgb = _gelu(u), _sigmoid(gb)
            dsg = db * (gb * sgb)
            out_ref[:, 2 * W + g * GMLP_GROUP_DIM: 2 * W + (g + 1) * GMLP_GROUP_DIM] = (
                db * (gu * mixed) * (sgb * (1.0 + gb * (1.0 - sgb)))).astype(BF16)
            out_ref[:, cols] = (dsg * mixed * _gelu_grad(u)).astype(BF16)
            dmixed = dsg * gu
            dmixed_b = dmixed.astype(BF16)
            dvn = lax.dot_general(w, dmixed_b, (((0,), (0,)), ((), ())), preferred_element_type=F32)
            dw = lax.dot_general(dmixed_b, vn, (((1,), (1,)), ((), ())), preferred_element_type=F32)
            dws_ref[g] += jnp.where(tril, dw, 0.0)
            dbs_ref[g] += jnp.sum(dmixed, axis=1, keepdims=True)
            dlg_ref[:, cols] += jnp.sum(dvn * vhat, axis=0, keepdims=True)
            dlb_ref[:, cols] += jnp.sum(dvn, axis=0, keepdims=True)
            dvh = dvn * lg_ref[:, cols]
            dvh_scr[:, cols] = dvh
            sum_dvh = sum_dvh + jnp.sum(dvh, axis=1, keepdims=True)
            sum_dvh_vh = sum_dvh_vh + jnp.sum(dvh * vhat, axis=1, keepdims=True)
        m1, m2 = sum_dvh / W, sum_dvh_vh / W
        for k in range(N_PARTS):
            cols = slice(k * COL_BLK, (k + 1) * COL_BLK)
            vhat = (gv_scr[:, cols] - mu) * rstd
            dgv = rstd * (dvh_scr[:, cols] - m1 - vhat * m2)
            out_ref[:, W + k * COL_BLK: W + (k + 1) * COL_BLK] = (
                dgv * _gelu_grad(vg_refs[k][...].astype(F32))).astype(BF16)

    full = lambda shape: pl.BlockSpec(shape, lambda i: tuple(0 for _ in shape))
    row = pl.BlockSpec((GMLP_CHUNK, W), lambda i: (i, 0))
    return pl.pallas_call(
        body, name="gmlp_bwd",
        out_shape=(jax.ShapeDtypeStruct((S, 3 * W), BF16), jax.ShapeDtypeStruct(ws.shape, F32),
                   jax.ShapeDtypeStruct((GMLP_GROUPS, GMLP_CHUNK, 1), F32), jax.ShapeDtypeStruct((1, W), F32),
                   jax.ShapeDtypeStruct((1, W), F32)),
        grid=(nb,),
        in_specs=_part_specs(OFF_U) + _part_specs(OFF_VG) + _part_specs(OFF_GB)
        + [full(ws.shape), full(bs_t.shape), full(lg.shape), full(lb.shape), row],
        out_specs=(pl.BlockSpec((GMLP_CHUNK, 3 * W), lambda i: (i, 0)), full(ws.shape),
                   full((GMLP_GROUPS, GMLP_CHUNK, 1)), full((1, W)), full((1, W))),
        scratch_shapes=[pltpu.VMEM((GMLP_CHUNK, W), F32), pltpu.VMEM((GMLP_CHUNK, W), F32)],
        compiler_params=_params("arbitrary"),
    )(*([proj] * 12), ws, bs_t, lg, lb, db_in)


def _merge_fwd(proj, y_a, y_b, rows=256):
    S = proj.shape[0]
    rows = min(rows, S)

    def body(ma_ref, mb_ref, ya_ref, yb_ref, out_ref):
        out_ref[...] = (_sigmoid(ma_ref[...].astype(F32)) * ya_ref[...]
                        + _sigmoid(mb_ref[...].astype(F32)) * yb_ref[...]).astype(BF16)

    blk = lambda off: pl.BlockSpec((rows, COL_BLK), lambda i, j: (i, off // COL_BLK + j))
    return pl.pallas_call(
        body, name="merge_fwd", out_shape=jax.ShapeDtypeStruct((S, D_MODEL), BF16), grid=(S // rows, D_MODEL // COL_BLK),
        in_specs=[blk(OFF_MA), blk(OFF_MB), blk(0), blk(0)], out_specs=blk(0),
        compiler_params=_params("parallel", "parallel"),
    )(proj, proj, y_a, y_b)


def _merge_bwd(proj, y_a, y_b, dmerged, rows=128):
    S = proj.shape[0]
    rows = min(rows, S)
    nj = D_MODEL // COL_BLK

    def body(*refs):
        ma_refs, mb_refs = refs[0:nj], refs[nj:2 * nj]
        ya_ref, yb_ref, dm_ref, dya_ref, dyb_ref, dg_ref = refs[2 * nj:]
        for k in range(nj):
            cols = slice(k * COL_BLK, (k + 1) * COL_BLK)
            dm = dm_ref[:, cols]
            sa, sb = _sigmoid(ma_refs[k][...].astype(F32)), _sigmoid(mb_refs[k][...].astype(F32))
            dya_ref[:, cols] = (dm * sa).astype(BF16)
            dyb_ref[:, cols] = (dm * sb).astype(BF16)
            dg_ref[:, cols] = (dm * ya_ref[:, cols] * sa * (1.0 - sa)).astype(BF16)
            dg_ref[:, D_MODEL + k * COL_BLK: D_MODEL + (k + 1) * COL_BLK] = (
                dm * yb_ref[:, cols] * sb * (1.0 - sb)).astype(BF16)

    part = lambda off: [pl.BlockSpec((rows, COL_BLK), functools.partial(lambda j, i: (i, j), off // COL_BLK + k))
                        for k in range(nj)]
    row = pl.BlockSpec((rows, D_MODEL), lambda i: (i, 0))
    return pl.pallas_call(
        body, name="merge_bwd",
        out_shape=(jax.ShapeDtypeStruct((S, D_MODEL), BF16), jax.ShapeDtypeStruct((S, D_MODEL), BF16),
                   jax.ShapeDtypeStruct((S, 2 * D_MODEL), BF16)),
        grid=(S // rows,), in_specs=part(OFF_MA) + part(OFF_MB) + [row, row, row],
        out_specs=(row, row, pl.BlockSpec((rows, 2 * D_MODEL), lambda i: (i, 0))),
        compiler_params=_params("parallel"),
    )(*([proj] * (2 * nj)), y_a, y_b, dmerged)


def _loss_head(x2, target, fg, rows=128):
    S, D = x2.shape
    rows = min(rows, S)

    def body(x_ref, t_ref, g_ref, dx_ref, dxb_ref, dg_ref, loss_ref):
        i = pl.program_id(0)

        @pl.when(i == 0)
        def _():
            dg_ref[...] = jnp.zeros_like(dg_ref)
            loss_ref[...] = jnp.zeros_like(loss_ref)

        xv, g = x_ref[...], g_ref[...]
        rstd = lax.rsqrt(jnp.mean(xv * xv, axis=-1, keepdims=True) + NORM_EPS)
        xhat = xv * rstd
        err = xhat * g - t_ref[...]
        loss_ref[...] += (0.5 / D) * jnp.sum(err * err)
        dy = err * (1.0 / D)
        dg_ref[...] += jnp.sum(dy * xhat, axis=0, keepdims=True)
        dxh = dy * g
        dx = rstd * (dxh - xhat * jnp.mean(dxh * xhat, axis=-1, keepdims=True))
        dx_ref[...] = dx
        dxb_ref[...] = dx.astype(BF16)

    row = pl.BlockSpec((rows, D), lambda i: (i, 0))
    vec = pl.BlockSpec((1, D), lambda i: (0, 0))
    return pl.pallas_call(
        body, name="loss_head",
        out_shape=(jax.ShapeDtypeStruct((S, D), F32), jax.ShapeDtypeStruct((S, D), BF16),
                   jax.ShapeDtypeStruct((1, D), F32), jax.ShapeDtypeStruct((8, LANES), F32)),
        grid=(S // rows,), in_specs=[row, row, vec],
        out_specs=(row, row, vec, pl.BlockSpec((8, LANES), lambda i: (0, 0))),
        compiler_params=_params("arbitrary"),
    )(x2, target, fg)


def _rms_bwd(x, dh, dx2, g, rows=128):
    S, D = x.shape
    rows = min(rows, S)

    def body(x_ref, dh_ref, dx2_ref, g_ref, gx_ref, dg_ref):
        i = pl.program_id(0)

        @pl.when(i == 0)
        def _():
            dg_ref[...] = jnp.zeros_like(dg_ref)

        xv, dh_v = x_ref[...], dh_ref[...]
        rstd = lax.rsqrt(jnp.mean(xv * xv, axis=-1, keepdims=True) + NORM_EPS)
        xhat = xv * rstd
        dg_ref[...] += jnp.sum(dh_v * xhat, axis=0, keepdims=True)
        dxh = dh_v * g_ref[...]
        gx_ref[...] = dx2_ref[...] + rstd * (dxh - xhat * jnp.mean(dxh * xhat, axis=-1, keepdims=True))

    row = pl.BlockSpec((rows, D), lambda i: (i, 0))
    vec = pl.BlockSpec((1, D), lambda i: (0, 0))
    return pl.pallas_call(
        body, name="rms_bwd", out_shape=(jax.ShapeDtypeStruct((S, D), F32), jax.ShapeDtypeStruct((1, D), F32)),
        grid=(S // rows,), in_specs=[row, row, row, vec], out_specs=(row, vec),
        compiler_params=_params("arbitrary"),
    )(x, dh, dx2, g)


def _adamw(w, g, m, v, name, rows=64):
    R, C = w.shape
    rows = min(rows, R)
    c1 = 1.0 - ADAM_B1 ** ADAM_STEP
    c2 = 1.0 - ADAM_B2 ** ADAM_STEP

    def body(w_ref, g_ref, m_ref, v_ref, go_ref, d_ref, mo_ref, vo_ref):
        gv = g_ref[...]
        mn = ADAM_B1 * m_ref[...] + (1.0 - ADAM_B1) * gv
        vn = ADAM_B2 * v_ref[...] + (1.0 - ADAM_B2) * (gv * gv)
        go_ref[...] = gv
        mo_ref[...] = mn
        vo_ref[...] = vn
        d_ref[...] = -ADAM_LR * ((mn / c1) / (jnp.sqrt(vn / c2) + ADAM_EPS) + ADAM_WD * w_ref[...])

    spec = pl.BlockSpec((rows, C), lambda i: (i, 0))
    shape = jax.ShapeDtypeStruct((R, C), F32)
    return pl.pallas_call(body, name=name, out_shape=(shape,) * 4, grid=(R // rows,), in_specs=[spec] * 4,
                          out_specs=(spec,) * 4, compiler_params=_params("parallel"))(w, g, m, v)


def _place():
    x, y, c = lax.axis_index("x"), lax.axis_index("y"), lax.axis_index("c")
    others = [(1 - x, y), (x, 1 - y), (1 - x, 1 - y)]
    return x, y, c, others


def _chip_index(chip):
    return 2 * chip[0] + chip[1]


def _remote(src, dst, sems, k, to):
    send_sems, recv_sems = sems
    return pltpu.make_async_remote_copy(src_ref=src, dst_ref=dst, send_sem=send_sems.at[k], recv_sem=recv_sems.at[k],
                                        device_id=to, device_id_type=MESH)


def _gather_weights(wi, wa, wg, wo):
    Dm = wi.shape[0]
    half_in, half_up = Dm // 2, UP_ROWS // 2
    n_up = 3

    def body(wi_ref, wa_ref, wg_ref, wo_ref, fi_ref, fa_ref, fg_ref, fo_ref, send_sems, recv_sems, local_sems):
        x, y, c, others = _place()
        me_chip = 2 * x + y
        sems = (send_sems, recv_sems)
        shards = [wa_ref, wg_ref, wo_ref]
        fulls = [fa_ref, fg_ref, fo_ref]

        def piece_in(chip, half):
            return fi_ref.at[pl.ds(half * half_in, half_in), pl.ds(chip * SLAB, SLAB)]

        def piece_up(full, chip, half):
            return full.at[pl.ds(chip * UP_ROWS + half * half_up, half_up), :]

        local = [pltpu.make_async_copy(wi_ref, fi_ref.at[:, pl.ds(me_chip * SLAB, SLAB)], local_sems.at[0])]
        local += [pltpu.make_async_copy(shards[t], fulls[t].at[pl.ds(me_chip * UP_ROWS, UP_ROWS), :], local_sems.at[1 + t])
                  for t in range(n_up)]
        for cp in local:
            cp.start()

        def piece(t, chip, half):
            return piece_in(chip, half) if t == 0 else piece_up(fulls[t - 1], chip, half)

        def my_half(t):
            if t == 0:
                return wi_ref.at[pl.ds(c * half_in, half_in), :]
            return shards[t - 1].at[pl.ds(c * half_up, half_up), :]

        sends = []
        for t in range(1 + n_up):
            for j, chip in enumerate(others):
                cp = _remote(my_half(t), piece(t, me_chip, c), sems, 6 * t + j, (*chip, c))
                cp.start()
                sends.append(cp)
        for t in range(1 + n_up):
            for j, chip in enumerate(others):
                got = piece(t, _chip_index(chip), c)
                _remote(got, got, sems, 6 * t + j, (x, y, c)).wait_recv()
                cp = _remote(got, got, sems, 6 * t + 3 + j, (x, y, 1 - c))
                cp.start()
                sends.append(cp)
        for t in range(1 + n_up):
            for j, chip in enumerate(others):
                got = piece(t, _chip_index(chip), 1 - c)
                _remote(got, got, sems, 6 * t + 3 + j, (x, y, c)).wait_recv()
        for cp in sends:
            cp.wait_send()
        for cp in local:
            cp.wait()

    n_sem = 6 * (1 + n_up)
    return pl.pallas_call(
        body, name="gather_weights",
        out_shape=(jax.ShapeDtypeStruct((Dm, PROJ_WIDTH), BF16),) + (jax.ShapeDtypeStruct((Dm, wa.shape[1]), BF16),) * n_up,
        in_specs=[HBM_SPEC] * 4, out_specs=(HBM_SPEC,) * 4,
        scratch_shapes=[pltpu.SemaphoreType.DMA((n_sem,)), pltpu.SemaphoreType.DMA((n_sem,)),
                        pltpu.SemaphoreType.DMA((1 + n_up,))],
    )(wi, wa, wg, wo)


def _pair_exchange(p_in, p_ups):
    Dm, P = p_in.shape
    half_in = Dm // 2
    n_up = len(p_ups)

    def body(*refs):
        pin_ref, pup_refs = refs[0], refs[1:1 + n_up]
        rin_ref, rup_refs = refs[1 + n_up], refs[2 + n_up:2 + 2 * n_up]
        send_sems, recv_sems = refs[2 + 2 * n_up:]
        x, y, c, _ = _place()
        sib = (x, y, 1 - c)
        sems = (send_sems, recv_sems)
        copies = [_remote(pin_ref.at[pl.ds((1 - c) * half_in, half_in), :], rin_ref, sems, 0, sib)]
        copies += [_remote(pup_refs[t].at[:, pl.ds(1 - c, 1)], rup_refs[t], sems, 1 + t, sib) for t in range(n_up)]
        for cp in copies:
            cp.start()
        for cp in copies:
            cp.wait()

    up_shape = jax.ShapeDtypeStruct((N_CHIPS, 1) + p_ups[0].shape[2:], BF16)
    return pl.pallas_call(
        body, name="pair_exchange",
        out_shape=(jax.ShapeDtypeStruct((half_in, P), BF16),) + (up_shape,) * n_up,
        in_specs=[HBM_SPEC] * (1 + n_up), out_specs=(HBM_SPEC,) * (1 + n_up),
        scratch_shapes=[pltpu.SemaphoreType.DMA((1 + n_up,)), pltpu.SemaphoreType.DMA((1 + n_up,))],
    )(p_in, *p_ups)


def _pair_add_in(idx, p_in, r_in, rows=256, cols=COL_BLK):
    Dm, P = p_in.shape
    half = Dm // 2
    rows = min(rows, half)
    nrb = half // rows

    def body(idx_ref, p_ref, r_ref, o_ref):
        o_ref[...] = (p_ref[...].astype(F32) + r_ref[...].astype(F32)).astype(BF16)

    grid_spec = pltpu.PrefetchScalarGridSpec(
        num_scalar_prefetch=1, grid=(nrb, P // cols),
        in_specs=[pl.BlockSpec((rows, cols), lambda i, j, idx: (idx[0] * nrb + i, j)),
                  pl.BlockSpec((rows, cols), lambda i, j, idx: (i, j))],
        out_specs=pl.BlockSpec((rows, cols), lambda i, j, idx: (i, j)))
    return pl.pallas_call(body, name="pair_add_in", out_shape=jax.ShapeDtypeStruct((half, P), BF16), grid_spec=grid_spec,
                          compiler_params=_params("parallel", "parallel"))(idx, p_in, r_in)


def _pair_add_up(idx, p_up, r_up, name):
    _, _, R, C = p_up.shape

    def body(idx_ref, p_ref, r_ref, o_ref):
        o_ref[...] = (p_ref[...].astype(F32) + r_ref[...].astype(F32)).astype(BF16)

    grid_spec = pltpu.PrefetchScalarGridSpec(
        num_scalar_prefetch=1, grid=(N_CHIPS,),
        in_specs=[pl.BlockSpec((1, 1, R, C), lambda j, idx: (j, idx[0], 0, 0)),
                  pl.BlockSpec((1, 1, R, C), lambda j, idx: (j, 0, 0, 0))],
        out_specs=pl.BlockSpec((1, 1, R, C), lambda j, idx: (j, 0, 0, 0)))
    return pl.pallas_call(body, name=name, out_shape=jax.ShapeDtypeStruct((N_CHIPS, 1, R, C), BF16), grid_spec=grid_spec,
                          compiler_params=_params("parallel"))(idx, p_up, r_up)


def _slab_exchange(q_in, q_ups):
    half_in = q_in.shape[0]
    n_up = len(q_ups)
    _, _, R, C = q_ups[0].shape

    def body(*refs):
        qin_ref, qup_refs = refs[0], refs[1:1 + n_up]
        rin_ref, rup_refs = refs[1 + n_up], refs[2 + n_up:2 + 2 * n_up]
        send_sems, recv_sems = refs[2 + 2 * n_up:]
        x, y, c, others = _place()
        sems = (send_sems, recv_sems)
        copies = []
        for j, chip in enumerate(others):
            to = (*chip, c)
            ci = _chip_index(chip)
            copies.append(_remote(qin_ref.at[:, pl.ds(ci * SLAB, SLAB)], rin_ref.at[j], sems, j, to))
            for t in range(n_up):
                copies.append(_remote(qup_refs[t].at[ci], rup_refs[t].at[j], sems, 3 * (1 + t) + j, to))
        for cp in copies:
            cp.start()
        for cp in copies:
            cp.wait()

    n_sem = 3 * (1 + n_up)
    return pl.pallas_call(
        body, name="slab_exchange",
        out_shape=(jax.ShapeDtypeStruct((3, half_in, SLAB), BF16),) + (jax.ShapeDtypeStruct((3, 1, R, C), BF16),) * n_up,
        in_specs=[HBM_SPEC] * (1 + n_up), out_specs=(HBM_SPEC,) * (1 + n_up),
        scratch_shapes=[pltpu.SemaphoreType.DMA((n_sem,)), pltpu.SemaphoreType.DMA((n_sem,))],
    )(q_in, *q_ups)


def _slab_add_in(idx, q_in, r2_in, rows=128):
    half = q_in.shape[0]
    rows = min(rows, half)

    def body(idx_ref, q_ref, r_ref, o_ref):
        o_ref[...] = ((q_ref[...].astype(F32) + r_ref[0].astype(F32)) + r_ref[1].astype(F32)) + r_ref[2].astype(F32)

    grid_spec = pltpu.PrefetchScalarGridSpec(
        num_scalar_prefetch=1, grid=(half // rows,),
        in_specs=[pl.BlockSpec((rows, SLAB), lambda i, idx: (i, idx[1])),
                  pl.BlockSpec((3, rows, SLAB), lambda i, idx: (0, i, 0))],
        out_specs=pl.BlockSpec((rows, SLAB), lambda i, idx: (i, 0)))
    return pl.pallas_call(body, name="slab_add_in", out_shape=jax.ShapeDtypeStruct((half, SLAB), F32), grid_spec=grid_spec,
                          compiler_params=_params("parallel"))(idx, q_in, r2_in)


def _slab_add_up(idx, q_up, r2_up, name, rows=128):
    _, _, R, C = q_up.shape
    rows = min(rows, R)

    def body(idx_ref, q_ref, r_ref, o_ref):
        o_ref[...] = ((q_ref[0, 0].astype(F32) + r_ref[0, 0].astype(F32)) + r_ref[1, 0].astype(F32)) + r_ref[2, 0].astype(F32)

    grid_spec = pltpu.PrefetchScalarGridSpec(
        num_scalar_prefetch=1, grid=(R // rows,),
        in_specs=[pl.BlockSpec((1, 1, rows, C), lambda i, idx: (idx[1], 0, i, 0)),
                  pl.BlockSpec((3, 1, rows, C), lambda i, idx: (0, 0, i, 0))],
        out_specs=pl.BlockSpec((rows, C), lambda i, idx: (i, 0)))
    return pl.pallas_call(body, name=name, out_shape=jax.ShapeDtypeStruct((R, C), F32), grid_spec=grid_spec,
                          compiler_params=_params("parallel"))(idx, q_up, r2_up)


def _pair_share(f_in, f_ups):
    half_in = f_in.shape[0]
    n_up = len(f_ups)
    half_up, C = f_ups[0].shape

    def body(*refs):
        fin_ref, fup_refs = refs[0], refs[1:1 + n_up]
        gin_ref, gup_refs = refs[1 + n_up], refs[2 + n_up:2 + 2 * n_up]
        send_sems, recv_sems, local_sems = refs[2 + 2 * n_up:]
        x, y, c, _ = _place()
        sib = (x, y, 1 - c)
        sems = (send_sems, recv_sems)
        srcs = [fin_ref] + list(fup_refs)
        dsts = [gin_ref.at[pl.ds(c * half_in, half_in), :]] + [g.at[pl.ds(c * half_up, half_up), :] for g in gup_refs]
        local = [pltpu.make_async_copy(s, d, local_sems.at[t]) for t, (s, d) in enumerate(zip(srcs, dsts))]
        remote = [_remote(s, d, sems, t, sib) for t, (s, d) in enumerate(zip(srcs, dsts))]
        for cp in local + remote:
            cp.start()
        for cp in remote + local:
            cp.wait()

    n = 1 + n_up
    return pl.pallas_call(
        body, name="pair_share",
        out_shape=(jax.ShapeDtypeStruct((2 * half_in, SLAB), F32),) + (jax.ShapeDtypeStruct((2 * half_up, C), F32),) * n_up,
        in_specs=[HBM_SPEC] * n, out_specs=(HBM_SPEC,) * n,
        scratch_shapes=[pltpu.SemaphoreType.DMA((n,)), pltpu.SemaphoreType.DMA((n,)), pltpu.SemaphoreType.DMA((n,))],
    )(f_in, *f_ups)


def _all_reduce_small(packed):
    R, C = packed.shape
    N_DEV = 2 * N_CHIPS

    def body(x_ref, out_ref, all_ref, send_sems, recv_sems, local_sem):
        x, y, c, others = _place()
        me, sib = (x, y, c), (x, y, 1 - c)
        sems = (send_sems, recv_sems)

        def rows(px, py, pc):
            return all_ref.at[4 * px + 2 * py + pc]

        mine = pltpu.make_async_copy(x_ref, rows(*me), local_sem)
        mine.start()
        first = [_remote(x_ref, rows(*me), sems, 0, sib)]
        first += [_remote(x_ref, rows(*me), sems, 1 + j, (*chip, c)) for j, chip in enumerate(others)]
        for cp in first:
            cp.start()
        passed = [_remote(rows(*chip, c), rows(*chip, c), sems, 4 + j, sib) for j, chip in enumerate(others)]
        for j, chip in enumerate(others):
            _remote(rows(*chip, c), rows(*chip, c), sems, 1 + j, me).wait_recv()
            passed[j].start()
        _remote(rows(*sib), rows(*sib), sems, 0, me).wait_recv()
        for j, chip in enumerate(others):
            _remote(rows(*chip, 1 - c), rows(*chip, 1 - c), sems, 4 + j, me).wait_recv()
        for cp in first + passed:
            cp.wait_send()
        mine.wait()
        total = all_ref[0]
        for d in range(1, N_DEV):
            total = total + all_ref[d]
        out_ref[...] = total

    vmem = pl.BlockSpec(memory_space=pltpu.VMEM)
    return pl.pallas_call(
        body, name="all_reduce_small", out_shape=jax.ShapeDtypeStruct((R, C), F32),
        in_specs=[vmem], out_specs=vmem,
        scratch_shapes=[pltpu.VMEM((N_DEV, R, C), F32), pltpu.SemaphoreType.DMA((7,)), pltpu.SemaphoreType.DMA((7,)),
                        pltpu.SemaphoreType.DMA],
        compiler_params=pltpu.CompilerParams(vmem_limit_bytes=VMEM_LIMIT),
    )(packed)


SMALL_NAMES = ("w_spatial", "b_spatial", "norm_g", "gmlp_ln_g", "gmlp_ln_b", "final_norm_g", "attn_sink")


def _pack_small(parts, extra=None):
    blocks = []
    for n in SMALL_NAMES:
        flat = parts[n].reshape(-1).astype(F32)
        rows = -(-flat.shape[0] // (8 * LANES)) * 8
        flat = jnp.pad(flat, (0, rows * LANES - flat.shape[0]))
        blocks.append(flat.reshape(rows, LANES))
    blocks.append(jnp.zeros((8, LANES), F32) if extra is None else extra)
    return jnp.concatenate(blocks, axis=0)


def _unpack_small(packed, shapes):
    out, r = {}, 0
    for n in SMALL_NAMES:
        size = 1
        for s in shapes[n]:
            size *= s
        rows = -(-size // (8 * LANES)) * 8
        out[n] = packed[r:r + rows].reshape(-1)[:size].reshape(shapes[n])
        r += rows
    return out, packed[r:r + 8]


def kernel(x, positions, norm_g, w_in, attn_sink, gmlp_ln_g, gmlp_ln_b, w_spatial, b_spatial, w_up_attn, w_up_gmlp, w_out, final_norm_g, loss_target, m_norm_g, m_w_in, m_attn_sink, m_gmlp_ln_g, m_gmlp_ln_b, m_w_spatial, m_b_spatial, m_w_up_attn, m_w_up_gmlp, m_w_out, m_final_norm_g, v_norm_g, v_w_in, v_attn_sink, v_gmlp_ln_g, v_gmlp_ln_b, v_w_spatial, v_b_spatial, v_w_up_attn, v_w_up_gmlp, v_w_out, v_final_norm_g):
    xs, tgt = x[0], loss_target[0]
    S = xs.shape[0]
    idx = jnp.stack([lax.axis_index("c"), 2 * lax.axis_index("x") + lax.axis_index("y")]).astype(jnp.int32)

    w_full, wua, wug, wo = _gather_weights(
        _cast_bf16(w_in[0], "cast_w_in"), _cast_bf16(w_up_attn[0], "cast_w_up_attn"),
        _cast_bf16(w_up_gmlp[0], "cast_w_up_gmlp"), _cast_bf16(w_out[0], "cast_w_out"))

    tables = _rope_tables(positions[0])
    sink = attn_sink[0]
    ws, bs_t = w_spatial[0], b_spatial[0].T
    h = _rms_fwd(xs, norm_g)
    proj = _matmul(h, w_full, mode="nn", out_dtype=BF16, name="proj")
    attn, a_in = _attn_fwd(proj, tables, sink)
    y_a = _matmul(a_in, wua, mode="nn", out_dtype=F32, name="up_attn")
    b_in = _gmlp_fwd(proj, ws, bs_t, gmlp_ln_g, gmlp_ln_b)
    y_b = _matmul(b_in, wug, mode="nn", out_dtype=F32, name="up_gmlp")
    merged = _merge_fwd(proj, y_a, y_b)
    x2 = _matmul(merged, wo, mode="nn", out_dtype=F32, name="out_proj", residual=xs)
    dx2, dx2_b, d_fg, loss_part = _loss_head(x2, tgt, final_norm_g.reshape(1, D_MODEL))

    dmerged = _matmul(dx2_b, wo, mode="nt", out_dtype=F32, name="d_merged")
    dy_a, dy_b, d_gates = _merge_bwd(proj, y_a, y_b, dmerged)
    da_in = _matmul(dy_a, wua, mode="nt", out_dtype=F32, name="d_a_in")
    db_in = _matmul(dy_b, wug, mode="nt", out_dtype=F32, name="d_b_in")
    dq, dk, dv, dga, d_sink = _attn_bwd(proj, tables, sink, attn, da_in)
    d_gmlp, d_ws, d_bs, d_lg, d_lb = _gmlp_bwd(proj, ws, bs_t, gmlp_ln_g, gmlp_ln_b, db_in)
    dproj = jnp.concatenate([dq, dk, dv, dga, d_gmlp, d_gates], axis=1)
    dh = _matmul(dproj, w_full, mode="nt", out_dtype=F32, name="d_h")
    grad_x, d_ng = _rms_bwd(xs, dh, dx2, norm_g)

    p_in = _matmul(h, dproj, mode="tn", out_dtype=BF16, name="dw_in")
    p_ups = [_matmul(a_in, dy_a, mode="tn", out_dtype=BF16, name="dw_up_attn"),
             _matmul(b_in, dy_b, mode="tn", out_dtype=BF16, name="dw_up_gmlp"),
             _matmul(merged, dx2_b, mode="tn", out_dtype=BF16, name="dw_out")]
    half_up = UP_ROWS // 2
    p_ups = [p.reshape(N_CHIPS, 2, half_up, D_MODEL) for p in p_ups]

    r1 = _pair_exchange(p_in, p_ups)
    q_in = _pair_add_in(idx, p_in, r1[0])
    q_ups = [_pair_add_up(idx, p, r, "pair_add_up%d" % t) for t, (p, r) in enumerate(zip(p_ups, r1[1:]))]
    r2 = _slab_exchange(q_in, q_ups)
    f_in = _slab_add_in(idx, q_in, r2[0])
    f_ups = [_slab_add_up(idx, q, r, "slab_add_up%d" % t) for t, (q, r) in enumerate(zip(q_ups, r2[1:]))]
    g_big = _pair_share(f_in, f_ups)

    small_shapes = {"w_spatial": w_spatial.shape, "b_spatial": b_spatial.shape, "norm_g": norm_g.shape,
                    "gmlp_ln_g": gmlp_ln_g.shape, "gmlp_ln_b": gmlp_ln_b.shape, "final_norm_g": final_norm_g.shape,
                    "attn_sink": attn_sink.shape}
    d_small = {"w_spatial": d_ws, "b_spatial": d_bs, "norm_g": d_ng, "gmlp_ln_g": d_lg, "gmlp_ln_b": d_lb,
               "final_norm_g": d_fg, "attn_sink": d_sink[:, 0, :HEADS_PER_STEP]}
    g_small = _all_reduce_small(_pack_small(d_small, loss_part))

    w_small = _pack_small(dict(w_spatial=w_spatial, b_spatial=b_spatial, norm_g=norm_g, gmlp_ln_g=gmlp_ln_g,
                               gmlp_ln_b=gmlp_ln_b, final_norm_g=final_norm_g, attn_sink=attn_sink))
    m_small = _pack_small(dict(w_spatial=m_w_spatial, b_spatial=m_b_spatial, norm_g=m_norm_g, gmlp_ln_g=m_gmlp_ln_g,
                               gmlp_ln_b=m_gmlp_ln_b, final_norm_g=m_final_norm_g, attn_sink=m_attn_sink))
    v_small = _pack_small(dict(w_spatial=v_w_spatial, b_spatial=v_b_spatial, norm_g=v_norm_g, gmlp_ln_g=v_gmlp_ln_g,
                               gmlp_ln_b=v_gmlp_ln_b, final_norm_g=v_final_norm_g, attn_sink=v_attn_sink))
    small_out = _adamw(w_small, g_small, m_small, v_small, "adamw_small", rows=w_small.shape[0])
    big = {
        "w_in": _adamw(w_in[0], g_big[0], m_w_in[0], v_w_in[0], "adamw_w_in"),
        "w_up_attn": _adamw(w_up_attn[0], g_big[1], m_w_up_attn[0], v_w_up_attn[0], "adamw_w_up_attn"),
        "w_up_gmlp": _adamw(w_up_gmlp[0], g_big[2], m_w_up_gmlp[0], v_w_up_gmlp[0], "adamw_w_up_gmlp"),
        "w_out": _adamw(w_out[0], g_big[3], m_w_out[0], v_w_out[0], "adamw_w_out"),
    }

    order = ("norm_g", "w_in", "attn_sink", "gmlp_ln_g", "gmlp_ln_b", "w_spatial", "b_spatial", "w_up_attn", "w_up_gmlp",
             "w_out", "final_norm_g")
    outs = []
    loss = None
    for kind in range(4):
        small, extra = _unpack_small(small_out[kind], small_shapes)
        if kind == 0:
            loss = extra[0, 0]
        for n in order:
            outs.append(big[n][kind][None] if n in big else small[n])
    return (loss, grad_x[None], *outs)
```

```python
import functools

import jax
import jax.numpy as jnp
from jax import lax
from jax.experimental import pallas as pl
from jax.experimental.pallas import tpu as pltpu

F32 = jnp.float32
BF16 = jnp.bfloat16

D_MODEL = 4096
N_Q_HEADS = 64
N_KV_HEADS = 8
HEAD_DIM = 64
Q_PER_KV = N_Q_HEADS // N_KV_HEADS
ATTN_WIDTH = N_Q_HEADS * HEAD_DIM
KV_WIDTH = N_KV_HEADS * HEAD_DIM
WINDOW = 128
BLOCK = 128
ROPE_THETA = 500000.0
ROPE_DIM = HEAD_DIM // 4
ROPE_HALF = ROPE_DIM // 2
GMLP_WIDTH = D_MODEL
GMLP_GROUPS = 8
GMLP_GROUP_DIM = GMLP_WIDTH // GMLP_GROUPS
GMLP_CHUNK = 128
NORM_EPS = 1e-5
LN_EPS = 1e-5

PROJ_SIZES = (ATTN_WIDTH, KV_WIDTH, KV_WIDTH, ATTN_WIDTH, GMLP_WIDTH, GMLP_WIDTH, GMLP_WIDTH, D_MODEL, D_MODEL)
PROJ_WIDTH = sum(PROJ_SIZES)
OFF_Q, OFF_K, OFF_V, OFF_GA, OFF_U, OFF_VG, OFF_GB, OFF_MA, OFF_MB = (
    sum(PROJ_SIZES[:i]) for i in range(len(PROJ_SIZES)))

ADAM_LR = 0.001
ADAM_B1 = 0.9
ADAM_B2 = 0.999
ADAM_EPS = 1e-08
ADAM_WD = 0.01
ADAM_STEP = 10

N_CHIPS = 4
SLAB = PROJ_WIDTH // N_CHIPS
UP_ROWS = D_MODEL // N_CHIPS
LANES = 128
COL_BLK = 1024
HEADS_PER_STEP = 2 * Q_PER_KV
VMEM_LIMIT = 56 * 1024 * 1024

MESH = pl.DeviceIdType.MESH
HBM_SPEC = pl.BlockSpec(memory_space=pltpu.HBM)
SMEM_SPEC = pl.BlockSpec(memory_space=pltpu.SMEM)


def _params(*sem):
    return pltpu.CompilerParams(dimension_semantics=sem, vmem_limit_bytes=VMEM_LIMIT)


def _sigmoid(x):
    return jax.nn.sigmoid(x)


def _gelu(x):
    return jax.nn.gelu(x, approximate=True)


def _gelu_grad(x):
    c = 0.7978845608028654
    inner = c * (x + 0.044715 * x * x * x)
    t = jnp.tanh(inner)
    return 0.5 * (1.0 + t) + 0.5 * x * (1.0 - t * t) * c * (1.0 + 3 * 0.044715 * x * x)


def _matmul(a, b, *, mode, out_dtype, name, residual=None, comm=None, tm=1024, tn=1024, tk=1024):
    if mode == "nn":
        (M, K), N = a.shape, b.shape[1]
    elif mode == "nt":
        (M, K), N = a.shape, b.shape[0]
    else:
        (K, M), N = a.shape, b.shape[1]
    tm, tn, tk = min(tm, M), min(tn, N), min(tk, K)
    assert M % tm == 0 and N % tn == 0 and K % tk == 0
    ni, nj, nk = M // tm, N // tn, K // tk
    if mode == "nn":
        a_spec = pl.BlockSpec((tm, tk), lambda i, j, k: (i, k))
        b_spec = pl.BlockSpec((tk, tn), lambda i, j, k: (k, j))
        dims = (((1,), (0,)), ((), ()))
    elif mode == "nt":
        a_spec = pl.BlockSpec((tm, tk), lambda i, j, k: (i, k))
        b_spec = pl.BlockSpec((tn, tk), lambda i, j, k: (j, k))
        dims = (((1,), (1,)), ((), ()))
    else:
        a_spec = pl.BlockSpec((tk, tm), lambda i, j, k: (k, i))
        b_spec = pl.BlockSpec((tk, tn), lambda i, j, k: (k, j))
        dims = (((0,), (0,)), ((), ()))
    o_spec = pl.BlockSpec((tm, tn), lambda i, j, k: (i, j))
    n_res = 1 if residual is not None else 0
    c_arrays, c_shapes, c_sems, c_build = comm if comm is not None else ((), (), 0, None)
    n_cin, n_cout = len(c_arrays), len(c_shapes)

    def body(*refs):
        a_ref, b_ref = refs[0], refs[1]
        r_ref = refs[2] if n_res else None
        cin = refs[2 + n_res:2 + n_res + n_cin]
        o_ref = refs[2 + n_res + n_cin]
        cout = refs[3 + n_res + n_cin:3 + n_res + n_cin + n_cout]
        acc = refs[3 + n_res + n_cin + n_cout]
        sems = refs[4 + n_res + n_cin + n_cout:]
        i, j, k = pl.program_id(0), pl.program_id(1), pl.program_id(2)
        if comm is not None:
            @pl.when((i == 0) & (j == 0) & (k == 0))
            def _():
                for cp in c_build(cin, cout, sems):
                    cp.start()

        part = lax.dot_general(a_ref[...], b_ref[...], dims, preferred_element_type=F32)

        @pl.when(k == 0)
        def _():
            acc[...] = part

        @pl.when(k > 0)
        def _():
            acc[...] += part

        @pl.when(k == nk - 1)
        def _():
            r = acc[...]
            if n_res:
                r = r + r_ref[...]
            o_ref[...] = r.astype(out_dtype)

        if comm is not None:
            @pl.when((i == ni - 1) & (j == nj - 1) & (k == nk - 1))
            def _():
                for cp in c_build(cin, cout, sems):
                    cp.wait()

    args = (a, b) + ((residual,) if n_res else ()) + tuple(c_arrays)
    in_specs = [a_spec, b_spec] + ([o_spec] if n_res else []) + [HBM_SPEC] * n_cin
    scratch = [pltpu.VMEM((tm, tn), F32)]
    if comm is not None:
        scratch += [pltpu.SemaphoreType.DMA((c_sems,)), pltpu.SemaphoreType.DMA((c_sems,))]
    sem = ("arbitrary",) * 3 if comm is not None else ("parallel", "parallel", "arbitrary")
    out = pl.pallas_call(
        body, name=name, out_shape=(jax.ShapeDtypeStruct((M, N), out_dtype),) + tuple(c_shapes),
        grid=(ni, nj, nk), in_specs=in_specs, out_specs=(o_spec,) + (HBM_SPEC,) * n_cout,
        scratch_shapes=scratch, compiler_params=_params(*sem),
    )(*args)
    return out if comm is not None else out[0]


def _cast_into(idx, w, full_shape, name, rows=64):
    R, C = w.shape
    rows = min(rows, R)
    nrb = R // rows
    by_cols = C != full_shape[1]

    def body(idx_ref, w_ref, o_ref):
        o_ref[...] = w_ref[...].astype(BF16)

    out_map = (lambda i, idx: (i, idx[1])) if by_cols else (lambda i, idx: (idx[1] * nrb + i, 0))
    grid_spec = pltpu.PrefetchScalarGridSpec(
        num_scalar_prefetch=1, grid=(nrb,), in_specs=[pl.BlockSpec((rows, C), lambda i, idx: (i, 0))],
        out_specs=pl.BlockSpec((rows, C), out_map))
    return pl.pallas_call(body, name=name, out_shape=jax.ShapeDtypeStruct(full_shape, BF16), grid_spec=grid_spec,
                          compiler_params=_params("parallel"))(idx, w)


def _rms_fwd(x, g, rows=256):
    S, D = x.shape
    rows = min(rows, S)

    def body(x_ref, g_ref, h_ref):
        xv = x_ref[...]
        ms = jnp.mean(xv * xv, axis=-1, keepdims=True)
        h_ref[...] = (xv * lax.rsqrt(ms + NORM_EPS) * g_ref[...]).astype(BF16)

    spec = pl.BlockSpec((rows, D), lambda i: (i, 0))
    return pl.pallas_call(body, name="rms_fwd", out_shape=jax.ShapeDtypeStruct((S, D), BF16), grid=(S // rows,),
                          in_specs=[spec, pl.BlockSpec((1, D), lambda i: (0, 0))], out_specs=spec,
                          compiler_params=_params("parallel"))(x, g)


def _rope_tables(positions):
    inv_freq = ROPE_THETA ** (-jnp.arange(ROPE_HALF, dtype=F32) * 2.0 / ROPE_DIM)
    ang = positions.astype(F32)[:, None] * inv_freq
    cos, sin = jnp.cos(ang), jnp.sin(ang)
    S = positions.shape[0]
    rest = HEAD_DIM - ROPE_DIM
    zeros_h, zeros_r = jnp.zeros((S, ROPE_HALF), F32), jnp.zeros((S, rest), F32)
    c = jnp.concatenate([cos, cos, jnp.ones((S, rest), F32)], axis=1)
    s1 = jnp.concatenate([-sin, zeros_h, zeros_r], axis=1)
    s2 = jnp.concatenate([zeros_h, sin, zeros_r], axis=1)
    reps = LANES // HEAD_DIM
    return jnp.tile(c, (1, reps)), jnp.tile(s1, (1, reps)), jnp.tile(s2, (1, reps))


def _rope(t, c, s1, s2, sign):
    n = t.shape[1]
    reps = n // c.shape[1]
    if reps > 1:
        c, s1, s2 = jnp.tile(c, (1, reps)), jnp.tile(s1, (1, reps)), jnp.tile(s2, (1, reps))
    up = pltpu.roll(t, n - ROPE_HALF, 1)
    down = pltpu.roll(t, ROPE_HALF, 1)
    return t * c + sign * (up * s1 + down * s2)


def _attn_specs(nb):
    last = nb - 1

    def cur(i):
        return jnp.minimum(i, last)

    def prev(i):
        return jnp.maximum(jnp.minimum(i, last) - 1, 0)

    kq, kk, kv, kg = OFF_Q // COL_BLK, OFF_K // LANES, OFF_V // LANES, OFF_GA // COL_BLK
    wide = lambda off: pl.BlockSpec((BLOCK, COL_BLK), lambda p, i: (cur(i), off + p))
    kv_cur = lambda off: pl.BlockSpec((BLOCK, LANES), lambda p, i: (cur(i), off + p))
    kv_prev = lambda off: pl.BlockSpec((BLOCK, LANES), lambda p, i: (prev(i), off + p))
    tab_cur = pl.BlockSpec((BLOCK, LANES), lambda p, i: (cur(i), 0))
    tab_prev = pl.BlockSpec((BLOCK, LANES), lambda p, i: (prev(i), 0))
    proj_specs = [wide(kq), kv_cur(kk), kv_prev(kk), kv_cur(kv), kv_prev(kv), wide(kg)]
    table_specs = [tab_cur] * 3 + [tab_prev] * 3
    return proj_specs, table_specs, cur, prev


def _softmax_band(q2, k2, sink_col, first_block):
    s = lax.dot_general(q2, k2, (((1,), (1,)), ((), ())), preferred_element_type=F32) * (HEAD_DIM ** -0.5)
    a = lax.broadcasted_iota(jnp.int32, s.shape, 0) & (BLOCK - 1)
    si = lax.broadcasted_iota(jnp.int32, s.shape, 1)
    mask = (si <= a + BLOCK) & (si > a + BLOCK - WINDOW) & (jnp.logical_not(first_block) | (si >= BLOCK))
    s = jnp.where(mask, s, -1e30)
    m = jnp.maximum(jnp.max(s, axis=1, keepdims=True), sink_col)
    e = jnp.exp(s - m)
    es = jnp.exp(sink_col - m)
    inv = 1.0 / (jnp.sum(e, axis=1, keepdims=True) + es)
    return e * inv, es * inv


def _sink_col(sink_ref, pair, kvh):
    return jnp.concatenate(
        [jnp.full((BLOCK, 1), sink_ref[(pair * 2 + kvh) * Q_PER_KV + g], F32) for g in range(Q_PER_KV)], axis=0)


def _stack_heads(ref, kvh):
    base = kvh * Q_PER_KV * HEAD_DIM
    return jnp.concatenate([ref[:, base + g * HEAD_DIM: base + (g + 1) * HEAD_DIM] for g in range(Q_PER_KV)], axis=0)


def _unstack_heads(ref, kvh, val):
    base = kvh * Q_PER_KV * HEAD_DIM
    for g in range(Q_PER_KV):
        ref[:, base + g * HEAD_DIM: base + (g + 1) * HEAD_DIM] = val[g * BLOCK:(g + 1) * BLOCK, :].astype(ref.dtype)


def _roped_band(k_ref, kp_ref, v_ref, vp_ref, tabs):
    c, s1, s2, cp, s1p, s2p = (t[...] for t in tabs)
    k_cur = _rope(k_ref[...].astype(F32), c, s1, s2, 1.0)
    k_prev = _rope(kp_ref[...].astype(F32), cp, s1p, s2p, 1.0)
    kband = jnp.concatenate([k_prev, k_cur], axis=0).astype(BF16)
    vband = jnp.concatenate([vp_ref[...], v_ref[...]], axis=0)
    return kband, vband


def _attn_fwd(proj, tables, sink):
    S = proj.shape[0]
    nb = S // BLOCK
    proj_specs, table_specs, _, _ = _attn_specs(nb)

    def body(q_ref, k_ref, kp_ref, v_ref, vp_ref, ga_ref, c_ref, s1_ref, s2_ref, cp_ref, s1p_ref, s2p_ref, sink_ref,
             attn_ref, ain_ref, qr_scr):
        pair, i = pl.program_id(0), pl.program_id(1)
        tabs = (c_ref, s1_ref, s2_ref, cp_ref, s1p_ref, s2p_ref)
        qr_scr[...] = _rope(q_ref[...].astype(F32), c_ref[...], s1_ref[...], s2_ref[...], 1.0)
        kband, vband = _roped_band(k_ref, kp_ref, v_ref, vp_ref, tabs)
        for kvh in range(2):
            q2 = _stack_heads(qr_scr, kvh).astype(BF16)
            k2 = kband[:, kvh * HEAD_DIM:(kvh + 1) * HEAD_DIM]
            v2 = vband[:, kvh * HEAD_DIM:(kvh + 1) * HEAD_DIM]
            p, _ = _softmax_band(q2, k2, _sink_col(sink_ref, pair, kvh), i == 0)
            o = jnp.dot(p.astype(BF16), v2, preferred_element_type=F32)
            _unstack_heads(attn_ref, kvh, o)
        ga = ga_ref[...].astype(F32)
        ain_ref[...] = (attn_ref[...] * (ga * _sigmoid(ga))).astype(BF16)

    out_spec = pl.BlockSpec((BLOCK, COL_BLK), lambda p, i: (i, p))
    return pl.pallas_call(
        body, name="attn_fwd",
        out_shape=(jax.ShapeDtypeStruct((S, ATTN_WIDTH), F32), jax.ShapeDtypeStruct((S, ATTN_WIDTH), BF16)),
        grid=(N_KV_HEADS // 2, nb), in_specs=proj_specs + table_specs + [SMEM_SPEC], out_specs=(out_spec, out_spec),
        scratch_shapes=[pltpu.VMEM((BLOCK, COL_BLK), F32)],
        compiler_params=_params("arbitrary", "arbitrary"),
    )(*([proj] * 6), *tables, *tables, sink)


def _attn_bwd(proj, tables, sink, attn, da_in):
    S = proj.shape[0]
    nb = S // BLOCK
    proj_specs, table_specs, cur, prev = _attn_specs(nb)
    wide_cur = pl.BlockSpec((BLOCK, COL_BLK), lambda p, i: (cur(i), p))
    kv_out = pl.BlockSpec((BLOCK, LANES), lambda p, i: (jnp.maximum(i - 1, 0), p))

    def body(q_ref, k_ref, kp_ref, v_ref, vp_ref, ga_ref, c_ref, s1_ref, s2_ref, cp_ref, s1p_ref, s2p_ref, sink_ref,
             attn_ref, da_ref, dq_ref, dk_ref, dv_ref, dga_ref, dsink_ref,
             qr_scr, do_scr, dq_scr, dk_scr, dv_scr, carry_k, carry_v, sink_acc):
        pair, i = pl.program_id(0), pl.program_id(1)
        tabs = (c_ref, s1_ref, s2_ref, cp_ref, s1p_ref, s2p_ref)

        @pl.when(i < nb)
        def _():
            ga = ga_ref[...].astype(F32)
            sg = _sigmoid(ga)
            da = da_ref[...]
            dga_ref[...] = (da * attn_ref[...] * (sg * (1.0 + ga * (1.0 - sg)))).astype(BF16)
            do_scr[...] = da * (ga * sg)
            qr_scr[...] = _rope(q_ref[...].astype(F32), c_ref[...], s1_ref[...], s2_ref[...], 1.0)
            kband, vband = _roped_band(k_ref, kp_ref, v_ref, vp_ref, tabs)
            for kvh in range(2):
                q2 = _stack_heads(qr_scr, kvh).astype(BF16)
                do2 = _stack_heads(do_scr, kvh)
                o2 = _stack_heads(attn_ref, kvh)
                k2 = kband[:, kvh * HEAD_DIM:(kvh + 1) * HEAD_DIM]
                v2 = vband[:, kvh * HEAD_DIM:(kvh + 1) * HEAD_DIM]
                p, p_sink = _softmax_band(q2, k2, _sink_col(sink_ref, pair, kvh), i == 0)
                do2b = do2.astype(BF16)
                dp = lax.dot_general(do2b, v2, (((1,), (1,)), ((), ())), preferred_element_type=F32)
                delta = jnp.sum(do2 * o2, axis=1, keepdims=True)
                ds = (p * (dp - delta) * (HEAD_DIM ** -0.5)).astype(BF16)
                dq2 = jnp.dot(ds, k2, preferred_element_type=F32)
                dk2 = lax.dot_general(ds, q2, (((0,), (0,)), ((), ())), preferred_element_type=F32)
                dv2 = lax.dot_general(p.astype(BF16), do2b, (((0,), (0,)), ((), ())), preferred_element_type=F32)
                _unstack_heads(dq_scr, kvh, dq2)
                dk_scr[:, kvh * HEAD_DIM:(kvh + 1) * HEAD_DIM] = dk2
                dv_scr[:, kvh * HEAD_DIM:(kvh + 1) * HEAD_DIM] = dv2
                contrib = -p_sink * delta

                @pl.when(i == 0)
                def _():
                    sink_acc[kvh] = contrib

                @pl.when(i > 0)
                def _():
                    sink_acc[kvh] += contrib

            dq_ref[...] = _rope(dq_scr[...], c_ref[...], s1_ref[...], s2_ref[...], -1.0).astype(BF16)
            dk_prev = _rope(dk_scr[0:BLOCK, :], cp_ref[...], s1p_ref[...], s2p_ref[...], -1.0)
            dk_cur = _rope(dk_scr[BLOCK:2 * BLOCK, :], c_ref[...], s1_ref[...], s2_ref[...], -1.0)
            dv_prev, dv_cur = dv_scr[0:BLOCK, :], dv_scr[BLOCK:2 * BLOCK, :]

            @pl.when(i > 0)
            def _():
                dk_ref[...] = (carry_k[...] + dk_prev).astype(BF16)
                dv_ref[...] = (carry_v[...] + dv_prev).astype(BF16)

            carry_k[...] = dk_cur
            carry_v[...] = dv_cur

        @pl.when(i == nb)
        def _():
            dk_ref[...] = carry_k[...].astype(BF16)
            dv_ref[...] = carry_v[...].astype(BF16)
            lane = lax.broadcasted_iota(jnp.int32, (8, LANES), 1)
            acc = jnp.zeros((8, LANES), F32)
            for kvh in range(2):
                for g in range(Q_PER_KV):
                    val = jnp.sum(sink_acc[kvh, g * BLOCK:(g + 1) * BLOCK, :], axis=0, keepdims=True)
                    acc = jnp.where(lane == kvh * Q_PER_KV + g, val, acc)
            dsink_ref[0] = acc

    return pl.pallas_call(
        body, name="attn_bwd",
        out_shape=(jax.ShapeDtypeStruct((S, ATTN_WIDTH), BF16), jax.ShapeDtypeStruct((S, KV_WIDTH), BF16),
                   jax.ShapeDtypeStruct((S, KV_WIDTH), BF16), jax.ShapeDtypeStruct((S, ATTN_WIDTH), BF16),
                   jax.ShapeDtypeStruct((N_KV_HEADS // 2, 8, LANES), F32)),
        grid=(N_KV_HEADS // 2, nb + 1),
        in_specs=proj_specs + table_specs + [SMEM_SPEC, wide_cur, wide_cur],
        out_specs=(wide_cur, kv_out, kv_out, wide_cur, pl.BlockSpec((1, 8, LANES), lambda p, i: (p, 0, 0))),
        scratch_shapes=[pltpu.VMEM((BLOCK, COL_BLK), F32), pltpu.VMEM((BLOCK, COL_BLK), F32),
                        pltpu.VMEM((BLOCK, COL_BLK), F32), pltpu.VMEM((2 * BLOCK, LANES), F32),
                        pltpu.VMEM((2 * BLOCK, LANES), F32), pltpu.VMEM((BLOCK, LANES), F32),
                        pltpu.VMEM((BLOCK, LANES), F32), pltpu.VMEM((2, Q_PER_KV * BLOCK, 1), F32)],
        compiler_params=_params("arbitrary", "arbitrary"),
    )(*([proj] * 6), *tables, *tables, sink, attn, da_in)


N_PARTS = GMLP_WIDTH // COL_BLK
GROUPS_PER_PART = COL_BLK // GMLP_GROUP_DIM


def _part_specs(off):
    return [pl.BlockSpec((GMLP_CHUNK, COL_BLK), functools.partial(lambda j, i: (i, j), off // COL_BLK + k))
            for k in range(N_PARTS)]


def _group(refs, g):
    lo = (g % GROUPS_PER_PART) * GMLP_GROUP_DIM
    return refs[g // GROUPS_PER_PART][:, lo:lo + GMLP_GROUP_DIM].astype(F32)


def _gmlp_norm_stats(vg_refs, gv_scr):
    total = jnp.zeros((GMLP_CHUNK, 1), F32)
    for k in range(N_PARTS):
        gv = _gelu(vg_refs[k][...].astype(F32))
        gv_scr[:, k * COL_BLK:(k + 1) * COL_BLK] = gv
        total = total + jnp.sum(gv, axis=1, keepdims=True)
    mu = total / GMLP_WIDTH
    xc = gv_scr[...] - mu
    var = jnp.sum(xc * xc, axis=1, keepdims=True) / GMLP_WIDTH
    return mu, lax.rsqrt(var + LN_EPS)


def _tril_bf16(ws_ref, g):
    t = lax.broadcasted_iota(jnp.int32, (GMLP_CHUNK, GMLP_CHUNK), 0)
    s = lax.broadcasted_iota(jnp.int32, (GMLP_CHUNK, GMLP_CHUNK), 1)
    return jnp.where(s <= t, ws_ref[g], 0.0).astype(BF16), s <= t


def _gmlp_fwd(proj, ws, bs_t, lg, lb):
    S = proj.shape[0]
    nb = S // GMLP_CHUNK

    def body(*refs):
        u_refs, vg_refs, gb_refs = refs[0:4], refs[4:8], refs[8:12]
        ws_ref, bst_ref, lg_ref, lb_ref, out_ref, gv_scr = refs[12:]
        mu, rstd = _gmlp_norm_stats(vg_refs, gv_scr)
        for g in range(GMLP_GROUPS):
            cols = slice(g * GMLP_GROUP_DIM, (g + 1) * GMLP_GROUP_DIM)
            vn = (gv_scr[:, cols] - mu) * rstd * lg_ref[:, cols] + lb_ref[:, cols]
            w, _ = _tril_bf16(ws_ref, g)
            mixed = jnp.dot(w, vn.astype(BF16), preferred_element_type=F32) + bst_ref[:, g:g + 1]
            gb = _group(gb_refs, g)
            out_ref[:, cols] = (_gelu(_group(u_refs, g)) * mixed * (gb * _sigmoid(gb))).astype(BF16)

    full = lambda shape: pl.BlockSpec(shape, lambda i: tuple(0 for _ in shape))
    return pl.pallas_call(
        body, name="gmlp_fwd", out_shape=jax.ShapeDtypeStruct((S, GMLP_WIDTH), BF16), grid=(nb,),
        in_specs=_part_specs(OFF_U) + _part_specs(OFF_VG) + _part_specs(OFF_GB)
        + [full(ws.shape), full(bs_t.shape), full(lg.shape), full(lb.shape)],
        out_specs=pl.BlockSpec((GMLP_CHUNK, GMLP_WIDTH), lambda i: (i, 0)),
        scratch_shapes=[pltpu.VMEM((GMLP_CHUNK, GMLP_WIDTH), F32)],
        compiler_params=_params("parallel"),
    )(*([proj] * 12), ws, bs_t, lg, lb)


def _gmlp_bwd(proj, ws, bs_t, lg, lb, db_in):
    S = proj.shape[0]
    nb = S // GMLP_CHUNK
    W = GMLP_WIDTH

    def body(*refs):
        u_refs, vg_refs, gb_refs = refs[0:4], refs[4:8], refs[8:12]
        ws_ref, bst_ref, lg_ref, lb_ref, db_ref = refs[12:17]
        out_ref, dws_ref, dbs_ref, dlg_ref, dlb_ref = refs[17:22]
        gv_scr, dvh_scr = refs[22:]
        i = pl.program_id(0)

        @pl.when(i == 0)
        def _():
            dws_ref[...] = jnp.zeros_like(dws_ref)
            dbs_ref[...] = jnp.zeros_like(dbs_ref)
            dlg_ref[...] = jnp.zeros_like(dlg_ref)
            dlb_ref[...] = jnp.zeros_like(dlb_ref)

        mu, rstd = _gmlp_norm_stats(vg_refs, gv_scr)
        sum_dvh = jnp.zeros((GMLP_CHUNK, 1), F32)
        sum_dvh_vh = jnp.zeros((GMLP_CHUNK, 1), F32)
        for g in range(GMLP_GROUPS):
            cols = slice(g * GMLP_GROUP_DIM, (g + 1) * GMLP_GROUP_DIM)
            vhat = (gv_scr[:, cols] - mu) * rstd
            vn = (vhat * lg_ref[:, cols] + lb_ref[:, cols]).astype(BF16)
            w, tril = _tril_bf16(ws_ref, g)
            mixed = jnp.dot(w, vn, preferred_element_type=F32) + bst_ref[:, g:g + 1]
            u, gb, db = _group(u_refs, g), _group(gb_refs, g), db_ref[:, cols]
            gu, sgb = _gelu(u), _sigmoid(gb)
            dsg = db * (gb * sgb)
            out_ref[:, 2 * W + g * GMLP_GROUP_DIM: 2 * W + (g + 1) * GMLP_GROUP_DIM] = (
                db * (gu * mixed) * (sgb * (1.0 + gb * (1.0 - sgb)))).astype(BF16)
            out_ref[:, cols] = (dsg * mixed * _gelu_grad(u)).astype(BF16)
            dmixed = dsg * gu
            dmixed_b = dmixed.astype(BF16)
            dvn = lax.dot_general(w, dmixed_b, (((0,), (0,)), ((), ())), preferred_element_type=F32)
            dw = lax.dot_general(dmixed_b, vn, (((1,), (1,)), ((), ())), preferred_element_type=F32)
            dws_ref[g] += jnp.where(tril, dw, 0.0)
            dbs_ref[g] += jnp.sum(dmixed, axis=1, keepdims=True)
            dlg_ref[:, cols] += jnp.sum(dvn * vhat, axis=0, keepdims=True)
            dlb_ref[:, cols] += jnp.sum(dvn, axis=0, keepdims=True)
            dvh = dvn * lg_ref[:, cols]
            dvh_scr[:, cols] = dvh
            sum_dvh = sum_dvh + jnp.sum(dvh, axis=1, keepdims=True)
            sum_dvh_vh = sum_dvh_vh + jnp.sum(dvh * vhat, axis=1, keepdims=True)
        m1, m2 = sum_dvh / W, sum_dvh_vh / W
        for k in range(N_PARTS):
            cols = slice(k * COL_BLK, (k + 1) * COL_BLK)
            vhat = (gv_scr[:, cols] - mu) * rstd
            dgv = rstd * (dvh_scr[:, cols] - m1 - vhat * m2)
            out_ref[:, W + k * COL_BLK: W + (k + 1) * COL_BLK] = (
                dgv * _gelu_grad(vg_refs[k][...].astype(F32))).astype(BF16)

    full = lambda shape: pl.BlockSpec(shape, lambda i: tuple(0 for _ in shape))
    row = pl.BlockSpec((GMLP_CHUNK, W), lambda i: (i, 0))
    return pl.pallas_call(
        body, name="gmlp_bwd",
        out_shape=(jax.ShapeDtypeStruct((S, 3 * W), BF16), jax.ShapeDtypeStruct(ws.shape, F32),
                   jax.ShapeDtypeStruct((GMLP_GROUPS, GMLP_CHUNK, 1), F32), jax.ShapeDtypeStruct((1, W), F32),
                   jax.ShapeDtypeStruct((1, W), F32)),
        grid=(nb,),
        in_specs=_part_specs(OFF_U) + _part_specs(OFF_VG) + _part_specs(OFF_GB)
        + [full(ws.shape), full(bs_t.shape), full(lg.shape), full(lb.shape), row],
        out_specs=(pl.BlockSpec((GMLP_CHUNK, 3 * W), lambda i: (i, 0)), full(ws.shape),
                   full((GMLP_GROUPS, GMLP_CHUNK, 1)), full((1, W)), full((1, W))),
        scratch_shapes=[pltpu.VMEM((GMLP_CHUNK, W), F32), pltpu.VMEM((GMLP_CHUNK, W), F32)],
        compiler_params=_params("arbitrary"),
    )(*([proj] * 12), ws, bs_t, lg, lb, db_in)


def _merge_fwd(proj, y_a, y_b, rows=256):
    S = proj.shape[0]
    rows = min(rows, S)

    def body(ma_ref, mb_ref, ya_ref, yb_ref, out_ref):
        out_ref[...] = (_sigmoid(ma_ref[...].astype(F32)) * ya_ref[...]
                        + _sigmoid(mb_ref[...].astype(F32)) * yb_ref[...]).astype(BF16)

    blk = lambda off: pl.BlockSpec((rows, COL_BLK), lambda i, j: (i, off // COL_BLK + j))
    return pl.pallas_call(
        body, name="merge_fwd", out_shape=jax.ShapeDtypeStruct((S, D_MODEL), BF16), grid=(S // rows, D_MODEL // COL_BLK),
        in_specs=[blk(OFF_MA), blk(OFF_MB), blk(0), blk(0)], out_specs=blk(0),
        compiler_params=_params("parallel", "parallel"),
    )(proj, proj, y_a, y_b)


def _merge_bwd(proj, y_a, y_b, dmerged, rows=128):
    S = proj.shape[0]
    rows = min(rows, S)
    nj = D_MODEL // COL_BLK

    def body(*refs):
        ma_refs, mb_refs = refs[0:nj], refs[nj:2 * nj]
        ya_ref, yb_ref, dm_ref, dya_ref, dyb_ref, dg_ref = refs[2 * nj:]
        for k in range(nj):
            cols = slice(k * COL_BLK, (k + 1) * COL_BLK)
            dm = dm_ref[:, cols]
            sa, sb = _sigmoid(ma_refs[k][...].astype(F32)), _sigmoid(mb_refs[k][...].astype(F32))
            dya_ref[:, cols] = (dm * sa).astype(BF16)
            dyb_ref[:, cols] = (dm * sb).astype(BF16)
            dg_ref[:, cols] = (dm * ya_ref[:, cols] * sa * (1.0 - sa)).astype(BF16)
            dg_ref[:, D_MODEL + k * COL_BLK: D_MODEL + (k + 1) * COL_BLK] = (
                dm * yb_ref[:, cols] * sb * (1.0 - sb)).astype(BF16)

    part = lambda off: [pl.BlockSpec((rows, COL_BLK), functools.partial(lambda j, i: (i, j), off // COL_BLK + k))
                        for k in range(nj)]
    row = pl.BlockSpec((rows, D_MODEL), lambda i: (i, 0))
    return pl.pallas_call(
        body, name="merge_bwd",
        out_shape=(jax.ShapeDtypeStruct((S, D_MODEL), BF16), jax.ShapeDtypeStruct((S, D_MODEL), BF16),
                   jax.ShapeDtypeStruct((S, 2 * D_MODEL), BF16)),
        grid=(S // rows,), in_specs=part(OFF_MA) + part(OFF_MB) + [row, row, row],
        out_specs=(row, row, pl.BlockSpec((rows, 2 * D_MODEL), lambda i: (i, 0))),
        compiler_params=_params("parallel"),
    )(*([proj] * (2 * nj)), y_a, y_b, dmerged)


def _loss_head(x2, target, fg, rows=128):
    S, D = x2.shape
    rows = min(rows, S)

    def body(x_ref, t_ref, g_ref, dx_ref, dxb_ref, dg_ref, loss_ref):
        i = pl.program_id(0)

        @pl.when(i == 0)
        def _():
            dg_ref[...] = jnp.zeros_like(dg_ref)
            loss_ref[...] = jnp.zeros_like(loss_ref)

        xv, g = x_ref[...], g_ref[...]
        rstd = lax.rsqrt(jnp.mean(xv * xv, axis=-1, keepdims=True) + NORM_EPS)
        xhat = xv * rstd
        err = xhat * g - t_ref[...]
        loss_ref[...] += (0.5 / D) * jnp.sum(err * err)
        dy = err * (1.0 / D)
        dg_ref[...] += jnp.sum(dy * xhat, axis=0, keepdims=True)
        dxh = dy * g
        dx = rstd * (dxh - xhat * jnp.mean(dxh * xhat, axis=-1, keepdims=True))
        dx_ref[...] = dx
        dxb_ref[...] = dx.astype(BF16)

    row = pl.BlockSpec((rows, D), lambda i: (i, 0))
    vec = pl.BlockSpec((1, D), lambda i: (0, 0))
    return pl.pallas_call(
        body, name="loss_head",
        out_shape=(jax.ShapeDtypeStruct((S, D), F32), jax.ShapeDtypeStruct((S, D), BF16),
                   jax.ShapeDtypeStruct((1, D), F32), jax.ShapeDtypeStruct((8, LANES), F32)),
        grid=(S // rows,), in_specs=[row, row, vec],
        out_specs=(row, row, vec, pl.BlockSpec((8, LANES), lambda i: (0, 0))),
        compiler_params=_params("arbitrary"),
    )(x2, target, fg)


def _rms_bwd(x, dh, dx2, g, rows=128):
    S, D = x.shape
    rows = min(rows, S)

    def body(x_ref, dh_ref, dx2_ref, g_ref, gx_ref, dg_ref):
        i = pl.program_id(0)

        @pl.when(i == 0)
        def _():
            dg_ref[...] = jnp.zeros_like(dg_ref)

        xv, dh_v = x_ref[...], dh_ref[...]
        rstd = lax.rsqrt(jnp.mean(xv * xv, axis=-1, keepdims=True) + NORM_EPS)
        xhat = xv * rstd
        dg_ref[...] += jnp.sum(dh_v * xhat, axis=0, keepdims=True)
        dxh = dh_v * g_ref[...]
        gx_ref[...] = dx2_ref[...] + rstd * (dxh - xhat * jnp.mean(dxh * xhat, axis=-1, keepdims=True))

    row = pl.BlockSpec((rows, D), lambda i: (i, 0))
    vec = pl.BlockSpec((1, D), lambda i: (0, 0))
    return pl.pallas_call(
        body, name="rms_bwd", out_shape=(jax.ShapeDtypeStruct((S, D), F32), jax.ShapeDtypeStruct((1, D), F32)),
        grid=(S // rows,), in_specs=[row, row, row, vec], out_specs=(row, vec),
        compiler_params=_params("arbitrary"),
    )(x, dh, dx2, g)


def _adamw(w, g, m, v, name, rows=64):
    R, C = w.shape
    rows = min(rows, R)
    c1 = 1.0 - ADAM_B1 ** ADAM_STEP
    c2 = 1.0 - ADAM_B2 ** ADAM_STEP

    def body(w_ref, g_ref, m_ref, v_ref, go_ref, d_ref, mo_ref, vo_ref):
        gv = g_ref[...]
        mn = ADAM_B1 * m_ref[...] + (1.0 - ADAM_B1) * gv
        vn = ADAM_B2 * v_ref[...] + (1.0 - ADAM_B2) * (gv * gv)
        go_ref[...] = gv
        mo_ref[...] = mn
        vo_ref[...] = vn
        d_ref[...] = -ADAM_LR * ((mn / c1) / (jnp.sqrt(vn / c2) + ADAM_EPS) + ADAM_WD * w_ref[...])

    spec = pl.BlockSpec((rows, C), lambda i: (i, 0))
    shape = jax.ShapeDtypeStruct((R, C), F32)
    return pl.pallas_call(body, name=name, out_shape=(shape,) * 4, grid=(R // rows,), in_specs=[spec] * 4,
                          out_specs=(spec,) * 4, compiler_params=_params("parallel"))(w, g, m, v)


def _place():
    x, y, c = lax.axis_index("x"), lax.axis_index("y"), lax.axis_index("c")
    others = [(1 - x, y), (x, 1 - y), (1 - x, 1 - y)]
    return x, y, c, others


def _chip_index(chip):
    return 2 * chip[0] + chip[1]


def _remote(src, dst, sems, k, to):
    send_sems, recv_sems = sems
    return pltpu.make_async_remote_copy(src_ref=src, dst_ref=dst, send_sem=send_sems.at[k], recv_sem=recv_sems.at[k],
                                        device_id=to, device_id_type=MESH)


def _proj_gather(order, h, w_full, ups, tm=1024, tk=512):
    S, Dm = h.shape
    tm, tk = min(tm, S), min(tk, Dm)
    tn = SLAB // 2
    ni, nj, nk = S // tm, 2 * N_CHIPS, Dm // tk
    total = nj * ni * nk
    half_in, half_up = Dm // 2, UP_ROWS // 2
    n_w = 1 + len(ups)

    def body(order_ref, h_ref, _wi, _wa, _wg, _wo, proj_ref, fi_ref, fa_ref, fg_ref, fo_ref,
             acc, bbuf, bsem, send_sems, recv_sems):
        j, i, k = pl.program_id(0), pl.program_id(1), pl.program_id(2)
        t = (j * ni + i) * nk + k
        x, y, c, others = _place()
        me_chip = 2 * x + y
        sems = (send_sems, recv_sems)
        fulls = [fa_ref, fg_ref, fo_ref]

        def piece(w, chip, half):
            if w == 0:
                return fi_ref.at[pl.ds(half * half_in, half_in), pl.ds(chip * SLAB, SLAB)]
            return fulls[w - 1].at[pl.ds(chip * UP_ROWS + half * half_up, half_up), :]

        def ici_send(w, r):
            mine = piece(w, me_chip, c)
            return _remote(mine, mine, sems, 6 * w + r, (*others[r], c))

        def ici_recv(w, r):
            got = piece(w, _chip_index(others[r]), c)
            return _remote(got, got, sems, 6 * w + r, (x, y, c))

        def pass_on(w, r):
            got = piece(w, _chip_index(others[r]), c)
            return _remote(got, got, sems, 6 * w + 3 + r, (x, y, 1 - c))

        def passed_recv(w, r):
            got = piece(w, _chip_index(others[r]), 1 - c)
            return _remote(got, got, sems, 6 * w + 3 + r, (x, y, c))

        def b_copy(jj, kk, slot):
            col = 2 * order_ref[jj // 2] + jj % 2
            return pltpu.make_async_copy(fi_ref.at[pl.ds(kk * tk, tk), pl.ds(col * tn, tn)], bbuf.at[slot], bsem.at[slot])

        @pl.when(t == 0)
        def _():
            for w in range(n_w):
                for r in range(3):
                    ici_send(w, r).start()
            b_copy(0, 0, 0).start()

        nxt = t + 1
        kn, i_n, jn = nxt % nk, (nxt // nk) % ni, nxt // (nk * ni)

        @pl.when(nxt < total)
        def _():
            for r in range(3):
                @pl.when((jn == 2 * (r + 1)) & (i_n == 0) & (kn == 0))
                def _():
                    ici_recv(0, r).wait_recv()
                    pass_on(0, r).start()
                    passed_recv(0, r).wait_recv()
            b_copy(jn, kn, nxt % 2).start()

        b_copy(j, k, t % 2).wait()
        part = jnp.dot(h_ref[...], bbuf[t % 2], preferred_element_type=F32)

        @pl.when(k == 0)
        def _():
            acc[...] = part

        @pl.when(k > 0)
        def _():
            acc[...] += part

        @pl.when(k == nk - 1)
        def _():
            proj_ref[...] = acc[...].astype(BF16)

        @pl.when(t == total - 1)
        def _():
            for w in range(1, n_w):
                for r in range(3):
                    ici_recv(w, r).wait_recv()
                    pass_on(w, r).start()
            for w in range(1, n_w):
                for r in range(3):
                    passed_recv(w, r).wait_recv()
            for w in range(n_w):
                for r in range(3):
                    ici_send(w, r).wait_send()
                    pass_on(w, r).wait_send()

    col_of = lambda j, order: 2 * order[j // 2] + j % 2
    grid_spec = pltpu.PrefetchScalarGridSpec(
        num_scalar_prefetch=1, grid=(nj, ni, nk),
        in_specs=[pl.BlockSpec((tm, tk), lambda j, i, k, order: (i, k))] + [HBM_SPEC] * n_w,
        out_specs=(pl.BlockSpec((tm, tn), lambda j, i, k, order: (i, col_of(j, order))),) + (HBM_SPEC,) * n_w,
        scratch_shapes=[pltpu.VMEM((tm, tn), F32), pltpu.VMEM((2, tk, tn), BF16), pltpu.SemaphoreType.DMA((2,)),
                        pltpu.SemaphoreType.DMA((6 * n_w,)), pltpu.SemaphoreType.DMA((6 * n_w,))])
    return pl.pallas_call(
        body, name="proj_gather",
        out_shape=(jax.ShapeDtypeStruct((S, PROJ_WIDTH), BF16), jax.ShapeDtypeStruct(w_full.shape, BF16))
        + tuple(jax.ShapeDtypeStruct(u.shape, BF16) for u in ups),
        grid_spec=grid_spec, input_output_aliases={2 + w: 1 + w for w in range(n_w)},
        compiler_params=_params("arbitrary", "arbitrary", "arbitrary"),
    )(order, h, w_full, *ups)


def _pair_exchange(parts, name):
    n = len(parts)

    def half_of(ref, which):
        if len(ref.shape) == 2:
            rows = ref.shape[0] // 2
            return ref.at[pl.ds(which * rows, rows), :]
        return ref.at[:, pl.ds(which, 1)]

    def body(*refs):
        p_refs, r_refs = refs[0:n], refs[n:2 * n]
        sems = refs[2 * n:]
        x, y, c, _ = _place()
        copies = [_remote(half_of(p_refs[t], 1 - c), r_refs[t], sems, t, (x, y, 1 - c)) for t in range(n)]
        for cp in copies:
            cp.start()
        for cp in copies:
            cp.wait()

    def out_shape(p):
        if p.ndim == 2:
            return jax.ShapeDtypeStruct((p.shape[0] // 2, p.shape[1]), BF16)
        return jax.ShapeDtypeStruct((p.shape[0], 1) + p.shape[2:], BF16)

    return pl.pallas_call(
        body, name=name, out_shape=tuple(out_shape(p) for p in parts),
        in_specs=[HBM_SPEC] * n, out_specs=(HBM_SPEC,) * n,
        scratch_shapes=[pltpu.SemaphoreType.DMA((n,)), pltpu.SemaphoreType.DMA((n,))],
    )(*parts)


def _pair_add_in(idx, p_in, r_in, rows=256, cols=COL_BLK):
    Dm, P = p_in.shape
    half = Dm // 2
    rows = min(rows, half)
    nrb = half // rows

    def body(idx_ref, p_ref, r_ref, o_ref):
        o_ref[...] = (p_ref[...].astype(F32) + r_ref[...].astype(F32)).astype(BF16)

    grid_spec = pltpu.PrefetchScalarGridSpec(
        num_scalar_prefetch=1, grid=(nrb, P // cols),
        in_specs=[pl.BlockSpec((rows, cols), lambda i, j, idx: (idx[0] * nrb + i, j)),
                  pl.BlockSpec((rows, cols), lambda i, j, idx: (i, j))],
        out_specs=pl.BlockSpec((rows, cols), lambda i, j, idx: (i, j)))
    return pl.pallas_call(body, name="pair_add_in", out_shape=jax.ShapeDtypeStruct((half, P), BF16), grid_spec=grid_spec,
                          compiler_params=_params("parallel", "parallel"))(idx, p_in, r_in)


def _pair_add_up(idx, p_up, r_up, name):
    _, _, R, C = p_up.shape

    def body(idx_ref, p_ref, r_ref, o_ref):
        o_ref[...] = (p_ref[...].astype(F32) + r_ref[...].astype(F32)).astype(BF16)

    grid_spec = pltpu.PrefetchScalarGridSpec(
        num_scalar_prefetch=1, grid=(N_CHIPS,),
        in_specs=[pl.BlockSpec((1, 1, R, C), lambda j, idx: (j, idx[0], 0, 0)),
                  pl.BlockSpec((1, 1, R, C), lambda j, idx: (j, 0, 0, 0))],
        out_specs=pl.BlockSpec((1, 1, R, C), lambda j, idx: (j, 0, 0, 0)))
    return pl.pallas_call(body, name=name, out_shape=jax.ShapeDtypeStruct((N_CHIPS, 1, R, C), BF16), grid_spec=grid_spec,
                          compiler_params=_params("parallel"))(idx, p_up, r_up)


def _slab_exchange(qs):
    n = len(qs)

    def out_shape(q):
        if q.ndim == 2:
            return jax.ShapeDtypeStruct((3, q.shape[0], SLAB), BF16)
        return jax.ShapeDtypeStruct((3,) + q.shape[1:], BF16)

    def build(q_refs, r_refs, sems):
        _, _, c, others = _place()
        copies = []
        for r, chip in enumerate(others):
            ci = _chip_index(chip)
            for t in range(n):
                src = q_refs[t].at[:, pl.ds(ci * SLAB, SLAB)] if len(q_refs[t].shape) == 2 else q_refs[t].at[ci]
                copies.append(_remote(src, r_refs[t].at[r], sems, 3 * t + r, (*chip, c)))
        return copies

    return (tuple(qs), tuple(out_shape(q) for q in qs), 3 * n, build)


def _slab_add_in(idx, q_in, r2_in, rows=128):
    half = q_in.shape[0]
    rows = min(rows, half)
    nrb = half // rows

    def body(idx_ref, q_ref, r_ref, o_ref):
        o_ref[...] = ((q_ref[...].astype(F32) + r_ref[0].astype(F32)) + r_ref[1].astype(F32)) + r_ref[2].astype(F32)

    grid_spec = pltpu.PrefetchScalarGridSpec(
        num_scalar_prefetch=1, grid=(nrb,),
        in_specs=[pl.BlockSpec((rows, SLAB), lambda i, idx: (i, idx[1])),
                  pl.BlockSpec((3, rows, SLAB), lambda i, idx: (0, i, 0))],
        out_specs=pl.BlockSpec((rows, SLAB), lambda i, idx: (idx[0] * nrb + i, 0)))
    return pl.pallas_call(body, name="slab_add_in", out_shape=jax.ShapeDtypeStruct((2 * half, SLAB), F32),
                          grid_spec=grid_spec, compiler_params=_params("parallel"))(idx, q_in, r2_in)


def _slab_add_up(idx, q_up, r2_up, name, rows=128):
    _, _, R, C = q_up.shape
    rows = min(rows, R)
    nrb = R // rows

    def body(idx_ref, q_ref, r_ref, o_ref):
        o_ref[...] = ((q_ref[0, 0].astype(F32) + r_ref[0, 0].astype(F32)) + r_ref[1, 0].astype(F32)) + r_ref[2, 0].astype(F32)

    grid_spec = pltpu.PrefetchScalarGridSpec(
        num_scalar_prefetch=1, grid=(nrb,),
        in_specs=[pl.BlockSpec((1, 1, rows, C), lambda i, idx: (idx[1], 0, i, 0)),
                  pl.BlockSpec((3, 1, rows, C), lambda i, idx: (0, 0, i, 0))],
        out_specs=pl.BlockSpec((rows, C), lambda i, idx: (idx[0] * nrb + i, 0)))
    return pl.pallas_call(body, name=name, out_shape=jax.ShapeDtypeStruct((2 * R, C), F32), grid_spec=grid_spec,
                          compiler_params=_params("parallel"))(idx, q_up, r2_up)


def _pair_share(gs):
    n = len(gs)

    def body(*refs):
        g_refs = refs[n:2 * n]
        sems = refs[2 * n:]
        x, y, c, _ = _place()
        copies = []
        for t, g in enumerate(g_refs):
            rows = g.shape[0] // 2
            mine = g.at[pl.ds(c * rows, rows), :]
            copies.append(_remote(mine, mine, sems, t, (x, y, 1 - c)))
        for cp in copies:
            cp.start()
        for cp in copies:
            cp.wait()

    return pl.pallas_call(
        body, name="pair_share", out_shape=tuple(jax.ShapeDtypeStruct(g.shape, F32) for g in gs),
        in_specs=[HBM_SPEC] * n, out_specs=(HBM_SPEC,) * n, input_output_aliases={t: t for t in range(n)},
        scratch_shapes=[pltpu.SemaphoreType.DMA((n,)), pltpu.SemaphoreType.DMA((n,))],
    )(*gs)


def _all_reduce_small(packed):
    R, C = packed.shape
    N_DEV = 2 * N_CHIPS

    def body(x_ref, out_ref, all_ref, send_sems, recv_sems, local_sem):
        x, y, c, others = _place()
        me, sib = (x, y, c), (x, y, 1 - c)
        sems = (send_sems, recv_sems)

        def rows(px, py, pc):
            return all_ref.at[4 * px + 2 * py + pc]

        mine = pltpu.make_async_copy(x_ref, rows(*me), local_sem)
        mine.start()
        first = [_remote(x_ref, rows(*me), sems, 0, sib)]
        first += [_remote(x_ref, rows(*me), sems, 1 + j, (*chip, c)) for j, chip in enumerate(others)]
        for cp in first:
            cp.start()
        passed = [_remote(rows(*chip, c), rows(*chip, c), sems, 4 + j, sib) for j, chip in enumerate(others)]
        for j, chip in enumerate(others):
            _remote(rows(*chip, c), rows(*chip, c), sems, 1 + j, me).wait_recv()
            passed[j].start()
        _remote(rows(*sib), rows(*sib), sems, 0, me).wait_recv()
        for j, chip in enumerate(others):
            _remote(rows(*chip, 1 - c), rows(*chip, 1 - c), sems, 4 + j, me).wait_recv()
        for cp in first + passed:
            cp.wait_send()
        mine.wait()
        total = all_ref[0]
        for d in range(1, N_DEV):
            total = total + all_ref[d]
        out_ref[...] = total

    vmem = pl.BlockSpec(memory_space=pltpu.VMEM)
    return pl.pallas_call(
        body, name="all_reduce_small", out_shape=jax.ShapeDtypeStruct((R, C), F32),
        in_specs=[vmem], out_specs=vmem,
        scratch_shapes=[pltpu.VMEM((N_DEV, R, C), F32), pltpu.SemaphoreType.DMA((7,)), pltpu.SemaphoreType.DMA((7,)),
                        pltpu.SemaphoreType.DMA],
        compiler_params=pltpu.CompilerParams(vmem_limit_bytes=VMEM_LIMIT),
    )(packed)


SMALL_NAMES = ("w_spatial", "b_spatial", "norm_g", "gmlp_ln_g", "gmlp_ln_b", "final_norm_g", "attn_sink")


def _pack_small(parts, extra=None):
    blocks = []
    for n in SMALL_NAMES:
        flat = parts[n].reshape(-1).astype(F32)
        rows = -(-flat.shape[0] // (8 * LANES)) * 8
        flat = jnp.pad(flat, (0, rows * LANES - flat.shape[0]))
        blocks.append(flat.reshape(rows, LANES))
    blocks.append(jnp.zeros((8, LANES), F32) if extra is None else extra)
    return jnp.concatenate(blocks, axis=0)


def _unpack_small(packed, shapes):
    out, r = {}, 0
    for n in SMALL_NAMES:
        size = 1
        for s in shapes[n]:
            size *= s
        rows = -(-size // (8 * LANES)) * 8
        out[n] = packed[r:r + rows].reshape(-1)[:size].reshape(shapes[n])
        r += rows
    return out, packed[r:r + 8]


def kernel(x, positions, norm_g, w_in, attn_sink, gmlp_ln_g, gmlp_ln_b, w_spatial, b_spatial, w_up_attn, w_up_gmlp, w_out, final_norm_g, loss_target, m_norm_g, m_w_in, m_attn_sink, m_gmlp_ln_g, m_gmlp_ln_b, m_w_spatial, m_b_spatial, m_w_up_attn, m_w_up_gmlp, m_w_out, m_final_norm_g, v_norm_g, v_w_in, v_attn_sink, v_gmlp_ln_g, v_gmlp_ln_b, v_w_spatial, v_b_spatial, v_w_up_attn, v_w_up_gmlp, v_w_out, v_final_norm_g):
    xs, tgt = x[0], loss_target[0]
    mx, my, mc = lax.axis_index("x"), lax.axis_index("y"), lax.axis_index("c")
    idx = jnp.stack([mc, 2 * mx + my]).astype(jnp.int32)
    arrival = jnp.stack([2 * mx + my, 2 * (1 - mx) + my, 2 * mx + (1 - my), 2 * (1 - mx) + (1 - my)]).astype(jnp.int32)

    square = (D_MODEL, D_MODEL)
    w_full = _cast_into(idx, w_in[0], (D_MODEL, PROJ_WIDTH), "cast_w_in")
    ups = [_cast_into(idx, w_up_attn[0], square, "cast_w_up_attn"), _cast_into(idx, w_up_gmlp[0], square, "cast_w_up_gmlp"),
           _cast_into(idx, w_out[0], square, "cast_w_out")]

    tables = _rope_tables(positions[0])
    sink = attn_sink[0]
    ws, bs_t = w_spatial[0], b_spatial[0].T
    h = _rms_fwd(xs, norm_g)
    proj, w_full, wua, wug, wo = _proj_gather(arrival, h, w_full, ups)
    attn, a_in = _attn_fwd(proj, tables, sink)
    y_a = _matmul(a_in, wua, mode="nn", out_dtype=F32, name="up_attn")
    b_in = _gmlp_fwd(proj, ws, bs_t, gmlp_ln_g, gmlp_ln_b)
    y_b = _matmul(b_in, wug, mode="nn", out_dtype=F32, name="up_gmlp")
    merged = _merge_fwd(proj, y_a, y_b)
    x2 = _matmul(merged, wo, mode="nn", out_dtype=F32, name="out_proj", residual=xs)
    dx2, dx2_b, d_fg, loss_part = _loss_head(x2, tgt, final_norm_g.reshape(1, D_MODEL))

    dmerged = _matmul(dx2_b, wo, mode="nt", out_dtype=F32, name="d_merged")
    dy_a, dy_b, d_gates = _merge_bwd(proj, y_a, y_b, dmerged)

    p_ups = [_matmul(a_in, dy_a, mode="tn", out_dtype=BF16, name="dw_up_attn"),
             _matmul(b_in, dy_b, mode="tn", out_dtype=BF16, name="dw_up_gmlp"),
             _matmul(merged, dx2_b, mode="tn", out_dtype=BF16, name="dw_out")]
    p_ups = [p.reshape(N_CHIPS, 2, UP_ROWS // 2, D_MODEL) for p in p_ups]
    r1_ups = _pair_exchange(p_ups, "pair_exchange_ups")
    q_ups = [_pair_add_up(idx, p, r, "pair_add_up%d" % t) for t, (p, r) in enumerate(zip(p_ups, r1_ups))]

    da_in = _matmul(dy_a, wua, mode="nt", out_dtype=F32, name="d_a_in")
    db_in = _matmul(dy_b, wug, mode="nt", out_dtype=F32, name="d_b_in")
    dq, dk, dv, dga, d_sink = _attn_bwd(proj, tables, sink, attn, da_in)
    d_gmlp, d_ws, d_bs, d_lg, d_lb = _gmlp_bwd(proj, ws, bs_t, gmlp_ln_g, gmlp_ln_b, db_in)
    dproj = jnp.concatenate([dq, dk, dv, dga, d_gmlp, d_gates], axis=1)

    p_in, *r2_ups = _matmul(h, dproj, mode="tn", out_dtype=BF16, name="dw_in", comm=_slab_exchange(q_ups))
    r1_in, = _pair_exchange([p_in], "pair_exchange_in")
    q_in = _pair_add_in(idx, p_in, r1_in)
    dh, r2_in = _matmul(dproj, w_full, mode="nt", out_dtype=F32, name="d_h", comm=_slab_exchange([q_in]))
    grad_x, d_ng = _rms_bwd(xs, dh, dx2, norm_g)

    g_big = _pair_share([_slab_add_in(idx, q_in, r2_in)]
                        + [_slab_add_up(idx, q, r, "slab_add_up%d" % t) for t, (q, r) in enumerate(zip(q_ups, r2_ups))])

    small_shapes = {"w_spatial": w_spatial.shape, "b_spatial": b_spatial.shape, "norm_g": norm_g.shape,
                    "gmlp_ln_g": gmlp_ln_g.shape, "gmlp_ln_b": gmlp_ln_b.shape, "final_norm_g": final_norm_g.shape,
                    "attn_sink": attn_sink.shape}
    d_small = {"w_spatial": d_ws, "b_spatial": d_bs, "norm_g": d_ng, "gmlp_ln_g": d_lg, "gmlp_ln_b": d_lb,
               "final_norm_g": d_fg, "attn_sink": d_sink[:, 0, :HEADS_PER_STEP]}
    g_small = _all_reduce_small(_pack_small(d_small, loss_part))

    w_small = _pack_small(dict(w_spatial=w_spatial, b_spatial=b_spatial, norm_g=norm_g, gmlp_ln_g=gmlp_ln_g,
                               gmlp_ln_b=gmlp_ln_b, final_norm_g=final_norm_g, attn_sink=attn_sink))
    m_small = _pack_small(dict(w_spatial=m_w_spatial, b_spatial=m_b_spatial, norm_g=m_norm_g, gmlp_ln_g=m_gmlp_ln_g,
                               gmlp_ln_b=m_gmlp_ln_b, final_norm_g=m_final_norm_g, attn_sink=m_attn_sink))
    v_small = _pack_small(dict(w_spatial=v_w_spatial, b_spatial=v_b_spatial, norm_g=v_norm_g, gmlp_ln_g=v_gmlp_ln_g,
                               gmlp_ln_b=v_gmlp_ln_b, final_norm_g=v_final_norm_g, attn_sink=v_attn_sink))
    small_out = _adamw(w_small, g_small, m_small, v_small, "adamw_small", rows=w_small.shape[0])
    big = {
        "w_in": _adamw(w_in[0], g_big[0], m_w_in[0], v_w_in[0], "adamw_w_in"),
        "w_up_attn": _adamw(w_up_attn[0], g_big[1], m_w_up_attn[0], v_w_up_attn[0], "adamw_w_up_attn"),
        "w_up_gmlp": _adamw(w_up_gmlp[0], g_big[2], m_w_up_gmlp[0], v_w_up_gmlp[0], "adamw_w_up_gmlp"),
        "w_out": _adamw(w_out[0], g_big[3], m_w_out[0], v_w_out[0], "adamw_w_out"),
    }

    order = ("norm_g", "w_in", "attn_sink", "gmlp_ln_g", "gmlp_ln_b", "w_spatial", "b_spatial", "w_up_attn", "w_up_gmlp",
             "w_out", "final_norm_g")
    outs = []
    loss = None
    for kind in range(4):
        small, extra = _unpack_small(small_out[kind], small_shapes)
        if kind == 0:
            loss = extra[0, 0]
        for n in order:
            outs.append(big[n][kind][None] if n in big else small[n])
    return (loss, grad_x[None], *outs)
```

```python
import functools

import jax
import jax.numpy as jnp
from jax import lax
from jax.experimental import pallas as pl
from jax.experimental.pallas import tpu as pltpu

F32 = jnp.float32
BF16 = jnp.bfloat16

D_MODEL = 4096
N_Q_HEADS = 64
N_KV_HEADS = 8
HEAD_DIM = 64
Q_PER_KV = N_Q_HEADS // N_KV_HEADS
ATTN_WIDTH = N_Q_HEADS * HEAD_DIM
KV_WIDTH = N_KV_HEADS * HEAD_DIM
WINDOW = 128
BLOCK = 128
ROPE_THETA = 500000.0
ROPE_DIM = HEAD_DIM // 4
ROPE_HALF = ROPE_DIM // 2
GMLP_WIDTH = D_MODEL
GMLP_GROUPS = 8
GMLP_GROUP_DIM = GMLP_WIDTH // GMLP_GROUPS
GMLP_CHUNK = 128
NORM_EPS = 1e-5
LN_EPS = 1e-5

PROJ_SIZES = (ATTN_WIDTH, KV_WIDTH, KV_WIDTH, ATTN_WIDTH, GMLP_WIDTH, GMLP_WIDTH, GMLP_WIDTH, D_MODEL, D_MODEL)
PROJ_WIDTH = sum(PROJ_SIZES)
OFF_Q, OFF_K, OFF_V, OFF_GA, OFF_U, OFF_VG, OFF_GB, OFF_MA, OFF_MB = (
    sum(PROJ_SIZES[:i]) for i in range(len(PROJ_SIZES)))

ADAM_LR = 0.001
ADAM_B1 = 0.9
ADAM_B2 = 0.999
ADAM_EPS = 1e-08
ADAM_WD = 0.01
ADAM_STEP = 10

N_CHIPS = 4
SLAB = PROJ_WIDTH // N_CHIPS
UP_ROWS = D_MODEL // N_CHIPS
LANES = 128
COL_BLK = 1024
HEADS_PER_STEP = 2 * Q_PER_KV
VMEM_LIMIT = 56 * 1024 * 1024

MESH = pl.DeviceIdType.MESH
HBM_SPEC = pl.BlockSpec(memory_space=pltpu.HBM)
SMEM_SPEC = pl.BlockSpec(memory_space=pltpu.SMEM)


def _params(*sem):
    return pltpu.CompilerParams(dimension_semantics=sem, vmem_limit_bytes=VMEM_LIMIT)


def _sigmoid(x):
    return jax.nn.sigmoid(x)


def _gelu(x):
    return jax.nn.gelu(x, approximate=True)


def _gelu_grad(x):
    c = 0.7978845608028654
    inner = c * (x + 0.044715 * x * x * x)
    t = jnp.tanh(inner)
    return 0.5 * (1.0 + t) + 0.5 * x * (1.0 - t * t) * c * (1.0 + 3 * 0.044715 * x * x)


def _matmul(a, b, *, mode, out_dtype, name, residual=None, comm=None, tm=1024, tn=1024, tk=4096):
    if mode == "nn":
        (M, K), N = a.shape, b.shape[1]
    elif mode == "nt":
        (M, K), N = a.shape, b.shape[0]
    else:
        (K, M), N = a.shape, b.shape[1]
    tm, tn, tk = min(tm, M), min(tn, N), min(tk, K)
    assert M % tm == 0 and N % tn == 0 and K % tk == 0
    ni, nj, nk = M // tm, N // tn, K // tk
    if mode == "nn":
        a_spec = pl.BlockSpec((tm, tk), lambda i, j, k: (i, k))
        b_spec = pl.BlockSpec((tk, tn), lambda i, j, k: (k, j))
        dims = (((1,), (0,)), ((), ()))
    elif mode == "nt":
        a_spec = pl.BlockSpec((tm, tk), lambda i, j, k: (i, k))
        b_spec = pl.BlockSpec((tn, tk), lambda i, j, k: (j, k))
        dims = (((1,), (1,)), ((), ()))
    else:
        a_spec = pl.BlockSpec((tk, tm), lambda i, j, k: (k, i))
        b_spec = pl.BlockSpec((tk, tn), lambda i, j, k: (k, j))
        dims = (((0,), (0,)), ((), ()))
    o_spec = pl.BlockSpec((tm, tn), lambda i, j, k: (i, j))
    n_res = 1 if residual is not None else 0
    c_arrays, c_shapes, c_sems, c_build, c_alias = _comm_fields(comm)
    n_cin, n_cout = len(c_arrays), len(c_shapes)

    def body(*refs):
        a_ref, b_ref = refs[0], refs[1]
        r_ref = refs[2] if n_res else None
        cin = refs[2 + n_res:2 + n_res + n_cin]
        o_ref = refs[2 + n_res + n_cin]
        cout = refs[3 + n_res + n_cin:3 + n_res + n_cin + n_cout]
        n_acc = 1 if nk > 1 else 0
        acc = refs[3 + n_res + n_cin + n_cout] if n_acc else None
        sems = refs[3 + n_acc + n_res + n_cin + n_cout:]
        i, j, k = pl.program_id(0), pl.program_id(1), pl.program_id(2)
        if comm is not None:
            _start_when((i == 0) & (j == 0) & (k == 0), c_build(cin, cout, sems))

        def product():
            return lax.dot_general(a_ref[...], b_ref[...], dims, preferred_element_type=F32)

        def finish(r):
            if n_res:
                r = r + r_ref[...]
            o_ref[...] = r.astype(out_dtype)

        if nk == 1:
            finish(product())
        else:
            @pl.when(k == 0)
            def _():
                acc[...] = product()

            @pl.when((k > 0) & (k < nk - 1))
            def _():
                acc[...] += product()

            @pl.when(k == nk - 1)
            def _():
                finish(acc[...] + product())

        if comm is not None:
            _wait_when((i == ni - 1) & (j == nj - 1) & (k == nk - 1), c_build(cin, cout, sems))

    args = (a, b) + ((residual,) if n_res else ()) + tuple(c_arrays)
    in_specs = [a_spec, b_spec] + ([o_spec] if n_res else []) + [HBM_SPEC] * n_cin
    scratch = ([pltpu.VMEM((tm, tn), F32)] if nk > 1 else []) + _comm_scratch(c_sems)
    sem = ("arbitrary",) * 3 if comm is not None else ("parallel", "parallel", "arbitrary")
    aliases = {2 + n_res + t: 1 + t for t in range(n_cin)} if c_alias else {}
    out = pl.pallas_call(
        body, name=name, out_shape=(jax.ShapeDtypeStruct((M, N), out_dtype),) + tuple(c_shapes),
        grid=(ni, nj, nk), in_specs=in_specs, out_specs=(o_spec,) + (HBM_SPEC,) * n_cout,
        scratch_shapes=scratch, input_output_aliases=aliases, compiler_params=_params(*sem),
    )(*args)
    return out if comm is not None else out[0]


def _cast_into(idx, w, full_shape, name, rows=64):
    R, C = w.shape
    rows = min(rows, R)
    nrb = R // rows
    by_cols = C != full_shape[1]

    def body(idx_ref, w_ref, o_ref):
        o_ref[...] = w_ref[...].astype(BF16)

    out_map = (lambda i, idx: (i, idx[1])) if by_cols else (lambda i, idx: (idx[1] * nrb + i, 0))
    grid_spec = pltpu.PrefetchScalarGridSpec(
        num_scalar_prefetch=1, grid=(nrb,), in_specs=[pl.BlockSpec((rows, C), lambda i, idx: (i, 0))],
        out_specs=pl.BlockSpec((rows, C), out_map))
    return pl.pallas_call(body, name=name, out_shape=jax.ShapeDtypeStruct(full_shape, BF16), grid_spec=grid_spec,
                          compiler_params=_params("parallel"))(idx, w)


def _rms_fwd(x, g, rows=256):
    S, D = x.shape
    rows = min(rows, S)

    def body(x_ref, g_ref, h_ref):
        xv = x_ref[...]
        ms = jnp.mean(xv * xv, axis=-1, keepdims=True)
        h_ref[...] = (xv * lax.rsqrt(ms + NORM_EPS) * g_ref[...]).astype(BF16)

    spec = pl.BlockSpec((rows, D), lambda i: (i, 0))
    return pl.pallas_call(body, name="rms_fwd", out_shape=jax.ShapeDtypeStruct((S, D), BF16), grid=(S // rows,),
                          in_specs=[spec, pl.BlockSpec((1, D), lambda i: (0, 0))], out_specs=spec,
                          compiler_params=_params("parallel"))(x, g)


def _rope_tables(positions):
    inv_freq = ROPE_THETA ** (-jnp.arange(ROPE_HALF, dtype=F32) * 2.0 / ROPE_DIM)
    ang = positions.astype(F32)[:, None] * inv_freq
    cos, sin = jnp.cos(ang), jnp.sin(ang)
    S = positions.shape[0]
    rest = HEAD_DIM - ROPE_DIM
    zeros_h, zeros_r = jnp.zeros((S, ROPE_HALF), F32), jnp.zeros((S, rest), F32)
    c = jnp.concatenate([cos, cos, jnp.ones((S, rest), F32)], axis=1)
    s1 = jnp.concatenate([-sin, zeros_h, zeros_r], axis=1)
    s2 = jnp.concatenate([zeros_h, sin, zeros_r], axis=1)
    reps = LANES // HEAD_DIM
    return jnp.tile(c, (1, reps)), jnp.tile(s1, (1, reps)), jnp.tile(s2, (1, reps))


def _rope(t, c, s1, s2, sign):
    n = t.shape[1]
    reps = n // c.shape[1]
    if reps > 1:
        c, s1, s2 = jnp.tile(c, (1, reps)), jnp.tile(s1, (1, reps)), jnp.tile(s2, (1, reps))
    up = pltpu.roll(t, n - ROPE_HALF, 1)
    down = pltpu.roll(t, ROPE_HALF, 1)
    return t * c + sign * (up * s1 + down * s2)


def _attn_specs(nb):
    last = nb - 1

    def cur(i):
        return jnp.minimum(i, last)

    def prev(i):
        return jnp.maximum(jnp.minimum(i, last) - 1, 0)

    kq, kk, kv, kg = OFF_Q // COL_BLK, OFF_K // LANES, OFF_V // LANES, OFF_GA // COL_BLK
    wide = lambda off: pl.BlockSpec((BLOCK, COL_BLK), lambda p, i: (cur(i), off + p))
    kv_cur = lambda off: pl.BlockSpec((BLOCK, LANES), lambda p, i: (cur(i), off + p))
    kv_prev = lambda off: pl.BlockSpec((BLOCK, LANES), lambda p, i: (prev(i), off + p))
    tab_cur = pl.BlockSpec((BLOCK, LANES), lambda p, i: (cur(i), 0))
    tab_prev = pl.BlockSpec((BLOCK, LANES), lambda p, i: (prev(i), 0))
    proj_specs = [wide(kq), kv_cur(kk), kv_prev(kk), kv_cur(kv), kv_prev(kv), wide(kg)]
    table_specs = [tab_cur] * 3 + [tab_prev] * 3
    bias_spec = pl.BlockSpec((1, Q_PER_KV * BLOCK, 2 * BLOCK), lambda p, i: (jnp.minimum(i, 1), 0, 0))
    return proj_specs, table_specs + [bias_spec], cur, prev


def _band_bias():
    a = (jnp.arange(Q_PER_KV * BLOCK) % BLOCK)[:, None]
    si = jnp.arange(2 * BLOCK)[None, :]
    band = (si <= a + BLOCK) & (si > a + BLOCK - WINDOW)
    return jnp.where(jnp.stack([band & (si >= BLOCK), band]), 0.0, -1e30).astype(F32)


def _softmax_band(q2, k2, sink_col, bias):
    s = lax.dot_general(q2, k2, (((1,), (1,)), ((), ())), preferred_element_type=F32) + bias
    m = jnp.maximum(jnp.max(s, axis=1, keepdims=True), sink_col)
    e = jnp.exp(s - m)
    es = jnp.exp(sink_col - m)
    inv = 1.0 / (jnp.sum(e, axis=1, keepdims=True) + es)
    return e * inv, es * inv


def _sink_col(sink_ref, pair, kvh):
    return jnp.concatenate(
        [jnp.full((BLOCK, 1), sink_ref[(pair * 2 + kvh) * Q_PER_KV + g], F32) for g in range(Q_PER_KV)], axis=0)


def _stack_heads(ref, kvh):
    base = kvh * Q_PER_KV * HEAD_DIM
    return jnp.concatenate([ref[:, base + g * HEAD_DIM: base + (g + 1) * HEAD_DIM] for g in range(Q_PER_KV)], axis=0)


def _unstack_heads(ref, kvh, val):
    base = kvh * Q_PER_KV * HEAD_DIM
    for g in range(Q_PER_KV):
        ref[:, base + g * HEAD_DIM: base + (g + 1) * HEAD_DIM] = val[g * BLOCK:(g + 1) * BLOCK, :].astype(ref.dtype)


def _roped_band(k_ref, kp_ref, v_ref, vp_ref, tabs):
    c, s1, s2, cp, s1p, s2p = (t[...] for t in tabs)
    k_cur = _rope(k_ref[...].astype(F32), c, s1, s2, 1.0)
    k_prev = _rope(kp_ref[...].astype(F32), cp, s1p, s2p, 1.0)
    kband = jnp.concatenate([k_prev, k_cur], axis=0).astype(BF16)
    vband = jnp.concatenate([vp_ref[...], v_ref[...]], axis=0)
    return kband, vband


SCALE = HEAD_DIM ** -0.5


def _attn_fwd(proj, tables, bias, sink, comm=None):
    S = proj.shape[0]
    nb = S // BLOCK
    npairs = N_KV_HEADS // 2
    proj_specs, table_specs, _, _ = _attn_specs(nb)
    c_arrays, c_shapes, c_sems, c_build, c_alias = _comm_fields(comm)
    n_in = len(proj_specs) + len(table_specs) + 1

    def body(*refs):
        (q_ref, k_ref, kp_ref, v_ref, vp_ref, ga_ref, c_ref, s1_ref, s2_ref, cp_ref, s1p_ref, s2p_ref, bias_ref,
         sink_ref) = refs[:n_in]
        cin = refs[n_in:n_in + len(c_arrays)]
        attn_ref, ain_ref = refs[n_in + len(c_arrays):n_in + len(c_arrays) + 2]
        cout = refs[n_in + len(c_arrays) + 2:n_in + len(c_arrays) + 2 + len(c_shapes)]
        qr_scr = refs[n_in + len(c_arrays) + 2 + len(c_shapes)]
        sems = refs[n_in + len(c_arrays) + 3 + len(c_shapes):]
        pair, i = pl.program_id(0), pl.program_id(1)
        if comm is not None:
            _start_when((pair == 0) & (i == 0), c_build(cin, cout, sems))
        tabs = (c_ref, s1_ref, s2_ref, cp_ref, s1p_ref, s2p_ref)
        qr_scr[...] = _rope(q_ref[...].astype(F32), c_ref[...], s1_ref[...], s2_ref[...], 1.0) * SCALE
        kband, vband = _roped_band(k_ref, kp_ref, v_ref, vp_ref, tabs)
        for kvh in range(2):
            q2 = _stack_heads(qr_scr, kvh).astype(BF16)
            k2 = kband[:, kvh * HEAD_DIM:(kvh + 1) * HEAD_DIM]
            v2 = vband[:, kvh * HEAD_DIM:(kvh + 1) * HEAD_DIM]
            p, _ = _softmax_band(q2, k2, _sink_col(sink_ref, pair, kvh), bias_ref[0])
            o = jnp.dot(p.astype(BF16), v2, preferred_element_type=F32)
            _unstack_heads(attn_ref, kvh, o)
        ga = ga_ref[...].astype(F32)
        ain_ref[...] = (attn_ref[...] * (ga * _sigmoid(ga))).astype(BF16)
        if comm is not None:
            _wait_when((pair == npairs - 1) & (i == nb - 1), c_build(cin, cout, sems))

    out_spec = pl.BlockSpec((BLOCK, COL_BLK), lambda p, i: (i, p))
    aliases = {n_in + t: 2 + t for t in range(len(c_arrays))} if c_alias else {}
    out = pl.pallas_call(
        body, name="attn_fwd",
        out_shape=(jax.ShapeDtypeStruct((S, ATTN_WIDTH), F32), jax.ShapeDtypeStruct((S, ATTN_WIDTH), BF16)) + tuple(c_shapes),
        grid=(npairs, nb), in_specs=proj_specs + table_specs + [SMEM_SPEC] + [HBM_SPEC] * len(c_arrays),
        out_specs=(out_spec, out_spec) + (HBM_SPEC,) * len(c_shapes),
        scratch_shapes=[pltpu.VMEM((BLOCK, COL_BLK), F32)] + _comm_scratch(c_sems), input_output_aliases=aliases,
        compiler_params=_params("arbitrary", "arbitrary"),
    )(*([proj] * 6), *tables, *tables, bias, sink, *c_arrays)
    return out


def _attn_bwd(proj, tables, bias, sink, attn, da_in):
    S = proj.shape[0]
    nb = S // BLOCK
    proj_specs, table_specs, cur, prev = _attn_specs(nb)
    wide_cur = pl.BlockSpec((BLOCK, COL_BLK), lambda p, i: (cur(i), p))
    kv_out = pl.BlockSpec((BLOCK, LANES), lambda p, i: (jnp.maximum(i - 1, 0), p))

    def body(q_ref, k_ref, kp_ref, v_ref, vp_ref, ga_ref, c_ref, s1_ref, s2_ref, cp_ref, s1p_ref, s2p_ref, bias_ref,
             sink_ref, attn_ref, da_ref, dq_ref, dk_ref, dv_ref, dga_ref, dsink_ref,
             qr_scr, do_scr, dq_scr, dk_scr, dv_scr, carry_k, carry_v, sink_acc):
        pair, i = pl.program_id(0), pl.program_id(1)
        tabs = (c_ref, s1_ref, s2_ref, cp_ref, s1p_ref, s2p_ref)

        @pl.when(i < nb)
        def _():
            ga = ga_ref[...].astype(F32)
            sg = _sigmoid(ga)
            da = da_ref[...]
            dga_ref[...] = (da * attn_ref[...] * (sg * (1.0 + ga * (1.0 - sg)))).astype(BF16)
            do_scr[...] = da * (ga * sg)
            qr_scr[...] = _rope(q_ref[...].astype(F32), c_ref[...], s1_ref[...], s2_ref[...], 1.0) * SCALE
            kband, vband = _roped_band(k_ref, kp_ref, v_ref, vp_ref, tabs)
            for kvh in range(2):
                q2 = _stack_heads(qr_scr, kvh).astype(BF16)
                do2 = _stack_heads(do_scr, kvh)
                o2 = _stack_heads(attn_ref, kvh)
                k2 = kband[:, kvh * HEAD_DIM:(kvh + 1) * HEAD_DIM]
                v2 = vband[:, kvh * HEAD_DIM:(kvh + 1) * HEAD_DIM]
                p, p_sink = _softmax_band(q2, k2, _sink_col(sink_ref, pair, kvh), bias_ref[0])
                do2b = do2.astype(BF16)
                dp = lax.dot_general(do2b, v2, (((1,), (1,)), ((), ())), preferred_element_type=F32)
                delta = jnp.sum(do2 * o2, axis=1, keepdims=True)
                ds = (p * (dp - delta)).astype(BF16)
                dq2 = jnp.dot(ds, k2, preferred_element_type=F32)
                dk2 = lax.dot_general(ds, q2, (((0,), (0,)), ((), ())), preferred_element_type=F32)
                dv2 = lax.dot_general(p.astype(BF16), do2b, (((0,), (0,)), ((), ())), preferred_element_type=F32)
                _unstack_heads(dq_scr, kvh, dq2)
                dk_scr[:, kvh * HEAD_DIM:(kvh + 1) * HEAD_DIM] = dk2
                dv_scr[:, kvh * HEAD_DIM:(kvh + 1) * HEAD_DIM] = dv2
                contrib = -p_sink * delta

                @pl.when(i == 0)
                def _():
                    sink_acc[kvh] = contrib

                @pl.when(i > 0)
                def _():
                    sink_acc[kvh] += contrib

            dq_ref[...] = _rope(dq_scr[...] * SCALE, c_ref[...], s1_ref[...], s2_ref[...], -1.0).astype(BF16)
            dk_prev = _rope(dk_scr[0:BLOCK, :], cp_ref[...], s1p_ref[...], s2p_ref[...], -1.0)
            dk_cur = _rope(dk_scr[BLOCK:2 * BLOCK, :], c_ref[...], s1_ref[...], s2_ref[...], -1.0)
            dv_prev, dv_cur = dv_scr[0:BLOCK, :], dv_scr[BLOCK:2 * BLOCK, :]

            @pl.when(i > 0)
            def _():
                dk_ref[...] = (carry_k[...] + dk_prev).astype(BF16)
                dv_ref[...] = (carry_v[...] + dv_prev).astype(BF16)

            carry_k[...] = dk_cur
            carry_v[...] = dv_cur

        @pl.when(i == nb)
        def _():
            dk_ref[...] = carry_k[...].astype(BF16)
            dv_ref[...] = carry_v[...].astype(BF16)
            lane = lax.broadcasted_iota(jnp.int32, (8, LANES), 1)
            acc = jnp.zeros((8, LANES), F32)
            for kvh in range(2):
                for g in range(Q_PER_KV):
                    val = jnp.sum(sink_acc[kvh, g * BLOCK:(g + 1) * BLOCK, :], axis=0, keepdims=True)
                    acc = jnp.where(lane == kvh * Q_PER_KV + g, val, acc)
            dsink_ref[0] = acc

    return pl.pallas_call(
        body, name="attn_bwd",
        out_shape=(jax.ShapeDtypeStruct((S, ATTN_WIDTH), BF16), jax.ShapeDtypeStruct((S, KV_WIDTH), BF16),
                   jax.ShapeDtypeStruct((S, KV_WIDTH), BF16), jax.ShapeDtypeStruct((S, ATTN_WIDTH), BF16),
                   jax.ShapeDtypeStruct((N_KV_HEADS // 2, 8, LANES), F32)),
        grid=(N_KV_HEADS // 2, nb + 1),
        in_specs=proj_specs + table_specs + [SMEM_SPEC, wide_cur, wide_cur],
        out_specs=(wide_cur, kv_out, kv_out, wide_cur, pl.BlockSpec((1, 8, LANES), lambda p, i: (p, 0, 0))),
        scratch_shapes=[pltpu.VMEM((BLOCK, COL_BLK), F32), pltpu.VMEM((BLOCK, COL_BLK), F32),
                        pltpu.VMEM((BLOCK, COL_BLK), F32), pltpu.VMEM((2 * BLOCK, LANES), F32),
                        pltpu.VMEM((2 * BLOCK, LANES), F32), pltpu.VMEM((BLOCK, LANES), F32),
                        pltpu.VMEM((BLOCK, LANES), F32), pltpu.VMEM((2, Q_PER_KV * BLOCK, 1), F32)],
        compiler_params=_params("arbitrary", "arbitrary"),
    )(*([proj] * 6), *tables, *tables, bias, sink, attn, da_in)


N_PARTS = GMLP_WIDTH // COL_BLK
GROUPS_PER_PART = COL_BLK // GMLP_GROUP_DIM


def _part_specs(off):
    return [pl.BlockSpec((GMLP_CHUNK, COL_BLK), functools.partial(lambda j, i: (i, j), off // COL_BLK + k))
            for k in range(N_PARTS)]


def _group(refs, g):
    lo = (g % GROUPS_PER_PART) * GMLP_GROUP_DIM
    return refs[g // GROUPS_PER_PART][:, lo:lo + GMLP_GROUP_DIM].astype(F32)


def _gmlp_norm_stats(vg_refs, gv_scr):
    total = jnp.zeros((GMLP_CHUNK, 1), F32)
    for k in range(N_PARTS):
        gv = _gelu(vg_refs[k][...].astype(F32))
        gv_scr[:, k * COL_BLK:(k + 1) * COL_BLK] = gv
        total = total + jnp.sum(gv, axis=1, keepdims=True)
    mu = total / GMLP_WIDTH
    xc = gv_scr[...] - mu
    var = jnp.sum(xc * xc, axis=1, keepdims=True) / GMLP_WIDTH
    return mu, lax.rsqrt(var + LN_EPS)


def _tril_bf16(ws_ref, g):
    t = lax.broadcasted_iota(jnp.int32, (GMLP_CHUNK, GMLP_CHUNK), 0)
    s = lax.broadcasted_iota(jnp.int32, (GMLP_CHUNK, GMLP_CHUNK), 1)
    return jnp.where(s <= t, ws_ref[g], 0.0).astype(BF16), s <= t


def _gmlp_fwd(proj, ws, bs_t, lg, lb, comm=None):
    S = proj.shape[0]
    nb = S // GMLP_CHUNK
    c_arrays, c_shapes, c_sems, c_build, c_alias = _comm_fields(comm)
    n_cin, n_cout = len(c_arrays), len(c_shapes)

    def body(*refs):
        u_refs, vg_refs, gb_refs = refs[0:4], refs[4:8], refs[8:12]
        ws_ref, bst_ref, lg_ref, lb_ref = refs[12:16]
        cin, out_ref, cout = refs[16:16 + n_cin], refs[16 + n_cin], refs[17 + n_cin:17 + n_cin + n_cout]
        gv_scr, sems = refs[17 + n_cin + n_cout], refs[18 + n_cin + n_cout:]
        if comm is not None:
            _start_when(pl.program_id(0) == 0, c_build(cin, cout, sems))
        mu, rstd = _gmlp_norm_stats(vg_refs, gv_scr)
        for g in range(GMLP_GROUPS):
            cols = slice(g * GMLP_GROUP_DIM, (g + 1) * GMLP_GROUP_DIM)
            vn = (gv_scr[:, cols] - mu) * rstd * lg_ref[:, cols] + lb_ref[:, cols]
            w, _ = _tril_bf16(ws_ref, g)
            mixed = jnp.dot(w, vn.astype(BF16), preferred_element_type=F32) + bst_ref[:, g:g + 1]
            gb = _group(gb_refs, g)
            out_ref[:, cols] = (_gelu(_group(u_refs, g)) * mixed * (gb * _sigmoid(gb))).astype(BF16)
        if comm is not None:
            _wait_when(pl.program_id(0) == nb - 1, c_build(cin, cout, sems))

    full = lambda shape: pl.BlockSpec(shape, lambda i: tuple(0 for _ in shape))
    aliases = {16 + t: 1 + t for t in range(n_cin)} if c_alias else {}
    out = pl.pallas_call(
        body, name="gmlp_fwd", out_shape=(jax.ShapeDtypeStruct((S, GMLP_WIDTH), BF16),) + tuple(c_shapes), grid=(nb,),
        in_specs=_part_specs(OFF_U) + _part_specs(OFF_VG) + _part_specs(OFF_GB)
        + [full(ws.shape), full(bs_t.shape), full(lg.shape), full(lb.shape)] + [HBM_SPEC] * n_cin,
        out_specs=(pl.BlockSpec((GMLP_CHUNK, GMLP_WIDTH), lambda i: (i, 0)),) + (HBM_SPEC,) * n_cout,
        scratch_shapes=[pltpu.VMEM((GMLP_CHUNK, GMLP_WIDTH), F32)] + _comm_scratch(c_sems), input_output_aliases=aliases,
        compiler_params=_params("arbitrary" if comm is not None else "parallel"),
    )(*([proj] * 12), ws, bs_t, lg, lb, *c_arrays)
    return out if comm is not None else out[0]


def _gmlp_bwd(proj, ws, bs_t, lg, lb, db_in):
    S = proj.shape[0]
    nb = S // GMLP_CHUNK
    W = GMLP_WIDTH

    def body(*refs):
        u_refs, vg_refs, gb_refs = refs[0:4], refs[4:8], refs[8:12]
        ws_ref, bst_ref, lg_ref, lb_ref, db_ref = refs[12:17]
        out_ref, dws_ref, dbs_ref, dlg_ref, dlb_ref = refs[17:22]
        gv_scr, dvh_scr = refs[22:]
        i = pl.program_id(0)

        @pl.when(i == 0)
        def _():
            dws_ref[...] = jnp.zeros_like(dws_ref)
            dbs_ref[...] = jnp.zeros_like(dbs_ref)
            dlg_ref[...] = jnp.zeros_like(dlg_ref)
            dlb_ref[...] = jnp.zeros_like(dlb_ref)

        mu, rstd = _gmlp_norm_stats(vg_refs, gv_scr)
        sum_dvh = jnp.zeros((GMLP_CHUNK, 1), F32)
        sum_dvh_vh = jnp.zeros((GMLP_CHUNK, 1), F32)
        for g in range(GMLP_GROUPS):
            cols = slice(g * GMLP_GROUP_DIM, (g + 1) * GMLP_GROUP_DIM)
            vhat = (gv_scr[:, cols] - mu) * rstd
            vn = (vhat * lg_ref[:, cols] + lb_ref[:, cols]).astype(BF16)
            w, tril = _tril_bf16(ws_ref, g)
            mixed = jnp.dot(w, vn, preferred_element_type=F32) + bst_ref[:, g:g + 1]
            u, gb, db = _group(u_refs, g), _group(gb_refs, g), db_ref[:, cols]
            gu, sgb = _gelu(u), _sigmoid(gb)
            dsg = db * (gb * sgb)
            out_ref[:, 2 * W + g * GMLP_GROUP_DIM: 2 * W + (g + 1) * GMLP_GROUP_DIM] = (
                db * (gu * mixed) * (sgb * (1.0 + gb * (1.0 - sgb)))).astype(BF16)
            out_ref[:, cols] = (dsg * mixed * _gelu_grad(u)).astype(BF16)
            dmixed = dsg * gu
            dmixed_b = dmixed.astype(BF16)
            dvn = lax.dot_general(w, dmixed_b, (((0,), (0,)), ((), ())), preferred_element_type=F32)
            dw = lax.dot_general(dmixed_b, vn, (((1,), (1,)), ((), ())), preferred_element_type=F32)
            dws_ref[g] += jnp.where(tril, dw, 0.0)
            dbs_ref[g] += jnp.sum(dmixed, axis=1, keepdims=True)
            dlg_ref[:, cols] += jnp.sum(dvn * vhat, axis=0, keepdims=True)
            dlb_ref[:, cols] += jnp.sum(dvn, axis=0, keepdims=True)
            dvh = dvn * lg_ref[:, cols]
            dvh_scr[:, cols] = dvh
            sum_dvh = sum_dvh + jnp.sum(dvh, axis=1, keepdims=True)
            sum_dvh_vh = sum_dvh_vh + jnp.sum(dvh * vhat, axis=1, keepdims=True)
        m1, m2 = sum_dvh / W, sum_dvh_vh / W
        for k in range(N_PARTS):
            cols = slice(k * COL_BLK, (k + 1) * COL_BLK)
            vhat = (gv_scr[:, cols] - mu) * rstd
            dgv = rstd * (dvh_scr[:, cols] - m1 - vhat * m2)
            out_ref[:, W + k * COL_BLK: W + (k + 1) * COL_BLK] = (
                dgv * _gelu_grad(vg_refs[k][...].astype(F32))).astype(BF16)

    full = lambda shape: pl.BlockSpec(shape, lambda i: tuple(0 for _ in shape))
    row = pl.BlockSpec((GMLP_CHUNK, W), lambda i: (i, 0))
    return pl.pallas_call(
        body, name="gmlp_bwd",
        out_shape=(jax.ShapeDtypeStruct((S, 3 * W), BF16), jax.ShapeDtypeStruct(ws.shape, F32),
                   jax.ShapeDtypeStruct((GMLP_GROUPS, GMLP_CHUNK, 1), F32), jax.ShapeDtypeStruct((1, W), F32),
                   jax.ShapeDtypeStruct((1, W), F32)),
        grid=(nb,),
        in_specs=_part_specs(OFF_U) + _part_specs(OFF_VG) + _part_specs(OFF_GB)
        + [full(ws.shape), full(bs_t.shape), full(lg.shape), full(lb.shape), row],
        out_specs=(pl.BlockSpec((GMLP_CHUNK, 3 * W), lambda i: (i, 0)), full(ws.shape),
                   full((GMLP_GROUPS, GMLP_CHUNK, 1)), full((1, W)), full((1, W))),
        scratch_shapes=[pltpu.VMEM((GMLP_CHUNK, W), F32), pltpu.VMEM((GMLP_CHUNK, W), F32)],
        compiler_params=_params("arbitrary"),
    )(*([proj] * 12), ws, bs_t, lg, lb, db_in)


def _merge_fwd(proj, y_a, y_b, rows=256):
    S = proj.shape[0]
    rows = min(rows, S)

    def body(ma_ref, mb_ref, ya_ref, yb_ref, out_ref):
        out_ref[...] = (_sigmoid(ma_ref[...].astype(F32)) * ya_ref[...]
                        + _sigmoid(mb_ref[...].astype(F32)) * yb_ref[...]).astype(BF16)

    blk = lambda off: pl.BlockSpec((rows, COL_BLK), lambda i, j: (i, off // COL_BLK + j))
    return pl.pallas_call(
        body, name="merge_fwd", out_shape=jax.ShapeDtypeStruct((S, D_MODEL), BF16), grid=(S // rows, D_MODEL // COL_BLK),
        in_specs=[blk(OFF_MA), blk(OFF_MB), blk(0), blk(0)], out_specs=blk(0),
        compiler_params=_params("parallel", "parallel"),
    )(proj, proj, y_a, y_b)


def _merge_bwd(proj, y_a, y_b, dmerged, rows=128):
    S = proj.shape[0]
    rows = min(rows, S)
    nj = D_MODEL // COL_BLK

    def body(*refs):
        ma_refs, mb_refs = refs[0:nj], refs[nj:2 * nj]
        ya_ref, yb_ref, dm_ref, dya_ref, dyb_ref, dg_ref = refs[2 * nj:]
        for k in range(nj):
            cols = slice(k * COL_BLK, (k + 1) * COL_BLK)
            dm = dm_ref[:, cols]
            sa, sb = _sigmoid(ma_refs[k][...].astype(F32)), _sigmoid(mb_refs[k][...].astype(F32))
            dya_ref[:, cols] = (dm * sa).astype(BF16)
            dyb_ref[:, cols] = (dm * sb).astype(BF16)
            dg_ref[:, cols] = (dm * ya_ref[:, cols] * sa * (1.0 - sa)).astype(BF16)
            dg_ref[:, D_MODEL + k * COL_BLK: D_MODEL + (k + 1) * COL_BLK] = (
                dm * yb_ref[:, cols] * sb * (1.0 - sb)).astype(BF16)

    part = lambda off: [pl.BlockSpec((rows, COL_BLK), functools.partial(lambda j, i: (i, j), off // COL_BLK + k))
                        for k in range(nj)]
    row = pl.BlockSpec((rows, D_MODEL), lambda i: (i, 0))
    return pl.pallas_call(
        body, name="merge_bwd",
        out_shape=(jax.ShapeDtypeStruct((S, D_MODEL), BF16), jax.ShapeDtypeStruct((S, D_MODEL), BF16),
                   jax.ShapeDtypeStruct((S, 2 * D_MODEL), BF16)),
        grid=(S // rows,), in_specs=part(OFF_MA) + part(OFF_MB) + [row, row, row],
        out_specs=(row, row, pl.BlockSpec((rows, 2 * D_MODEL), lambda i: (i, 0))),
        compiler_params=_params("parallel"),
    )(*([proj] * (2 * nj)), y_a, y_b, dmerged)


def _loss_head(x2, target, fg, rows=128):
    S, D = x2.shape
    rows = min(rows, S)

    def body(x_ref, t_ref, g_ref, dx_ref, dxb_ref, dg_ref, loss_ref):
        i = pl.program_id(0)

        @pl.when(i == 0)
        def _():
            dg_ref[...] = jnp.zeros_like(dg_ref)
            loss_ref[...] = jnp.zeros_like(loss_ref)

        xv, g = x_ref[...], g_ref[...]
        rstd = lax.rsqrt(jnp.mean(xv * xv, axis=-1, keepdims=True) + NORM_EPS)
        xhat = xv * rstd
        err = xhat * g - t_ref[...]
        loss_ref[...] += (0.5 / D) * jnp.sum(err * err)
        dy = err * (1.0 / D)
        dg_ref[...] += jnp.sum(dy * xhat, axis=0, keepdims=True)
        dxh = dy * g
        dx = rstd * (dxh - xhat * jnp.mean(dxh * xhat, axis=-1, keepdims=True))
        dx_ref[...] = dx
        dxb_ref[...] = dx.astype(BF16)

    row = pl.BlockSpec((rows, D), lambda i: (i, 0))
    vec = pl.BlockSpec((1, D), lambda i: (0, 0))
    return pl.pallas_call(
        body, name="loss_head",
        out_shape=(jax.ShapeDtypeStruct((S, D), F32), jax.ShapeDtypeStruct((S, D), BF16),
                   jax.ShapeDtypeStruct((1, D), F32), jax.ShapeDtypeStruct((8, LANES), F32)),
        grid=(S // rows,), in_specs=[row, row, vec],
        out_specs=(row, row, vec, pl.BlockSpec((8, LANES), lambda i: (0, 0))),
        compiler_params=_params("arbitrary"),
    )(x2, target, fg)


def _rms_bwd(x, dh, dx2, g, rows=128):
    S, D = x.shape
    rows = min(rows, S)

    def body(x_ref, dh_ref, dx2_ref, g_ref, gx_ref, dg_ref):
        i = pl.program_id(0)

        @pl.when(i == 0)
        def _():
            dg_ref[...] = jnp.zeros_like(dg_ref)

        xv, dh_v = x_ref[...], dh_ref[...]
        rstd = lax.rsqrt(jnp.mean(xv * xv, axis=-1, keepdims=True) + NORM_EPS)
        xhat = xv * rstd
        dg_ref[...] += jnp.sum(dh_v * xhat, axis=0, keepdims=True)
        dxh = dh_v * g_ref[...]
        gx_ref[...] = dx2_ref[...] + rstd * (dxh - xhat * jnp.mean(dxh * xhat, axis=-1, keepdims=True))

    row = pl.BlockSpec((rows, D), lambda i: (i, 0))
    vec = pl.BlockSpec((1, D), lambda i: (0, 0))
    return pl.pallas_call(
        body, name="rms_bwd", out_shape=(jax.ShapeDtypeStruct((S, D), F32), jax.ShapeDtypeStruct((1, D), F32)),
        grid=(S // rows,), in_specs=[row, row, row, vec], out_specs=(row, vec),
        compiler_params=_params("arbitrary"),
    )(x, dh, dx2, g)


def _adamw(w, g, m, v, name, rows=64):
    R, C = w.shape
    rows = min(rows, R)
    c1 = 1.0 - ADAM_B1 ** ADAM_STEP
    c2 = 1.0 - ADAM_B2 ** ADAM_STEP

    def body(w_ref, g_ref, m_ref, v_ref, go_ref, d_ref, mo_ref, vo_ref):
        gv = g_ref[...]
        mn = ADAM_B1 * m_ref[...] + (1.0 - ADAM_B1) * gv
        vn = ADAM_B2 * v_ref[...] + (1.0 - ADAM_B2) * (gv * gv)
        go_ref[...] = gv
        mo_ref[...] = mn
        vo_ref[...] = vn
        d_ref[...] = -ADAM_LR * ((mn / c1) / (jnp.sqrt(vn / c2) + ADAM_EPS) + ADAM_WD * w_ref[...])

    spec = pl.BlockSpec((rows, C), lambda i: (i, 0))
    shape = jax.ShapeDtypeStruct((R, C), F32)
    return pl.pallas_call(body, name=name, out_shape=(shape,) * 4, grid=(R // rows,), in_specs=[spec] * 4,
                          out_specs=(spec,) * 4, compiler_params=_params("parallel"))(w, g, m, v)


def _place():
    x, y, c = lax.axis_index("x"), lax.axis_index("y"), lax.axis_index("c")
    others = [(1 - x, y), (x, 1 - y), (1 - x, 1 - y)]
    return x, y, c, others


def _chip_index(chip):
    return 2 * chip[0] + chip[1]


def _remote(src, dst, sems, k, to):
    send_sems, recv_sems = sems
    return pltpu.make_async_remote_copy(src_ref=src, dst_ref=dst, send_sem=send_sems.at[k], recv_sem=recv_sems.at[k],
                                        device_id=to, device_id_type=MESH)


def _comm_fields(comm):
    return comm if comm is not None else ((), (), 0, None, False)


def _comm_scratch(n_sems):
    return [pltpu.SemaphoreType.DMA((n_sems,)), pltpu.SemaphoreType.DMA((n_sems,))] if n_sems else []


def _start_when(cond, copies):
    @pl.when(cond)
    def _():
        for cp in copies:
            cp.start()


def _wait_when(cond, copies):
    @pl.when(cond)
    def _():
        for cp in copies:
            cp.wait()


def _proj_gather(order, h, w_full, tm=1024, tk=512):
    S, Dm = h.shape
    tm, tk = min(tm, S), min(tk, Dm)
    tn = SLAB // 2
    ni, nj, nk = S // tm, 2 * N_CHIPS, Dm // tk
    total = nj * ni * nk
    half_in = Dm // 2

    def body(order_ref, h_ref, _wi, proj_ref, fi_ref, acc, bbuf, bsem, send_sems, recv_sems):
        j, i, k = pl.program_id(0), pl.program_id(1), pl.program_id(2)
        t = (j * ni + i) * nk + k
        x, y, c, others = _place()
        me_chip = 2 * x + y
        sems = (send_sems, recv_sems)

        def piece(chip, half):
            return fi_ref.at[pl.ds(half * half_in, half_in), pl.ds(chip * SLAB, SLAB)]

        def ici_send(r):
            mine = piece(me_chip, c)
            return _remote(mine, mine, sems, r, (*others[r], c))

        def ici_recv(r):
            got = piece(_chip_index(others[r]), c)
            return _remote(got, got, sems, r, (x, y, c))

        def pass_on(r):
            got = piece(_chip_index(others[r]), c)
            return _remote(got, got, sems, 3 + r, (x, y, 1 - c))

        def passed_recv(r):
            got = piece(_chip_index(others[r]), 1 - c)
            return _remote(got, got, sems, 3 + r, (x, y, c))

        def b_copy(jj, kk, slot):
            col = 2 * order_ref[jj // 2] + jj % 2
            return pltpu.make_async_copy(fi_ref.at[pl.ds(kk * tk, tk), pl.ds(col * tn, tn)], bbuf.at[slot], bsem.at[slot])

        @pl.when(t == 0)
        def _():
            for r in range(3):
                ici_send(r).start()
            b_copy(0, 0, 0).start()

        nxt = t + 1
        kn, i_n, jn = nxt % nk, (nxt // nk) % ni, nxt // (nk * ni)

        @pl.when(nxt < total)
        def _():
            for r in range(3):
                @pl.when((jn == 2 * (r + 1)) & (i_n == 0) & (kn == 0))
                def _():
                    ici_recv(r).wait_recv()
                    pass_on(r).start()
                    passed_recv(r).wait_recv()
            b_copy(jn, kn, nxt % 2).start()

        b_copy(j, k, t % 2).wait()

        def product():
            return jnp.dot(h_ref[...], bbuf[t % 2], preferred_element_type=F32)

        @pl.when(k == 0)
        def _():
            acc[...] = product()

        @pl.when((k > 0) & (k < nk - 1))
        def _():
            acc[...] += product()

        @pl.when(k == nk - 1)
        def _():
            proj_ref[...] = (acc[...] + product()).astype(BF16)

        @pl.when(t == total - 1)
        def _():
            for r in range(3):
                ici_send(r).wait_send()
                pass_on(r).wait_send()

    col_of = lambda j, order: 2 * order[j // 2] + j % 2
    grid_spec = pltpu.PrefetchScalarGridSpec(
        num_scalar_prefetch=1, grid=(nj, ni, nk),
        in_specs=[pl.BlockSpec((tm, tk), lambda j, i, k, order: (i, k)), HBM_SPEC],
        out_specs=(pl.BlockSpec((tm, tn), lambda j, i, k, order: (i, col_of(j, order))), HBM_SPEC),
        scratch_shapes=[pltpu.VMEM((tm, tn), F32), pltpu.VMEM((2, tk, tn), BF16), pltpu.SemaphoreType.DMA((2,)),
                        pltpu.SemaphoreType.DMA((6,)), pltpu.SemaphoreType.DMA((6,))])
    return pl.pallas_call(
        body, name="proj_gather",
        out_shape=(jax.ShapeDtypeStruct((S, PROJ_WIDTH), BF16), jax.ShapeDtypeStruct(w_full.shape, BF16)),
        grid_spec=grid_spec, input_output_aliases={2: 1},
        compiler_params=_params("arbitrary", "arbitrary", "arbitrary"),
    )(order, h, w_full)


def _exchange(comm, name):
    arrays, shapes, n_sems, build, aliased = comm
    n_in, n_out = len(arrays), len(shapes)

    def body(*refs):
        copies = build(refs[:n_in], refs[n_in:n_in + n_out], refs[n_in + n_out:])
        for cp in copies:
            cp.start()
        for cp in copies:
            cp.wait()

    return pl.pallas_call(
        body, name=name, out_shape=tuple(shapes), in_specs=[HBM_SPEC] * n_in, out_specs=(HBM_SPEC,) * n_out,
        scratch_shapes=_comm_scratch(n_sems), input_output_aliases={t: t for t in range(n_in)} if aliased else {},
    )(*arrays)


def _pair_halves(parts):
    def half_of(ref, which):
        if len(ref.shape) == 2:
            rows = ref.shape[0] // 2
            return ref.at[pl.ds(which * rows, rows), :]
        return ref.at[:, pl.ds(which, 1)]

    def out_shape(p):
        if p.ndim == 2:
            return jax.ShapeDtypeStruct((p.shape[0] // 2, p.shape[1]), BF16)
        return jax.ShapeDtypeStruct((p.shape[0], 1) + p.shape[2:], BF16)

    def build(p_refs, r_refs, sems):
        x, y, c, _ = _place()
        return [_remote(half_of(p, 1 - c), r, sems, t, (x, y, 1 - c)) for t, (p, r) in enumerate(zip(p_refs, r_refs))]

    return (tuple(parts), tuple(out_shape(p) for p in parts), len(parts), build, False)


def _gather_squares(fulls, over_ici):
    half_up = UP_ROWS // 2

    def build(_, f_refs, sems):
        x, y, c, others = _place()
        copies = []
        for t, f in enumerate(f_refs):
            for r, chip in enumerate(others):
                src_chip = 2 * x + y if over_ici else _chip_index(chip)
                rows = f.at[pl.ds(src_chip * UP_ROWS + c * half_up, half_up), :]
                copies.append(_remote(rows, rows, sems, 3 * t + r, (*chip, c) if over_ici else (x, y, 1 - c)))
        return copies

    return (tuple(fulls), tuple(jax.ShapeDtypeStruct(f.shape, f.dtype) for f in fulls), 3 * len(fulls), build, True)


def _pair_add_in(idx, p_in, r_in, rows=256, cols=SLAB):
    Dm, P = p_in.shape
    half = Dm // 2
    rows = min(rows, half)
    nrb = half // rows

    def body(idx_ref, p_ref, r_ref, o_ref):
        o_ref[...] = (p_ref[...].astype(F32) + r_ref[...].astype(F32)).astype(BF16)

    grid_spec = pltpu.PrefetchScalarGridSpec(
        num_scalar_prefetch=1, grid=(nrb, P // cols),
        in_specs=[pl.BlockSpec((rows, cols), lambda i, j, idx: (idx[0] * nrb + i, j)),
                  pl.BlockSpec((rows, cols), lambda i, j, idx: (i, j))],
        out_specs=pl.BlockSpec((rows, cols), lambda i, j, idx: (i, j)))
    return pl.pallas_call(body, name="pair_add_in", out_shape=jax.ShapeDtypeStruct((half, P), BF16), grid_spec=grid_spec,
                          compiler_params=_params("parallel", "parallel"))(idx, p_in, r_in)


def _pair_add_up(idx, p_up, r_up, name):
    _, _, R, C = p_up.shape

    def body(idx_ref, p_ref, r_ref, o_ref):
        o_ref[...] = (p_ref[...].astype(F32) + r_ref[...].astype(F32)).astype(BF16)

    grid_spec = pltpu.PrefetchScalarGridSpec(
        num_scalar_prefetch=1, grid=(N_CHIPS,),
        in_specs=[pl.BlockSpec((1, 1, R, C), lambda j, idx: (j, idx[0], 0, 0)),
                  pl.BlockSpec((1, 1, R, C), lambda j, idx: (j, 0, 0, 0))],
        out_specs=pl.BlockSpec((1, 1, R, C), lambda j, idx: (j, 0, 0, 0)))
    return pl.pallas_call(body, name=name, out_shape=jax.ShapeDtypeStruct((N_CHIPS, 1, R, C), BF16), grid_spec=grid_spec,
                          compiler_params=_params("parallel"))(idx, p_up, r_up)


def _slab_exchange(qs):
    n = len(qs)

    def out_shape(q):
        if q.ndim == 2:
            return jax.ShapeDtypeStruct((3, q.shape[0], SLAB), BF16)
        return jax.ShapeDtypeStruct((3,) + q.shape[1:], BF16)

    def build(q_refs, r_refs, sems):
        _, _, c, others = _place()
        copies = []
        for r, chip in enumerate(others):
            ci = _chip_index(chip)
            for t in range(n):
                src = q_refs[t].at[:, pl.ds(ci * SLAB, SLAB)] if len(q_refs[t].shape) == 2 else q_refs[t].at[ci]
                copies.append(_remote(src, r_refs[t].at[r], sems, 3 * t + r, (*chip, c)))
        return copies

    return (tuple(qs), tuple(out_shape(q) for q in qs), 3 * n, build, False)


def _slab_add_in(idx, q_in, r2_in, rows=128):
    half = q_in.shape[0]
    rows = min(rows, half)
    nrb = half // rows

    def body(idx_ref, q_ref, r_ref, o_ref):
        o_ref[...] = ((q_ref[...].astype(F32) + r_ref[0].astype(F32)) + r_ref[1].astype(F32)) + r_ref[2].astype(F32)

    grid_spec = pltpu.PrefetchScalarGridSpec(
        num_scalar_prefetch=1, grid=(nrb,),
        in_specs=[pl.BlockSpec((rows, SLAB), lambda i, idx: (i, idx[1])),
                  pl.BlockSpec((3, rows, SLAB), lambda i, idx: (0, i, 0))],
        out_specs=pl.BlockSpec((rows, SLAB), lambda i, idx: (idx[0] * nrb + i, 0)))
    return pl.pallas_call(body, name="slab_add_in", out_shape=jax.ShapeDtypeStruct((2 * half, SLAB), F32),
                          grid_spec=grid_spec, compiler_params=_params("parallel"))(idx, q_in, r2_in)


def _slab_add_up(idx, q_up, r2_up, name, rows=128):
    _, _, R, C = q_up.shape
    rows = min(rows, R)
    nrb = R // rows

    def body(idx_ref, q_ref, r_ref, o_ref):
        o_ref[...] = ((q_ref[0, 0].astype(F32) + r_ref[0, 0].astype(F32)) + r_ref[1, 0].astype(F32)) + r_ref[2, 0].astype(F32)

    grid_spec = pltpu.PrefetchScalarGridSpec(
        num_scalar_prefetch=1, grid=(nrb,),
        in_specs=[pl.BlockSpec((1, 1, rows, C), lambda i, idx: (idx[1], 0, i, 0)),
                  pl.BlockSpec((3, 1, rows, C), lambda i, idx: (0, 0, i, 0))],
        out_specs=pl.BlockSpec((rows, C), lambda i, idx: (idx[0] * nrb + i, 0)))
    return pl.pallas_call(body, name=name, out_shape=jax.ShapeDtypeStruct((2 * R, C), F32), grid_spec=grid_spec,
                          compiler_params=_params("parallel"))(idx, q_up, r2_up)


def _pair_share(gs):
    def build(_, g_refs, sems):
        x, y, c, _ = _place()
        copies = []
        for t, g in enumerate(g_refs):
            rows = g.shape[0] // 2
            mine = g.at[pl.ds(c * rows, rows), :]
            copies.append(_remote(mine, mine, sems, t, (x, y, 1 - c)))
        return copies

    return (tuple(gs), tuple(jax.ShapeDtypeStruct(g.shape, g.dtype) for g in gs), len(gs), build, True)


def _all_reduce_small(packed):
    R, C = packed.shape
    N_DEV = 2 * N_CHIPS

    def body(x_ref, out_ref, all_ref, send_sems, recv_sems, local_sem):
        x, y, c, others = _place()
        me, sib = (x, y, c), (x, y, 1 - c)
        sems = (send_sems, recv_sems)

        def rows(px, py, pc):
            return all_ref.at[4 * px + 2 * py + pc]

        mine = pltpu.make_async_copy(x_ref, rows(*me), local_sem)
        mine.start()
        first = [_remote(x_ref, rows(*me), sems, 0, sib)]
        first += [_remote(x_ref, rows(*me), sems, 1 + j, (*chip, c)) for j, chip in enumerate(others)]
        for cp in first:
            cp.start()
        passed = [_remote(rows(*chip, c), rows(*chip, c), sems, 4 + j, sib) for j, chip in enumerate(others)]
        for j, chip in enumerate(others):
            _remote(rows(*chip, c), rows(*chip, c), sems, 1 + j, me).wait_recv()
            passed[j].start()
        _remote(rows(*sib), rows(*sib), sems, 0, me).wait_recv()
        for j, chip in enumerate(others):
            _remote(rows(*chip, 1 - c), rows(*chip, 1 - c), sems, 4 + j, me).wait_recv()
        for cp in first + passed:
            cp.wait_send()
        mine.wait()
        total = all_ref[0]
        for d in range(1, N_DEV):
            total = total + all_ref[d]
        out_ref[...] = total

    vmem = pl.BlockSpec(memory_space=pltpu.VMEM)
    return pl.pallas_call(
        body, name="all_reduce_small", out_shape=jax.ShapeDtypeStruct((R, C), F32),
        in_specs=[vmem], out_specs=vmem,
        scratch_shapes=[pltpu.VMEM((N_DEV, R, C), F32), pltpu.SemaphoreType.DMA((7,)), pltpu.SemaphoreType.DMA((7,)),
                        pltpu.SemaphoreType.DMA],
        compiler_params=pltpu.CompilerParams(vmem_limit_bytes=VMEM_LIMIT),
    )(packed)


SMALL_NAMES = ("w_spatial", "b_spatial", "norm_g", "gmlp_ln_g", "gmlp_ln_b", "final_norm_g", "attn_sink")


def _pack_small(parts, extra=None):
    blocks = []
    for n in SMALL_NAMES:
        flat = parts[n].reshape(-1).astype(F32)
        rows = -(-flat.shape[0] // (8 * LANES)) * 8
        flat = jnp.pad(flat, (0, rows * LANES - flat.shape[0]))
        blocks.append(flat.reshape(rows, LANES))
    blocks.append(jnp.zeros((8, LANES), F32) if extra is None else extra)
    return jnp.concatenate(blocks, axis=0)


def _unpack_small(packed, shapes):
    out, r = {}, 0
    for n in SMALL_NAMES:
        size = 1
        for s in shapes[n]:
            size *= s
        rows = -(-size // (8 * LANES)) * 8
        out[n] = packed[r:r + rows].reshape(-1)[:size].reshape(shapes[n])
        r += rows
    return out, packed[r:r + 8]


def kernel(x, positions, norm_g, w_in, attn_sink, gmlp_ln_g, gmlp_ln_b, w_spatial, b_spatial, w_up_attn, w_up_gmlp, w_out, final_norm_g, loss_target, m_norm_g, m_w_in, m_attn_sink, m_gmlp_ln_g, m_gmlp_ln_b, m_w_spatial, m_b_spatial, m_w_up_attn, m_w_up_gmlp, m_w_out, m_final_norm_g, v_norm_g, v_w_in, v_attn_sink, v_gmlp_ln_g, v_gmlp_ln_b, v_w_spatial, v_b_spatial, v_w_up_attn, v_w_up_gmlp, v_w_out, v_final_norm_g):
    xs, tgt = x[0], loss_target[0]
    mx, my, mc = lax.axis_index("x"), lax.axis_index("y"), lax.axis_index("c")
    idx = jnp.stack([mc, 2 * mx + my]).astype(jnp.int32)
    arrival = jnp.stack([2 * mx + my, 2 * (1 - mx) + my, 2 * mx + (1 - my), 2 * (1 - mx) + (1 - my)]).astype(jnp.int32)

    square = (D_MODEL, D_MODEL)
    w_full = _cast_into(idx, w_in[0], (D_MODEL, PROJ_WIDTH), "cast_w_in")
    ups = [_cast_into(idx, w_up_attn[0], square, "cast_w_up_attn"), _cast_into(idx, w_up_gmlp[0], square, "cast_w_up_gmlp"),
           _cast_into(idx, w_out[0], square, "cast_w_out")]

    tables = _rope_tables(positions[0])
    bias = _band_bias()
    sink = attn_sink[0]
    ws, bs_t = w_spatial[0], b_spatial[0].T
    h = _rms_fwd(xs, norm_g)
    proj, w_full = _proj_gather(arrival, h, w_full)
    attn, a_in, *ups = _attn_fwd(proj, tables, bias, sink, comm=_gather_squares(ups, over_ici=True))
    b_in, wua, wug, wo = _gmlp_fwd(proj, ws, bs_t, gmlp_ln_g, gmlp_ln_b, comm=_gather_squares(ups, over_ici=False))
    y_a = _matmul(a_in, wua, mode="nn", out_dtype=F32, name="up_attn")
    y_b = _matmul(b_in, wug, mode="nn", out_dtype=F32, name="up_gmlp")
    merged = _merge_fwd(proj, y_a, y_b)
    x2 = _matmul(merged, wo, mode="nn", out_dtype=F32, name="out_proj", residual=xs, tk=2048)
    dx2, dx2_b, d_fg, loss_part = _loss_head(x2, tgt, final_norm_g.reshape(1, D_MODEL))

    dmerged = _matmul(dx2_b, wo, mode="nt", out_dtype=F32, name="d_merged")
    dy_a, dy_b, d_gates = _merge_bwd(proj, y_a, y_b, dmerged)

    p_ups = [_matmul(a_in, dy_a, mode="tn", out_dtype=BF16, name="dw_up_attn"),
             _matmul(b_in, dy_b, mode="tn", out_dtype=BF16, name="dw_up_gmlp"),
             _matmul(merged, dx2_b, mode="tn", out_dtype=BF16, name="dw_out")]
    p_ups = [p.reshape(N_CHIPS, 2, UP_ROWS // 2, D_MODEL) for p in p_ups]
    da_in, *r1_ups = _matmul(dy_a, wua, mode="nt", out_dtype=F32, name="d_a_in", comm=_pair_halves(p_ups))
    q_ups = [_pair_add_up(idx, p, r, "pair_add_up%d" % t) for t, (p, r) in enumerate(zip(p_ups, r1_ups))]
    db_in = _matmul(dy_b, wug, mode="nt", out_dtype=F32, name="d_b_in")
    dq, dk, dv, dga, d_sink = _attn_bwd(proj, tables, bias, sink, attn, da_in)
    d_gmlp, d_ws, d_bs, d_lg, d_lb = _gmlp_bwd(proj, ws, bs_t, gmlp_ln_g, gmlp_ln_b, db_in)
    dproj = jnp.concatenate([dq, dk, dv, dga, d_gmlp, d_gates], axis=1)

    p_in, *r2_ups = _matmul(h, dproj, mode="tn", out_dtype=BF16, name="dw_in", comm=_slab_exchange(q_ups))
    r1_in, = _exchange(_pair_halves([p_in]), "pair_exchange_in")
    q_in = _pair_add_in(idx, p_in, r1_in)
    dh, r2_in = _matmul(dproj, w_full, mode="nt", out_dtype=F32, name="d_h", comm=_slab_exchange([q_in]), tk=SLAB // 2)
    grad_x, d_ng = _rms_bwd(xs, dh, dx2, norm_g)

    g_half = [_slab_add_in(idx, q_in, r2_in)] + [_slab_add_up(idx, q, r, "slab_add_up%d" % t)
                                                 for t, (q, r) in enumerate(zip(q_ups, r2_ups))]
    g_big = _exchange(_pair_share(g_half), "pair_share")

    small_shapes = {"w_spatial": w_spatial.shape, "b_spatial": b_spatial.shape, "norm_g": norm_g.shape,
                    "gmlp_ln_g": gmlp_ln_g.shape, "gmlp_ln_b": gmlp_ln_b.shape, "final_norm_g": final_norm_g.shape,
                    "attn_sink": attn_sink.shape}
    d_small = {"w_spatial": d_ws, "b_spatial": d_bs, "norm_g": d_ng, "gmlp_ln_g": d_lg, "gmlp_ln_b": d_lb,
               "final_norm_g": d_fg, "attn_sink": d_sink[:, 0, :HEADS_PER_STEP]}
    g_small = _all_reduce_small(_pack_small(d_small, loss_part))

    w_small = _pack_small(dict(w_spatial=w_spatial, b_spatial=b_spatial, norm_g=norm_g, gmlp_ln_g=gmlp_ln_g,
                               gmlp_ln_b=gmlp_ln_b, final_norm_g=final_norm_g, attn_sink=attn_sink))
    m_small = _pack_small(dict(w_spatial=m_w_spatial, b_spatial=m_b_spatial, norm_g=m_norm_g, gmlp_ln_g=m_gmlp_ln_g,
                               gmlp_ln_b=m_gmlp_ln_b, final_norm_g=m_final_norm_g, attn_sink=m_attn_sink))
    v_small = _pack_small(dict(w_spatial=v_w_spatial, b_spatial=v_b_spatial, norm_g=v_norm_g, gmlp_ln_g=v_gmlp_ln_g,
                               gmlp_ln_b=v_gmlp_ln_b, final_norm_g=v_final_norm_g, attn_sink=v_attn_sink))
    small_out = _adamw(w_small, g_small, m_small, v_small, "adamw_small", rows=w_small.shape[0])
    big = {
        "w_in": _adamw(w_in[0], g_big[0], m_w_in[0], v_w_in[0], "adamw_w_in"),
        "w_up_attn": _adamw(w_up_attn[0], g_big[1], m_w_up_attn[0], v_w_up_attn[0], "adamw_w_up_attn"),
        "w_up_gmlp": _adamw(w_up_gmlp[0], g_big[2], m_w_up_gmlp[0], v_w_up_gmlp[0], "adamw_w_up_gmlp"),
        "w_out": _adamw(w_out[0], g_big[3], m_w_out[0], v_w_out[0], "adamw_w_out"),
    }

    order = ("norm_g", "w_in", "attn_sink", "gmlp_ln_g", "gmlp_ln_b", "w_spatial", "b_spatial", "w_up_attn", "w_up_gmlp",
             "w_out", "final_norm_g")
    outs = []
    loss = None
    for kind in range(4):
        small, extra = _unpack_small(small_out[kind], small_shapes)
        if kind == 0:
            loss = extra[0, 0]
        for n in order:
            outs.append(big[n][kind][None] if n in big else small[n])
    return (loss, grad_x[None], *outs)
```

```python
import functools

import jax
import jax.numpy as jnp
from jax import lax
from jax.experimental import pallas as pl
from jax.experimental.pallas import tpu as pltpu

F32 = jnp.float32
BF16 = jnp.bfloat16

D_MODEL = 4096
N_Q_HEADS = 64
N_KV_HEADS = 8
HEAD_DIM = 64
Q_PER_KV = N_Q_HEADS // N_KV_HEADS
ATTN_WIDTH = N_Q_HEADS * HEAD_DIM
KV_WIDTH = N_KV_HEADS * HEAD_DIM
WINDOW = 128
BLOCK = 128
ROPE_THETA = 500000.0
ROPE_DIM = HEAD_DIM // 4
ROPE_HALF = ROPE_DIM // 2
GMLP_WIDTH = D_MODEL
GMLP_GROUPS = 8
GMLP_GROUP_DIM = GMLP_WIDTH // GMLP_GROUPS
GMLP_CHUNK = 128
NORM_EPS = 1e-5
LN_EPS = 1e-5

PROJ_SIZES = (ATTN_WIDTH, KV_WIDTH, KV_WIDTH, ATTN_WIDTH, GMLP_WIDTH, GMLP_WIDTH, GMLP_WIDTH, D_MODEL, D_MODEL)
PROJ_WIDTH = sum(PROJ_SIZES)
OFF_Q, OFF_K, OFF_V, OFF_GA, OFF_U, OFF_VG, OFF_GB, OFF_MA, OFF_MB = (
    sum(PROJ_SIZES[:i]) for i in range(len(PROJ_SIZES)))

ADAM_LR = 0.001
ADAM_B1 = 0.9
ADAM_B2 = 0.999
ADAM_EPS = 1e-08
ADAM_WD = 0.01
ADAM_STEP = 10

N_CHIPS = 4
SLAB = PROJ_WIDTH // N_CHIPS
UP_ROWS = D_MODEL // N_CHIPS
LANES = 128
COL_BLK = 1024
HEADS_PER_STEP = 2 * Q_PER_KV
VMEM_LIMIT = 56 * 1024 * 1024

MESH = pl.DeviceIdType.MESH
HBM_SPEC = pl.BlockSpec(memory_space=pltpu.HBM)
SMEM_SPEC = pl.BlockSpec(memory_space=pltpu.SMEM)


def _params(*sem):
    return pltpu.CompilerParams(dimension_semantics=sem, vmem_limit_bytes=VMEM_LIMIT)


def _sigmoid(x):
    return jax.nn.sigmoid(x)


def _gelu(x):
    return jax.nn.gelu(x, approximate=True)


def _gelu_grad(x):
    c = 0.7978845608028654
    inner = c * (x + 0.044715 * x * x * x)
    t = jnp.tanh(inner)
    return 0.5 * (1.0 + t) + 0.5 * x * (1.0 - t * t) * c * (1.0 + 3 * 0.044715 * x * x)


def _matmul(a, b, *, mode, out_dtype, name, residual=None, comm=None, tm=1024, tn=1024, tk=4096):
    if mode == "nn":
        (M, K), N = a.shape, b.shape[1]
    elif mode == "nt":
        (M, K), N = a.shape, b.shape[0]
    else:
        (K, M), N = a.shape, b.shape[1]
    tm, tn, tk = min(tm, M), min(tn, N), min(tk, K)
    assert M % tm == 0 and N % tn == 0 and K % tk == 0
    ni, nj, nk = M // tm, N // tn, K // tk
    if mode == "nn":
        a_spec = pl.BlockSpec((tm, tk), lambda i, j, k: (i, k))
        b_spec = pl.BlockSpec((tk, tn), lambda i, j, k: (k, j))
        dims = (((1,), (0,)), ((), ()))
    elif mode == "nt":
        a_spec = pl.BlockSpec((tm, tk), lambda i, j, k: (i, k))
        b_spec = pl.BlockSpec((tn, tk), lambda i, j, k: (j, k))
        dims = (((1,), (1,)), ((), ()))
    else:
        a_spec = pl.BlockSpec((tk, tm), lambda i, j, k: (k, i))
        b_spec = pl.BlockSpec((tk, tn), lambda i, j, k: (k, j))
        dims = (((0,), (0,)), ((), ()))
    o_spec = pl.BlockSpec((tm, tn), lambda i, j, k: (i, j))
    n_res = 1 if residual is not None else 0
    c_arrays, c_shapes, c_sems, c_build, c_alias = _comm_fields(comm)
    n_cin, n_cout = len(c_arrays), len(c_shapes)

    def body(*refs):
        a_ref, b_ref = refs[0], refs[1]
        r_ref = refs[2] if n_res else None
        cin = refs[2 + n_res:2 + n_res + n_cin]
        o_ref = refs[2 + n_res + n_cin]
        cout = refs[3 + n_res + n_cin:3 + n_res + n_cin + n_cout]
        n_acc = 1 if nk > 1 else 0
        acc = refs[3 + n_res + n_cin + n_cout] if n_acc else None
        sems = refs[3 + n_acc + n_res + n_cin + n_cout:]
        i, j, k = pl.program_id(0), pl.program_id(1), pl.program_id(2)
        if comm is not None:
            _start_when((i == 0) & (j == 0) & (k == 0), c_build(cin, cout, sems))

        def product():
            return lax.dot_general(a_ref[...], b_ref[...], dims, preferred_element_type=F32)

        def finish(r):
            if n_res:
                r = r + r_ref[...]
            o_ref[...] = r.astype(out_dtype)

        if nk == 1:
            finish(product())
        else:
            @pl.when(k == 0)
            def _():
                acc[...] = product()

            @pl.when((k > 0) & (k < nk - 1))
            def _():
                acc[...] += product()

            @pl.when(k == nk - 1)
            def _():
                finish(acc[...] + product())

        if comm is not None:
            _wait_when((i == ni - 1) & (j == nj - 1) & (k == nk - 1), c_build(cin, cout, sems))

    args = (a, b) + ((residual,) if n_res else ()) + tuple(c_arrays)
    in_specs = [a_spec, b_spec] + ([o_spec] if n_res else []) + [HBM_SPEC] * n_cin
    scratch = ([pltpu.VMEM((tm, tn), F32)] if nk > 1 else []) + _comm_scratch(c_sems)
    sem = ("arbitrary",) * 3 if comm is not None else ("parallel", "parallel", "arbitrary")
    aliases = {2 + n_res + t: 1 + t for t in range(n_cin)} if c_alias else {}
    out = pl.pallas_call(
        body, name=name, out_shape=(jax.ShapeDtypeStruct((M, N), out_dtype),) + tuple(c_shapes),
        grid=(ni, nj, nk), in_specs=in_specs, out_specs=(o_spec,) + (HBM_SPEC,) * n_cout,
        scratch_shapes=scratch, input_output_aliases=aliases, compiler_params=_params(*sem),
    )(*args)
    return out if comm is not None else out[0]


def _cast_into(idx, w, full_shape, name, rows=64):
    R, C = w.shape
    rows = min(rows, R)
    nrb = R // rows
    by_cols = C != full_shape[1]

    def body(idx_ref, w_ref, o_ref):
        o_ref[...] = w_ref[...].astype(BF16)

    out_map = (lambda i, idx: (i, idx[1])) if by_cols else (lambda i, idx: (idx[1] * nrb + i, 0))
    grid_spec = pltpu.PrefetchScalarGridSpec(
        num_scalar_prefetch=1, grid=(nrb,), in_specs=[pl.BlockSpec((rows, C), lambda i, idx: (i, 0))],
        out_specs=pl.BlockSpec((rows, C), out_map))
    return pl.pallas_call(body, name=name, out_shape=jax.ShapeDtypeStruct(full_shape, BF16), grid_spec=grid_spec,
                          compiler_params=_params("parallel"))(idx, w)


def _rms_fwd(x, g, rows=256):
    S, D = x.shape
    rows = min(rows, S)

    def body(x_ref, g_ref, h_ref):
        xv = x_ref[...]
        ms = jnp.mean(xv * xv, axis=-1, keepdims=True)
        h_ref[...] = (xv * lax.rsqrt(ms + NORM_EPS) * g_ref[...]).astype(BF16)

    spec = pl.BlockSpec((rows, D), lambda i: (i, 0))
    return pl.pallas_call(body, name="rms_fwd", out_shape=jax.ShapeDtypeStruct((S, D), BF16), grid=(S // rows,),
                          in_specs=[spec, pl.BlockSpec((1, D), lambda i: (0, 0))], out_specs=spec,
                          compiler_params=_params("parallel"))(x, g)


def _rope_tables(positions):
    inv_freq = ROPE_THETA ** (-jnp.arange(ROPE_HALF, dtype=F32) * 2.0 / ROPE_DIM)
    ang = positions.astype(F32)[:, None] * inv_freq
    cos, sin = jnp.cos(ang), jnp.sin(ang)
    S = positions.shape[0]
    rest = HEAD_DIM - ROPE_DIM
    zeros_h, zeros_r = jnp.zeros((S, ROPE_HALF), F32), jnp.zeros((S, rest), F32)
    c = jnp.concatenate([cos, cos, jnp.ones((S, rest), F32)], axis=1)
    s1 = jnp.concatenate([-sin, zeros_h, zeros_r], axis=1)
    s2 = jnp.concatenate([zeros_h, sin, zeros_r], axis=1)
    reps = LANES // HEAD_DIM
    return jnp.tile(c, (1, reps)), jnp.tile(s1, (1, reps)), jnp.tile(s2, (1, reps))


def _rope(t, c, s1, s2, sign):
    n = t.shape[1]
    reps = n // c.shape[1]
    if reps > 1:
        c, s1, s2 = jnp.tile(c, (1, reps)), jnp.tile(s1, (1, reps)), jnp.tile(s2, (1, reps))
    up = pltpu.roll(t, n - ROPE_HALF, 1)
    down = pltpu.roll(t, ROPE_HALF, 1)
    return t * c + sign * (up * s1 + down * s2)


def _attn_specs(nb):
    last = nb - 1

    def cur(i):
        return jnp.minimum(i, last)

    def prev(i):
        return jnp.maximum(jnp.minimum(i, last) - 1, 0)

    kq, kk, kv, kg = OFF_Q // COL_BLK, OFF_K // LANES, OFF_V // LANES, OFF_GA // COL_BLK
    wide = lambda off: pl.BlockSpec((BLOCK, COL_BLK), lambda p, i: (cur(i), off + p))
    kv_cur = lambda off: pl.BlockSpec((BLOCK, LANES), lambda p, i: (cur(i), off + p))
    kv_prev = lambda off: pl.BlockSpec((BLOCK, LANES), lambda p, i: (prev(i), off + p))
    tab_cur = pl.BlockSpec((BLOCK, LANES), lambda p, i: (cur(i), 0))
    tab_prev = pl.BlockSpec((BLOCK, LANES), lambda p, i: (prev(i), 0))
    proj_specs = [wide(kq), kv_cur(kk), kv_prev(kk), kv_cur(kv), kv_prev(kv), wide(kg)]
    table_specs = [tab_cur] * 3 + [tab_prev] * 3
    bias_spec = pl.BlockSpec((1, Q_PER_KV * BLOCK, BLOCK), lambda p, i: (jnp.minimum(i, 1), 0, 0))
    return proj_specs, table_specs + [bias_spec], cur, prev


def _band_bias():
    a = (jnp.arange(Q_PER_KV * BLOCK) % BLOCK)[:, None]
    col = jnp.arange(BLOCK)[None, :]
    first = jnp.where(col > a, -1e30, 0.0)
    return jnp.stack([first, jnp.zeros_like(first)]).astype(F32)


def _from_prev(shape):
    a = lax.broadcasted_iota(jnp.int32, shape, 0) & (BLOCK - 1)
    return lax.broadcasted_iota(jnp.int32, shape, 1) > a


def _softmax_band(q2, kp2, kc2, sink_col, bias, from_prev):
    nt = (((1,), (1,)), ((), ()))
    s = jnp.where(from_prev, lax.dot_general(q2, kp2, nt, preferred_element_type=F32),
                  lax.dot_general(q2, kc2, nt, preferred_element_type=F32)) + bias
    m = jnp.maximum(jnp.max(s, axis=1, keepdims=True), sink_col)
    e = jnp.exp(s - m)
    es = jnp.exp(sink_col - m)
    inv = 1.0 / (jnp.sum(e, axis=1, keepdims=True) + es)
    return e * inv, es * inv


def _sink_col(sink_ref, pair, kvh):
    return jnp.concatenate(
        [jnp.full((BLOCK, 1), sink_ref[(pair * 2 + kvh) * Q_PER_KV + g], F32) for g in range(Q_PER_KV)], axis=0)


def _stack_heads(ref, kvh):
    base = kvh * Q_PER_KV * HEAD_DIM
    return jnp.concatenate([ref[:, base + g * HEAD_DIM: base + (g + 1) * HEAD_DIM] for g in range(Q_PER_KV)], axis=0)


def _unstack_heads(ref, kvh, val):
    base = kvh * Q_PER_KV * HEAD_DIM
    for g in range(Q_PER_KV):
        ref[:, base + g * HEAD_DIM: base + (g + 1) * HEAD_DIM] = val[g * BLOCK:(g + 1) * BLOCK, :].astype(ref.dtype)


def _roped_keys(k_ref, kp_ref, tabs):
    c, s1, s2, cp, s1p, s2p = (t[...] for t in tabs)
    k_cur = _rope(k_ref[...].astype(F32), c, s1, s2, 1.0)
    k_prev = _rope(kp_ref[...].astype(F32), cp, s1p, s2p, 1.0)
    return k_prev.astype(BF16), k_cur.astype(BF16)


def _head(val, kvh):
    return val[:, kvh * HEAD_DIM:(kvh + 1) * HEAD_DIM]


SCALE = HEAD_DIM ** -0.5


def _attn_fwd(proj, tables, bias, sink, comm=None):
    S = proj.shape[0]
    nb = S // BLOCK
    npairs = N_KV_HEADS // 2
    proj_specs, table_specs, _, _ = _attn_specs(nb)
    c_arrays, c_shapes, c_sems, c_build, c_alias = _comm_fields(comm)
    n_in = len(proj_specs) + len(table_specs) + 1

    def body(*refs):
        (q_ref, k_ref, kp_ref, v_ref, vp_ref, ga_ref, c_ref, s1_ref, s2_ref, cp_ref, s1p_ref, s2p_ref, bias_ref,
         sink_ref) = refs[:n_in]
        cin = refs[n_in:n_in + len(c_arrays)]
        attn_ref, ain_ref = refs[n_in + len(c_arrays):n_in + len(c_arrays) + 2]
        cout = refs[n_in + len(c_arrays) + 2:n_in + len(c_arrays) + 2 + len(c_shapes)]
        qr_scr = refs[n_in + len(c_arrays) + 2 + len(c_shapes)]
        sems = refs[n_in + len(c_arrays) + 3 + len(c_shapes):]
        pair, i = pl.program_id(0), pl.program_id(1)
        if comm is not None:
            _start_when((pair == 0) & (i == 0), c_build(cin, cout, sems))
        tabs = (c_ref, s1_ref, s2_ref, cp_ref, s1p_ref, s2p_ref)
        qr_scr[...] = _rope(q_ref[...].astype(F32), c_ref[...], s1_ref[...], s2_ref[...], 1.0) * SCALE
        k_prev, k_cur = _roped_keys(k_ref, kp_ref, tabs)
        v_prev, v_cur = vp_ref[...], v_ref[...]
        from_prev = _from_prev((Q_PER_KV * BLOCK, BLOCK))
        for kvh in range(2):
            q2 = _stack_heads(qr_scr, kvh).astype(BF16)
            p, _ = _softmax_band(q2, _head(k_prev, kvh), _head(k_cur, kvh), _sink_col(sink_ref, pair, kvh), bias_ref[0],
                                 from_prev)
            pb = p.astype(BF16)
            o = (jnp.dot(jnp.where(from_prev, pb, 0), _head(v_prev, kvh), preferred_element_type=F32)
                 + jnp.dot(jnp.where(from_prev, 0, pb), _head(v_cur, kvh), preferred_element_type=F32))
            _unstack_heads(attn_ref, kvh, o)
        ga = ga_ref[...].astype(F32)
        ain_ref[...] = (attn_ref[...] * (ga * _sigmoid(ga))).astype(BF16)
        if comm is not None:
            _wait_when((pair == npairs - 1) & (i == nb - 1), c_build(cin, cout, sems))

    out_spec = pl.BlockSpec((BLOCK, COL_BLK), lambda p, i: (i, p))
    aliases = {n_in + t: 2 + t for t in range(len(c_arrays))} if c_alias else {}
    out = pl.pallas_call(
        body, name="attn_fwd",
        out_shape=(jax.ShapeDtypeStruct((S, ATTN_WIDTH), F32), jax.ShapeDtypeStruct((S, ATTN_WIDTH), BF16)) + tuple(c_shapes),
        grid=(npairs, nb), in_specs=proj_specs + table_specs + [SMEM_SPEC] + [HBM_SPEC] * len(c_arrays),
        out_specs=(out_spec, out_spec) + (HBM_SPEC,) * len(c_shapes),
        scratch_shapes=[pltpu.VMEM((BLOCK, COL_BLK), F32)] + _comm_scratch(c_sems), input_output_aliases=aliases,
        compiler_params=_params("arbitrary", "arbitrary"),
    )(*([proj] * 6), *tables, *tables, bias, sink, *c_arrays)
    return out


def _attn_bwd(proj, tables, bias, sink, attn, da_in):
    S = proj.shape[0]
    nb = S // BLOCK
    proj_specs, table_specs, cur, prev = _attn_specs(nb)
    wide_cur = pl.BlockSpec((BLOCK, COL_BLK), lambda p, i: (cur(i), p))
    kv_out = pl.BlockSpec((BLOCK, LANES), lambda p, i: (jnp.maximum(i - 1, 0), p))

    def body(q_ref, k_ref, kp_ref, v_ref, vp_ref, ga_ref, c_ref, s1_ref, s2_ref, cp_ref, s1p_ref, s2p_ref, bias_ref,
             sink_ref, attn_ref, da_ref, dq_ref, dk_ref, dv_ref, dga_ref, dsink_ref,
             qr_scr, do_scr, dq_scr, dk_scr, dv_scr, carry_k, carry_v, sink_acc):
        pair, i = pl.program_id(0), pl.program_id(1)
        tabs = (c_ref, s1_ref, s2_ref, cp_ref, s1p_ref, s2p_ref)

        @pl.when(i < nb)
        def _():
            ga = ga_ref[...].astype(F32)
            sg = _sigmoid(ga)
            da = da_ref[...]
            dga_ref[...] = (da * attn_ref[...] * (sg * (1.0 + ga * (1.0 - sg)))).astype(BF16)
            do_scr[...] = da * (ga * sg)
            qr_scr[...] = _rope(q_ref[...].astype(F32), c_ref[...], s1_ref[...], s2_ref[...], 1.0) * SCALE
            k_prev, k_cur = _roped_keys(k_ref, kp_ref, tabs)
            v_prev, v_cur = vp_ref[...], v_ref[...]
            from_prev = _from_prev((Q_PER_KV * BLOCK, BLOCK))
            nt, tn = (((1,), (1,)), ((), ())), (((0,), (0,)), ((), ()))
            for kvh in range(2):
                q2 = _stack_heads(qr_scr, kvh).astype(BF16)
                do2 = _stack_heads(do_scr, kvh)
                o2 = _stack_heads(attn_ref, kvh)
                kp2, kc2, vp2, vc2 = _head(k_prev, kvh), _head(k_cur, kvh), _head(v_prev, kvh), _head(v_cur, kvh)
                p, p_sink = _softmax_band(q2, kp2, kc2, _sink_col(sink_ref, pair, kvh), bias_ref[0], from_prev)
                do2b = do2.astype(BF16)
                dp = jnp.where(from_prev, lax.dot_general(do2b, vp2, nt, preferred_element_type=F32),
                               lax.dot_general(do2b, vc2, nt, preferred_element_type=F32))
                delta = jnp.sum(do2 * o2, axis=1, keepdims=True)
                ds = (p * (dp - delta)).astype(BF16)
                pb = p.astype(BF16)
                ds_prev, ds_cur = jnp.where(from_prev, ds, 0), jnp.where(from_prev, 0, ds)
                p_prev, p_cur = jnp.where(from_prev, pb, 0), jnp.where(from_prev, 0, pb)
                dq2 = jnp.dot(ds_prev, kp2, preferred_element_type=F32) + jnp.dot(ds_cur, kc2, preferred_element_type=F32)
                _unstack_heads(dq_scr, kvh, dq2)
                cols = slice(kvh * HEAD_DIM, (kvh + 1) * HEAD_DIM)
                dk_scr[0:BLOCK, cols] = lax.dot_general(ds_prev, q2, tn, preferred_element_type=F32)
                dk_scr[BLOCK:2 * BLOCK, cols] = lax.dot_general(ds_cur, q2, tn, preferred_element_type=F32)
                dv_scr[0:BLOCK, cols] = lax.dot_general(p_prev, do2b, tn, preferred_element_type=F32)
                dv_scr[BLOCK:2 * BLOCK, cols] = lax.dot_general(p_cur, do2b, tn, preferred_element_type=F32)
                contrib = -p_sink * delta

                @pl.when(i == 0)
                def _():
                    sink_acc[kvh] = contrib

                @pl.when(i > 0)
                def _():
                    sink_acc[kvh] += contrib

            dq_ref[...] = _rope(dq_scr[...] * SCALE, c_ref[...], s1_ref[...], s2_ref[...], -1.0).astype(BF16)
            dk_prev = _rope(dk_scr[0:BLOCK, :], cp_ref[...], s1p_ref[...], s2p_ref[...], -1.0)
            dk_cur = _rope(dk_scr[BLOCK:2 * BLOCK, :], c_ref[...], s1_ref[...], s2_ref[...], -1.0)
            dv_prev, dv_cur = dv_scr[0:BLOCK, :], dv_scr[BLOCK:2 * BLOCK, :]

            @pl.when(i > 0)
            def _():
                dk_ref[...] = (carry_k[...] + dk_prev).astype(BF16)
                dv_ref[...] = (carry_v[...] + dv_prev).astype(BF16)

            carry_k[...] = dk_cur
            carry_v[...] = dv_cur

        @pl.when(i == nb)
        def _():
            dk_ref[...] = carry_k[...].astype(BF16)
            dv_ref[...] = carry_v[...].astype(BF16)
            lane = lax.broadcasted_iota(jnp.int32, (8, LANES), 1)
            acc = jnp.zeros((8, LANES), F32)
            for kvh in range(2):
                for g in range(Q_PER_KV):
                    val = jnp.sum(sink_acc[kvh, g * BLOCK:(g + 1) * BLOCK, :], axis=0, keepdims=True)
                    acc = jnp.where(lane == kvh * Q_PER_KV + g, val, acc)
            dsink_ref[0] = acc

    return pl.pallas_call(
        body, name="attn_bwd",
        out_shape=(jax.ShapeDtypeStruct((S, ATTN_WIDTH), BF16), jax.ShapeDtypeStruct((S, KV_WIDTH), BF16),
                   jax.ShapeDtypeStruct((S, KV_WIDTH), BF16), jax.ShapeDtypeStruct((S, ATTN_WIDTH), BF16),
                   jax.ShapeDtypeStruct((N_KV_HEADS // 2, 8, LANES), F32)),
        grid=(N_KV_HEADS // 2, nb + 1),
        in_specs=proj_specs + table_specs + [SMEM_SPEC, wide_cur, wide_cur],
        out_specs=(wide_cur, kv_out, kv_out, wide_cur, pl.BlockSpec((1, 8, LANES), lambda p, i: (p, 0, 0))),
        scratch_shapes=[pltpu.VMEM((BLOCK, COL_BLK), F32), pltpu.VMEM((BLOCK, COL_BLK), F32),
                        pltpu.VMEM((BLOCK, COL_BLK), F32), pltpu.VMEM((2 * BLOCK, LANES), F32),
                        pltpu.VMEM((2 * BLOCK, LANES), F32), pltpu.VMEM((BLOCK, LANES), F32),
                        pltpu.VMEM((BLOCK, LANES), F32), pltpu.VMEM((2, Q_PER_KV * BLOCK, 1), F32)],
        compiler_params=_params("arbitrary", "arbitrary"),
    )(*([proj] * 6), *tables, *tables, bias, sink, attn, da_in)


N_PARTS = GMLP_WIDTH // COL_BLK
GROUPS_PER_PART = COL_BLK // GMLP_GROUP_DIM


def _part_specs(off):
    return [pl.BlockSpec((GMLP_CHUNK, COL_BLK), functools.partial(lambda j, i: (i, j), off // COL_BLK + k))
            for k in range(N_PARTS)]


def _group(refs, g):
    lo = (g % GROUPS_PER_PART) * GMLP_GROUP_DIM
    return refs[g // GROUPS_PER_PART][:, lo:lo + GMLP_GROUP_DIM].astype(F32)


def _gmlp_norm_stats(vg_refs, gv_scr):
    total = jnp.zeros((GMLP_CHUNK, 1), F32)
    for k in range(N_PARTS):
        gv = _gelu(vg_refs[k][...].astype(F32))
        gv_scr[:, k * COL_BLK:(k + 1) * COL_BLK] = gv
        total = total + jnp.sum(gv, axis=1, keepdims=True)
    mu = total / GMLP_WIDTH
    xc = gv_scr[...] - mu
    var = jnp.sum(xc * xc, axis=1, keepdims=True) / GMLP_WIDTH
    return mu, lax.rsqrt(var + LN_EPS)


def _tril_bf16(ws_ref, g):
    t = lax.broadcasted_iota(jnp.int32, (GMLP_CHUNK, GMLP_CHUNK), 0)
    s = lax.broadcasted_iota(jnp.int32, (GMLP_CHUNK, GMLP_CHUNK), 1)
    return jnp.where(s <= t, ws_ref[g], 0.0).astype(BF16), s <= t


def _gmlp_fwd(proj, ws, bs_t, lg, lb, comm=None):
    S = proj.shape[0]
    nb = S // GMLP_CHUNK
    c_arrays, c_shapes, c_sems, c_build, c_alias = _comm_fields(comm)
    n_cin, n_cout = len(c_arrays), len(c_shapes)

    def body(*refs):
        u_refs, vg_refs, gb_refs = refs[0:4], refs[4:8], refs[8:12]
        ws_ref, bst_ref, lg_ref, lb_ref = refs[12:16]
        cin, out_ref, cout = refs[16:16 + n_cin], refs[16 + n_cin], refs[17 + n_cin:17 + n_cin + n_cout]
        gv_scr, sems = refs[17 + n_cin + n_cout], refs[18 + n_cin + n_cout:]
        if comm is not None:
            _start_when(pl.program_id(0) == 0, c_build(cin, cout, sems))
        mu, rstd = _gmlp_norm_stats(vg_refs, gv_scr)
        for g in range(GMLP_GROUPS):
            cols = slice(g * GMLP_GROUP_DIM, (g + 1) * GMLP_GROUP_DIM)
            vn = (gv_scr[:, cols] - mu) * rstd * lg_ref[:, cols] + lb_ref[:, cols]
            w, _ = _tril_bf16(ws_ref, g)
            mixed = jnp.dot(w, vn.astype(BF16), preferred_element_type=F32) + bst_ref[:, g:g + 1]
            gb = _group(gb_refs, g)
            out_ref[:, cols] = (_gelu(_group(u_refs, g)) * mixed * (gb * _sigmoid(gb))).astype(BF16)
        if comm is not None:
            _wait_when(pl.program_id(0) == nb - 1, c_build(cin, cout, sems))

    full = lambda shape: pl.BlockSpec(shape, lambda i: tuple(0 for _ in shape))
    aliases = {16 + t: 1 + t for t in range(n_cin)} if c_alias else {}
    out = pl.pallas_call(
        body, name="gmlp_fwd", out_shape=(jax.ShapeDtypeStruct((S, GMLP_WIDTH), BF16),) + tuple(c_shapes), grid=(nb,),
        in_specs=_part_specs(OFF_U) + _part_specs(OFF_VG) + _part_specs(OFF_GB)
        + [full(ws.shape), full(bs_t.shape), full(lg.shape), full(lb.shape)] + [HBM_SPEC] * n_cin,
        out_specs=(pl.BlockSpec((GMLP_CHUNK, GMLP_WIDTH), lambda i: (i, 0)),) + (HBM_SPEC,) * n_cout,
        scratch_shapes=[pltpu.VMEM((GMLP_CHUNK, GMLP_WIDTH), F32)] + _comm_scratch(c_sems), input_output_aliases=aliases,
        compiler_params=_params("arbitrary" if comm is not None else "parallel"),
    )(*([proj] * 12), ws, bs_t, lg, lb, *c_arrays)
    return out if comm is not None else out[0]


def _gmlp_bwd(proj, ws, bs_t, lg, lb, db_in):
    S = proj.shape[0]
    nb = S // GMLP_CHUNK
    W = GMLP_WIDTH

    def body(*refs):
        u_refs, vg_refs, gb_refs = refs[0:4], refs[4:8], refs[8:12]
        ws_ref, bst_ref, lg_ref, lb_ref, db_ref = refs[12:17]
        out_ref, dws_ref, dbs_ref, dlg_ref, dlb_ref = refs[17:22]
        gv_scr, dvh_scr = refs[22:]
        i = pl.program_id(0)

        @pl.when(i == 0)
        def _():
            dws_ref[...] = jnp.zeros_like(dws_ref)
            dbs_ref[...] = jnp.zeros_like(dbs_ref)
            dlg_ref[...] = jnp.zeros_like(dlg_ref)
            dlb_ref[...] = jnp.zeros_like(dlb_ref)

        mu, rstd = _gmlp_norm_stats(vg_refs, gv_scr)
        sum_dvh = jnp.zeros((GMLP_CHUNK, 1), F32)
        sum_dvh_vh = jnp.zeros((GMLP_CHUNK, 1), F32)
        for g in range(GMLP_GROUPS):
            cols = slice(g * GMLP_GROUP_DIM, (g + 1) * GMLP_GROUP_DIM)
            vhat = (gv_scr[:, cols] - mu) * rstd
            vn = (vhat * lg_ref[:, cols] + lb_ref[:, cols]).astype(BF16)
            w, tril = _tril_bf16(ws_ref, g)
            mixed = jnp.dot(w, vn, preferred_element_type=F32) + bst_ref[:, g:g + 1]
            u, gb, db = _group(u_refs, g), _group(gb_refs, g), db_ref[:, cols]
            gu, sgb = _gelu(u), _sigmoid(gb)
            dsg = db * (gb * sgb)
            out_ref[:, 2 * W + g * GMLP_GROUP_DIM: 2 * W + (g + 1) * GMLP_GROUP_DIM] = (
                db * (gu * mixed) * (sgb * (1.0 + gb * (1.0 - sgb)))).astype(BF16)
            out_ref[:, cols] = (dsg * mixed * _gelu_grad(u)).astype(BF16)
            dmixed = dsg * gu
            dmixed_b = dmixed.astype(BF16)
            dvn = lax.dot_general(w, dmixed_b, (((0,), (0,)), ((), ())), preferred_element_type=F32)
            dw = lax.dot_general(dmixed_b, vn, (((1,), (1,)), ((), ())), preferred_element_type=F32)
            dws_ref[g] += jnp.where(tril, dw, 0.0)
            dbs_ref[g] += jnp.sum(dmixed, axis=1, keepdims=True)
            dlg_ref[:, cols] += jnp.sum(dvn * vhat, axis=0, keepdims=True)
            dlb_ref[:, cols] += jnp.sum(dvn, axis=0, keepdims=True)
            dvh = dvn * lg_ref[:, cols]
            dvh_scr[:, cols] = dvh
            sum_dvh = sum_dvh + jnp.sum(dvh, axis=1, keepdims=True)
            sum_dvh_vh = sum_dvh_vh + jnp.sum(dvh * vhat, axis=1, keepdims=True)
        m1, m2 = sum_dvh / W, sum_dvh_vh / W
        for k in range(N_PARTS):
            cols = slice(k * COL_BLK, (k + 1) * COL_BLK)
            vhat = (gv_scr[:, cols] - mu) * rstd
            dgv = rstd * (dvh_scr[:, cols] - m1 - vhat * m2)
            out_ref[:, W + k * COL_BLK: W + (k + 1) * COL_BLK] = (
                dgv * _gelu_grad(vg_refs[k][...].astype(F32))).astype(BF16)

    full = lambda shape: pl.BlockSpec(shape, lambda i: tuple(0 for _ in shape))
    row = pl.BlockSpec((GMLP_CHUNK, W), lambda i: (i, 0))
    return pl.pallas_call(
        body, name="gmlp_bwd",
        out_shape=(jax.ShapeDtypeStruct((S, 3 * W), BF16), jax.ShapeDtypeStruct(ws.shape, F32),
                   jax.ShapeDtypeStruct((GMLP_GROUPS, GMLP_CHUNK, 1), F32), jax.ShapeDtypeStruct((1, W), F32),
                   jax.ShapeDtypeStruct((1, W), F32)),
        grid=(nb,),
        in_specs=_part_specs(OFF_U) + _part_specs(OFF_VG) + _part_specs(OFF_GB)
        + [full(ws.shape), full(bs_t.shape), full(lg.shape), full(lb.shape), row],
        out_specs=(pl.BlockSpec((GMLP_CHUNK, 3 * W), lambda i: (i, 0)), full(ws.shape),
                   full((GMLP_GROUPS, GMLP_CHUNK, 1)), full((1, W)), full((1, W))),
        scratch_shapes=[pltpu.VMEM((GMLP_CHUNK, W), F32), pltpu.VMEM((GMLP_CHUNK, W), F32)],
        compiler_params=_params("arbitrary"),
    )(*([proj] * 12), ws, bs_t, lg, lb, db_in)


def _merge_fwd(proj, y_a, y_b, rows=256):
    S = proj.shape[0]
    rows = min(rows, S)

    def body(ma_ref, mb_ref, ya_ref, yb_ref, out_ref):
        out_ref[...] = (_sigmoid(ma_ref[...].astype(F32)) * ya_ref[...]
                        + _sigmoid(mb_ref[...].astype(F32)) * yb_ref[...]).astype(BF16)

    blk = lambda off: pl.BlockSpec((rows, COL_BLK), lambda i, j: (i, off // COL_BLK + j))
    return pl.pallas_call(
        body, name="merge_fwd", out_shape=jax.ShapeDtypeStruct((S, D_MODEL), BF16), grid=(S // rows, D_MODEL // COL_BLK),
        in_specs=[blk(OFF_MA), blk(OFF_MB), blk(0), blk(0)], out_specs=blk(0),
        compiler_params=_params("parallel", "parallel"),
    )(proj, proj, y_a, y_b)


def _merge_bwd(proj, y_a, y_b, dmerged, rows=128):
    S = proj.shape[0]
    rows = min(rows, S)
    nj = D_MODEL // COL_BLK

    def body(*refs):
        ma_refs, mb_refs = refs[0:nj], refs[nj:2 * nj]
        ya_ref, yb_ref, dm_ref, dya_ref, dyb_ref, dg_ref = refs[2 * nj:]
        for k in range(nj):
            cols = slice(k * COL_BLK, (k + 1) * COL_BLK)
            dm = dm_ref[:, cols]
            sa, sb = _sigmoid(ma_refs[k][...].astype(F32)), _sigmoid(mb_refs[k][...].astype(F32))
            dya_ref[:, cols] = (dm * sa).astype(BF16)
            dyb_ref[:, cols] = (dm * sb).astype(BF16)
            dg_ref[:, cols] = (dm * ya_ref[:, cols] * sa * (1.0 - sa)).astype(BF16)
            dg_ref[:, D_MODEL + k * COL_BLK: D_MODEL + (k + 1) * COL_BLK] = (
                dm * yb_ref[:, cols] * sb * (1.0 - sb)).astype(BF16)

    part = lambda off: [pl.BlockSpec((rows, COL_BLK), functools.partial(lambda j, i: (i, j), off // COL_BLK + k))
                        for k in range(nj)]
    row = pl.BlockSpec((rows, D_MODEL), lambda i: (i, 0))
    return pl.pallas_call(
        body, name="merge_bwd",
        out_shape=(jax.ShapeDtypeStruct((S, D_MODEL), BF16), jax.ShapeDtypeStruct((S, D_MODEL), BF16),
                   jax.ShapeDtypeStruct((S, 2 * D_MODEL), BF16)),
        grid=(S // rows,), in_specs=part(OFF_MA) + part(OFF_MB) + [row, row, row],
        out_specs=(row, row, pl.BlockSpec((rows, 2 * D_MODEL), lambda i: (i, 0))),
        compiler_params=_params("parallel"),
    )(*([proj] * (2 * nj)), y_a, y_b, dmerged)


def _loss_head(x2, target, fg, rows=128):
    S, D = x2.shape
    rows = min(rows, S)

    def body(x_ref, t_ref, g_ref, dx_ref, dxb_ref, dg_ref, loss_ref):
        i = pl.program_id(0)

        @pl.when(i == 0)
        def _():
            dg_ref[...] = jnp.zeros_like(dg_ref)
            loss_ref[...] = jnp.zeros_like(loss_ref)

        xv, g = x_ref[...], g_ref[...]
        rstd = lax.rsqrt(jnp.mean(xv * xv, axis=-1, keepdims=True) + NORM_EPS)
        xhat = xv * rstd
        err = xhat * g - t_ref[...]
        loss_ref[...] += (0.5 / D) * jnp.sum(err * err)
        dy = err * (1.0 / D)
        dg_ref[...] += jnp.sum(dy * xhat, axis=0, keepdims=True)
        dxh = dy * g
        dx = rstd * (dxh - xhat * jnp.mean(dxh * xhat, axis=-1, keepdims=True))
        dx_ref[...] = dx
        dxb_ref[...] = dx.astype(BF16)

    row = pl.BlockSpec((rows, D), lambda i: (i, 0))
    vec = pl.BlockSpec((1, D), lambda i: (0, 0))
    return pl.pallas_call(
        body, name="loss_head",
        out_shape=(jax.ShapeDtypeStruct((S, D), F32), jax.ShapeDtypeStruct((S, D), BF16),
                   jax.ShapeDtypeStruct((1, D), F32), jax.ShapeDtypeStruct((8, LANES), F32)),
        grid=(S // rows,), in_specs=[row, row, vec],
        out_specs=(row, row, vec, pl.BlockSpec((8, LANES), lambda i: (0, 0))),
        compiler_params=_params("arbitrary"),
    )(x2, target, fg)


def _rms_bwd(x, dh, dx2, g, rows=128):
    S, D = x.shape
    rows = min(rows, S)

    def body(x_ref, dh_ref, dx2_ref, g_ref, gx_ref, dg_ref):
        i = pl.program_id(0)

        @pl.when(i == 0)
        def _():
            dg_ref[...] = jnp.zeros_like(dg_ref)

        xv, dh_v = x_ref[...], dh_ref[...]
        rstd = lax.rsqrt(jnp.mean(xv * xv, axis=-1, keepdims=True) + NORM_EPS)
        xhat = xv * rstd
        dg_ref[...] += jnp.sum(dh_v * xhat, axis=0, keepdims=True)
        dxh = dh_v * g_ref[...]
        gx_ref[...] = dx2_ref[...] + rstd * (dxh - xhat * jnp.mean(dxh * xhat, axis=-1, keepdims=True))

    row = pl.BlockSpec((rows, D), lambda i: (i, 0))
    vec = pl.BlockSpec((1, D), lambda i: (0, 0))
    return pl.pallas_call(
        body, name="rms_bwd", out_shape=(jax.ShapeDtypeStruct((S, D), F32), jax.ShapeDtypeStruct((1, D), F32)),
        grid=(S // rows,), in_specs=[row, row, row, vec], out_specs=(row, vec),
        compiler_params=_params("arbitrary"),
    )(x, dh, dx2, g)


def _adamw(w, g, m, v, name, rows=64):
    R, C = w.shape
    rows = min(rows, R)
    c1 = 1.0 - ADAM_B1 ** ADAM_STEP
    c2 = 1.0 - ADAM_B2 ** ADAM_STEP

    def body(w_ref, g_ref, m_ref, v_ref, go_ref, d_ref, mo_ref, vo_ref):
        gv = g_ref[...]
        mn = ADAM_B1 * m_ref[...] + (1.0 - ADAM_B1) * gv
        vn = ADAM_B2 * v_ref[...] + (1.0 - ADAM_B2) * (gv * gv)
        go_ref[...] = gv
        mo_ref[...] = mn
        vo_ref[...] = vn
        d_ref[...] = -ADAM_LR * ((mn / c1) / (jnp.sqrt(vn / c2) + ADAM_EPS) + ADAM_WD * w_ref[...])

    spec = pl.BlockSpec((rows, C), lambda i: (i, 0))
    shape = jax.ShapeDtypeStruct((R, C), F32)
    return pl.pallas_call(body, name=name, out_shape=(shape,) * 4, grid=(R // rows,), in_specs=[spec] * 4,
                          out_specs=(spec,) * 4, compiler_params=_params("parallel"))(w, g, m, v)


def _place():
    x, y, c = lax.axis_index("x"), lax.axis_index("y"), lax.axis_index("c")
    others = [(1 - x, y), (x, 1 - y), (1 - x, 1 - y)]
    return x, y, c, others


def _chip_index(chip):
    return 2 * chip[0] + chip[1]


def _remote(src, dst, sems, k, to):
    send_sems, recv_sems = sems
    return pltpu.make_async_remote_copy(src_ref=src, dst_ref=dst, send_sem=send_sems.at[k], recv_sem=recv_sems.at[k],
                                        device_id=to, device_id_type=MESH)


def _comm_fields(comm):
    return comm if comm is not None else ((), (), 0, None, False)


def _comm_scratch(n_sems):
    return [pltpu.SemaphoreType.DMA((n_sems,)), pltpu.SemaphoreType.DMA((n_sems,))] if n_sems else []


def _start_when(cond, copies):
    @pl.when(cond)
    def _():
        for cp in copies:
            cp.start()


def _wait_when(cond, copies):
    @pl.when(cond)
    def _():
        for cp in copies:
            cp.wait()


def _proj_gather(order, h, w_full, tm=1024, tk=512):
    S, Dm = h.shape
    tm, tk = min(tm, S), min(tk, Dm)
    tn = SLAB // 2
    ni, nj, nk = S // tm, 2 * N_CHIPS, Dm // tk
    total = nj * ni * nk
    half_in = Dm // 2

    def body(order_ref, h_ref, _wi, proj_ref, fi_ref, acc, bbuf, bsem, send_sems, recv_sems):
        j, i, k = pl.program_id(0), pl.program_id(1), pl.program_id(2)
        t = (j * ni + i) * nk + k
        x, y, c, others = _place()
        me_chip = 2 * x + y
        sems = (send_sems, recv_sems)

        def piece(chip, half):
            return fi_ref.at[pl.ds(half * half_in, half_in), pl.ds(chip * SLAB, SLAB)]

        def ici_send(r):
            mine = piece(me_chip, c)
            return _remote(mine, mine, sems, r, (*others[r], c))

        def ici_recv(r):
            got = piece(_chip_index(others[r]), c)
            return _remote(got, got, sems, r, (x, y, c))

        def pass_on(r):
            got = piece(_chip_index(others[r]), c)
            return _remote(got, got, sems, 3 + r, (x, y, 1 - c))

        def passed_recv(r):
            got = piece(_chip_index(others[r]), 1 - c)
            return _remote(got, got, sems, 3 + r, (x, y, c))

        def b_copy(jj, kk, slot):
            col = 2 * order_ref[jj // 2] + jj % 2
            return pltpu.make_async_copy(fi_ref.at[pl.ds(kk * tk, tk), pl.ds(col * tn, tn)], bbuf.at[slot], bsem.at[slot])

        @pl.when(t == 0)
        def _():
            for r in range(3):
                ici_send(r).start()
            b_copy(0, 0, 0).start()

        nxt = t + 1
        kn, i_n, jn = nxt % nk, (nxt // nk) % ni, nxt // (nk * ni)

        @pl.when(nxt < total)
        def _():
            for r in range(3):
                @pl.when((jn == 2 * (r + 1)) & (i_n == 0) & (kn == 0))
                def _():
                    ici_recv(r).wait_recv()
                    pass_on(r).start()
                    passed_recv(r).wait_recv()
            b_copy(jn, kn, nxt % 2).start()

        b_copy(j, k, t % 2).wait()

        def product():
            return jnp.dot(h_ref[...], bbuf[t % 2], preferred_element_type=F32)

        @pl.when(k == 0)
        def _():
            acc[...] = product()

        @pl.when((k > 0) & (k < nk - 1))
        def _():
            acc[...] += product()

        @pl.when(k == nk - 1)
        def _():
            proj_ref[...] = (acc[...] + product()).astype(BF16)

        @pl.when(t == total - 1)
        def _():
            for r in range(3):
                ici_send(r).wait_send()
                pass_on(r).wait_send()

    col_of = lambda j, order: 2 * order[j // 2] + j % 2
    grid_spec = pltpu.PrefetchScalarGridSpec(
        num_scalar_prefetch=1, grid=(nj, ni, nk),
        in_specs=[pl.BlockSpec((tm, tk), lambda j, i, k, order: (i, k)), HBM_SPEC],
        out_specs=(pl.BlockSpec((tm, tn), lambda j, i, k, order: (i, col_of(j, order))), HBM_SPEC),
        scratch_shapes=[pltpu.VMEM((tm, tn), F32), pltpu.VMEM((2, tk, tn), BF16), pltpu.SemaphoreType.DMA((2,)),
                        pltpu.SemaphoreType.DMA((6,)), pltpu.SemaphoreType.DMA((6,))])
    return pl.pallas_call(
        body, name="proj_gather",
        out_shape=(jax.ShapeDtypeStruct((S, PROJ_WIDTH), BF16), jax.ShapeDtypeStruct(w_full.shape, BF16)),
        grid_spec=grid_spec, input_output_aliases={2: 1},
        compiler_params=_params("arbitrary", "arbitrary", "arbitrary"),
    )(order, h, w_full)


def _exchange(comm, name):
    arrays, shapes, n_sems, build, aliased = comm
    n_in, n_out = len(arrays), len(shapes)

    def body(*refs):
        copies = build(refs[:n_in], refs[n_in:n_in + n_out], refs[n_in + n_out:])
        for cp in copies:
            cp.start()
        for cp in copies:
            cp.wait()

    return pl.pallas_call(
        body, name=name, out_shape=tuple(shapes), in_specs=[HBM_SPEC] * n_in, out_specs=(HBM_SPEC,) * n_out,
        scratch_shapes=_comm_scratch(n_sems), input_output_aliases={t: t for t in range(n_in)} if aliased else {},
    )(*arrays)


def _pair_halves(parts):
    def half_of(ref, which):
        if len(ref.shape) == 2:
            rows = ref.shape[0] // 2
            return ref.at[pl.ds(which * rows, rows), :]
        return ref.at[:, pl.ds(which, 1)]

    def out_shape(p):
        if p.ndim == 2:
            return jax.ShapeDtypeStruct((p.shape[0] // 2, p.shape[1]), BF16)
        return jax.ShapeDtypeStruct((p.shape[0], 1) + p.shape[2:], BF16)

    def build(p_refs, r_refs, sems):
        x, y, c, _ = _place()
        return [_remote(half_of(p, 1 - c), r, sems, t, (x, y, 1 - c)) for t, (p, r) in enumerate(zip(p_refs, r_refs))]

    return (tuple(parts), tuple(out_shape(p) for p in parts), len(parts), build, False)


def _gather_squares(fulls, over_ici):
    half_up = UP_ROWS // 2

    def build(_, f_refs, sems):
        x, y, c, others = _place()
        copies = []
        for t, f in enumerate(f_refs):
            for r, chip in enumerate(others):
                src_chip = 2 * x + y if over_ici else _chip_index(chip)
                rows = f.at[pl.ds(src_chip * UP_ROWS + c * half_up, half_up), :]
                copies.append(_remote(rows, rows, sems, 3 * t + r, (*chip, c) if over_ici else (x, y, 1 - c)))
        return copies

    return (tuple(fulls), tuple(jax.ShapeDtypeStruct(f.shape, f.dtype) for f in fulls), 3 * len(fulls), build, True)


def _send_whole(parts):
    def build(p_refs, r_refs, sems):
        x, y, c, _ = _place()
        return [_remote(p, r, sems, t, (x, y, 1 - c)) for t, (p, r) in enumerate(zip(p_refs, r_refs))]

    return (tuple(parts), tuple(jax.ShapeDtypeStruct(p.shape, p.dtype) for p in parts), len(parts), build, False)


def _pair_add_in(p_own, r_in, rows=256, cols=SLAB):
    half, P = p_own.shape
    rows = min(rows, half)

    def body(p_ref, r_ref, o_ref):
        o_ref[...] = (p_ref[...].astype(F32) + r_ref[...].astype(F32)).astype(BF16)

    spec = pl.BlockSpec((rows, cols), lambda i, j: (i, j))
    return pl.pallas_call(body, name="pair_add_in", out_shape=jax.ShapeDtypeStruct((half, P), BF16),
                          grid=(half // rows, P // cols), in_specs=[spec, spec], out_specs=spec,
                          compiler_params=_params("parallel", "parallel"))(p_own, r_in)


def _pair_add_up(idx, p_up, r_up, name):
    _, _, R, C = p_up.shape

    def body(idx_ref, p_ref, r_ref, o_ref):
        o_ref[...] = (p_ref[...].astype(F32) + r_ref[...].astype(F32)).astype(BF16)

    grid_spec = pltpu.PrefetchScalarGridSpec(
        num_scalar_prefetch=1, grid=(N_CHIPS,),
        in_specs=[pl.BlockSpec((1, 1, R, C), lambda j, idx: (j, idx[0], 0, 0)),
                  pl.BlockSpec((1, 1, R, C), lambda j, idx: (j, 0, 0, 0))],
        out_specs=pl.BlockSpec((1, 1, R, C), lambda j, idx: (j, 0, 0, 0)))
    return pl.pallas_call(body, name=name, out_shape=jax.ShapeDtypeStruct((N_CHIPS, 1, R, C), BF16), grid_spec=grid_spec,
                          compiler_params=_params("parallel"))(idx, p_up, r_up)


def _slab_exchange(qs):
    n = len(qs)

    def out_shape(q):
        if q.ndim == 2:
            return jax.ShapeDtypeStruct((3, q.shape[0], SLAB), BF16)
        return jax.ShapeDtypeStruct((3,) + q.shape[1:], BF16)

    def build(q_refs, r_refs, sems):
        _, _, c, others = _place()
        copies = []
        for r, chip in enumerate(others):
            ci = _chip_index(chip)
            for t in range(n):
                src = q_refs[t].at[:, pl.ds(ci * SLAB, SLAB)] if len(q_refs[t].shape) == 2 else q_refs[t].at[ci]
                copies.append(_remote(src, r_refs[t].at[r], sems, 3 * t + r, (*chip, c)))
        return copies

    return (tuple(qs), tuple(out_shape(q) for q in qs), 3 * n, build, False)


def _slab_add_in(idx, q_in, r2_in, rows=128):
    half = q_in.shape[0]
    rows = min(rows, half)
    nrb = half // rows

    def body(idx_ref, q_ref, r_ref, o_ref):
        o_ref[...] = ((q_ref[...].astype(F32) + r_ref[0].astype(F32)) + r_ref[1].astype(F32)) + r_ref[2].astype(F32)

    grid_spec = pltpu.PrefetchScalarGridSpec(
        num_scalar_prefetch=1, grid=(nrb,),
        in_specs=[pl.BlockSpec((rows, SLAB), lambda i, idx: (i, idx[1])),
                  pl.BlockSpec((3, rows, SLAB), lambda i, idx: (0, i, 0))],
        out_specs=pl.BlockSpec((rows, SLAB), lambda i, idx: (idx[0] * nrb + i, 0)))
    return pl.pallas_call(body, name="slab_add_in", out_shape=jax.ShapeDtypeStruct((2 * half, SLAB), F32),
                          grid_spec=grid_spec, compiler_params=_params("parallel"))(idx, q_in, r2_in)


def _slab_add_up(idx, q_up, r2_up, name, rows=128):
    _, _, R, C = q_up.shape
    rows = min(rows, R)
    nrb = R // rows

    def body(idx_ref, q_ref, r_ref, o_ref):
        o_ref[...] = ((q_ref[0, 0].astype(F32) + r_ref[0, 0].astype(F32)) + r_ref[1, 0].astype(F32)) + r_ref[2, 0].astype(F32)

    grid_spec = pltpu.PrefetchScalarGridSpec(
        num_scalar_prefetch=1, grid=(nrb,),
        in_specs=[pl.BlockSpec((1, 1, rows, C), lambda i, idx: (idx[1], 0, i, 0)),
                  pl.BlockSpec((3, 1, rows, C), lambda i, idx: (0, 0, i, 0))],
        out_specs=pl.BlockSpec((rows, C), lambda i, idx: (idx[0] * nrb + i, 0)))
    return pl.pallas_call(body, name=name, out_shape=jax.ShapeDtypeStruct((2 * R, C), F32), grid_spec=grid_spec,
                          compiler_params=_params("parallel"))(idx, q_up, r2_up)


def _pair_share(gs):
    def build(_, g_refs, sems):
        x, y, c, _ = _place()
        copies = []
        for t, g in enumerate(g_refs):
            rows = g.shape[0] // 2
            mine = g.at[pl.ds(c * rows, rows), :]
            copies.append(_remote(mine, mine, sems, t, (x, y, 1 - c)))
        return copies

    return (tuple(gs), tuple(jax.ShapeDtypeStruct(g.shape, g.dtype) for g in gs), len(gs), build, True)


def _all_reduce_small(packed):
    R, C = packed.shape
    N_DEV = 2 * N_CHIPS

    def body(x_ref, out_ref, all_ref, send_sems, recv_sems, local_sem):
        x, y, c, others = _place()
        me, sib = (x, y, c), (x, y, 1 - c)
        sems = (send_sems, recv_sems)

        def rows(px, py, pc):
            return all_ref.at[4 * px + 2 * py + pc]

        mine = pltpu.make_async_copy(x_ref, rows(*me), local_sem)
        mine.start()
        first = [_remote(x_ref, rows(*me), sems, 0, sib)]
        first += [_remote(x_ref, rows(*me), sems, 1 + j, (*chip, c)) for j, chip in enumerate(others)]
        for cp in first:
            cp.start()
        passed = [_remote(rows(*chip, c), rows(*chip, c), sems, 4 + j, sib) for j, chip in enumerate(others)]
        for j, chip in enumerate(others):
            _remote(rows(*chip, c), rows(*chip, c), sems, 1 + j, me).wait_recv()
            passed[j].start()
        _remote(rows(*sib), rows(*sib), sems, 0, me).wait_recv()
        for j, chip in enumerate(others):
            _remote(rows(*chip, 1 - c), rows(*chip, 1 - c), sems, 4 + j, me).wait_recv()
        for cp in first + passed:
            cp.wait_send()
        mine.wait()
        total = all_ref[0]
        for d in range(1, N_DEV):
            total = total + all_ref[d]
        out_ref[...] = total

    vmem = pl.BlockSpec(memory_space=pltpu.VMEM)
    return pl.pallas_call(
        body, name="all_reduce_small", out_shape=jax.ShapeDtypeStruct((R, C), F32),
        in_specs=[vmem], out_specs=vmem,
        scratch_shapes=[pltpu.VMEM((N_DEV, R, C), F32), pltpu.SemaphoreType.DMA((7,)), pltpu.SemaphoreType.DMA((7,)),
                        pltpu.SemaphoreType.DMA],
        compiler_params=pltpu.CompilerParams(vmem_limit_bytes=VMEM_LIMIT),
    )(packed)


SMALL_NAMES = ("w_spatial", "b_spatial", "norm_g", "gmlp_ln_g", "gmlp_ln_b", "final_norm_g", "attn_sink")


def _pack_small(parts, extra=None):
    blocks = []
    for n in SMALL_NAMES:
        flat = parts[n].reshape(-1).astype(F32)
        rows = -(-flat.shape[0] // (8 * LANES)) * 8
        flat = jnp.pad(flat, (0, rows * LANES - flat.shape[0]))
        blocks.append(flat.reshape(rows, LANES))
    blocks.append(jnp.zeros((8, LANES), F32) if extra is None else extra)
    return jnp.concatenate(blocks, axis=0)


def _unpack_small(packed, shapes):
    out, r = {}, 0
    for n in SMALL_NAMES:
        size = 1
        for s in shapes[n]:
            size *= s
        rows = -(-size // (8 * LANES)) * 8
        out[n] = packed[r:r + rows].reshape(-1)[:size].reshape(shapes[n])
        r += rows
    return out, packed[r:r + 8]


def kernel(x, positions, norm_g, w_in, attn_sink, gmlp_ln_g, gmlp_ln_b, w_spatial, b_spatial, w_up_attn, w_up_gmlp, w_out, final_norm_g, loss_target, m_norm_g, m_w_in, m_attn_sink, m_gmlp_ln_g, m_gmlp_ln_b, m_w_spatial, m_b_spatial, m_w_up_attn, m_w_up_gmlp, m_w_out, m_final_norm_g, v_norm_g, v_w_in, v_attn_sink, v_gmlp_ln_g, v_gmlp_ln_b, v_w_spatial, v_b_spatial, v_w_up_attn, v_w_up_gmlp, v_w_out, v_final_norm_g):
    xs, tgt = x[0], loss_target[0]
    mx, my, mc = lax.axis_index("x"), lax.axis_index("y"), lax.axis_index("c")
    idx = jnp.stack([mc, 2 * mx + my]).astype(jnp.int32)
    arrival = jnp.stack([2 * mx + my, 2 * (1 - mx) + my, 2 * mx + (1 - my), 2 * (1 - mx) + (1 - my)]).astype(jnp.int32)

    square = (D_MODEL, D_MODEL)
    w_full = _cast_into(idx, w_in[0], (D_MODEL, PROJ_WIDTH), "cast_w_in")
    ups = [_cast_into(idx, w_up_attn[0], square, "cast_w_up_attn"), _cast_into(idx, w_up_gmlp[0], square, "cast_w_up_gmlp"),
           _cast_into(idx, w_out[0], square, "cast_w_out")]

    tables = _rope_tables(positions[0])
    bias = _band_bias()
    sink = attn_sink[0]
    ws, bs_t = w_spatial[0], b_spatial[0].T
    h = _rms_fwd(xs, norm_g)
    proj, w_full = _proj_gather(arrival, h, w_full)
    attn, a_in, *got = _attn_fwd(proj, tables, bias, sink, comm=_gather_squares(ups[:2], over_ici=True))
    b_in, wua, wug = _gmlp_fwd(proj, ws, bs_t, gmlp_ln_g, gmlp_ln_b, comm=_gather_squares(got, over_ici=False))
    y_a, wo = _matmul(a_in, wua, mode="nn", out_dtype=F32, name="up_attn", comm=_gather_squares(ups[2:], over_ici=True))
    y_b, wo = _matmul(b_in, wug, mode="nn", out_dtype=F32, name="up_gmlp", comm=_gather_squares([wo], over_ici=False))
    merged = _merge_fwd(proj, y_a, y_b)
    x2 = _matmul(merged, wo, mode="nn", out_dtype=F32, name="out_proj", residual=xs, tk=2048)
    dx2, dx2_b, d_fg, loss_part = _loss_head(x2, tgt, final_norm_g.reshape(1, D_MODEL))

    dmerged = _matmul(dx2_b, wo, mode="nt", out_dtype=F32, name="d_merged")
    dy_a, dy_b, d_gates = _merge_bwd(proj, y_a, y_b, dmerged)

    p_ups = [_matmul(a_in, dy_a, mode="tn", out_dtype=BF16, name="dw_up_attn"),
             _matmul(b_in, dy_b, mode="tn", out_dtype=BF16, name="dw_up_gmlp"),
             _matmul(merged, dx2_b, mode="tn", out_dtype=BF16, name="dw_out")]
    p_ups = [p.reshape(N_CHIPS, 2, UP_ROWS // 2, D_MODEL) for p in p_ups]
    da_in, *r1_ups = _matmul(dy_a, wua, mode="nt", out_dtype=F32, name="d_a_in", comm=_pair_halves(p_ups))
    q_ups = [_pair_add_up(idx, p, r, "pair_add_up%d" % t) for t, (p, r) in enumerate(zip(p_ups, r1_ups))]
    db_in = _matmul(dy_b, wug, mode="nt", out_dtype=F32, name="d_b_in")
    dq, dk, dv, dga, d_sink = _attn_bwd(proj, tables, bias, sink, attn, da_in)
    d_gmlp, d_ws, d_bs, d_lg, d_lb = _gmlp_bwd(proj, ws, bs_t, gmlp_ln_g, gmlp_ln_b, db_in)
    dproj = jnp.concatenate([dq, dk, dv, dga, d_gmlp, d_gates], axis=1)

    half = D_MODEL // 2
    h_sib = lax.dynamic_slice(h, (0, (1 - mc) * half), (h.shape[0], half))
    h_own = lax.dynamic_slice(h, (0, mc * half), (h.shape[0], half))
    p_sib, *r2_ups = _matmul(h_sib, dproj, mode="tn", out_dtype=BF16, name="dw_in_sibling", comm=_slab_exchange(q_ups))
    p_own, r1_in = _matmul(h_own, dproj, mode="tn", out_dtype=BF16, name="dw_in_own", comm=_send_whole([p_sib]))
    q_in = _pair_add_in(p_own, r1_in)
    dh, r2_in = _matmul(dproj, w_full, mode="nt", out_dtype=F32, name="d_h", comm=_slab_exchange([q_in]), tk=SLAB // 2)
    grad_x, d_ng = _rms_bwd(xs, dh, dx2, norm_g)

    g_half = [_slab_add_in(idx, q_in, r2_in)] + [_slab_add_up(idx, q, r, "slab_add_up%d" % t)
                                                 for t, (q, r) in enumerate(zip(q_ups, r2_ups))]
    g_big = _exchange(_pair_share(g_half), "pair_share")

    small_shapes = {"w_spatial": w_spatial.shape, "b_spatial": b_spatial.shape, "norm_g": norm_g.shape,
                    "gmlp_ln_g": gmlp_ln_g.shape, "gmlp_ln_b": gmlp_ln_b.shape, "final_norm_g": final_norm_g.shape,
                    "attn_sink": attn_sink.shape}
    d_small = {"w_spatial": d_ws, "b_spatial": d_bs, "norm_g": d_ng, "gmlp_ln_g": d_lg, "gmlp_ln_b": d_lb,
               "final_norm_g": d_fg, "attn_sink": d_sink[:, 0, :HEADS_PER_STEP]}
    g_small = _all_reduce_small(_pack_small(d_small, loss_part))

    w_small = _pack_small(dict(w_spatial=w_spatial, b_spatial=b_spatial, norm_g=norm_g, gmlp_ln_g=gmlp_ln_g,
                               gmlp_ln_b=gmlp_ln_b, final_norm_g=final_norm_g, attn_sink=attn_sink))
    m_small = _pack_small(dict(w_spatial=m_w_spatial, b_spatial=m_b_spatial, norm_g=m_norm_g, gmlp_ln_g=m_gmlp_ln_g,
                               gmlp_ln_b=m_gmlp_ln_b, final_norm_g=m_final_norm_g, attn_sink=m_attn_sink))
    v_small = _pack_small(dict(w_spatial=v_w_spatial, b_spatial=v_b_spatial, norm_g=v_norm_g, gmlp_ln_g=v_gmlp_ln_g,
                               gmlp_ln_b=v_gmlp_ln_b, final_norm_g=v_final_norm_g, attn_sink=v_attn_sink))
    small_out = _adamw(w_small, g_small, m_small, v_small, "adamw_small", rows=w_small.shape[0])
    big = {
        "w_in": _adamw(w_in[0], g_big[0], m_w_in[0], v_w_in[0], "adamw_w_in"),
        "w_up_attn": _adamw(w_up_attn[0], g_big[1], m_w_up_attn[0], v_w_up_attn[0], "adamw_w_up_attn"),
        "w_up_gmlp": _adamw(w_up_gmlp[0], g_big[2], m_w_up_gmlp[0], v_w_up_gmlp[0], "adamw_w_up_gmlp"),
        "w_out": _adamw(w_out[0], g_big[3], m_w_out[0], v_w_out[0], "adamw_w_out"),
    }

    order = ("norm_g", "w_in", "attn_sink", "gmlp_ln_g", "gmlp_ln_b", "w_spatial", "b_spatial", "w_up_attn", "w_up_gmlp",
             "w_out", "final_norm_g")
    outs = []
    loss = None
    for kind in range(4):
        small, extra = _unpack_small(small_out[kind], small_shapes)
        if kind == 0:
            loss = extra[0, 0]
        for n in order:
            outs.append(big[n][kind][None] if n in big else small[n])
    return (loss, grad_x[None], *outs)
```

```python
import functools

import jax
import jax.numpy as jnp
from jax import lax
from jax.experimental import pallas as pl
from jax.experimental.pallas import tpu as pltpu

F32 = jnp.float32
BF16 = jnp.bfloat16

D_MODEL = 4096
N_Q_HEADS = 64
N_KV_HEADS = 8
HEAD_DIM = 64
Q_PER_KV = N_Q_HEADS // N_KV_HEADS
ATTN_WIDTH = N_Q_HEADS * HEAD_DIM
KV_WIDTH = N_KV_HEADS * HEAD_DIM
WINDOW = 128
BLOCK = 128
ROPE_THETA = 500000.0
ROPE_DIM = HEAD_DIM // 4
ROPE_HALF = ROPE_DIM // 2
GMLP_WIDTH = D_MODEL
GMLP_GROUPS = 8
GMLP_GROUP_DIM = GMLP_WIDTH // GMLP_GROUPS
GMLP_CHUNK = 128
NORM_EPS = 1e-5
LN_EPS = 1e-5

PROJ_SIZES = (ATTN_WIDTH, KV_WIDTH, KV_WIDTH, ATTN_WIDTH, GMLP_WIDTH, GMLP_WIDTH, GMLP_WIDTH, D_MODEL, D_MODEL)
PROJ_WIDTH = sum(PROJ_SIZES)
OFF_Q, OFF_K, OFF_V, OFF_GA, OFF_U, OFF_VG, OFF_GB, OFF_MA, OFF_MB = (
    sum(PROJ_SIZES[:i]) for i in range(len(PROJ_SIZES)))

ADAM_LR = 0.001
ADAM_B1 = 0.9
ADAM_B2 = 0.999
ADAM_EPS = 1e-08
ADAM_WD = 0.01
ADAM_STEP = 10

N_CHIPS = 4
SLAB = PROJ_WIDTH // N_CHIPS
UP_ROWS = D_MODEL // N_CHIPS
LANES = 128
COL_BLK = 1024
HEADS_PER_STEP = 2 * Q_PER_KV
VMEM_LIMIT = 56 * 1024 * 1024

MESH = pl.DeviceIdType.MESH
HBM_SPEC = pl.BlockSpec(memory_space=pltpu.HBM)
SMEM_SPEC = pl.BlockSpec(memory_space=pltpu.SMEM)


def _params(*sem):
    return pltpu.CompilerParams(dimension_semantics=sem, vmem_limit_bytes=VMEM_LIMIT)


def _sigmoid(x):
    return jax.nn.sigmoid(x)


def _gelu(x):
    return jax.nn.gelu(x, approximate=True)


def _gelu_grad(x):
    c = 0.7978845608028654
    inner = c * (x + 0.044715 * x * x * x)
    t = jnp.tanh(inner)
    return 0.5 * (1.0 + t) + 0.5 * x * (1.0 - t * t) * c * (1.0 + 3 * 0.044715 * x * x)


def _matmul(a, b, *, mode, out_dtype, name, residual=None, comm=None, tm=1024, tn=1024, tk=4096):
    if mode == "nn":
        (M, K), N = a.shape, b.shape[1]
    elif mode == "nt":
        (M, K), N = a.shape, b.shape[0]
    else:
        (K, M), N = a.shape, b.shape[1]
    tm, tn, tk = min(tm, M), min(tn, N), min(tk, K)
    assert M % tm == 0 and N % tn == 0 and K % tk == 0
    ni, nj, nk = M // tm, N // tn, K // tk
    if mode == "nn":
        a_spec = pl.BlockSpec((tm, tk), lambda i, j, k: (i, k))
        b_spec = pl.BlockSpec((tk, tn), lambda i, j, k: (k, j))
        dims = (((1,), (0,)), ((), ()))
    elif mode == "nt":
        a_spec = pl.BlockSpec((tm, tk), lambda i, j, k: (i, k))
        b_spec = pl.BlockSpec((tn, tk), lambda i, j, k: (j, k))
        dims = (((1,), (1,)), ((), ()))
    else:
        a_spec = pl.BlockSpec((tk, tm), lambda i, j, k: (k, i))
        b_spec = pl.BlockSpec((tk, tn), lambda i, j, k: (k, j))
        dims = (((0,), (0,)), ((), ()))
    o_spec = pl.BlockSpec((tm, tn), lambda i, j, k: (i, j))
    n_res = 1 if residual is not None else 0
    c_arrays, c_shapes, c_sems, c_build, c_alias = _comm_fields(comm)
    n_cin, n_cout = len(c_arrays), len(c_shapes)

    def body(*refs):
        a_ref, b_ref = refs[0], refs[1]
        r_ref = refs[2] if n_res else None
        cin = refs[2 + n_res:2 + n_res + n_cin]
        o_ref = refs[2 + n_res + n_cin]
        cout = refs[3 + n_res + n_cin:3 + n_res + n_cin + n_cout]
        n_acc = 1 if nk > 1 else 0
        acc = refs[3 + n_res + n_cin + n_cout] if n_acc else None
        sems = refs[3 + n_acc + n_res + n_cin + n_cout:]
        i, j, k = pl.program_id(0), pl.program_id(1), pl.program_id(2)
        if comm is not None:
            _start_when((i == 0) & (j == 0) & (k == 0), c_build(cin, cout, sems))

        def product():
            return lax.dot_general(a_ref[...], b_ref[...], dims, preferred_element_type=F32)

        def finish(r):
            if n_res:
                r = r + r_ref[...]
            o_ref[...] = r.astype(out_dtype)

        if nk == 1:
            finish(product())
        else:
            @pl.when(k == 0)
            def _():
                acc[...] = product()

            @pl.when((k > 0) & (k < nk - 1))
            def _():
                acc[...] += product()

            @pl.when(k == nk - 1)
            def _():
                finish(acc[...] + product())

        if comm is not None:
            _wait_when((i == ni - 1) & (j == nj - 1) & (k == nk - 1), c_build(cin, cout, sems))

    args = (a, b) + ((residual,) if n_res else ()) + tuple(c_arrays)
    in_specs = [a_spec, b_spec] + ([o_spec] if n_res else []) + [HBM_SPEC] * n_cin
    scratch = ([pltpu.VMEM((tm, tn), F32)] if nk > 1 else []) + _comm_scratch(c_sems)
    sem = ("arbitrary",) * 3 if comm is not None else ("parallel", "parallel", "arbitrary")
    aliases = {2 + n_res + t: 1 + t for t in range(n_cin)} if c_alias else {}
    out = pl.pallas_call(
        body, name=name, out_shape=(jax.ShapeDtypeStruct((M, N), out_dtype),) + tuple(c_shapes),
        grid=(ni, nj, nk), in_specs=in_specs, out_specs=(o_spec,) + (HBM_SPEC,) * n_cout,
        scratch_shapes=scratch, input_output_aliases=aliases, compiler_params=_params(*sem),
    )(*args)
    return out if comm is not None else out[0]


def _cast_into(idx, w, full_shape, name, rows=64):
    R, C = w.shape
    rows = min(rows, R)
    nrb = R // rows
    by_cols = C != full_shape[1]

    def body(idx_ref, w_ref, o_ref):
        o_ref[...] = w_ref[...].astype(BF16)

    out_map = (lambda i, idx: (i, idx[1])) if by_cols else (lambda i, idx: (idx[1] * nrb + i, 0))
    grid_spec = pltpu.PrefetchScalarGridSpec(
        num_scalar_prefetch=1, grid=(nrb,), in_specs=[pl.BlockSpec((rows, C), lambda i, idx: (i, 0))],
        out_specs=pl.BlockSpec((rows, C), out_map))
    return pl.pallas_call(body, name=name, out_shape=jax.ShapeDtypeStruct(full_shape, BF16), grid_spec=grid_spec,
                          compiler_params=_params("parallel"))(idx, w)


def _rms_fwd(x, g, rows=256):
    S, D = x.shape
    rows = min(rows, S)

    def body(x_ref, g_ref, h_ref):
        xv = x_ref[...]
        ms = jnp.mean(xv * xv, axis=-1, keepdims=True)
        h_ref[...] = (xv * lax.rsqrt(ms + NORM_EPS) * g_ref[...]).astype(BF16)

    spec = pl.BlockSpec((rows, D), lambda i: (i, 0))
    return pl.pallas_call(body, name="rms_fwd", out_shape=jax.ShapeDtypeStruct((S, D), BF16), grid=(S // rows,),
                          in_specs=[spec, pl.BlockSpec((1, D), lambda i: (0, 0))], out_specs=spec,
                          compiler_params=_params("parallel"))(x, g)


def _rope_tables(positions):
    inv_freq = ROPE_THETA ** (-jnp.arange(ROPE_HALF, dtype=F32) * 2.0 / ROPE_DIM)
    ang = positions.astype(F32)[:, None] * inv_freq
    cos, sin = jnp.cos(ang), jnp.sin(ang)
    S = positions.shape[0]
    rest = HEAD_DIM - ROPE_DIM
    zeros_h, zeros_r = jnp.zeros((S, ROPE_HALF), F32), jnp.zeros((S, rest), F32)
    c = jnp.concatenate([cos, cos, jnp.ones((S, rest), F32)], axis=1)
    s1 = jnp.concatenate([-sin, zeros_h, zeros_r], axis=1)
    s2 = jnp.concatenate([zeros_h, sin, zeros_r], axis=1)
    reps = LANES // HEAD_DIM
    return jnp.tile(c, (1, reps)), jnp.tile(s1, (1, reps)), jnp.tile(s2, (1, reps))


def _rope(t, c, s1, s2, sign):
    n = t.shape[1]
    reps = n // c.shape[1]
    if reps > 1:
        c, s1, s2 = jnp.tile(c, (1, reps)), jnp.tile(s1, (1, reps)), jnp.tile(s2, (1, reps))
    up = pltpu.roll(t, n - ROPE_HALF, 1)
    down = pltpu.roll(t, ROPE_HALF, 1)
    return t * c + sign * (up * s1 + down * s2)


def _attn_specs(nb):
    last = nb - 1

    def cur(i):
        return jnp.minimum(i, last)

    def prev(i):
        return jnp.maximum(jnp.minimum(i, last) - 1, 0)

    kq, kk, kv, kg = OFF_Q // COL_BLK, OFF_K // LANES, OFF_V // LANES, OFF_GA // COL_BLK
    wide = lambda off: pl.BlockSpec((BLOCK, COL_BLK), lambda p, i: (cur(i), off + p))
    kv_cur = lambda off: pl.BlockSpec((BLOCK, LANES), lambda p, i: (cur(i), off + p))
    kv_prev = lambda off: pl.BlockSpec((BLOCK, LANES), lambda p, i: (prev(i), off + p))
    tab_cur = pl.BlockSpec((BLOCK, LANES), lambda p, i: (cur(i), 0))
    tab_prev = pl.BlockSpec((BLOCK, LANES), lambda p, i: (prev(i), 0))
    proj_specs = [wide(kq), kv_cur(kk), kv_prev(kk), kv_cur(kv), kv_prev(kv), wide(kg)]
    table_specs = [tab_cur] * 3 + [tab_prev] * 3
    bias_spec = pl.BlockSpec((1, Q_PER_KV * BLOCK, BLOCK), lambda p, i: (jnp.minimum(i, 1), 0, 0))
    return proj_specs, table_specs + [bias_spec], cur, prev


def _band_bias():
    a = (jnp.arange(Q_PER_KV * BLOCK) % BLOCK)[:, None]
    col = jnp.arange(BLOCK)[None, :]
    first = jnp.where(col > a, -1e30, 0.0)
    return jnp.stack([first, jnp.zeros_like(first)]).astype(F32)


def _from_prev(shape):
    a = lax.broadcasted_iota(jnp.int32, shape, 0) & (BLOCK - 1)
    return lax.broadcasted_iota(jnp.int32, shape, 1) > a


def _softmax_band(q2, kp2, kc2, sink_col, bias, from_prev):
    nt = (((1,), (1,)), ((), ()))
    s = jnp.where(from_prev, lax.dot_general(q2, kp2, nt, preferred_element_type=F32),
                  lax.dot_general(q2, kc2, nt, preferred_element_type=F32)) + bias
    m = jnp.maximum(jnp.max(s, axis=1, keepdims=True), sink_col)
    e = jnp.exp(s - m)
    es = jnp.exp(sink_col - m)
    inv = 1.0 / (jnp.sum(e, axis=1, keepdims=True) + es)
    return e * inv, es * inv


def _sink_col(sink_ref, pair, kvh):
    return jnp.concatenate(
        [jnp.full((BLOCK, 1), sink_ref[(pair * 2 + kvh) * Q_PER_KV + g], F32) for g in range(Q_PER_KV)], axis=0)


def _stack_heads(ref, kvh):
    base = kvh * Q_PER_KV * HEAD_DIM
    return jnp.concatenate([ref[:, base + g * HEAD_DIM: base + (g + 1) * HEAD_DIM] for g in range(Q_PER_KV)], axis=0)


def _unstack_heads(ref, kvh, val):
    base = kvh * Q_PER_KV * HEAD_DIM
    for g in range(Q_PER_KV):
        ref[:, base + g * HEAD_DIM: base + (g + 1) * HEAD_DIM] = val[g * BLOCK:(g + 1) * BLOCK, :].astype(ref.dtype)


def _roped_keys(k_ref, kp_ref, tabs):
    c, s1, s2, cp, s1p, s2p = (t[...] for t in tabs)
    k_cur = _rope(k_ref[...].astype(F32), c, s1, s2, 1.0)
    k_prev = _rope(kp_ref[...].astype(F32), cp, s1p, s2p, 1.0)
    return k_prev.astype(BF16), k_cur.astype(BF16)


def _head(val, kvh):
    return val[:, kvh * HEAD_DIM:(kvh + 1) * HEAD_DIM]


SCALE = HEAD_DIM ** -0.5


def _attn_fwd(proj, tables, bias, sink, comm=None):
    S = proj.shape[0]
    nb = S // BLOCK
    npairs = N_KV_HEADS // 2
    proj_specs, table_specs, _, _ = _attn_specs(nb)
    c_arrays, c_shapes, c_sems, c_build, c_alias = _comm_fields(comm)
    n_in = len(proj_specs) + len(table_specs) + 1

    def body(*refs):
        (q_ref, k_ref, kp_ref, v_ref, vp_ref, ga_ref, c_ref, s1_ref, s2_ref, cp_ref, s1p_ref, s2p_ref, bias_ref,
         sink_ref) = refs[:n_in]
        cin = refs[n_in:n_in + len(c_arrays)]
        attn_ref, ain_ref = refs[n_in + len(c_arrays):n_in + len(c_arrays) + 2]
        cout = refs[n_in + len(c_arrays) + 2:n_in + len(c_arrays) + 2 + len(c_shapes)]
        qr_scr = refs[n_in + len(c_arrays) + 2 + len(c_shapes)]
        sems = refs[n_in + len(c_arrays) + 3 + len(c_shapes):]
        pair, i = pl.program_id(0), pl.program_id(1)
        if comm is not None:
            _start_when((pair == 0) & (i == 0), c_build(cin, cout, sems))
        tabs = (c_ref, s1_ref, s2_ref, cp_ref, s1p_ref, s2p_ref)
        qr_scr[...] = _rope(q_ref[...].astype(F32), c_ref[...], s1_ref[...], s2_ref[...], 1.0) * SCALE
        k_prev, k_cur = _roped_keys(k_ref, kp_ref, tabs)
        v_prev, v_cur = vp_ref[...], v_ref[...]
        from_prev = _from_prev((Q_PER_KV * BLOCK, BLOCK))
        for kvh in range(2):
            q2 = _stack_heads(qr_scr, kvh).astype(BF16)
            p, _ = _softmax_band(q2, _head(k_prev, kvh), _head(k_cur, kvh), _sink_col(sink_ref, pair, kvh), bias_ref[0],
                                 from_prev)
            pb = p.astype(BF16)
            o = (jnp.dot(jnp.where(from_prev, pb, 0), _head(v_prev, kvh), preferred_element_type=F32)
                 + jnp.dot(jnp.where(from_prev, 0, pb), _head(v_cur, kvh), preferred_element_type=F32))
            _unstack_heads(attn_ref, kvh, o)
        ga = ga_ref[...].astype(F32)
        ain_ref[...] = (attn_ref[...] * (ga * _sigmoid(ga))).astype(BF16)
        if comm is not None:
            _wait_when((pair == npairs - 1) & (i == nb - 1), c_build(cin, cout, sems))

    out_spec = pl.BlockSpec((BLOCK, COL_BLK), lambda p, i: (i, p))
    aliases = {n_in + t: 2 + t for t in range(len(c_arrays))} if c_alias else {}
    out = pl.pallas_call(
        body, name="attn_fwd",
        out_shape=(jax.ShapeDtypeStruct((S, ATTN_WIDTH), F32), jax.ShapeDtypeStruct((S, ATTN_WIDTH), BF16)) + tuple(c_shapes),
        grid=(npairs, nb), in_specs=proj_specs + table_specs + [SMEM_SPEC] + [HBM_SPEC] * len(c_arrays),
        out_specs=(out_spec, out_spec) + (HBM_SPEC,) * len(c_shapes),
        scratch_shapes=[pltpu.VMEM((BLOCK, COL_BLK), F32)] + _comm_scratch(c_sems), input_output_aliases=aliases,
        compiler_params=_params("arbitrary", "arbitrary"),
    )(*([proj] * 6), *tables, *tables, bias, sink, *c_arrays)
    return out


def _attn_bwd(proj, tables, bias, sink, attn, da_in):
    S = proj.shape[0]
    nb = S // BLOCK
    proj_specs, table_specs, cur, prev = _attn_specs(nb)
    wide_cur = pl.BlockSpec((BLOCK, COL_BLK), lambda p, i: (cur(i), p))
    kv_out = pl.BlockSpec((BLOCK, LANES), lambda p, i: (jnp.maximum(i - 1, 0), p))

    def body(q_ref, k_ref, kp_ref, v_ref, vp_ref, ga_ref, c_ref, s1_ref, s2_ref, cp_ref, s1p_ref, s2p_ref, bias_ref,
             sink_ref, attn_ref, da_ref, dq_ref, dk_ref, dv_ref, dga_ref, dsink_ref,
             qr_scr, do_scr, dq_scr, dk_scr, dv_scr, carry_k, carry_v, sink_acc):
        pair, i = pl.program_id(0), pl.program_id(1)
        tabs = (c_ref, s1_ref, s2_ref, cp_ref, s1p_ref, s2p_ref)

        @pl.when(i < nb)
        def _():
            ga = ga_ref[...].astype(F32)
            sg = _sigmoid(ga)
            da = da_ref[...]
            dga_ref[...] = (da * attn_ref[...] * (sg * (1.0 + ga * (1.0 - sg)))).astype(BF16)
            do_scr[...] = da * (ga * sg)
            qr_scr[...] = _rope(q_ref[...].astype(F32), c_ref[...], s1_ref[...], s2_ref[...], 1.0) * SCALE
            k_prev, k_cur = _roped_keys(k_ref, kp_ref, tabs)
            v_prev, v_cur = vp_ref[...], v_ref[...]
            from_prev = _from_prev((Q_PER_KV * BLOCK, BLOCK))
            nt, tn = (((1,), (1,)), ((), ())), (((0,), (0,)), ((), ()))
            for kvh in range(2):
                q2 = _stack_heads(qr_scr, kvh).astype(BF16)
                do2 = _stack_heads(do_scr, kvh)
                o2 = _stack_heads(attn_ref, kvh)
                kp2, kc2, vp2, vc2 = _head(k_prev, kvh), _head(k_cur, kvh), _head(v_prev, kvh), _head(v_cur, kvh)
                p, p_sink = _softmax_band(q2, kp2, kc2, _sink_col(sink_ref, pair, kvh), bias_ref[0], from_prev)
                do2b = do2.astype(BF16)
                dp = jnp.where(from_prev, lax.dot_general(do2b, vp2, nt, preferred_element_type=F32),
                               lax.dot_general(do2b, vc2, nt, preferred_element_type=F32))
                delta = jnp.sum(do2 * o2, axis=1, keepdims=True)
                ds = (p * (dp - delta)).astype(BF16)
                pb = p.astype(BF16)
                ds_prev, ds_cur = jnp.where(from_prev, ds, 0), jnp.where(from_prev, 0, ds)
                p_prev, p_cur = jnp.where(from_prev, pb, 0), jnp.where(from_prev, 0, pb)
                dq2 = jnp.dot(ds_prev, kp2, preferred_element_type=F32) + jnp.dot(ds_cur, kc2, preferred_element_type=F32)
                _unstack_heads(dq_scr, kvh, dq2)
                cols = slice(kvh * HEAD_DIM, (kvh + 1) * HEAD_DIM)
                dk_scr[0:BLOCK, cols] = lax.dot_general(ds_prev, q2, tn, preferred_element_type=F32)
                dk_scr[BLOCK:2 * BLOCK, cols] = lax.dot_general(ds_cur, q2, tn, preferred_element_type=F32)
                dv_scr[0:BLOCK, cols] = lax.dot_general(p_prev, do2b, tn, preferred_element_type=F32)
                dv_scr[BLOCK:2 * BLOCK, cols] = lax.dot_general(p_cur, do2b, tn, preferred_element_type=F32)
                contrib = -p_sink * delta

                @pl.when(i == 0)
                def _():
                    sink_acc[kvh] = contrib

                @pl.when(i > 0)
                def _():
                    sink_acc[kvh] += contrib

            dq_ref[...] = _rope(dq_scr[...] * SCALE, c_ref[...], s1_ref[...], s2_ref[...], -1.0).astype(BF16)
            dk_prev = _rope(dk_scr[0:BLOCK, :], cp_ref[...], s1p_ref[...], s2p_ref[...], -1.0)
            dk_cur = _rope(dk_scr[BLOCK:2 * BLOCK, :], c_ref[...], s1_ref[...], s2_ref[...], -1.0)
            dv_prev, dv_cur = dv_scr[0:BLOCK, :], dv_scr[BLOCK:2 * BLOCK, :]

            @pl.when(i > 0)
            def _():
                dk_ref[...] = (carry_k[...] + dk_prev).astype(BF16)
                dv_ref[...] = (carry_v[...] + dv_prev).astype(BF16)

            carry_k[...] = dk_cur
            carry_v[...] = dv_cur

        @pl.when(i == nb)
        def _():
            dk_ref[...] = carry_k[...].astype(BF16)
            dv_ref[...] = carry_v[...].astype(BF16)
            lane = lax.broadcasted_iota(jnp.int32, (8, LANES), 1)
            acc = jnp.zeros((8, LANES), F32)
            for kvh in range(2):
                for g in range(Q_PER_KV):
                    val = jnp.sum(sink_acc[kvh, g * BLOCK:(g + 1) * BLOCK, :], axis=0, keepdims=True)
                    acc = jnp.where(lane == kvh * Q_PER_KV + g, val, acc)
            dsink_ref[0] = acc

    return pl.pallas_call(
        body, name="attn_bwd",
        out_shape=(jax.ShapeDtypeStruct((S, ATTN_WIDTH), BF16), jax.ShapeDtypeStruct((S, KV_WIDTH), BF16),
                   jax.ShapeDtypeStruct((S, KV_WIDTH), BF16), jax.ShapeDtypeStruct((S, ATTN_WIDTH), BF16),
                   jax.ShapeDtypeStruct((N_KV_HEADS // 2, 8, LANES), F32)),
        grid=(N_KV_HEADS // 2, nb + 1),
        in_specs=proj_specs + table_specs + [SMEM_SPEC, wide_cur, wide_cur],
        out_specs=(wide_cur, kv_out, kv_out, wide_cur, pl.BlockSpec((1, 8, LANES), lambda p, i: (p, 0, 0))),
        scratch_shapes=[pltpu.VMEM((BLOCK, COL_BLK), F32), pltpu.VMEM((BLOCK, COL_BLK), F32),
                        pltpu.VMEM((BLOCK, COL_BLK), F32), pltpu.VMEM((2 * BLOCK, LANES), F32),
                        pltpu.VMEM((2 * BLOCK, LANES), F32), pltpu.VMEM((BLOCK, LANES), F32),
                        pltpu.VMEM((BLOCK, LANES), F32), pltpu.VMEM((2, Q_PER_KV * BLOCK, 1), F32)],
        compiler_params=_params("arbitrary", "arbitrary"),
    )(*([proj] * 6), *tables, *tables, bias, sink, attn, da_in)


N_PARTS = GMLP_WIDTH // COL_BLK
GROUPS_PER_PART = COL_BLK // GMLP_GROUP_DIM


def _part_specs(off):
    return [pl.BlockSpec((GMLP_CHUNK, COL_BLK), functools.partial(lambda j, i: (i, j), off // COL_BLK + k))
            for k in range(N_PARTS)]


def _group(refs, g):
    lo = (g % GROUPS_PER_PART) * GMLP_GROUP_DIM
    return refs[g // GROUPS_PER_PART][:, lo:lo + GMLP_GROUP_DIM].astype(F32)


def _gmlp_norm_stats(vg_refs, gv_scr):
    total = jnp.zeros((GMLP_CHUNK, 1), F32)
    for k in range(N_PARTS):
        gv = _gelu(vg_refs[k][...].astype(F32))
        gv_scr[:, k * COL_BLK:(k + 1) * COL_BLK] = gv
        total = total + jnp.sum(gv, axis=1, keepdims=True)
    mu = total / GMLP_WIDTH
    xc = gv_scr[...] - mu
    var = jnp.sum(xc * xc, axis=1, keepdims=True) / GMLP_WIDTH
    return mu, lax.rsqrt(var + LN_EPS)


def _tril_bf16(ws_ref, g):
    t = lax.broadcasted_iota(jnp.int32, (GMLP_CHUNK, GMLP_CHUNK), 0)
    s = lax.broadcasted_iota(jnp.int32, (GMLP_CHUNK, GMLP_CHUNK), 1)
    return jnp.where(s <= t, ws_ref[g], 0.0).astype(BF16), s <= t


def _gmlp_fwd(proj, ws, bs_t, lg, lb, comm=None):
    S = proj.shape[0]
    nb = S // GMLP_CHUNK
    c_arrays, c_shapes, c_sems, c_build, c_alias = _comm_fields(comm)
    n_cin, n_cout = len(c_arrays), len(c_shapes)

    def body(*refs):
        u_refs, vg_refs, gb_refs = refs[0:4], refs[4:8], refs[8:12]
        ws_ref, bst_ref, lg_ref, lb_ref = refs[12:16]
        cin, out_ref, cout = refs[16:16 + n_cin], refs[16 + n_cin], refs[17 + n_cin:17 + n_cin + n_cout]
        gv_scr, sems = refs[17 + n_cin + n_cout], refs[18 + n_cin + n_cout:]
        if comm is not None:
            _start_when(pl.program_id(0) == 0, c_build(cin, cout, sems))
        mu, rstd = _gmlp_norm_stats(vg_refs, gv_scr)
        for g in range(GMLP_GROUPS):
            cols = slice(g * GMLP_GROUP_DIM, (g + 1) * GMLP_GROUP_DIM)
            vn = (gv_scr[:, cols] - mu) * rstd * lg_ref[:, cols] + lb_ref[:, cols]
            w, _ = _tril_bf16(ws_ref, g)
            mixed = jnp.dot(w, vn.astype(BF16), preferred_element_type=F32) + bst_ref[:, g:g + 1]
            gb = _group(gb_refs, g)
            out_ref[:, cols] = (_gelu(_group(u_refs, g)) * mixed * (gb * _sigmoid(gb))).astype(BF16)
        if comm is not None:
            _wait_when(pl.program_id(0) == nb - 1, c_build(cin, cout, sems))

    full = lambda shape: pl.BlockSpec(shape, lambda i: tuple(0 for _ in shape))
    aliases = {16 + t: 1 + t for t in range(n_cin)} if c_alias else {}
    out = pl.pallas_call(
        body, name="gmlp_fwd", out_shape=(jax.ShapeDtypeStruct((S, GMLP_WIDTH), BF16),) + tuple(c_shapes), grid=(nb,),
        in_specs=_part_specs(OFF_U) + _part_specs(OFF_VG) + _part_specs(OFF_GB)
        + [full(ws.shape), full(bs_t.shape), full(lg.shape), full(lb.shape)] + [HBM_SPEC] * n_cin,
        out_specs=(pl.BlockSpec((GMLP_CHUNK, GMLP_WIDTH), lambda i: (i, 0)),) + (HBM_SPEC,) * n_cout,
        scratch_shapes=[pltpu.VMEM((GMLP_CHUNK, GMLP_WIDTH), F32)] + _comm_scratch(c_sems), input_output_aliases=aliases,
        compiler_params=_params("arbitrary" if comm is not None else "parallel"),
    )(*([proj] * 12), ws, bs_t, lg, lb, *c_arrays)
    return out if comm is not None else out[0]


def _gmlp_bwd(proj, ws, bs_t, lg, lb, db_in):
    S = proj.shape[0]
    nb = S // GMLP_CHUNK
    W = GMLP_WIDTH

    def body(*refs):
        u_refs, vg_refs, gb_refs = refs[0:4], refs[4:8], refs[8:12]
        ws_ref, bst_ref, lg_ref, lb_ref, db_ref = refs[12:17]
        out_ref, dws_ref, dbs_ref, dlg_ref, dlb_ref = refs[17:22]
        gv_scr, dvh_scr = refs[22:]
        i = pl.program_id(0)

        @pl.when(i == 0)
        def _():
            dws_ref[...] = jnp.zeros_like(dws_ref)
            dbs_ref[...] = jnp.zeros_like(dbs_ref)
            dlg_ref[...] = jnp.zeros_like(dlg_ref)
            dlb_ref[...] = jnp.zeros_like(dlb_ref)

        mu, rstd = _gmlp_norm_stats(vg_refs, gv_scr)
        sum_dvh = jnp.zeros((GMLP_CHUNK, 1), F32)
        sum_dvh_vh = jnp.zeros((GMLP_CHUNK, 1), F32)
        for g in range(GMLP_GROUPS):
            cols = slice(g * GMLP_GROUP_DIM, (g + 1) * GMLP_GROUP_DIM)
            vhat = (gv_scr[:, cols] - mu) * rstd
            vn = (vhat * lg_ref[:, cols] + lb_ref[:, cols]).astype(BF16)
            w, tril = _tril_bf16(ws_ref, g)
            mixed = jnp.dot(w, vn, preferred_element_type=F32) + bst_ref[:, g:g + 1]
            u, gb, db = _group(u_refs, g), _group(gb_refs, g), db_ref[:, cols]
            gu, sgb = _gelu(u), _sigmoid(gb)
            dsg = db * (gb * sgb)
            out_ref[:, 2 * W + g * GMLP_GROUP_DIM: 2 * W + (g + 1) * GMLP_GROUP_DIM] = (
                db * (gu * mixed) * (sgb * (1.0 + gb * (1.0 - sgb)))).astype(BF16)
            out_ref[:, cols] = (dsg * mixed * _gelu_grad(u)).astype(BF16)
            dmixed = dsg * gu
            dmixed_b = dmixed.astype(BF16)
            dvn = lax.dot_general(w, dmixed_b, (((0,), (0,)), ((), ())), preferred_element_type=F32)
            dw = lax.dot_general(dmixed_b, vn, (((1,), (1,)), ((), ())), preferred_element_type=F32)
            dws_ref[g] += jnp.where(tril, dw, 0.0)
            dbs_ref[g] += jnp.sum(dmixed, axis=1, keepdims=True)
            dlg_ref[:, cols] += jnp.sum(dvn * vhat, axis=0, keepdims=True)
            dlb_ref[:, cols] += jnp.sum(dvn, axis=0, keepdims=True)
            dvh = dvn * lg_ref[:, cols]
            dvh_scr[:, cols] = dvh
            sum_dvh = sum_dvh + jnp.sum(dvh, axis=1, keepdims=True)
            sum_dvh_vh = sum_dvh_vh + jnp.sum(dvh * vhat, axis=1, keepdims=True)
        m1, m2 = sum_dvh / W, sum_dvh_vh / W
        for k in range(N_PARTS):
            cols = slice(k * COL_BLK, (k + 1) * COL_BLK)
            vhat = (gv_scr[:, cols] - mu) * rstd
            dgv = rstd * (dvh_scr[:, cols] - m1 - vhat * m2)
            out_ref[:, W + k * COL_BLK: W + (k + 1) * COL_BLK] = (
                dgv * _gelu_grad(vg_refs[k][...].astype(F32))).astype(BF16)

    full = lambda shape: pl.BlockSpec(shape, lambda i: tuple(0 for _ in shape))
    row = pl.BlockSpec((GMLP_CHUNK, W), lambda i: (i, 0))
    return pl.pallas_call(
        body, name="gmlp_bwd",
        out_shape=(jax.ShapeDtypeStruct((S, 3 * W), BF16), jax.ShapeDtypeStruct(ws.shape, F32),
                   jax.ShapeDtypeStruct((GMLP_GROUPS, GMLP_CHUNK, 1), F32), jax.ShapeDtypeStruct((1, W), F32),
                   jax.ShapeDtypeStruct((1, W), F32)),
        grid=(nb,),
        in_specs=_part_specs(OFF_U) + _part_specs(OFF_VG) + _part_specs(OFF_GB)
        + [full(ws.shape), full(bs_t.shape), full(lg.shape), full(lb.shape), row],
        out_specs=(pl.BlockSpec((GMLP_CHUNK, 3 * W), lambda i: (i, 0)), full(ws.shape),
                   full((GMLP_GROUPS, GMLP_CHUNK, 1)), full((1, W)), full((1, W))),
        scratch_shapes=[pltpu.VMEM((GMLP_CHUNK, W), F32), pltpu.VMEM((GMLP_CHUNK, W), F32)],
        compiler_params=_params("arbitrary"),
    )(*([proj] * 12), ws, bs_t, lg, lb, db_in)


def _merge_fwd(proj, y_a, y_b, rows=256):
    S = proj.shape[0]
    rows = min(rows, S)

    def body(ma_ref, mb_ref, ya_ref, yb_ref, out_ref):
        out_ref[...] = (_sigmoid(ma_ref[...].astype(F32)) * ya_ref[...]
                        + _sigmoid(mb_ref[...].astype(F32)) * yb_ref[...]).astype(BF16)

    blk = lambda off: pl.BlockSpec((rows, COL_BLK), lambda i, j: (i, off // COL_BLK + j))
    return pl.pallas_call(
        body, name="merge_fwd", out_shape=jax.ShapeDtypeStruct((S, D_MODEL), BF16), grid=(S // rows, D_MODEL // COL_BLK),
        in_specs=[blk(OFF_MA), blk(OFF_MB), blk(0), blk(0)], out_specs=blk(0),
        compiler_params=_params("parallel", "parallel"),
    )(proj, proj, y_a, y_b)


def _merge_bwd(proj, y_a, y_b, dmerged, rows=128):
    S = proj.shape[0]
    rows = min(rows, S)
    nj = D_MODEL // COL_BLK

    def body(*refs):
        ma_refs, mb_refs = refs[0:nj], refs[nj:2 * nj]
        ya_ref, yb_ref, dm_ref, dya_ref, dyb_ref, dg_ref = refs[2 * nj:]
        for k in range(nj):
            cols = slice(k * COL_BLK, (k + 1) * COL_BLK)
            dm = dm_ref[:, cols]
            sa, sb = _sigmoid(ma_refs[k][...].astype(F32)), _sigmoid(mb_refs[k][...].astype(F32))
            dya_ref[:, cols] = (dm * sa).astype(BF16)
            dyb_ref[:, cols] = (dm * sb).astype(BF16)
            dg_ref[:, cols] = (dm * ya_ref[:, cols] * sa * (1.0 - sa)).astype(BF16)
            dg_ref[:, D_MODEL + k * COL_BLK: D_MODEL + (k + 1) * COL_BLK] = (
                dm * yb_ref[:, cols] * sb * (1.0 - sb)).astype(BF16)

    part = lambda off: [pl.BlockSpec((rows, COL_BLK), functools.partial(lambda j, i: (i, j), off // COL_BLK + k))
                        for k in range(nj)]
    row = pl.BlockSpec((rows, D_MODEL), lambda i: (i, 0))
    return pl.pallas_call(
        body, name="merge_bwd",
        out_shape=(jax.ShapeDtypeStruct((S, D_MODEL), BF16), jax.ShapeDtypeStruct((S, D_MODEL), BF16),
                   jax.ShapeDtypeStruct((S, 2 * D_MODEL), BF16)),
        grid=(S // rows,), in_specs=part(OFF_MA) + part(OFF_MB) + [row, row, row],
        out_specs=(row, row, pl.BlockSpec((rows, 2 * D_MODEL), lambda i: (i, 0))),
        compiler_params=_params("parallel"),
    )(*([proj] * (2 * nj)), y_a, y_b, dmerged)


def _loss_head(x2, target, fg, rows=128):
    S, D = x2.shape
    rows = min(rows, S)

    def body(x_ref, t_ref, g_ref, dx_ref, dxb_ref, dg_ref, loss_ref):
        i = pl.program_id(0)

        @pl.when(i == 0)
        def _():
            dg_ref[...] = jnp.zeros_like(dg_ref)
            loss_ref[...] = jnp.zeros_like(loss_ref)

        xv, g = x_ref[...], g_ref[...]
        rstd = lax.rsqrt(jnp.mean(xv * xv, axis=-1, keepdims=True) + NORM_EPS)
        xhat = xv * rstd
        err = xhat * g - t_ref[...]
        loss_ref[...] += (0.5 / D) * jnp.sum(err * err)
        dy = err * (1.0 / D)
        dg_ref[...] += jnp.sum(dy * xhat, axis=0, keepdims=True)
        dxh = dy * g
        dx = rstd * (dxh - xhat * jnp.mean(dxh * xhat, axis=-1, keepdims=True))
        dx_ref[...] = dx
        dxb_ref[...] = dx.astype(BF16)

    row = pl.BlockSpec((rows, D), lambda i: (i, 0))
    vec = pl.BlockSpec((1, D), lambda i: (0, 0))
    return pl.pallas_call(
        body, name="loss_head",
        out_shape=(jax.ShapeDtypeStruct((S, D), F32), jax.ShapeDtypeStruct((S, D), BF16),
                   jax.ShapeDtypeStruct((1, D), F32), jax.ShapeDtypeStruct((8, LANES), F32)),
        grid=(S // rows,), in_specs=[row, row, vec],
        out_specs=(row, row, vec, pl.BlockSpec((8, LANES), lambda i: (0, 0))),
        compiler_params=_params("arbitrary"),
    )(x2, target, fg)


def _rms_bwd(x, dh, dx2, g, rows=128):
    S, D = x.shape
    rows = min(rows, S)

    def body(x_ref, dh_ref, dx2_ref, g_ref, gx_ref, dg_ref):
        i = pl.program_id(0)

        @pl.when(i == 0)
        def _():
            dg_ref[...] = jnp.zeros_like(dg_ref)

        xv, dh_v = x_ref[...], dh_ref[...]
        rstd = lax.rsqrt(jnp.mean(xv * xv, axis=-1, keepdims=True) + NORM_EPS)
        xhat = xv * rstd
        dg_ref[...] += jnp.sum(dh_v * xhat, axis=0, keepdims=True)
        dxh = dh_v * g_ref[...]
        gx_ref[...] = dx2_ref[...] + rstd * (dxh - xhat * jnp.mean(dxh * xhat, axis=-1, keepdims=True))

    row = pl.BlockSpec((rows, D), lambda i: (i, 0))
    vec = pl.BlockSpec((1, D), lambda i: (0, 0))
    return pl.pallas_call(
        body, name="rms_bwd", out_shape=(jax.ShapeDtypeStruct((S, D), F32), jax.ShapeDtypeStruct((1, D), F32)),
        grid=(S // rows,), in_specs=[row, row, row, vec], out_specs=(row, vec),
        compiler_params=_params("arbitrary"),
    )(x, dh, dx2, g)


def _adamw(w, g, m, v, name, rows=64):
    R, C = w.shape
    rows = min(rows, R)
    c1 = 1.0 - ADAM_B1 ** ADAM_STEP
    c2 = 1.0 - ADAM_B2 ** ADAM_STEP

    def body(w_ref, g_ref, m_ref, v_ref, go_ref, d_ref, mo_ref, vo_ref):
        gv = g_ref[...]
        mn = ADAM_B1 * m_ref[...] + (1.0 - ADAM_B1) * gv
        vn = ADAM_B2 * v_ref[...] + (1.0 - ADAM_B2) * (gv * gv)
        go_ref[...] = gv
        mo_ref[...] = mn
        vo_ref[...] = vn
        d_ref[...] = -ADAM_LR * ((mn / c1) / (jnp.sqrt(vn / c2) + ADAM_EPS) + ADAM_WD * w_ref[...])

    spec = pl.BlockSpec((rows, C), lambda i: (i, 0))
    shape = jax.ShapeDtypeStruct((R, C), F32)
    return pl.pallas_call(body, name=name, out_shape=(shape,) * 4, grid=(R // rows,), in_specs=[spec] * 4,
                          out_specs=(spec,) * 4, compiler_params=_params("parallel"))(w, g, m, v)


def _place():
    x, y, c = lax.axis_index("x"), lax.axis_index("y"), lax.axis_index("c")
    others = [(1 - x, y), (x, 1 - y), (1 - x, 1 - y)]
    return x, y, c, others


def _chip_index(chip):
    return 2 * chip[0] + chip[1]


def _remote(src, dst, sems, k, to):
    send_sems, recv_sems = sems
    return pltpu.make_async_remote_copy(src_ref=src, dst_ref=dst, send_sem=send_sems.at[k], recv_sem=recv_sems.at[k],
                                        device_id=to, device_id_type=MESH)


def _comm_fields(comm):
    return comm if comm is not None else ((), (), 0, None, False)


def _comm_scratch(n_sems):
    return [pltpu.SemaphoreType.DMA((n_sems,)), pltpu.SemaphoreType.DMA((n_sems,))] if n_sems else []


def _start_when(cond, copies):
    @pl.when(cond)
    def _():
        for cp in copies:
            cp.start()


def _wait_when(cond, copies):
    @pl.when(cond)
    def _():
        for cp in copies:
            cp.wait()


def _proj_gather(order, h, w_full, tm=1024, tk=512):
    S, Dm = h.shape
    tm, tk = min(tm, S), min(tk, Dm)
    tn = SLAB // 2
    ni, nj, nk = S // tm, 2 * N_CHIPS, Dm // tk
    total = nj * ni * nk
    half_in = Dm // 2

    def body(order_ref, h_ref, _wi, proj_ref, fi_ref, acc, bbuf, bsem, send_sems, recv_sems):
        j, i, k = pl.program_id(0), pl.program_id(1), pl.program_id(2)
        t = (j * ni + i) * nk + k
        x, y, c, others = _place()
        me_chip = 2 * x + y
        sems = (send_sems, recv_sems)

        def piece(chip, half):
            return fi_ref.at[pl.ds(half * half_in, half_in), pl.ds(chip * SLAB, SLAB)]

        def ici_send(r):
            mine = piece(me_chip, c)
            return _remote(mine, mine, sems, r, (*others[r], c))

        def forward():
            src_chip = (x + (1 - c) * (1 - 2 * x), y + c * (1 - 2 * y))
            dst_chip = (x + c * (1 - 2 * x), y + (1 - c) * (1 - 2 * y))
            got = piece(_chip_index(src_chip), c)
            return _remote(got, got, sems, 2, (*dst_chip, c))

        def ici_recv(r):
            got = piece(_chip_index(others[r]), c)
            return _remote(got, got, sems, r, (x, y, c))

        def pass_on(r):
            got = piece(_chip_index(others[r]), c)
            return _remote(got, got, sems, 3 + r, (x, y, 1 - c))

        def passed_recv(r):
            got = piece(_chip_index(others[r]), 1 - c)
            return _remote(got, got, sems, 3 + r, (x, y, c))

        def b_copy(jj, kk, slot):
            col = 2 * order_ref[jj // 2] + jj % 2
            return pltpu.make_async_copy(fi_ref.at[pl.ds(kk * tk, tk), pl.ds(col * tn, tn)], bbuf.at[slot], bsem.at[slot])

        @pl.when(t == 0)
        def _():
            ici_send(0).start()
            ici_send(1).start()
            b_copy(0, 0, 0).start()

        nxt = t + 1
        kn, i_n, jn = nxt % nk, (nxt // nk) % ni, nxt // (nk * ni)

        def opens(r):
            return (jn == 2 * (r + 1)) & (i_n == 0) & (kn == 0)

        @pl.when(nxt < total)
        def _():
            @pl.when(opens(0))
            def _():
                ici_recv(0).wait_recv()
                ici_recv(1).wait_recv()
                forward().start()
                pass_on(0).start()
                pass_on(1).start()
                passed_recv(0).wait_recv()

            @pl.when(opens(1))
            def _():
                passed_recv(1).wait_recv()

            @pl.when(opens(2))
            def _():
                ici_recv(2).wait_recv()
                pass_on(2).start()
                passed_recv(2).wait_recv()

            b_copy(jn, kn, nxt % 2).start()

        b_copy(j, k, t % 2).wait()

        def product():
            return jnp.dot(h_ref[...], bbuf[t % 2], preferred_element_type=F32)

        @pl.when(k == 0)
        def _():
            acc[...] = product()

        @pl.when((k > 0) & (k < nk - 1))
        def _():
            acc[...] += product()

        @pl.when(k == nk - 1)
        def _():
            proj_ref[...] = (acc[...] + product()).astype(BF16)

        @pl.when(t == total - 1)
        def _():
            ici_send(0).wait_send()
            ici_send(1).wait_send()
            forward().wait_send()
            for r in range(3):
                pass_on(r).wait_send()

    col_of = lambda j, order: 2 * order[j // 2] + j % 2
    grid_spec = pltpu.PrefetchScalarGridSpec(
        num_scalar_prefetch=1, grid=(nj, ni, nk),
        in_specs=[pl.BlockSpec((tm, tk), lambda j, i, k, order: (i, k)), HBM_SPEC],
        out_specs=(pl.BlockSpec((tm, tn), lambda j, i, k, order: (i, col_of(j, order))), HBM_SPEC),
        scratch_shapes=[pltpu.VMEM((tm, tn), F32), pltpu.VMEM((2, tk, tn), BF16), pltpu.SemaphoreType.DMA((2,)),
                        pltpu.SemaphoreType.DMA((6,)), pltpu.SemaphoreType.DMA((6,))])
    return pl.pallas_call(
        body, name="proj_gather",
        out_shape=(jax.ShapeDtypeStruct((S, PROJ_WIDTH), BF16), jax.ShapeDtypeStruct(w_full.shape, BF16)),
        grid_spec=grid_spec, input_output_aliases={2: 1},
        compiler_params=_params("arbitrary", "arbitrary", "arbitrary"),
    )(order, h, w_full)


def _exchange(comm, name):
    arrays, shapes, n_sems, build, aliased = comm
    n_in, n_out = len(arrays), len(shapes)

    def body(*refs):
        copies = build(refs[:n_in], refs[n_in:n_in + n_out], refs[n_in + n_out:])
        for cp in copies:
            cp.start()
        for cp in copies:
            cp.wait()

    return pl.pallas_call(
        body, name=name, out_shape=tuple(shapes), in_specs=[HBM_SPEC] * n_in, out_specs=(HBM_SPEC,) * n_out,
        scratch_shapes=_comm_scratch(n_sems), input_output_aliases={t: t for t in range(n_in)} if aliased else {},
    )(*arrays)


def _pair_halves(parts):
    def half_of(ref, which):
        if len(ref.shape) == 2:
            rows = ref.shape[0] // 2
            return ref.at[pl.ds(which * rows, rows), :]
        return ref.at[:, pl.ds(which, 1)]

    def out_shape(p):
        if p.ndim == 2:
            return jax.ShapeDtypeStruct((p.shape[0] // 2, p.shape[1]), BF16)
        return jax.ShapeDtypeStruct((p.shape[0], 1) + p.shape[2:], BF16)

    def build(p_refs, r_refs, sems):
        x, y, c, _ = _place()
        return [_remote(half_of(p, 1 - c), r, sems, t, (x, y, 1 - c)) for t, (p, r) in enumerate(zip(p_refs, r_refs))]

    return (tuple(parts), tuple(out_shape(p) for p in parts), len(parts), build, False)


def _gather_squares(fulls, over_ici):
    half_up = UP_ROWS // 2

    def build(_, f_refs, sems):
        x, y, c, others = _place()
        copies = []
        for t, f in enumerate(f_refs):
            for r, chip in enumerate(others):
                src_chip = 2 * x + y if over_ici else _chip_index(chip)
                rows = f.at[pl.ds(src_chip * UP_ROWS + c * half_up, half_up), :]
                copies.append(_remote(rows, rows, sems, 3 * t + r, (*chip, c) if over_ici else (x, y, 1 - c)))
        return copies

    return (tuple(fulls), tuple(jax.ShapeDtypeStruct(f.shape, f.dtype) for f in fulls), 3 * len(fulls), build, True)


def _send_whole(parts):
    def build(p_refs, r_refs, sems):
        x, y, c, _ = _place()
        return [_remote(p, r, sems, t, (x, y, 1 - c)) for t, (p, r) in enumerate(zip(p_refs, r_refs))]

    return (tuple(parts), tuple(jax.ShapeDtypeStruct(p.shape, p.dtype) for p in parts), len(parts), build, False)


def _pair_add_in(p_own, r_in, rows=256, cols=SLAB):
    half, P = p_own.shape
    rows = min(rows, half)

    def body(p_ref, r_ref, o_ref):
        o_ref[...] = (p_ref[...].astype(F32) + r_ref[...].astype(F32)).astype(BF16)

    spec = pl.BlockSpec((rows, cols), lambda i, j: (i, j))
    return pl.pallas_call(body, name="pair_add_in", out_shape=jax.ShapeDtypeStruct((half, P), BF16),
                          grid=(half // rows, P // cols), in_specs=[spec, spec], out_specs=spec,
                          compiler_params=_params("parallel", "parallel"))(p_own, r_in)


def _pair_add_up(idx, p_up, r_up, name):
    _, _, R, C = p_up.shape

    def body(idx_ref, p_ref, r_ref, o_ref):
        o_ref[...] = (p_ref[...].astype(F32) + r_ref[...].astype(F32)).astype(BF16)

    grid_spec = pltpu.PrefetchScalarGridSpec(
        num_scalar_prefetch=1, grid=(N_CHIPS,),
        in_specs=[pl.BlockSpec((1, 1, R, C), lambda j, idx: (j, idx[0], 0, 0)),
                  pl.BlockSpec((1, 1, R, C), lambda j, idx: (j, 0, 0, 0))],
        out_specs=pl.BlockSpec((1, 1, R, C), lambda j, idx: (j, 0, 0, 0)))
    return pl.pallas_call(body, name=name, out_shape=jax.ShapeDtypeStruct((N_CHIPS, 1, R, C), BF16), grid_spec=grid_spec,
                          compiler_params=_params("parallel"))(idx, p_up, r_up)


def _slab_exchange(qs):
    n = len(qs)

    def out_shape(q):
        if q.ndim == 2:
            return jax.ShapeDtypeStruct((3, q.shape[0], SLAB), BF16)
        return jax.ShapeDtypeStruct((3,) + q.shape[1:], BF16)

    def build(q_refs, r_refs, sems):
        _, _, c, others = _place()
        copies = []
        for r, chip in enumerate(others):
            ci = _chip_index(chip)
            for t in range(n):
                src = q_refs[t].at[:, pl.ds(ci * SLAB, SLAB)] if len(q_refs[t].shape) == 2 else q_refs[t].at[ci]
                copies.append(_remote(src, r_refs[t].at[r], sems, 3 * t + r, (*chip, c)))
        return copies

    return (tuple(qs), tuple(out_shape(q) for q in qs), 3 * n, build, False)


def _slab_add_in(idx, q_in, r2_in, rows=128):
    half = q_in.shape[0]
    rows = min(rows, half)
    nrb = half // rows

    def body(idx_ref, q_ref, r_ref, o_ref):
        o_ref[...] = ((q_ref[...].astype(F32) + r_ref[0].astype(F32)) + r_ref[1].astype(F32)) + r_ref[2].astype(F32)

    grid_spec = pltpu.PrefetchScalarGridSpec(
        num_scalar_prefetch=1, grid=(nrb,),
        in_specs=[pl.BlockSpec((rows, SLAB), lambda i, idx: (i, idx[1])),
                  pl.BlockSpec((3, rows, SLAB), lambda i, idx: (0, i, 0))],
        out_specs=pl.BlockSpec((rows, SLAB), lambda i, idx: (idx[0] * nrb + i, 0)))
    return pl.pallas_call(body, name="slab_add_in", out_shape=jax.ShapeDtypeStruct((2 * half, SLAB), F32),
                          grid_spec=grid_spec, compiler_params=_params("parallel"))(idx, q_in, r2_in)


def _slab_add_up(idx, q_up, r2_up, name, rows=128):
    _, _, R, C = q_up.shape
    rows = min(rows, R)
    nrb = R // rows

    def body(idx_ref, q_ref, r_ref, o_ref):
        o_ref[...] = ((q_ref[0, 0].astype(F32) + r_ref[0, 0].astype(F32)) + r_ref[1, 0].astype(F32)) + r_ref[2, 0].astype(F32)

    grid_spec = pltpu.PrefetchScalarGridSpec(
        num_scalar_prefetch=1, grid=(nrb,),
        in_specs=[pl.BlockSpec((1, 1, rows, C), lambda i, idx: (idx[1], 0, i, 0)),
                  pl.BlockSpec((3, 1, rows, C), lambda i, idx: (0, 0, i, 0))],
        out_specs=pl.BlockSpec((rows, C), lambda i, idx: (idx[0] * nrb + i, 0)))
    return pl.pallas_call(body, name=name, out_shape=jax.ShapeDtypeStruct((2 * R, C), F32), grid_spec=grid_spec,
                          compiler_params=_params("parallel"))(idx, q_up, r2_up)


def _pair_share(gs):
    def build(_, g_refs, sems):
        x, y, c, _ = _place()
        copies = []
        for t, g in enumerate(g_refs):
            rows = g.shape[0] // 2
            mine = g.at[pl.ds(c * rows, rows), :]
            copies.append(_remote(mine, mine, sems, t, (x, y, 1 - c)))
        return copies

    return (tuple(gs), tuple(jax.ShapeDtypeStruct(g.shape, g.dtype) for g in gs), len(gs), build, True)


def _all_reduce_small(packed):
    R, C = packed.shape
    N_DEV = 2 * N_CHIPS

    def body(x_ref, out_ref, all_ref, send_sems, recv_sems, local_sem):
        x, y, c, others = _place()
        me, sib = (x, y, c), (x, y, 1 - c)
        sems = (send_sems, recv_sems)

        def rows(px, py, pc):
            return all_ref.at[4 * px + 2 * py + pc]

        mine = pltpu.make_async_copy(x_ref, rows(*me), local_sem)
        mine.start()
        first = [_remote(x_ref, rows(*me), sems, 0, sib)]
        first += [_remote(x_ref, rows(*me), sems, 1 + j, (*chip, c)) for j, chip in enumerate(others)]
        for cp in first:
            cp.start()
        passed = [_remote(rows(*chip, c), rows(*chip, c), sems, 4 + j, sib) for j, chip in enumerate(others)]
        for j, chip in enumerate(others):
            _remote(rows(*chip, c), rows(*chip, c), sems, 1 + j, me).wait_recv()
            passed[j].start()
        _remote(rows(*sib), rows(*sib), sems, 0, me).wait_recv()
        for j, chip in enumerate(others):
            _remote(rows(*chip, 1 - c), rows(*chip, 1 - c), sems, 4 + j, me).wait_recv()
        for cp in first + passed:
            cp.wait_send()
        mine.wait()
        total = all_ref[0]
        for d in range(1, N_DEV):
            total = total + all_ref[d]
        out_ref[...] = total

    vmem = pl.BlockSpec(memory_space=pltpu.VMEM)
    return pl.pallas_call(
        body, name="all_reduce_small", out_shape=jax.ShapeDtypeStruct((R, C), F32),
        in_specs=[vmem], out_specs=vmem,
        scratch_shapes=[pltpu.VMEM((N_DEV, R, C), F32), pltpu.SemaphoreType.DMA((7,)), pltpu.SemaphoreType.DMA((7,)),
                        pltpu.SemaphoreType.DMA],
        compiler_params=pltpu.CompilerParams(vmem_limit_bytes=VMEM_LIMIT),
    )(packed)


SMALL_NAMES = ("w_spatial", "b_spatial", "norm_g", "gmlp_ln_g", "gmlp_ln_b", "final_norm_g", "attn_sink")


def _pack_small(parts, extra=None):
    blocks = []
    for n in SMALL_NAMES:
        flat = parts[n].reshape(-1).astype(F32)
        rows = -(-flat.shape[0] // (8 * LANES)) * 8
        flat = jnp.pad(flat, (0, rows * LANES - flat.shape[0]))
        blocks.append(flat.reshape(rows, LANES))
    blocks.append(jnp.zeros((8, LANES), F32) if extra is None else extra)
    return jnp.concatenate(blocks, axis=0)


def _unpack_small(packed, shapes):
    out, r = {}, 0
    for n in SMALL_NAMES:
        size = 1
        for s in shapes[n]:
            size *= s
        rows = -(-size // (8 * LANES)) * 8
        out[n] = packed[r:r + rows].reshape(-1)[:size].reshape(shapes[n])
        r += rows
    return out, packed[r:r + 8]


def kernel(x, positions, norm_g, w_in, attn_sink, gmlp_ln_g, gmlp_ln_b, w_spatial, b_spatial, w_up_attn, w_up_gmlp, w_out, final_norm_g, loss_target, m_norm_g, m_w_in, m_attn_sink, m_gmlp_ln_g, m_gmlp_ln_b, m_w_spatial, m_b_spatial, m_w_up_attn, m_w_up_gmlp, m_w_out, m_final_norm_g, v_norm_g, v_w_in, v_attn_sink, v_gmlp_ln_g, v_gmlp_ln_b, v_w_spatial, v_b_spatial, v_w_up_attn, v_w_up_gmlp, v_w_out, v_final_norm_g):
    xs, tgt = x[0], loss_target[0]
    mx, my, mc = lax.axis_index("x"), lax.axis_index("y"), lax.axis_index("c")
    idx = jnp.stack([mc, 2 * mx + my]).astype(jnp.int32)
    arrival = jnp.stack([2 * mx + my, 2 * (1 - mx) + my, 2 * mx + (1 - my), 2 * (1 - mx) + (1 - my)]).astype(jnp.int32)

    square = (D_MODEL, D_MODEL)
    w_full = _cast_into(idx, w_in[0], (D_MODEL, PROJ_WIDTH), "cast_w_in")
    ups = [_cast_into(idx, w_up_attn[0], square, "cast_w_up_attn"), _cast_into(idx, w_up_gmlp[0], square, "cast_w_up_gmlp"),
           _cast_into(idx, w_out[0], square, "cast_w_out")]

    tables = _rope_tables(positions[0])
    bias = _band_bias()
    sink = attn_sink[0]
    ws, bs_t = w_spatial[0], b_spatial[0].T
    h = _rms_fwd(xs, norm_g)
    proj, w_full = _proj_gather(arrival, h, w_full)
    attn, a_in, *got = _attn_fwd(proj, tables, bias, sink, comm=_gather_squares(ups[:2], over_ici=True))
    b_in, wua, wug = _gmlp_fwd(proj, ws, bs_t, gmlp_ln_g, gmlp_ln_b, comm=_gather_squares(got, over_ici=False))
    y_a, wo = _matmul(a_in, wua, mode="nn", out_dtype=F32, name="up_attn", comm=_gather_squares(ups[2:], over_ici=True))
    y_b, wo = _matmul(b_in, wug, mode="nn", out_dtype=F32, name="up_gmlp", comm=_gather_squares([wo], over_ici=False))
    merged = _merge_fwd(proj, y_a, y_b)
    x2 = _matmul(merged, wo, mode="nn", out_dtype=F32, name="out_proj", residual=xs, tk=2048)
    dx2, dx2_b, d_fg, loss_part = _loss_head(x2, tgt, final_norm_g.reshape(1, D_MODEL))

    dmerged = _matmul(dx2_b, wo, mode="nt", out_dtype=F32, name="d_merged")
    dy_a, dy_b, d_gates = _merge_bwd(proj, y_a, y_b, dmerged)

    p_ups = [_matmul(a_in, dy_a, mode="tn", out_dtype=BF16, name="dw_up_attn"),
             _matmul(b_in, dy_b, mode="tn", out_dtype=BF16, name="dw_up_gmlp"),
             _matmul(merged, dx2_b, mode="tn", out_dtype=BF16, name="dw_out")]
    p_ups = [p.reshape(N_CHIPS, 2, UP_ROWS // 2, D_MODEL) for p in p_ups]
    da_in, *r1_ups = _matmul(dy_a, wua, mode="nt", out_dtype=F32, name="d_a_in", comm=_pair_halves(p_ups))
    q_ups = [_pair_add_up(idx, p, r, "pair_add_up%d" % t) for t, (p, r) in enumerate(zip(p_ups, r1_ups))]
    db_in = _matmul(dy_b, wug, mode="nt", out_dtype=F32, name="d_b_in")
    dq, dk, dv, dga, d_sink = _attn_bwd(proj, tables, bias, sink, attn, da_in)
    d_gmlp, d_ws, d_bs, d_lg, d_lb = _gmlp_bwd(proj, ws, bs_t, gmlp_ln_g, gmlp_ln_b, db_in)
    dproj = jnp.concatenate([dq, dk, dv, dga, d_gmlp, d_gates], axis=1)

    half = D_MODEL // 2
    h_sib = lax.dynamic_slice(h, (0, (1 - mc) * half), (h.shape[0], half))
    h_own = lax.dynamic_slice(h, (0, mc * half), (h.shape[0], half))
    p_sib, *r2_ups = _matmul(h_sib, dproj, mode="tn", out_dtype=BF16, name="dw_in_sibling", comm=_slab_exchange(q_ups))
    p_own, r1_in = _matmul(h_own, dproj, mode="tn", out_dtype=BF16, name="dw_in_own", comm=_send_whole([p_sib]))
    q_in = _pair_add_in(p_own, r1_in)
    dh, r2_in = _matmul(dproj, w_full, mode="nt", out_dtype=F32, name="d_h", comm=_slab_exchange([q_in]), tk=SLAB // 2)
    grad_x, d_ng = _rms_bwd(xs, dh, dx2, norm_g)

    g_half = [_slab_add_in(idx, q_in, r2_in)] + [_slab_add_up(idx, q, r, "slab_add_up%d" % t)
                                                 for t, (q, r) in enumerate(zip(q_ups, r2_ups))]
    g_big = _exchange(_pair_share(g_half), "pair_share")

    small_shapes = {"w_spatial": w_spatial.shape, "b_spatial": b_spatial.shape, "norm_g": norm_g.shape,
                    "gmlp_ln_g": gmlp_ln_g.shape, "gmlp_ln_b": gmlp_ln_b.shape, "final_norm_g": final_norm_g.shape,
                    "attn_sink": attn_sink.shape}
    d_small = {"w_spatial": d_ws, "b_spatial": d_bs, "norm_g": d_ng, "gmlp_ln_g": d_lg, "gmlp_ln_b": d_lb,
               "final_norm_g": d_fg, "attn_sink": d_sink[:, 0, :HEADS_PER_STEP]}
    g_small = _all_reduce_small(_pack_small(d_small, loss_part))

    w_small = _pack_small(dict(w_spatial=w_spatial, b_spatial=b_spatial, norm_g=norm_g, gmlp_ln_g=gmlp_ln_g,
                               gmlp_ln_b=gmlp_ln_b, final_norm_g=final_norm_g, attn_sink=attn_sink))
    m_small = _pack_small(dict(w_spatial=m_w_spatial, b_spatial=m_b_spatial, norm_g=m_norm_g, gmlp_ln_g=m_gmlp_ln_g,
                               gmlp_ln_b=m_gmlp_ln_b, final_norm_g=m_final_norm_g, attn_sink=m_attn_sink))
    v_small = _pack_small(dict(w_spatial=v_w_spatial, b_spatial=v_b_spatial, norm_g=v_norm_g, gmlp_ln_g=v_gmlp_ln_g,
                               gmlp_ln_b=v_gmlp_ln_b, final_norm_g=v_final_norm_g, attn_sink=v_attn_sink))
    small_out = _adamw(w_small, g_small, m_small, v_small, "adamw_small", rows=w_small.shape[0])
    big = {
        "w_in": _adamw(w_in[0], g_big[0], m_w_in[0], v_w_in[0], "adamw_w_in"),
        "w_up_attn": _adamw(w_up_attn[0], g_big[1], m_w_up_attn[0], v_w_up_attn[0], "adamw_w_up_attn"),
        "w_up_gmlp": _adamw(w_up_gmlp[0], g_big[2], m_w_up_gmlp[0], v_w_up_gmlp[0], "adamw_w_up_gmlp"),
        "w_out": _adamw(w_out[0], g_big[3], m_w_out[0], v_w_out[0], "adamw_w_out"),
    }

    order = ("norm_g", "w_in", "attn_sink", "gmlp_ln_g", "gmlp_ln_b", "w_spatial", "b_spatial", "w_up_attn", "w_up_gmlp",
             "w_out", "final_norm_g")
    outs = []
    loss = None
    for kind in range(4):
        small, extra = _unpack_small(small_out[kind], small_shapes)
        if kind == 0:
            loss = extra[0, 0]
        for n in order:
            outs.append(big[n][kind][None] if n in big else small[n])
    return (loss, grad_x[None], *outs)
```

```python
import functools

import jax
import jax.numpy as jnp
from jax import lax
from jax.experimental import pallas as pl
from jax.experimental.pallas import tpu as pltpu

F32 = jnp.float32
BF16 = jnp.bfloat16

D_MODEL = 4096
N_Q_HEADS = 64
N_KV_HEADS = 8
HEAD_DIM = 64
Q_PER_KV = N_Q_HEADS // N_KV_HEADS
ATTN_WIDTH = N_Q_HEADS * HEAD_DIM
KV_WIDTH = N_KV_HEADS * HEAD_DIM
WINDOW = 128
BLOCK = 128
ROPE_THETA = 500000.0
ROPE_DIM = HEAD_DIM // 4
ROPE_HALF = ROPE_DIM // 2
GMLP_WIDTH = D_MODEL
GMLP_GROUPS = 8
GMLP_GROUP_DIM = GMLP_WIDTH // GMLP_GROUPS
GMLP_CHUNK = 128
NORM_EPS = 1e-5
LN_EPS = 1e-5

PROJ_SIZES = (ATTN_WIDTH, KV_WIDTH, KV_WIDTH, ATTN_WIDTH, GMLP_WIDTH, GMLP_WIDTH, GMLP_WIDTH, D_MODEL, D_MODEL)
PROJ_WIDTH = sum(PROJ_SIZES)
OFF_Q, OFF_K, OFF_V, OFF_GA, OFF_U, OFF_VG, OFF_GB, OFF_MA, OFF_MB = (
    sum(PROJ_SIZES[:i]) for i in range(len(PROJ_SIZES)))

ADAM_LR = 0.001
ADAM_B1 = 0.9
ADAM_B2 = 0.999
ADAM_EPS = 1e-08
ADAM_WD = 0.01
ADAM_STEP = 10

N_CHIPS = 4
SLAB = PROJ_WIDTH // N_CHIPS
UP_ROWS = D_MODEL // N_CHIPS
LANES = 128
COL_BLK = 1024
HEADS_PER_STEP = 2 * Q_PER_KV
VMEM_LIMIT = 56 * 1024 * 1024

MESH = pl.DeviceIdType.MESH
HBM_SPEC = pl.BlockSpec(memory_space=pltpu.HBM)
SMEM_SPEC = pl.BlockSpec(memory_space=pltpu.SMEM)


def _params(*sem):
    return pltpu.CompilerParams(dimension_semantics=sem, vmem_limit_bytes=VMEM_LIMIT)


def _sigmoid(x):
    return jax.nn.sigmoid(x)


def _gelu(x):
    return jax.nn.gelu(x, approximate=True)


def _gelu_grad(x):
    c = 0.7978845608028654
    inner = c * (x + 0.044715 * x * x * x)
    t = jnp.tanh(inner)
    return 0.5 * (1.0 + t) + 0.5 * x * (1.0 - t * t) * c * (1.0 + 3 * 0.044715 * x * x)


def _matmul(a, b, *, mode, out_dtype, name, residual=None, comm=None, tm=1024, tn=1024, tk=4096):
    if mode == "nn":
        (M, K), N = a.shape, b.shape[1]
    elif mode == "nt":
        (M, K), N = a.shape, b.shape[0]
    else:
        (K, M), N = a.shape, b.shape[1]
    tm, tn, tk = min(tm, M), min(tn, N), min(tk, K)
    assert M % tm == 0 and N % tn == 0 and K % tk == 0
    ni, nj, nk = M // tm, N // tn, K // tk
    if mode == "nn":
        a_spec = pl.BlockSpec((tm, tk), lambda i, j, k: (i, k))
        b_spec = pl.BlockSpec((tk, tn), lambda i, j, k: (k, j))
        dims = (((1,), (0,)), ((), ()))
    elif mode == "nt":
        a_spec = pl.BlockSpec((tm, tk), lambda i, j, k: (i, k))
        b_spec = pl.BlockSpec((tn, tk), lambda i, j, k: (j, k))
        dims = (((1,), (1,)), ((), ()))
    else:
        a_spec = pl.BlockSpec((tk, tm), lambda i, j, k: (k, i))
        b_spec = pl.BlockSpec((tk, tn), lambda i, j, k: (k, j))
        dims = (((0,), (0,)), ((), ()))
    o_spec = pl.BlockSpec((tm, tn), lambda i, j, k: (i, j))
    n_res = 1 if residual is not None else 0
    c_arrays, c_shapes, c_sems, c_build, c_alias = _comm_fields(comm)
    n_cin, n_cout = len(c_arrays), len(c_shapes)

    def body(*refs):
        a_ref, b_ref = refs[0], refs[1]
        r_ref = refs[2] if n_res else None
        cin = refs[2 + n_res:2 + n_res + n_cin]
        o_ref = refs[2 + n_res + n_cin]
        cout = refs[3 + n_res + n_cin:3 + n_res + n_cin + n_cout]
        n_acc = 1 if nk > 1 else 0
        acc = refs[3 + n_res + n_cin + n_cout] if n_acc else None
        sems = refs[3 + n_acc + n_res + n_cin + n_cout:]
        i, j, k = pl.program_id(0), pl.program_id(1), pl.program_id(2)
        if comm is not None:
            _start_when((i == 0) & (j == 0) & (k == 0), c_build(cin, cout, sems))

        def product():
            return lax.dot_general(a_ref[...], b_ref[...], dims, preferred_element_type=F32)

        def finish(r):
            if n_res:
                r = r + r_ref[...]
            o_ref[...] = r.astype(out_dtype)

        if nk == 1:
            finish(product())
        else:
            @pl.when(k == 0)
            def _():
                acc[...] = product()

            @pl.when((k > 0) & (k < nk - 1))
            def _():
                acc[...] += product()

            @pl.when(k == nk - 1)
            def _():
                finish(acc[...] + product())

        if comm is not None:
            _wait_when((i == ni - 1) & (j == nj - 1) & (k == nk - 1), c_build(cin, cout, sems))

    args = (a, b) + ((residual,) if n_res else ()) + tuple(c_arrays)
    in_specs = [a_spec, b_spec] + ([o_spec] if n_res else []) + [HBM_SPEC] * n_cin
    scratch = ([pltpu.VMEM((tm, tn), F32)] if nk > 1 else []) + _comm_scratch(c_sems)
    sem = ("arbitrary",) * 3 if comm is not None else ("parallel", "parallel", "arbitrary")
    aliases = {2 + n_res + t: 1 + t for t in range(n_cin)} if c_alias else {}
    out = pl.pallas_call(
        body, name=name, out_shape=(jax.ShapeDtypeStruct((M, N), out_dtype),) + tuple(c_shapes),
        grid=(ni, nj, nk), in_specs=in_specs, out_specs=(o_spec,) + (HBM_SPEC,) * n_cout,
        scratch_shapes=scratch, input_output_aliases=aliases, compiler_params=_params(*sem),
    )(*args)
    return out if comm is not None else out[0]


def _cast_into(idx, w, full_shape, name, rows=64):
    R, C = w.shape
    rows = min(rows, R)
    nrb = R // rows
    by_cols = C != full_shape[1]

    def body(idx_ref, w_ref, o_ref):
        o_ref[...] = w_ref[...].astype(BF16)

    out_map = (lambda i, idx: (i, idx[1])) if by_cols else (lambda i, idx: (idx[1] * nrb + i, 0))
    grid_spec = pltpu.PrefetchScalarGridSpec(
        num_scalar_prefetch=1, grid=(nrb,), in_specs=[pl.BlockSpec((rows, C), lambda i, idx: (i, 0))],
        out_specs=pl.BlockSpec((rows, C), out_map))
    return pl.pallas_call(body, name=name, out_shape=jax.ShapeDtypeStruct(full_shape, BF16), grid_spec=grid_spec,
                          compiler_params=_params("parallel"))(idx, w)


def _rms_fwd(x, g, rows=256):
    S, D = x.shape
    rows = min(rows, S)

    def body(x_ref, g_ref, h_ref):
        xv = x_ref[...]
        ms = jnp.mean(xv * xv, axis=-1, keepdims=True)
        h_ref[...] = (xv * lax.rsqrt(ms + NORM_EPS) * g_ref[...]).astype(BF16)

    spec = pl.BlockSpec((rows, D), lambda i: (i, 0))
    return pl.pallas_call(body, name="rms_fwd", out_shape=jax.ShapeDtypeStruct((S, D), BF16), grid=(S // rows,),
                          in_specs=[spec, pl.BlockSpec((1, D), lambda i: (0, 0))], out_specs=spec,
                          compiler_params=_params("parallel"))(x, g)


def _rope_tables(positions):
    inv_freq = ROPE_THETA ** (-jnp.arange(ROPE_HALF, dtype=F32) * 2.0 / ROPE_DIM)
    ang = positions.astype(F32)[:, None] * inv_freq
    cos, sin = jnp.cos(ang), jnp.sin(ang)
    S = positions.shape[0]
    rest = HEAD_DIM - ROPE_DIM
    zeros_h, zeros_r = jnp.zeros((S, ROPE_HALF), F32), jnp.zeros((S, rest), F32)
    c = jnp.concatenate([cos, cos, jnp.ones((S, rest), F32)], axis=1)
    s1 = jnp.concatenate([-sin, zeros_h, zeros_r], axis=1)
    s2 = jnp.concatenate([zeros_h, sin, zeros_r], axis=1)
    reps = LANES // HEAD_DIM
    return jnp.tile(c, (1, reps)), jnp.tile(s1, (1, reps)), jnp.tile(s2, (1, reps))


def _rope(t, c, s1, s2, sign):
    n = t.shape[1]
    reps = n // c.shape[1]
    if reps > 1:
        c, s1, s2 = jnp.tile(c, (1, reps)), jnp.tile(s1, (1, reps)), jnp.tile(s2, (1, reps))
    up = pltpu.roll(t, n - ROPE_HALF, 1)
    down = pltpu.roll(t, ROPE_HALF, 1)
    return t * c + sign * (up * s1 + down * s2)


def _attn_specs(nb):
    last = nb - 1

    def cur(i):
        return jnp.minimum(i, last)

    def prev(i):
        return jnp.maximum(jnp.minimum(i, last) - 1, 0)

    kq, kk, kv, kg = OFF_Q // COL_BLK, OFF_K // LANES, OFF_V // LANES, OFF_GA // COL_BLK
    wide = lambda off: pl.BlockSpec((BLOCK, COL_BLK), lambda p, i: (cur(i), off + p))
    kv_cur = lambda off: pl.BlockSpec((BLOCK, LANES), lambda p, i: (cur(i), off + p))
    kv_prev = lambda off: pl.BlockSpec((BLOCK, LANES), lambda p, i: (prev(i), off + p))
    tab_cur = pl.BlockSpec((BLOCK, LANES), lambda p, i: (cur(i), 0))
    tab_prev = pl.BlockSpec((BLOCK, LANES), lambda p, i: (prev(i), 0))
    proj_specs = [wide(kq), kv_cur(kk), kv_prev(kk), kv_cur(kv), kv_prev(kv), wide(kg)]
    table_specs = [tab_cur] * 3 + [tab_prev] * 3
    bias_spec = pl.BlockSpec((1, BLOCK, QCOLS), lambda p, i: (jnp.minimum(i, 1), 0, 0))
    return proj_specs, table_specs + [bias_spec], cur, prev


PAIRS = Q_PER_KV // 2
QCOLS = PAIRS * BLOCK
NT = (((1,), (1,)), ((), ()))
TN = (((0,), (0,)), ((), ()))


def _band_bias():
    first = jnp.where(jnp.arange(BLOCK)[:, None] > (jnp.arange(QCOLS) % BLOCK)[None, :], -1e30, 0.0)
    return jnp.stack([first, jnp.zeros_like(first)]).astype(F32)


def _from_prev():
    a = lax.broadcasted_iota(jnp.int32, (BLOCK, QCOLS), 1) & (BLOCK - 1)
    return lax.broadcasted_iota(jnp.int32, (BLOCK, QCOLS), 0) > a


def _pair_rows(ref, kvh):
    base = kvh * Q_PER_KV * HEAD_DIM
    return jnp.concatenate([ref[:, base + b * LANES: base + (b + 1) * LANES] for b in range(PAIRS)], axis=0).astype(BF16)


def _unpair_rows(ref, kvh, val_t):
    base = kvh * Q_PER_KV * HEAD_DIM
    for b in range(PAIRS):
        ref[:, base + b * LANES: base + (b + 1) * LANES] = val_t[:, b * BLOCK:(b + 1) * BLOCK].T.astype(ref.dtype)


def _split_heads(t, kvh):
    lane = lax.broadcasted_iota(jnp.int32, t.shape, 1)
    other = pltpu.roll(t, HEAD_DIM, 1)
    lo = jnp.where(lane < HEAD_DIM, t if kvh == 0 else other, 0.0)
    hi = jnp.where(lane >= HEAD_DIM, other if kvh == 0 else t, 0.0)
    return lo.astype(BF16), hi.astype(BF16)


def _join_heads(even, odd, kvh):
    lane = lax.broadcasted_iota(jnp.int32, even.shape, 1)
    lo = jnp.where(lane < HEAD_DIM, even, 0.0) + pltpu.roll(jnp.where(lane >= HEAD_DIM, odd, 0.0), HEAD_DIM, 1)
    return lo if kvh == 0 else pltpu.roll(lo, HEAD_DIM, 1)


def _sink_row(sink_ref, pair, kvh, odd):
    first = (pair * 2 + kvh) * Q_PER_KV + odd
    return jnp.concatenate([jnp.full((1, BLOCK), sink_ref[first + 2 * b], F32) for b in range(PAIRS)], axis=1)


def _softmax_t(kp, kc, qp, sink_row, bias, from_prev):
    st = jnp.where(from_prev, lax.dot_general(kp, qp, NT, preferred_element_type=F32),
                   lax.dot_general(kc, qp, NT, preferred_element_type=F32)) + bias
    m = jnp.maximum(jnp.max(st, axis=0, keepdims=True), sink_row)
    e = jnp.exp(st - m)
    es = jnp.exp(sink_row - m)
    inv = 1.0 / (jnp.sum(e, axis=0, keepdims=True) + es)
    return e * inv, es * inv


def _roped_keys(k_ref, kp_ref, tabs):
    c, s1, s2, cp, s1p, s2p = (t[...] for t in tabs)
    k_cur = _rope(k_ref[...].astype(F32), c, s1, s2, 1.0)
    k_prev = _rope(kp_ref[...].astype(F32), cp, s1p, s2p, 1.0)
    return k_prev, k_cur


SCALE = HEAD_DIM ** -0.5


def _attn_fwd(proj, tables, bias, sink, comm=None):
    S = proj.shape[0]
    nb = S // BLOCK
    npairs = N_KV_HEADS // 2
    proj_specs, table_specs, _, _ = _attn_specs(nb)
    c_arrays, c_shapes, c_sems, c_build, c_alias = _comm_fields(comm)
    n_in = len(proj_specs) + len(table_specs) + 1

    def body(*refs):
        (q_ref, k_ref, kp_ref, v_ref, vp_ref, ga_ref, c_ref, s1_ref, s2_ref, cp_ref, s1p_ref, s2p_ref, bias_ref,
         sink_ref) = refs[:n_in]
        cin = refs[n_in:n_in + len(c_arrays)]
        attn_ref, ain_ref = refs[n_in + len(c_arrays):n_in + len(c_arrays) + 2]
        cout = refs[n_in + len(c_arrays) + 2:n_in + len(c_arrays) + 2 + len(c_shapes)]
        qr_scr = refs[n_in + len(c_arrays) + 2 + len(c_shapes)]
        sems = refs[n_in + len(c_arrays) + 3 + len(c_shapes):]
        pair, i = pl.program_id(0), pl.program_id(1)
        if comm is not None:
            _start_when((pair == 0) & (i == 0), c_build(cin, cout, sems))
        tabs = (c_ref, s1_ref, s2_ref, cp_ref, s1p_ref, s2p_ref)
        qr_scr[...] = _rope(q_ref[...].astype(F32), c_ref[...], s1_ref[...], s2_ref[...], 1.0) * SCALE
        k_prev, k_cur = _roped_keys(k_ref, kp_ref, tabs)
        v_prev, v_cur = vp_ref[...].astype(F32), v_ref[...].astype(F32)
        from_prev = _from_prev()
        for kvh in range(2):
            qp = _pair_rows(qr_scr, kvh)
            kps, kcs = _split_heads(k_prev, kvh), _split_heads(k_cur, kvh)
            vps, vcs = _split_heads(v_prev, kvh), _split_heads(v_cur, kvh)
            o_t = None
            for odd in range(2):
                p_t, _ = _softmax_t(kps[odd], kcs[odd], qp, _sink_row(sink_ref, pair, kvh, odd), bias_ref[0], from_prev)
                pb = p_t.astype(BF16)
                part = (lax.dot_general(vps[odd], jnp.where(from_prev, pb, 0), TN, preferred_element_type=F32)
                        + lax.dot_general(vcs[odd], jnp.where(from_prev, 0, pb), TN, preferred_element_type=F32))
                o_t = part if o_t is None else o_t + part
            _unpair_rows(attn_ref, kvh, o_t)
        ga = ga_ref[...].astype(F32)
        ain_ref[...] = (attn_ref[...] * (ga * _sigmoid(ga))).astype(BF16)
        if comm is not None:
            _wait_when((pair == npairs - 1) & (i == nb - 1), c_build(cin, cout, sems))

    out_spec = pl.BlockSpec((BLOCK, COL_BLK), lambda p, i: (i, p))
    aliases = {n_in + t: 2 + t for t in range(len(c_arrays))} if c_alias else {}
    out = pl.pallas_call(
        body, name="attn_fwd",
        out_shape=(jax.ShapeDtypeStruct((S, ATTN_WIDTH), F32), jax.ShapeDtypeStruct((S, ATTN_WIDTH), BF16)) + tuple(c_shapes),
        grid=(npairs, nb), in_specs=proj_specs + table_specs + [SMEM_SPEC] + [HBM_SPEC] * len(c_arrays),
        out_specs=(out_spec, out_spec) + (HBM_SPEC,) * len(c_shapes),
        scratch_shapes=[pltpu.VMEM((BLOCK, COL_BLK), F32)] + _comm_scratch(c_sems), input_output_aliases=aliases,
        compiler_params=_params("arbitrary", "arbitrary"),
    )(*([proj] * 6), *tables, *tables, bias, sink, *c_arrays)
    return out


def _attn_bwd(proj, tables, bias, sink, attn, da_in):
    S = proj.shape[0]
    nb = S // BLOCK
    proj_specs, table_specs, cur, prev = _attn_specs(nb)
    wide_cur = pl.BlockSpec((BLOCK, COL_BLK), lambda p, i: (cur(i), p))
    kv_out = pl.BlockSpec((BLOCK, LANES), lambda p, i: (jnp.maximum(i - 1, 0), p))

    def body(q_ref, k_ref, kp_ref, v_ref, vp_ref, ga_ref, c_ref, s1_ref, s2_ref, cp_ref, s1p_ref, s2p_ref, bias_ref,
             sink_ref, attn_ref, da_ref, dq_ref, dk_ref, dv_ref, dga_ref, dsink_ref,
             qr_scr, do_scr, dq_scr, dk_scr, dv_scr, carry_k, carry_v, sink_acc):
        pair, i = pl.program_id(0), pl.program_id(1)
        tabs = (c_ref, s1_ref, s2_ref, cp_ref, s1p_ref, s2p_ref)

        @pl.when(i < nb)
        def _():
            ga = ga_ref[...].astype(F32)
            sg = _sigmoid(ga)
            da = da_ref[...]
            dga_ref[...] = (da * attn_ref[...] * (sg * (1.0 + ga * (1.0 - sg)))).astype(BF16)
            do_scr[...] = da * (ga * sg)
            qr_scr[...] = _rope(q_ref[...].astype(F32), c_ref[...], s1_ref[...], s2_ref[...], 1.0) * SCALE
            k_prev, k_cur = _roped_keys(k_ref, kp_ref, tabs)
            v_prev, v_cur = vp_ref[...].astype(F32), v_ref[...].astype(F32)
            from_prev = _from_prev()
            dkv = [None] * 4
            for kvh in range(2):
                qp, dop = _pair_rows(qr_scr, kvh), _pair_rows(do_scr, kvh)
                kps, kcs = _split_heads(k_prev, kvh), _split_heads(k_cur, kvh)
                vps, vcs = _split_heads(v_prev, kvh), _split_heads(v_cur, kvh)
                dq_t, halves = None, []
                for odd in range(2):
                    p_t, p_sink = _softmax_t(kps[odd], kcs[odd], qp, _sink_row(sink_ref, pair, kvh, odd), bias_ref[0],
                                             from_prev)
                    dp_t = jnp.where(from_prev, lax.dot_general(vps[odd], dop, NT, preferred_element_type=F32),
                                     lax.dot_general(vcs[odd], dop, NT, preferred_element_type=F32))
                    delta = jnp.sum(p_t * dp_t, axis=0, keepdims=True)
                    ds = (p_t * (dp_t - delta)).astype(BF16)
                    pb = p_t.astype(BF16)
                    ds_prev, ds_cur = jnp.where(from_prev, ds, 0), jnp.where(from_prev, 0, ds)
                    p_prev, p_cur = jnp.where(from_prev, pb, 0), jnp.where(from_prev, 0, pb)
                    part = (lax.dot_general(kps[odd], ds_prev, TN, preferred_element_type=F32)
                            + lax.dot_general(kcs[odd], ds_cur, TN, preferred_element_type=F32))
                    dq_t = part if dq_t is None else dq_t + part
                    halves.append([jnp.dot(ds_prev, qp, preferred_element_type=F32),
                                   jnp.dot(ds_cur, qp, preferred_element_type=F32),
                                   jnp.dot(p_prev, dop, preferred_element_type=F32),
                                   jnp.dot(p_cur, dop, preferred_element_type=F32)])
                    contrib = -p_sink * delta

                    @pl.when(i == 0)
                    def _():
                        sink_acc[2 * kvh + odd] = contrib

                    @pl.when(i > 0)
                    def _():
                        sink_acc[2 * kvh + odd] += contrib

                _unpair_rows(dq_scr, kvh, dq_t)
                for n in range(4):
                    joined = _join_heads(halves[0][n], halves[1][n], kvh)
                    dkv[n] = joined if dkv[n] is None else dkv[n] + joined
            dk_scr[0:BLOCK, :], dk_scr[BLOCK:2 * BLOCK, :] = dkv[0], dkv[1]
            dv_scr[0:BLOCK, :], dv_scr[BLOCK:2 * BLOCK, :] = dkv[2], dkv[3]

            dq_ref[...] = _rope(dq_scr[...] * SCALE, c_ref[...], s1_ref[...], s2_ref[...], -1.0).astype(BF16)
            dk_prev = _rope(dk_scr[0:BLOCK, :], cp_ref[...], s1p_ref[...], s2p_ref[...], -1.0)
            dk_cur = _rope(dk_scr[BLOCK:2 * BLOCK, :], c_ref[...], s1_ref[...], s2_ref[...], -1.0)
            dv_prev, dv_cur = dv_scr[0:BLOCK, :], dv_scr[BLOCK:2 * BLOCK, :]

            @pl.when(i > 0)
            def _():
                dk_ref[...] = (carry_k[...] + dk_prev).astype(BF16)
                dv_ref[...] = (carry_v[...] + dv_prev).astype(BF16)

            carry_k[...] = dk_cur
            carry_v[...] = dv_cur

        @pl.when(i == nb)
        def _():
            dk_ref[...] = carry_k[...].astype(BF16)
            dv_ref[...] = carry_v[...].astype(BF16)
            lane = lax.broadcasted_iota(jnp.int32, (8, LANES), 1)
            acc = jnp.zeros((8, LANES), F32)
            for kvh in range(2):
                for odd in range(2):
                    for b in range(PAIRS):
                        val = jnp.sum(sink_acc[2 * kvh + odd, :, b * BLOCK:(b + 1) * BLOCK], axis=1, keepdims=True)
                        acc = jnp.where(lane == kvh * Q_PER_KV + 2 * b + odd, val, acc)
            dsink_ref[0] = acc

    return pl.pallas_call(
        body, name="attn_bwd",
        out_shape=(jax.ShapeDtypeStruct((S, ATTN_WIDTH), BF16), jax.ShapeDtypeStruct((S, KV_WIDTH), BF16),
                   jax.ShapeDtypeStruct((S, KV_WIDTH), BF16), jax.ShapeDtypeStruct((S, ATTN_WIDTH), BF16),
                   jax.ShapeDtypeStruct((N_KV_HEADS // 2, 8, LANES), F32)),
        grid=(N_KV_HEADS // 2, nb + 1),
        in_specs=proj_specs + table_specs + [SMEM_SPEC, wide_cur, wide_cur],
        out_specs=(wide_cur, kv_out, kv_out, wide_cur, pl.BlockSpec((1, 8, LANES), lambda p, i: (p, 0, 0))),
        scratch_shapes=[pltpu.VMEM((BLOCK, COL_BLK), F32), pltpu.VMEM((BLOCK, COL_BLK), F32),
                        pltpu.VMEM((BLOCK, COL_BLK), F32), pltpu.VMEM((2 * BLOCK, LANES), F32),
                        pltpu.VMEM((2 * BLOCK, LANES), F32), pltpu.VMEM((BLOCK, LANES), F32),
                        pltpu.VMEM((BLOCK, LANES), F32), pltpu.VMEM((4, 1, QCOLS), F32)],
        compiler_params=_params("arbitrary", "arbitrary"),
    )(*([proj] * 6), *tables, *tables, bias, sink, attn, da_in)


N_PARTS = GMLP_WIDTH // COL_BLK
GROUPS_PER_PART = COL_BLK // GMLP_GROUP_DIM


def _part_specs(off):
    return [pl.BlockSpec((GMLP_CHUNK, COL_BLK), functools.partial(lambda j, i: (i, j), off // COL_BLK + k))
            for k in range(N_PARTS)]


def _group(refs, g):
    lo = (g % GROUPS_PER_PART) * GMLP_GROUP_DIM
    return refs[g // GROUPS_PER_PART][:, lo:lo + GMLP_GROUP_DIM].astype(F32)


def _gmlp_norm_stats(vg_refs, gv_scr):
    total = jnp.zeros((GMLP_CHUNK, 1), F32)
    for k in range(N_PARTS):
        gv = _gelu(vg_refs[k][...].astype(F32))
        gv_scr[:, k * COL_BLK:(k + 1) * COL_BLK] = gv
        total = total + jnp.sum(gv, axis=1, keepdims=True)
    mu = total / GMLP_WIDTH
    xc = gv_scr[...] - mu
    var = jnp.sum(xc * xc, axis=1, keepdims=True) / GMLP_WIDTH
    return mu, lax.rsqrt(var + LN_EPS)


def _tril_bf16(ws_ref, g):
    t = lax.broadcasted_iota(jnp.int32, (GMLP_CHUNK, GMLP_CHUNK), 0)
    s = lax.broadcasted_iota(jnp.int32, (GMLP_CHUNK, GMLP_CHUNK), 1)
    return jnp.where(s <= t, ws_ref[g], 0.0).astype(BF16), s <= t


def _gmlp_fwd(proj, ws, bs_t, lg, lb, comm=None):
    S = proj.shape[0]
    nb = S // GMLP_CHUNK
    c_arrays, c_shapes, c_sems, c_build, c_alias = _comm_fields(comm)
    n_cin, n_cout = len(c_arrays), len(c_shapes)

    def body(*refs):
        u_refs, vg_refs, gb_refs = refs[0:4], refs[4:8], refs[8:12]
        ws_ref, bst_ref, lg_ref, lb_ref = refs[12:16]
        cin, out_ref, cout = refs[16:16 + n_cin], refs[16 + n_cin], refs[17 + n_cin:17 + n_cin + n_cout]
        gv_scr, sems = refs[17 + n_cin + n_cout], refs[18 + n_cin + n_cout:]
        if comm is not None:
            _start_when(pl.program_id(0) == 0, c_build(cin, cout, sems))
        mu, rstd = _gmlp_norm_stats(vg_refs, gv_scr)
        for g in range(GMLP_GROUPS):
            cols = slice(g * GMLP_GROUP_DIM, (g + 1) * GMLP_GROUP_DIM)
            vn = (gv_scr[:, cols] - mu) * rstd * lg_ref[:, cols] + lb_ref[:, cols]
            w, _ = _tril_bf16(ws_ref, g)
            mixed = jnp.dot(w, vn.astype(BF16), preferred_element_type=F32) + bst_ref[:, g:g + 1]
            gb = _group(gb_refs, g)
            out_ref[:, cols] = (_gelu(_group(u_refs, g)) * mixed * (gb * _sigmoid(gb))).astype(BF16)
        if comm is not None:
            _wait_when(pl.program_id(0) == nb - 1, c_build(cin, cout, sems))

    full = lambda shape: pl.BlockSpec(shape, lambda i: tuple(0 for _ in shape))
    aliases = {16 + t: 1 + t for t in range(n_cin)} if c_alias else {}
    out = pl.pallas_call(
        body, name="gmlp_fwd", out_shape=(jax.ShapeDtypeStruct((S, GMLP_WIDTH), BF16),) + tuple(c_shapes), grid=(nb,),
        in_specs=_part_specs(OFF_U) + _part_specs(OFF_VG) + _part_specs(OFF_GB)
        + [full(ws.shape), full(bs_t.shape), full(lg.shape), full(lb.shape)] + [HBM_SPEC] * n_cin,
        out_specs=(pl.BlockSpec((GMLP_CHUNK, GMLP_WIDTH), lambda i: (i, 0)),) + (HBM_SPEC,) * n_cout,
        scratch_shapes=[pltpu.VMEM((GMLP_CHUNK, GMLP_WIDTH), F32)] + _comm_scratch(c_sems), input_output_aliases=aliases,
        compiler_params=_params("arbitrary" if comm is not None else "parallel"),
    )(*([proj] * 12), ws, bs_t, lg, lb, *c_arrays)
    return out if comm is not None else out[0]


def _gmlp_bwd(proj, ws, bs_t, lg, lb, db_in):
    S = proj.shape[0]
    nb = S // GMLP_CHUNK
    W = GMLP_WIDTH

    def body(*refs):
        u_refs, vg_refs, gb_refs = refs[0:4], refs[4:8], refs[8:12]
        ws_ref, bst_ref, lg_ref, lb_ref, db_ref = refs[12:17]
        out_ref, dws_ref, dbs_ref, dlg_ref, dlb_ref = refs[17:22]
        gv_scr, dvh_scr = refs[22:]
        i = pl.program_id(0)

        @pl.when(i == 0)
        def _():
            dws_ref[...] = jnp.zeros_like(dws_ref)
            dbs_ref[...] = jnp.zeros_like(dbs_ref)
            dlg_ref[...] = jnp.zeros_like(dlg_ref)
            dlb_ref[...] = jnp.zeros_like(dlb_ref)

        mu, rstd = _gmlp_norm_stats(vg_refs, gv_scr)
        sum_dvh = jnp.zeros((GMLP_CHUNK, 1), F32)
        sum_dvh_vh = jnp.zeros((GMLP_CHUNK, 1), F32)
        for g in range(GMLP_GROUPS):
            cols = slice(g * GMLP_GROUP_DIM, (g + 1) * GMLP_GROUP_DIM)
            vhat = (gv_scr[:, cols] - mu) * rstd
            vn = (vhat * lg_ref[:, cols] + lb_ref[:, cols]).astype(BF16)
            w, tril = _tril_bf16(ws_ref, g)
            mixed = jnp.dot(w, vn, preferred_element_type=F32) + bst_ref[:, g:g + 1]
            u, gb, db = _group(u_refs, g), _group(gb_refs, g), db_ref[:, cols]
            gu, sgb = _gelu(u), _sigmoid(gb)
            dsg = db * (gb * sgb)
            out_ref[:, 2 * W + g * GMLP_GROUP_DIM: 2 * W + (g + 1) * GMLP_GROUP_DIM] = (
                db * (gu * mixed) * (sgb * (1.0 + gb * (1.0 - sgb)))).astype(BF16)
            out_ref[:, cols] = (dsg * mixed * _gelu_grad(u)).astype(BF16)
            dmixed = dsg * gu
            dmixed_b = dmixed.astype(BF16)
            dvn = lax.dot_general(w, dmixed_b, (((0,), (0,)), ((), ())), preferred_element_type=F32)
            dw = lax.dot_general(dmixed_b, vn, (((1,), (1,)), ((), ())), preferred_element_type=F32)
            dws_ref[g] += jnp.where(tril, dw, 0.0)
            dbs_ref[g] += jnp.sum(dmixed, axis=1, keepdims=True)
            dlg_ref[:, cols] += jnp.sum(dvn * vhat, axis=0, keepdims=True)
            dlb_ref[:, cols] += jnp.sum(dvn, axis=0, keepdims=True)
            dvh = dvn * lg_ref[:, cols]
            dvh_scr[:, cols] = dvh
            sum_dvh = sum_dvh + jnp.sum(dvh, axis=1, keepdims=True)
            sum_dvh_vh = sum_dvh_vh + jnp.sum(dvh * vhat, axis=1, keepdims=True)
        m1, m2 = sum_dvh / W, sum_dvh_vh / W
        for k in range(N_PARTS):
            cols = slice(k * COL_BLK, (k + 1) * COL_BLK)
            vhat = (gv_scr[:, cols] - mu) * rstd
            dgv = rstd * (dvh_scr[:, cols] - m1 - vhat * m2)
            out_ref[:, W + k * COL_BLK: W + (k + 1) * COL_BLK] = (
                dgv * _gelu_grad(vg_refs[k][...].astype(F32))).astype(BF16)

    full = lambda shape: pl.BlockSpec(shape, lambda i: tuple(0 for _ in shape))
    row = pl.BlockSpec((GMLP_CHUNK, W), lambda i: (i, 0))
    return pl.pallas_call(
        body, name="gmlp_bwd",
        out_shape=(jax.ShapeDtypeStruct((S, 3 * W), BF16), jax.ShapeDtypeStruct(ws.shape, F32),
                   jax.ShapeDtypeStruct((GMLP_GROUPS, GMLP_CHUNK, 1), F32), jax.ShapeDtypeStruct((1, W), F32),
                   jax.ShapeDtypeStruct((1, W), F32)),
        grid=(nb,),
        in_specs=_part_specs(OFF_U) + _part_specs(OFF_VG) + _part_specs(OFF_GB)
        + [full(ws.shape), full(bs_t.shape), full(lg.shape), full(lb.shape), row],
        out_specs=(pl.BlockSpec((GMLP_CHUNK, 3 * W), lambda i: (i, 0)), full(ws.shape),
                   full((GMLP_GROUPS, GMLP_CHUNK, 1)), full((1, W)), full((1, W))),
        scratch_shapes=[pltpu.VMEM((GMLP_CHUNK, W), F32), pltpu.VMEM((GMLP_CHUNK, W), F32)],
        compiler_params=_params("arbitrary"),
    )(*([proj] * 12), ws, bs_t, lg, lb, db_in)


def _merge_fwd(proj, y_a, y_b, rows=256):
    S = proj.shape[0]
    rows = min(rows, S)

    def body(ma_ref, mb_ref, ya_ref, yb_ref, out_ref):
        out_ref[...] = (_sigmoid(ma_ref[...].astype(F32)) * ya_ref[...]
                        + _sigmoid(mb_ref[...].astype(F32)) * yb_ref[...]).astype(BF16)

    blk = lambda off: pl.BlockSpec((rows, COL_BLK), lambda i, j: (i, off // COL_BLK + j))
    return pl.pallas_call(
        body, name="merge_fwd", out_shape=jax.ShapeDtypeStruct((S, D_MODEL), BF16), grid=(S // rows, D_MODEL // COL_BLK),
        in_specs=[blk(OFF_MA), blk(OFF_MB), blk(0), blk(0)], out_specs=blk(0),
        compiler_params=_params("parallel", "parallel"),
    )(proj, proj, y_a, y_b)


def _merge_bwd(proj, y_a, y_b, dmerged, rows=128):
    S = proj.shape[0]
    rows = min(rows, S)
    nj = D_MODEL // COL_BLK

    def body(*refs):
        ma_refs, mb_refs = refs[0:nj], refs[nj:2 * nj]
        ya_ref, yb_ref, dm_ref, dya_ref, dyb_ref, dg_ref = refs[2 * nj:]
        for k in range(nj):
            cols = slice(k * COL_BLK, (k + 1) * COL_BLK)
            dm = dm_ref[:, cols]
            sa, sb = _sigmoid(ma_refs[k][...].astype(F32)), _sigmoid(mb_refs[k][...].astype(F32))
            dya_ref[:, cols] = (dm * sa).astype(BF16)
            dyb_ref[:, cols] = (dm * sb).astype(BF16)
            dg_ref[:, cols] = (dm * ya_ref[:, cols] * sa * (1.0 - sa)).astype(BF16)
            dg_ref[:, D_MODEL + k * COL_BLK: D_MODEL + (k + 1) * COL_BLK] = (
                dm * yb_ref[:, cols] * sb * (1.0 - sb)).astype(BF16)

    part = lambda off: [pl.BlockSpec((rows, COL_BLK), functools.partial(lambda j, i: (i, j), off // COL_BLK + k))
                        for k in range(nj)]
    row = pl.BlockSpec((rows, D_MODEL), lambda i: (i, 0))
    return pl.pallas_call(
        body, name="merge_bwd",
        out_shape=(jax.ShapeDtypeStruct((S, D_MODEL), BF16), jax.ShapeDtypeStruct((S, D_MODEL), BF16),
                   jax.ShapeDtypeStruct((S, 2 * D_MODEL), BF16)),
        grid=(S // rows,), in_specs=part(OFF_MA) + part(OFF_MB) + [row, row, row],
        out_specs=(row, row, pl.BlockSpec((rows, 2 * D_MODEL), lambda i: (i, 0))),
        compiler_params=_params("parallel"),
    )(*([proj] * (2 * nj)), y_a, y_b, dmerged)


def _loss_head(x2, target, fg, rows=128):
    S, D = x2.shape
    rows = min(rows, S)

    def body(x_ref, t_ref, g_ref, dx_ref, dxb_ref, dg_ref, loss_ref):
        i = pl.program_id(0)

        @pl.when(i == 0)
        def _():
            dg_ref[...] = jnp.zeros_like(dg_ref)
            loss_ref[...] = jnp.zeros_like(loss_ref)

        xv, g = x_ref[...], g_ref[...]
        rstd = lax.rsqrt(jnp.mean(xv * xv, axis=-1, keepdims=True) + NORM_EPS)
        xhat = xv * rstd
        err = xhat * g - t_ref[...]
        loss_ref[...] += (0.5 / D) * jnp.sum(err * err)
        dy = err * (1.0 / D)
        dg_ref[...] += jnp.sum(dy * xhat, axis=0, keepdims=True)
        dxh = dy * g
        dx = rstd * (dxh - xhat * jnp.mean(dxh * xhat, axis=-1, keepdims=True))
        dx_ref[...] = dx
        dxb_ref[...] = dx.astype(BF16)

    row = pl.BlockSpec((rows, D), lambda i: (i, 0))
    vec = pl.BlockSpec((1, D), lambda i: (0, 0))
    return pl.pallas_call(
        body, name="loss_head",
        out_shape=(jax.ShapeDtypeStruct((S, D), F32), jax.ShapeDtypeStruct((S, D), BF16),
                   jax.ShapeDtypeStruct((1, D), F32), jax.ShapeDtypeStruct((8, LANES), F32)),
        grid=(S // rows,), in_specs=[row, row, vec],
        out_specs=(row, row, vec, pl.BlockSpec((8, LANES), lambda i: (0, 0))),
        compiler_params=_params("arbitrary"),
    )(x2, target, fg)


def _rms_bwd(x, dh, dx2, g, rows=128):
    S, D = x.shape
    rows = min(rows, S)

    def body(x_ref, dh_ref, dx2_ref, g_ref, gx_ref, dg_ref):
        i = pl.program_id(0)

        @pl.when(i == 0)
        def _():
            dg_ref[...] = jnp.zeros_like(dg_ref)

        xv, dh_v = x_ref[...], dh_ref[...]
        rstd = lax.rsqrt(jnp.mean(xv * xv, axis=-1, keepdims=True) + NORM_EPS)
        xhat = xv * rstd
        dg_ref[...] += jnp.sum(dh_v * xhat, axis=0, keepdims=True)
        dxh = dh_v * g_ref[...]
        gx_ref[...] = dx2_ref[...] + rstd * (dxh - xhat * jnp.mean(dxh * xhat, axis=-1, keepdims=True))

    row = pl.BlockSpec((rows, D), lambda i: (i, 0))
    vec = pl.BlockSpec((1, D), lambda i: (0, 0))
    return pl.pallas_call(
        body, name="rms_bwd", out_shape=(jax.ShapeDtypeStruct((S, D), F32), jax.ShapeDtypeStruct((1, D), F32)),
        grid=(S // rows,), in_specs=[row, row, row, vec], out_specs=(row, vec),
        compiler_params=_params("arbitrary"),
    )(x, dh, dx2, g)


def _adamw(w, g, m, v, name, rows=64):
    R, C = w.shape
    rows = min(rows, R)
    c1 = 1.0 - ADAM_B1 ** ADAM_STEP
    c2 = 1.0 - ADAM_B2 ** ADAM_STEP

    def body(w_ref, g_ref, m_ref, v_ref, go_ref, d_ref, mo_ref, vo_ref):
        gv = g_ref[...]
        mn = ADAM_B1 * m_ref[...] + (1.0 - ADAM_B1) * gv
        vn = ADAM_B2 * v_ref[...] + (1.0 - ADAM_B2) * (gv * gv)
        go_ref[...] = gv
        mo_ref[...] = mn
        vo_ref[...] = vn
        d_ref[...] = -ADAM_LR * ((mn / c1) / (jnp.sqrt(vn / c2) + ADAM_EPS) + ADAM_WD * w_ref[...])

    spec = pl.BlockSpec((rows, C), lambda i: (i, 0))
    shape = jax.ShapeDtypeStruct((R, C), F32)
    return pl.pallas_call(body, name=name, out_shape=(shape,) * 4, grid=(R // rows,), in_specs=[spec] * 4,
                          out_specs=(spec,) * 4, compiler_params=_params("parallel"))(w, g, m, v)


def _place():
    x, y, c = lax.axis_index("x"), lax.axis_index("y"), lax.axis_index("c")
    others = [(1 - x, y), (x, 1 - y), (1 - x, 1 - y)]
    return x, y, c, others


def _chip_index(chip):
    return 2 * chip[0] + chip[1]


def _remote(src, dst, sems, k, to):
    send_sems, recv_sems = sems
    return pltpu.make_async_remote_copy(src_ref=src, dst_ref=dst, send_sem=send_sems.at[k], recv_sem=recv_sems.at[k],
                                        device_id=to, device_id_type=MESH)


def _comm_fields(comm):
    return comm if comm is not None else ((), (), 0, None, False)


def _comm_scratch(n_sems):
    return [pltpu.SemaphoreType.DMA((n_sems,)), pltpu.SemaphoreType.DMA((n_sems,))] if n_sems else []


def _start_when(cond, copies):
    @pl.when(cond)
    def _():
        for cp in copies:
            cp.start()


def _wait_when(cond, copies):
    @pl.when(cond)
    def _():
        for cp in copies:
            cp.wait()


def _proj_gather(order, h, w_full, tm=1024, tk=512):
    S, Dm = h.shape
    tm, tk = min(tm, S), min(tk, Dm)
    tn = SLAB // 2
    ni, nj, nk = S // tm, 2 * N_CHIPS, Dm // tk
    total = nj * ni * nk
    half_in = Dm // 2

    def body(order_ref, h_ref, _wi, proj_ref, fi_ref, acc, bbuf, bsem, send_sems, recv_sems):
        j, i, k = pl.program_id(0), pl.program_id(1), pl.program_id(2)
        t = (j * ni + i) * nk + k
        x, y, c, others = _place()
        me_chip = 2 * x + y
        sems = (send_sems, recv_sems)

        def piece(chip, half):
            return fi_ref.at[pl.ds(half * half_in, half_in), pl.ds(chip * SLAB, SLAB)]

        def ici_send(r):
            mine = piece(me_chip, c)
            return _remote(mine, mine, sems, r, (*others[r], c))

        def forward():
            src_chip = (x + (1 - c) * (1 - 2 * x), y + c * (1 - 2 * y))
            dst_chip = (x + c * (1 - 2 * x), y + (1 - c) * (1 - 2 * y))
            got = piece(_chip_index(src_chip), c)
            return _remote(got, got, sems, 2, (*dst_chip, c))

        def ici_recv(r):
            got = piece(_chip_index(others[r]), c)
            return _remote(got, got, sems, r, (x, y, c))

        def pass_on(r):
            got = piece(_chip_index(others[r]), c)
            return _remote(got, got, sems, 3 + r, (x, y, 1 - c))

        def passed_recv(r):
            got = piece(_chip_index(others[r]), 1 - c)
            return _remote(got, got, sems, 3 + r, (x, y, c))

        def b_copy(jj, kk, slot):
            col = 2 * order_ref[jj // 2] + jj % 2
            return pltpu.make_async_copy(fi_ref.at[pl.ds(kk * tk, tk), pl.ds(col * tn, tn)], bbuf.at[slot], bsem.at[slot])

        @pl.when(t == 0)
        def _():
            ici_send(0).start()
            ici_send(1).start()
            b_copy(0, 0, 0).start()

        nxt = t + 1
        kn, i_n, jn = nxt % nk, (nxt // nk) % ni, nxt // (nk * ni)

        def opens(r):
            return (jn == 2 * (r + 1)) & (i_n == 0) & (kn == 0)

        @pl.when(nxt < total)
        def _():
            @pl.when(opens(0))
            def _():
                ici_recv(0).wait_recv()
                ici_recv(1).wait_recv()
                forward().start()
                pass_on(0).start()
                pass_on(1).start()
                passed_recv(0).wait_recv()

            @pl.when(opens(1))
            def _():
                passed_recv(1).wait_recv()

            @pl.when(opens(2))
            def _():
                ici_recv(2).wait_recv()
                pass_on(2).start()
                passed_recv(2).wait_recv()

            b_copy(jn, kn, nxt % 2).start()

        b_copy(j, k, t % 2).wait()

        def product():
            return jnp.dot(h_ref[...], bbuf[t % 2], preferred_element_type=F32)

        @pl.when(k == 0)
        def _():
            acc[...] = product()

        @pl.when((k > 0) & (k < nk - 1))
        def _():
            acc[...] += product()

        @pl.when(k == nk - 1)
        def _():
            proj_ref[...] = (acc[...] + product()).astype(BF16)

        @pl.when(t == total - 1)
        def _():
            ici_send(0).wait_send()
            ici_send(1).wait_send()
            forward().wait_send()
            for r in range(3):
                pass_on(r).wait_send()

    col_of = lambda j, order: 2 * order[j // 2] + j % 2
    grid_spec = pltpu.PrefetchScalarGridSpec(
        num_scalar_prefetch=1, grid=(nj, ni, nk),
        in_specs=[pl.BlockSpec((tm, tk), lambda j, i, k, order: (i, k)), HBM_SPEC],
        out_specs=(pl.BlockSpec((tm, tn), lambda j, i, k, order: (i, col_of(j, order))), HBM_SPEC),
        scratch_shapes=[pltpu.VMEM((tm, tn), F32), pltpu.VMEM((2, tk, tn), BF16), pltpu.SemaphoreType.DMA((2,)),
                        pltpu.SemaphoreType.DMA((6,)), pltpu.SemaphoreType.DMA((6,))])
    return pl.pallas_call(
        body, name="proj_gather",
        out_shape=(jax.ShapeDtypeStruct((S, PROJ_WIDTH), BF16), jax.ShapeDtypeStruct(w_full.shape, BF16)),
        grid_spec=grid_spec, input_output_aliases={2: 1},
        compiler_params=_params("arbitrary", "arbitrary", "arbitrary"),
    )(order, h, w_full)


def _exchange(comm, name):
    arrays, shapes, n_sems, build, aliased = comm
    n_in, n_out = len(arrays), len(shapes)

    def body(*refs):
        copies = build(refs[:n_in], refs[n_in:n_in + n_out], refs[n_in + n_out:])
        for cp in copies:
            cp.start()
        for cp in copies:
            cp.wait()

    return pl.pallas_call(
        body, name=name, out_shape=tuple(shapes), in_specs=[HBM_SPEC] * n_in, out_specs=(HBM_SPEC,) * n_out,
        scratch_shapes=_comm_scratch(n_sems), input_output_aliases={t: t for t in range(n_in)} if aliased else {},
    )(*arrays)


def _pair_halves(parts):
    def half_of(ref, which):
        if len(ref.shape) == 2:
            rows = ref.shape[0] // 2
            return ref.at[pl.ds(which * rows, rows), :]
        return ref.at[:, pl.ds(which, 1)]

    def out_shape(p):
        if p.ndim == 2:
            return jax.ShapeDtypeStruct((p.shape[0] // 2, p.shape[1]), BF16)
        return jax.ShapeDtypeStruct((p.shape[0], 1) + p.shape[2:], BF16)

    def build(p_refs, r_refs, sems):
        x, y, c, _ = _place()
        return [_remote(half_of(p, 1 - c), r, sems, t, (x, y, 1 - c)) for t, (p, r) in enumerate(zip(p_refs, r_refs))]

    return (tuple(parts), tuple(out_shape(p) for p in parts), len(parts), build, False)


def _gather_squares(fulls, over_ici):
    half_up = UP_ROWS // 2

    def build(_, f_refs, sems):
        x, y, c, others = _place()
        copies = []
        for t, f in enumerate(f_refs):
            for r, chip in enumerate(others):
                src_chip = 2 * x + y if over_ici else _chip_index(chip)
                rows = f.at[pl.ds(src_chip * UP_ROWS + c * half_up, half_up), :]
                copies.append(_remote(rows, rows, sems, 3 * t + r, (*chip, c) if over_ici else (x, y, 1 - c)))
        return copies

    return (tuple(fulls), tuple(jax.ShapeDtypeStruct(f.shape, f.dtype) for f in fulls), 3 * len(fulls), build, True)


def _send_whole(parts):
    def build(p_refs, r_refs, sems):
        x, y, c, _ = _place()
        return [_remote(p, r, sems, t, (x, y, 1 - c)) for t, (p, r) in enumerate(zip(p_refs, r_refs))]

    return (tuple(parts), tuple(jax.ShapeDtypeStruct(p.shape, p.dtype) for p in parts), len(parts), build, False)


def _pair_add_in(p_own, r_in, rows=256, cols=SLAB):
    half, P = p_own.shape
    rows = min(rows, half)

    def body(p_ref, r_ref, o_ref):
        o_ref[...] = (p_ref[...].astype(F32) + r_ref[...].astype(F32)).astype(BF16)

    spec = pl.BlockSpec((rows, cols), lambda i, j: (i, j))
    return pl.pallas_call(body, name="pair_add_in", out_shape=jax.ShapeDtypeStruct((half, P), BF16),
                          grid=(half // rows, P // cols), in_specs=[spec, spec], out_specs=spec,
                          compiler_params=_params("parallel", "parallel"))(p_own, r_in)


def _pair_add_up(idx, p_up, r_up, name):
    _, _, R, C = p_up.shape

    def body(idx_ref, p_ref, r_ref, o_ref):
        o_ref[...] = (p_ref[...].astype(F32) + r_ref[...].astype(F32)).astype(BF16)

    grid_spec = pltpu.PrefetchScalarGridSpec(
        num_scalar_prefetch=1, grid=(N_CHIPS,),
        in_specs=[pl.BlockSpec((1, 1, R, C), lambda j, idx: (j, idx[0], 0, 0)),
                  pl.BlockSpec((1, 1, R, C), lambda j, idx: (j, 0, 0, 0))],
        out_specs=pl.BlockSpec((1, 1, R, C), lambda j, idx: (j, 0, 0, 0)))
    return pl.pallas_call(body, name=name, out_shape=jax.ShapeDtypeStruct((N_CHIPS, 1, R, C), BF16), grid_spec=grid_spec,
                          compiler_params=_params("parallel"))(idx, p_up, r_up)


def _slab_exchange(qs):
    n = len(qs)

    def out_shape(q):
        if q.ndim == 2:
            return jax.ShapeDtypeStruct((3, q.shape[0], SLAB), BF16)
        return jax.ShapeDtypeStruct((3,) + q.shape[1:], BF16)

    def build(q_refs, r_refs, sems):
        _, _, c, others = _place()
        copies = []
        for r, chip in enumerate(others):
            ci = _chip_index(chip)
            for t in range(n):
                src = q_refs[t].at[:, pl.ds(ci * SLAB, SLAB)] if len(q_refs[t].shape) == 2 else q_refs[t].at[ci]
                copies.append(_remote(src, r_refs[t].at[r], sems, 3 * t + r, (*chip, c)))
        return copies

    return (tuple(qs), tuple(out_shape(q) for q in qs), 3 * n, build, False)


def _slab_add_in(idx, q_in, r2_in, rows=128):
    half = q_in.shape[0]
    rows = min(rows, half)
    nrb = half // rows

    def body(idx_ref, q_ref, r_ref, o_ref):
        o_ref[...] = ((q_ref[...].astype(F32) + r_ref[0].astype(F32)) + r_ref[1].astype(F32)) + r_ref[2].astype(F32)

    grid_spec = pltpu.PrefetchScalarGridSpec(
        num_scalar_prefetch=1, grid=(nrb,),
        in_specs=[pl.BlockSpec((rows, SLAB), lambda i, idx: (i, idx[1])),
                  pl.BlockSpec((3, rows, SLAB), lambda i, idx: (0, i, 0))],
        out_specs=pl.BlockSpec((rows, SLAB), lambda i, idx: (idx[0] * nrb + i, 0)))
    return pl.pallas_call(body, name="slab_add_in", out_shape=jax.ShapeDtypeStruct((2 * half, SLAB), F32),
                          grid_spec=grid_spec, compiler_params=_params("parallel"))(idx, q_in, r2_in)


def _slab_add_up(idx, q_up, r2_up, name, rows=128):
    _, _, R, C = q_up.shape
    rows = min(rows, R)
    nrb = R // rows

    def body(idx_ref, q_ref, r_ref, o_ref):
        o_ref[...] = ((q_ref[0, 0].astype(F32) + r_ref[0, 0].astype(F32)) + r_ref[1, 0].astype(F32)) + r_ref[2, 0].astype(F32)

    grid_spec = pltpu.PrefetchScalarGridSpec(
        num_scalar_prefetch=1, grid=(nrb,),
        in_specs=[pl.BlockSpec((1, 1, rows, C), lambda i, idx: (idx[1], 0, i, 0)),
                  pl.BlockSpec((3, 1, rows, C), lambda i, idx: (0, 0, i, 0))],
        out_specs=pl.BlockSpec((rows, C), lambda i, idx: (idx[0] * nrb + i, 0)))
    return pl.pallas_call(body, name=name, out_shape=jax.ShapeDtypeStruct((2 * R, C), F32), grid_spec=grid_spec,
                          compiler_params=_params("parallel"))(idx, q_up, r2_up)


def _pair_share(gs):
    def build(_, g_refs, sems):
        x, y, c, _ = _place()
        copies = []
        for t, g in enumerate(g_refs):
            rows = g.shape[0] // 2
            mine = g.at[pl.ds(c * rows, rows), :]
            copies.append(_remote(mine, mine, sems, t, (x, y, 1 - c)))
        return copies

    return (tuple(gs), tuple(jax.ShapeDtypeStruct(g.shape, g.dtype) for g in gs), len(gs), build, True)


def _all_reduce_small(packed):
    R, C = packed.shape
    N_DEV = 2 * N_CHIPS

    def body(x_ref, out_ref, all_ref, send_sems, recv_sems, local_sem):
        x, y, c, others = _place()
        me, sib = (x, y, c), (x, y, 1 - c)
        sems = (send_sems, recv_sems)

        def rows(px, py, pc):
            return all_ref.at[4 * px + 2 * py + pc]

        mine = pltpu.make_async_copy(x_ref, rows(*me), local_sem)
        mine.start()
        first = [_remote(x_ref, rows(*me), sems, 0, sib)]
        first += [_remote(x_ref, rows(*me), sems, 1 + j, (*chip, c)) for j, chip in enumerate(others)]
        for cp in first:
            cp.start()
        passed = [_remote(rows(*chip, c), rows(*chip, c), sems, 4 + j, sib) for j, chip in enumerate(others)]
        for j, chip in enumerate(others):
            _remote(rows(*chip, c), rows(*chip, c), sems, 1 + j, me).wait_recv()
            passed[j].start()
        _remote(rows(*sib), rows(*sib), sems, 0, me).wait_recv()
        for j, chip in enumerate(others):
            _remote(rows(*chip, 1 - c), rows(*chip, 1 - c), sems, 4 + j, me).wait_recv()
        for cp in first + passed:
            cp.wait_send()
        mine.wait()
        total = all_ref[0]
        for d in range(1, N_DEV):
            total = total + all_ref[d]
        out_ref[...] = total

    vmem = pl.BlockSpec(memory_space=pltpu.VMEM)
    return pl.pallas_call(
        body, name="all_reduce_small", out_shape=jax.ShapeDtypeStruct((R, C), F32),
        in_specs=[vmem], out_specs=vmem,
        scratch_shapes=[pltpu.VMEM((N_DEV, R, C), F32), pltpu.SemaphoreType.DMA((7,)), pltpu.SemaphoreType.DMA((7,)),
                        pltpu.SemaphoreType.DMA],
        compiler_params=pltpu.CompilerParams(vmem_limit_bytes=VMEM_LIMIT),
    )(packed)


SMALL_NAMES = ("w_spatial", "b_spatial", "norm_g", "gmlp_ln_g", "gmlp_ln_b", "final_norm_g", "attn_sink")


def _pack_small(parts, extra=None):
    blocks = []
    for n in SMALL_NAMES:
        flat = parts[n].reshape(-1).astype(F32)
        rows = -(-flat.shape[0] // (8 * LANES)) * 8
        flat = jnp.pad(flat, (0, rows * LANES - flat.shape[0]))
        blocks.append(flat.reshape(rows, LANES))
    blocks.append(jnp.zeros((8, LANES), F32) if extra is None else extra)
    return jnp.concatenate(blocks, axis=0)


def _unpack_small(packed, shapes):
    out, r = {}, 0
    for n in SMALL_NAMES:
        size = 1
        for s in shapes[n]:
            size *= s
        rows = -(-size // (8 * LANES)) * 8
        out[n] = packed[r:r + rows].reshape(-1)[:size].reshape(shapes[n])
        r += rows
    return out, packed[r:r + 8]


def kernel(x, positions, norm_g, w_in, attn_sink, gmlp_ln_g, gmlp_ln_b, w_spatial, b_spatial, w_up_attn, w_up_gmlp, w_out, final_norm_g, loss_target, m_norm_g, m_w_in, m_attn_sink, m_gmlp_ln_g, m_gmlp_ln_b, m_w_spatial, m_b_spatial, m_w_up_attn, m_w_up_gmlp, m_w_out, m_final_norm_g, v_norm_g, v_w_in, v_attn_sink, v_gmlp_ln_g, v_gmlp_ln_b, v_w_spatial, v_b_spatial, v_w_up_attn, v_w_up_gmlp, v_w_out, v_final_norm_g):
    xs, tgt = x[0], loss_target[0]
    mx, my, mc = lax.axis_index("x"), lax.axis_index("y"), lax.axis_index("c")
    idx = jnp.stack([mc, 2 * mx + my]).astype(jnp.int32)
    arrival = jnp.stack([2 * mx + my, 2 * (1 - mx) + my, 2 * mx + (1 - my), 2 * (1 - mx) + (1 - my)]).astype(jnp.int32)

    square = (D_MODEL, D_MODEL)
    w_full = _cast_into(idx, w_in[0], (D_MODEL, PROJ_WIDTH), "cast_w_in")
    ups = [_cast_into(idx, w_up_attn[0], square, "cast_w_up_attn"), _cast_into(idx, w_up_gmlp[0], square, "cast_w_up_gmlp"),
           _cast_into(idx, w_out[0], square, "cast_w_out")]

    tables = _rope_tables(positions[0])
    bias = _band_bias()
    sink = attn_sink[0]
    ws, bs_t = w_spatial[0], b_spatial[0].T
    h = _rms_fwd(xs, norm_g)
    proj, w_full = _proj_gather(arrival, h, w_full)
    attn, a_in, *got = _attn_fwd(proj, tables, bias, sink, comm=_gather_squares(ups[:2], over_ici=True))
    b_in, wua, wug = _gmlp_fwd(proj, ws, bs_t, gmlp_ln_g, gmlp_ln_b, comm=_gather_squares(got, over_ici=False))
    y_a, wo = _matmul(a_in, wua, mode="nn", out_dtype=F32, name="up_attn", comm=_gather_squares(ups[2:], over_ici=True))
    y_b, wo = _matmul(b_in, wug, mode="nn", out_dtype=F32, name="up_gmlp", comm=_gather_squares([wo], over_ici=False))
    merged = _merge_fwd(proj, y_a, y_b)
    x2 = _matmul(merged, wo, mode="nn", out_dtype=F32, name="out_proj", residual=xs, tk=2048)
    dx2, dx2_b, d_fg, loss_part = _loss_head(x2, tgt, final_norm_g.reshape(1, D_MODEL))

    dmerged = _matmul(dx2_b, wo, mode="nt", out_dtype=F32, name="d_merged")
    dy_a, dy_b, d_gates = _merge_bwd(proj, y_a, y_b, dmerged)

    p_ups = [_matmul(a_in, dy_a, mode="tn", out_dtype=BF16, name="dw_up_attn"),
             _matmul(b_in, dy_b, mode="tn", out_dtype=BF16, name="dw_up_gmlp"),
             _matmul(merged, dx2_b, mode="tn", out_dtype=BF16, name="dw_out")]
    p_ups = [p.reshape(N_CHIPS, 2, UP_ROWS // 2, D_MODEL) for p in p_ups]
    da_in, *r1_ups = _matmul(dy_a, wua, mode="nt", out_dtype=F32, name="d_a_in", comm=_pair_halves(p_ups))
    q_ups = [_pair_add_up(idx, p, r, "pair_add_up%d" % t) for t, (p, r) in enumerate(zip(p_ups, r1_ups))]
    db_in = _matmul(dy_b, wug, mode="nt", out_dtype=F32, name="d_b_in")
    dq, dk, dv, dga, d_sink = _attn_bwd(proj, tables, bias, sink, attn, da_in)
    d_gmlp, d_ws, d_bs, d_lg, d_lb = _gmlp_bwd(proj, ws, bs_t, gmlp_ln_g, gmlp_ln_b, db_in)
    dproj = jnp.concatenate([dq, dk, dv, dga, d_gmlp, d_gates], axis=1)

    half = D_MODEL // 2
    h_sib = lax.dynamic_slice(h, (0, (1 - mc) * half), (h.shape[0], half))
    h_own = lax.dynamic_slice(h, (0, mc * half), (h.shape[0], half))
    p_sib, *r2_ups = _matmul(h_sib, dproj, mode="tn", out_dtype=BF16, name="dw_in_sibling", comm=_slab_exchange(q_ups))
    p_own, r1_in = _matmul(h_own, dproj, mode="tn", out_dtype=BF16, name="dw_in_own", comm=_send_whole([p_sib]))
    q_in = _pair_add_in(p_own, r1_in)
    dh, r2_in = _matmul(dproj, w_full, mode="nt", out_dtype=F32, name="d_h", comm=_slab_exchange([q_in]), tk=SLAB // 2)
    grad_x, d_ng = _rms_bwd(xs, dh, dx2, norm_g)

    g_half = [_slab_add_in(idx, q_in, r2_in)] + [_slab_add_up(idx, q, r, "slab_add_up%d" % t)
                                                 for t, (q, r) in enumerate(zip(q_ups, r2_ups))]
    g_big = _exchange(_pair_share(g_half), "pair_share")

    small_shapes = {"w_spatial": w_spatial.shape, "b_spatial": b_spatial.shape, "norm_g": norm_g.shape,
                    "gmlp_ln_g": gmlp_ln_g.shape, "gmlp_ln_b": gmlp_ln_b.shape, "final_norm_g": final_norm_g.shape,
                    "attn_sink": attn_sink.shape}
    d_small = {"w_spatial": d_ws, "b_spatial": d_bs, "norm_g": d_ng, "gmlp_ln_g": d_lg, "gmlp_ln_b": d_lb,
               "final_norm_g": d_fg, "attn_sink": d_sink[:, 0, :HEADS_PER_STEP]}
    g_small = _all_reduce_small(_pack_small(d_small, loss_part))

    w_small = _pack_small(dict(w_spatial=w_spatial, b_spatial=b_spatial, norm_g=norm_g, gmlp_ln_g=gmlp_ln_g,
                               gmlp_ln_b=gmlp_ln_b, final_norm_g=final_norm_g, attn_sink=attn_sink))
    m_small = _pack_small(dict(w_spatial=m_w_spatial, b_spatial=m_b_spatial, norm_g=m_norm_g, gmlp_ln_g=m_gmlp_ln_g,
                               gmlp_ln_b=m_gmlp_ln_b, final_norm_g=m_final_norm_g, attn_sink=m_attn_sink))
    v_small = _pack_small(dict(w_spatial=v_w_spatial, b_spatial=v_b_spatial, norm_g=v_norm_g, gmlp_ln_g=v_gmlp_ln_g,
                               gmlp_ln_b=v_gmlp_ln_b, final_norm_g=v_final_norm_g, attn_sink=v_attn_sink))
    small_out = _adamw(w_small, g_small, m_small, v_small, "adamw_small", rows=w_small.shape[0])
    big = {
        "w_in": _adamw(w_in[0], g_big[0], m_w_in[0], v_w_in[0], "adamw_w_in"),
        "w_up_attn": _adamw(w_up_attn[0], g_big[1], m_w_up_attn[0], v_w_up_attn[0], "adamw_w_up_attn"),
        "w_up_gmlp": _adamw(w_up_gmlp[0], g_big[2], m_w_up_gmlp[0], v_w_up_gmlp[0], "adamw_w_up_gmlp"),
        "w_out": _adamw(w_out[0], g_big[3], m_w_out[0], v_w_out[0], "adamw_w_out"),
    }

    order = ("norm_g", "w_in", "attn_sink", "gmlp_ln_g", "gmlp_ln_b", "w_spatial", "b_spatial", "w_up_attn", "w_up_gmlp",
             "w_out", "final_norm_g")
    outs = []
    loss = None
    for kind in range(4):
        small, extra = _unpack_small(small_out[kind], small_shapes)
        if kind == 0:
            loss = extra[0, 0]
        for n in order:
            outs.append(big[n][kind][None] if n in big else small[n])
    return (loss, grad_x[None], *outs)
```

```python
import functools

import jax
import jax.numpy as jnp
from jax import lax
from jax.experimental import pallas as pl
from jax.experimental.pallas import tpu as pltpu

F32 = jnp.float32
BF16 = jnp.bfloat16

D_MODEL = 4096
N_Q_HEADS = 64
N_KV_HEADS = 8
HEAD_DIM = 64
Q_PER_KV = N_Q_HEADS // N_KV_HEADS
ATTN_WIDTH = N_Q_HEADS * HEAD_DIM
KV_WIDTH = N_KV_HEADS * HEAD_DIM
WINDOW = 128
BLOCK = 128
ROPE_THETA = 500000.0
ROPE_DIM = HEAD_DIM // 4
ROPE_HALF = ROPE_DIM // 2
GMLP_WIDTH = D_MODEL
GMLP_GROUPS = 8
GMLP_GROUP_DIM = GMLP_WIDTH // GMLP_GROUPS
GMLP_CHUNK = 128
NORM_EPS = 1e-5
LN_EPS = 1e-5

PROJ_SIZES = (ATTN_WIDTH, KV_WIDTH, KV_WIDTH, ATTN_WIDTH, GMLP_WIDTH, GMLP_WIDTH, GMLP_WIDTH, D_MODEL, D_MODEL)
PROJ_WIDTH = sum(PROJ_SIZES)
OFF_Q, OFF_K, OFF_V, OFF_GA, OFF_U, OFF_VG, OFF_GB, OFF_MA, OFF_MB = (
    sum(PROJ_SIZES[:i]) for i in range(len(PROJ_SIZES)))

ADAM_LR = 0.001
ADAM_B1 = 0.9
ADAM_B2 = 0.999
ADAM_EPS = 1e-08
ADAM_WD = 0.01
ADAM_STEP = 10

N_CHIPS = 4
SLAB = PROJ_WIDTH // N_CHIPS
UP_ROWS = D_MODEL // N_CHIPS
LANES = 128
COL_BLK = 1024
HEADS_PER_STEP = 2 * Q_PER_KV
VMEM_LIMIT = 56 * 1024 * 1024

MESH = pl.DeviceIdType.MESH
HBM_SPEC = pl.BlockSpec(memory_space=pltpu.HBM)
SMEM_SPEC = pl.BlockSpec(memory_space=pltpu.SMEM)


def _params(*sem):
    return pltpu.CompilerParams(dimension_semantics=sem, vmem_limit_bytes=VMEM_LIMIT)


def _sigmoid(x):
    return jax.nn.sigmoid(x)


def _gelu(x):
    return jax.nn.gelu(x, approximate=True)


def _gelu_grad(x):
    c = 0.7978845608028654
    inner = c * (x + 0.044715 * x * x * x)
    t = jnp.tanh(inner)
    return 0.5 * (1.0 + t) + 0.5 * x * (1.0 - t * t) * c * (1.0 + 3 * 0.044715 * x * x)


def _matmul(a, b, *, mode, out_dtype, name, residual=None, comm=None, tm=1024, tn=1024, tk=4096):
    if mode == "nn":
        (M, K), N = a.shape, b.shape[1]
    elif mode == "nt":
        (M, K), N = a.shape, b.shape[0]
    else:
        (K, M), N = a.shape, b.shape[1]
    tm, tn, tk = min(tm, M), min(tn, N), min(tk, K)
    assert M % tm == 0 and N % tn == 0 and K % tk == 0
    ni, nj, nk = M // tm, N // tn, K // tk
    if mode == "nn":
        a_spec = pl.BlockSpec((tm, tk), lambda i, j, k: (i, k))
        b_spec = pl.BlockSpec((tk, tn), lambda i, j, k: (k, j))
        dims = (((1,), (0,)), ((), ()))
    elif mode == "nt":
        a_spec = pl.BlockSpec((tm, tk), lambda i, j, k: (i, k))
        b_spec = pl.BlockSpec((tn, tk), lambda i, j, k: (j, k))
        dims = (((1,), (1,)), ((), ()))
    else:
        a_spec = pl.BlockSpec((tk, tm), lambda i, j, k: (k, i))
        b_spec = pl.BlockSpec((tk, tn), lambda i, j, k: (k, j))
        dims = (((0,), (0,)), ((), ()))
    o_spec = pl.BlockSpec((tm, tn), lambda i, j, k: (i, j))
    n_res = 1 if residual is not None else 0
    c_arrays, c_shapes, c_sems, c_build, c_alias = _comm_fields(comm)
    n_cin, n_cout = len(c_arrays), len(c_shapes)

    def body(*refs):
        a_ref, b_ref = refs[0], refs[1]
        r_ref = refs[2] if n_res else None
        cin = refs[2 + n_res:2 + n_res + n_cin]
        o_ref = refs[2 + n_res + n_cin]
        cout = refs[3 + n_res + n_cin:3 + n_res + n_cin + n_cout]
        n_acc = 1 if nk > 1 else 0
        acc = refs[3 + n_res + n_cin + n_cout] if n_acc else None
        sems = refs[3 + n_acc + n_res + n_cin + n_cout:]
        i, j, k = pl.program_id(0), pl.program_id(1), pl.program_id(2)
        if comm is not None:
            _start_when((i == 0) & (j == 0) & (k == 0), c_build(cin, cout, sems))

        def product():
            return lax.dot_general(a_ref[...], b_ref[...], dims, preferred_element_type=F32)

        def finish(r):
            if n_res:
                r = r + r_ref[...]
            o_ref[...] = r.astype(out_dtype)

        if nk == 1:
            finish(product())
        else:
            @pl.when(k == 0)
            def _():
                acc[...] = product()

            @pl.when((k > 0) & (k < nk - 1))
            def _():
                acc[...] += product()

            @pl.when(k == nk - 1)
            def _():
                finish(acc[...] + product())

        if comm is not None:
            _wait_when((i == ni - 1) & (j == nj - 1) & (k == nk - 1), c_build(cin, cout, sems))

    args = (a, b) + ((residual,) if n_res else ()) + tuple(c_arrays)
    in_specs = [a_spec, b_spec] + ([o_spec] if n_res else []) + [HBM_SPEC] * n_cin
    scratch = ([pltpu.VMEM((tm, tn), F32)] if nk > 1 else []) + _comm_scratch(c_sems)
    sem = ("arbitrary",) * 3 if comm is not None else ("parallel", "parallel", "arbitrary")
    aliases = {2 + n_res + t: 1 + t for t in range(n_cin)} if c_alias else {}
    out = pl.pallas_call(
        body, name=name, out_shape=(jax.ShapeDtypeStruct((M, N), out_dtype),) + tuple(c_shapes),
        grid=(ni, nj, nk), in_specs=in_specs, out_specs=(o_spec,) + (HBM_SPEC,) * n_cout,
        scratch_shapes=scratch, input_output_aliases=aliases, compiler_params=_params(*sem),
    )(*args)
    return out if comm is not None else out[0]


def _cast_into(idx, w, full_shape, name, rows=64):
    R, C = w.shape
    rows = min(rows, R)
    nrb = R // rows
    by_cols = C != full_shape[1]

    def body(idx_ref, w_ref, o_ref):
        o_ref[...] = w_ref[...].astype(BF16)

    out_map = (lambda i, idx: (i, idx[1])) if by_cols else (lambda i, idx: (idx[1] * nrb + i, 0))
    grid_spec = pltpu.PrefetchScalarGridSpec(
        num_scalar_prefetch=1, grid=(nrb,), in_specs=[pl.BlockSpec((rows, C), lambda i, idx: (i, 0))],
        out_specs=pl.BlockSpec((rows, C), out_map))
    return pl.pallas_call(body, name=name, out_shape=jax.ShapeDtypeStruct(full_shape, BF16), grid_spec=grid_spec,
                          compiler_params=_params("parallel"))(idx, w)


def _rms_fwd(x, g, rows=256):
    S, D = x.shape
    rows = min(rows, S)

    def body(x_ref, g_ref, h_ref):
        xv = x_ref[...]
        ms = jnp.mean(xv * xv, axis=-1, keepdims=True)
        h_ref[...] = (xv * lax.rsqrt(ms + NORM_EPS) * g_ref[...]).astype(BF16)

    spec = pl.BlockSpec((rows, D), lambda i: (i, 0))
    return pl.pallas_call(body, name="rms_fwd", out_shape=jax.ShapeDtypeStruct((S, D), BF16), grid=(S // rows,),
                          in_specs=[spec, pl.BlockSpec((1, D), lambda i: (0, 0))], out_specs=spec,
                          compiler_params=_params("parallel"))(x, g)


def _rope_tables(positions):
    inv_freq = ROPE_THETA ** (-jnp.arange(ROPE_HALF, dtype=F32) * 2.0 / ROPE_DIM)
    ang = positions.astype(F32)[:, None] * inv_freq
    cos, sin = jnp.cos(ang), jnp.sin(ang)
    S = positions.shape[0]
    rest = HEAD_DIM - ROPE_DIM
    zeros_h, zeros_r = jnp.zeros((S, ROPE_HALF), F32), jnp.zeros((S, rest), F32)
    c = jnp.concatenate([cos, cos, jnp.ones((S, rest), F32)], axis=1)
    s1 = jnp.concatenate([-sin, zeros_h, zeros_r], axis=1)
    s2 = jnp.concatenate([zeros_h, sin, zeros_r], axis=1)
    reps = LANES // HEAD_DIM
    return jnp.tile(c, (1, reps)), jnp.tile(s1, (1, reps)), jnp.tile(s2, (1, reps))


def _rope(t, c, s1, s2, sign):
    n = t.shape[1]
    reps = n // c.shape[1]
    if reps > 1:
        c, s1, s2 = jnp.tile(c, (1, reps)), jnp.tile(s1, (1, reps)), jnp.tile(s2, (1, reps))
    up = pltpu.roll(t, n - ROPE_HALF, 1)
    down = pltpu.roll(t, ROPE_HALF, 1)
    return t * c + sign * (up * s1 + down * s2)


def _attn_specs(nb):
    last = nb - 1

    def cur(i):
        return jnp.minimum(i, last)

    def prev(i):
        return jnp.maximum(jnp.minimum(i, last) - 1, 0)

    kq, kk, kv, kg = OFF_Q // COL_BLK, OFF_K // LANES, OFF_V // LANES, OFF_GA // COL_BLK
    wide = lambda off: pl.BlockSpec((BLOCK, COL_BLK), lambda p, i: (cur(i), off + p))
    kv_cur = lambda off: pl.BlockSpec((BLOCK, LANES), lambda p, i: (cur(i), off + p))
    kv_prev = lambda off: pl.BlockSpec((BLOCK, LANES), lambda p, i: (prev(i), off + p))
    tab_cur = pl.BlockSpec((BLOCK, LANES), lambda p, i: (cur(i), 0))
    tab_prev = pl.BlockSpec((BLOCK, LANES), lambda p, i: (prev(i), 0))
    proj_specs = [wide(kq), kv_cur(kk), kv_prev(kk), kv_cur(kv), kv_prev(kv), wide(kg)]
    table_specs = [tab_cur] * 3 + [tab_prev] * 3
    bias_spec = pl.BlockSpec((1, BLOCK, QCOLS), lambda p, i: (jnp.minimum(i, 1), 0, 0))
    return proj_specs, table_specs + [bias_spec], cur, prev


PAIRS = Q_PER_KV // 2
QCOLS = PAIRS * BLOCK
NT = (((1,), (1,)), ((), ()))
TN = (((0,), (0,)), ((), ()))


def _band_bias():
    first = jnp.where(jnp.arange(BLOCK)[:, None] > (jnp.arange(QCOLS) % BLOCK)[None, :], -1e30, 0.0)
    return jnp.stack([first, jnp.zeros_like(first)]).astype(F32)


def _from_prev():
    a = lax.broadcasted_iota(jnp.int32, (BLOCK, QCOLS), 1) & (BLOCK - 1)
    return lax.broadcasted_iota(jnp.int32, (BLOCK, QCOLS), 0) > a


def _pair_rows(ref, kvh):
    base = kvh * Q_PER_KV * HEAD_DIM
    return jnp.concatenate([ref[:, base + b * LANES: base + (b + 1) * LANES] for b in range(PAIRS)], axis=0).astype(BF16)


def _unpair_rows(ref, kvh, val_t):
    base = kvh * Q_PER_KV * HEAD_DIM
    for b in range(PAIRS):
        ref[:, base + b * LANES: base + (b + 1) * LANES] = val_t[:, b * BLOCK:(b + 1) * BLOCK].T.astype(ref.dtype)


def _split_heads(t, kvh):
    lane = lax.broadcasted_iota(jnp.int32, t.shape, 1)
    other = pltpu.roll(t, HEAD_DIM, 1)
    lo = jnp.where(lane < HEAD_DIM, t if kvh == 0 else other, 0.0)
    hi = jnp.where(lane >= HEAD_DIM, other if kvh == 0 else t, 0.0)
    return lo.astype(BF16), hi.astype(BF16)


def _join_heads(even, odd, kvh):
    lane = lax.broadcasted_iota(jnp.int32, even.shape, 1)
    lo = jnp.where(lane < HEAD_DIM, even, 0.0) + pltpu.roll(jnp.where(lane >= HEAD_DIM, odd, 0.0), HEAD_DIM, 1)
    return lo if kvh == 0 else pltpu.roll(lo, HEAD_DIM, 1)


def _sink_row(sink_ref, pair, kvh, odd):
    first = (pair * 2 + kvh) * Q_PER_KV + odd
    return jnp.concatenate([jnp.full((1, BLOCK), sink_ref[first + 2 * b], F32) for b in range(PAIRS)], axis=1)


def _softmax_t(kp, kc, qp, sink_row, bias, from_prev):
    st = jnp.where(from_prev, lax.dot_general(kp, qp, NT, preferred_element_type=F32),
                   lax.dot_general(kc, qp, NT, preferred_element_type=F32)) + bias
    m = jnp.maximum(jnp.max(st, axis=0, keepdims=True), sink_row)
    e = jnp.exp(st - m)
    es = jnp.exp(sink_row - m)
    inv = 1.0 / (jnp.sum(e, axis=0, keepdims=True) + es)
    return e * inv, es * inv


def _roped_keys(k_ref, kp_ref, tabs):
    c, s1, s2, cp, s1p, s2p = (t[...] for t in tabs)
    k_cur = _rope(k_ref[...].astype(F32), c, s1, s2, 1.0)
    k_prev = _rope(kp_ref[...].astype(F32), cp, s1p, s2p, 1.0)
    return k_prev, k_cur


SCALE = HEAD_DIM ** -0.5


def _attn_fwd(proj, tables, bias, sink, comm=None):
    S = proj.shape[0]
    nb = S // BLOCK
    npairs = N_KV_HEADS // 2
    proj_specs, table_specs, _, _ = _attn_specs(nb)
    c_arrays, c_shapes, c_sems, c_build, c_alias = _comm_fields(comm)
    n_in = len(proj_specs) + len(table_specs) + 1

    def body(*refs):
        (q_ref, k_ref, kp_ref, v_ref, vp_ref, ga_ref, c_ref, s1_ref, s2_ref, cp_ref, s1p_ref, s2p_ref, bias_ref,
         sink_ref) = refs[:n_in]
        cin = refs[n_in:n_in + len(c_arrays)]
        attn_ref, ain_ref = refs[n_in + len(c_arrays):n_in + len(c_arrays) + 2]
        cout = refs[n_in + len(c_arrays) + 2:n_in + len(c_arrays) + 2 + len(c_shapes)]
        qr_scr = refs[n_in + len(c_arrays) + 2 + len(c_shapes)]
        sems = refs[n_in + len(c_arrays) + 3 + len(c_shapes):]
        pair, i = pl.program_id(0), pl.program_id(1)
        if comm is not None:
            _start_when((pair == 0) & (i == 0), c_build(cin, cout, sems))
        tabs = (c_ref, s1_ref, s2_ref, cp_ref, s1p_ref, s2p_ref)
        qr_scr[...] = _rope(q_ref[...].astype(F32), c_ref[...], s1_ref[...], s2_ref[...], 1.0) * SCALE
        k_prev, k_cur = _roped_keys(k_ref, kp_ref, tabs)
        v_prev, v_cur = vp_ref[...].astype(F32), v_ref[...].astype(F32)
        from_prev = _from_prev()
        for kvh in range(2):
            qp = _pair_rows(qr_scr, kvh)
            kps, kcs = _split_heads(k_prev, kvh), _split_heads(k_cur, kvh)
            vps, vcs = _split_heads(v_prev, kvh), _split_heads(v_cur, kvh)
            o_t = None
            for odd in range(2):
                p_t, _ = _softmax_t(kps[odd], kcs[odd], qp, _sink_row(sink_ref, pair, kvh, odd), bias_ref[0], from_prev)
                pb = p_t.astype(BF16)
                part = (lax.dot_general(vps[odd], jnp.where(from_prev, pb, 0), TN, preferred_element_type=F32)
                        + lax.dot_general(vcs[odd], jnp.where(from_prev, 0, pb), TN, preferred_element_type=F32))
                o_t = part if o_t is None else o_t + part
            _unpair_rows(attn_ref, kvh, o_t)
        ga = ga_ref[...].astype(F32)
        ain_ref[...] = (attn_ref[...] * (ga * _sigmoid(ga))).astype(BF16)
        if comm is not None:
            _wait_when((pair == npairs - 1) & (i == nb - 1), c_build(cin, cout, sems))

    out_spec = pl.BlockSpec((BLOCK, COL_BLK), lambda p, i: (i, p))
    aliases = {n_in + t: 2 + t for t in range(len(c_arrays))} if c_alias else {}
    out = pl.pallas_call(
        body, name="attn_fwd",
        out_shape=(jax.ShapeDtypeStruct((S, ATTN_WIDTH), F32), jax.ShapeDtypeStruct((S, ATTN_WIDTH), BF16)) + tuple(c_shapes),
        grid=(npairs, nb), in_specs=proj_specs + table_specs + [SMEM_SPEC] + [HBM_SPEC] * len(c_arrays),
        out_specs=(out_spec, out_spec) + (HBM_SPEC,) * len(c_shapes),
        scratch_shapes=[pltpu.VMEM((BLOCK, COL_BLK), F32)] + _comm_scratch(c_sems), input_output_aliases=aliases,
        compiler_params=_params("arbitrary", "arbitrary"),
    )(*([proj] * 6), *tables, *tables, bias, sink, *c_arrays)
    return out


def _attn_bwd(proj, tables, bias, sink, attn, da_in):
    S = proj.shape[0]
    nb = S // BLOCK
    proj_specs, table_specs, cur, prev = _attn_specs(nb)
    wide_cur = pl.BlockSpec((BLOCK, COL_BLK), lambda p, i: (cur(i), p))
    kv_out = pl.BlockSpec((BLOCK, LANES), lambda p, i: (jnp.maximum(i - 1, 0), p))

    def body(q_ref, k_ref, kp_ref, v_ref, vp_ref, ga_ref, c_ref, s1_ref, s2_ref, cp_ref, s1p_ref, s2p_ref, bias_ref,
             sink_ref, attn_ref, da_ref, dq_ref, dk_ref, dv_ref, dga_ref, dsink_ref,
             qr_scr, do_scr, dq_scr, dk_scr, dv_scr, carry_k, carry_v, sink_acc):
        pair, i = pl.program_id(0), pl.program_id(1)
        tabs = (c_ref, s1_ref, s2_ref, cp_ref, s1p_ref, s2p_ref)

        @pl.when(i < nb)
        def _():
            ga = ga_ref[...].astype(F32)
            sg = _sigmoid(ga)
            da = da_ref[...]
            dga_ref[...] = (da * attn_ref[...] * (sg * (1.0 + ga * (1.0 - sg)))).astype(BF16)
            do_scr[...] = da * (ga * sg)
            qr_scr[...] = _rope(q_ref[...].astype(F32), c_ref[...], s1_ref[...], s2_ref[...], 1.0) * SCALE
            k_prev, k_cur = _roped_keys(k_ref, kp_ref, tabs)
            v_prev, v_cur = vp_ref[...].astype(F32), v_ref[...].astype(F32)
            from_prev = _from_prev()
            dkv = [None] * 4
            for kvh in range(2):
                qp, dop = _pair_rows(qr_scr, kvh), _pair_rows(do_scr, kvh)
                kps, kcs = _split_heads(k_prev, kvh), _split_heads(k_cur, kvh)
                vps, vcs = _split_heads(v_prev, kvh), _split_heads(v_cur, kvh)
                dq_t, halves = None, []
                for odd in range(2):
                    p_t, p_sink = _softmax_t(kps[odd], kcs[odd], qp, _sink_row(sink_ref, pair, kvh, odd), bias_ref[0],
                                             from_prev)
                    dp_t = jnp.where(from_prev, lax.dot_general(vps[odd], dop, NT, preferred_element_type=F32),
                                     lax.dot_general(vcs[odd], dop, NT, preferred_element_type=F32))
                    delta = jnp.sum(p_t * dp_t, axis=0, keepdims=True)
                    ds = (p_t * (dp_t - delta)).astype(BF16)
                    pb = p_t.astype(BF16)
                    ds_prev, ds_cur = jnp.where(from_prev, ds, 0), jnp.where(from_prev, 0, ds)
                    p_prev, p_cur = jnp.where(from_prev, pb, 0), jnp.where(from_prev, 0, pb)
                    part = (lax.dot_general(kps[odd], ds_prev, TN, preferred_element_type=F32)
                            + lax.dot_general(kcs[odd], ds_cur, TN, preferred_element_type=F32))
                    dq_t = part if dq_t is None else dq_t + part
                    halves.append([jnp.dot(ds_prev, qp, preferred_element_type=F32),
                                   jnp.dot(ds_cur, qp, preferred_element_type=F32),
                                   jnp.dot(p_prev, dop, preferred_element_type=F32),
                                   jnp.dot(p_cur, dop, preferred_element_type=F32)])
                    contrib = -p_sink * delta

                    @pl.when(i == 0)
                    def _():
                        sink_acc[2 * kvh + odd] = contrib

                    @pl.when(i > 0)
                    def _():
                        sink_acc[2 * kvh + odd] += contrib

                _unpair_rows(dq_scr, kvh, dq_t)
                for n in range(4):
                    joined = _join_heads(halves[0][n], halves[1][n], kvh)
                    dkv[n] = joined if dkv[n] is None else dkv[n] + joined
            dk_scr[0:BLOCK, :], dk_scr[BLOCK:2 * BLOCK, :] = dkv[0], dkv[1]
            dv_scr[0:BLOCK, :], dv_scr[BLOCK:2 * BLOCK, :] = dkv[2], dkv[3]

            dq_ref[...] = _rope(dq_scr[...] * SCALE, c_ref[...], s1_ref[...], s2_ref[...], -1.0).astype(BF16)
            dk_prev = _rope(dk_scr[0:BLOCK, :], cp_ref[...], s1p_ref[...], s2p_ref[...], -1.0)
            dk_cur = _rope(dk_scr[BLOCK:2 * BLOCK, :], c_ref[...], s1_ref[...], s2_ref[...], -1.0)
            dv_prev, dv_cur = dv_scr[0:BLOCK, :], dv_scr[BLOCK:2 * BLOCK, :]

            @pl.when(i > 0)
            def _():
                dk_ref[...] = (carry_k[...] + dk_prev).astype(BF16)
                dv_ref[...] = (carry_v[...] + dv_prev).astype(BF16)

            carry_k[...] = dk_cur
            carry_v[...] = dv_cur

        @pl.when(i == nb)
        def _():
            dk_ref[...] = carry_k[...].astype(BF16)
            dv_ref[...] = carry_v[...].astype(BF16)
            lane = lax.broadcasted_iota(jnp.int32, (8, LANES), 1)
            acc = jnp.zeros((8, LANES), F32)
            for kvh in range(2):
                for odd in range(2):
                    for b in range(PAIRS):
                        val = jnp.sum(sink_acc[2 * kvh + odd, :, b * BLOCK:(b + 1) * BLOCK], axis=1, keepdims=True)
                        acc = jnp.where(lane == kvh * Q_PER_KV + 2 * b + odd, val, acc)
            dsink_ref[0] = acc

    return pl.pallas_call(
        body, name="attn_bwd",
        out_shape=(jax.ShapeDtypeStruct((S, ATTN_WIDTH), BF16), jax.ShapeDtypeStruct((S, KV_WIDTH), BF16),
                   jax.ShapeDtypeStruct((S, KV_WIDTH), BF16), jax.ShapeDtypeStruct((S, ATTN_WIDTH), BF16),
                   jax.ShapeDtypeStruct((N_KV_HEADS // 2, 8, LANES), F32)),
        grid=(N_KV_HEADS // 2, nb + 1),
        in_specs=proj_specs + table_specs + [SMEM_SPEC, wide_cur, wide_cur],
        out_specs=(wide_cur, kv_out, kv_out, wide_cur, pl.BlockSpec((1, 8, LANES), lambda p, i: (p, 0, 0))),
        scratch_shapes=[pltpu.VMEM((BLOCK, COL_BLK), F32), pltpu.VMEM((BLOCK, COL_BLK), F32),
                        pltpu.VMEM((BLOCK, COL_BLK), F32), pltpu.VMEM((2 * BLOCK, LANES), F32),
                        pltpu.VMEM((2 * BLOCK, LANES), F32), pltpu.VMEM((BLOCK, LANES), F32),
                        pltpu.VMEM((BLOCK, LANES), F32), pltpu.VMEM((4, 1, QCOLS), F32)],
        compiler_params=_params("arbitrary", "arbitrary"),
    )(*([proj] * 6), *tables, *tables, bias, sink, attn, da_in)


N_PARTS = GMLP_WIDTH // COL_BLK
GROUPS_PER_PART = COL_BLK // GMLP_GROUP_DIM


def _part_specs(off):
    return [pl.BlockSpec((GMLP_CHUNK, COL_BLK), functools.partial(lambda j, i: (i, j), off // COL_BLK + k))
            for k in range(N_PARTS)]


def _group(refs, g):
    lo = (g % GROUPS_PER_PART) * GMLP_GROUP_DIM
    return refs[g // GROUPS_PER_PART][:, lo:lo + GMLP_GROUP_DIM].astype(F32)


def _gmlp_norm_stats(vg_refs, gv_scr):
    total = jnp.zeros((GMLP_CHUNK, 1), F32)
    for k in range(N_PARTS):
        gv = _gelu(vg_refs[k][...].astype(F32))
        gv_scr[:, k * COL_BLK:(k + 1) * COL_BLK] = gv
        total = total + jnp.sum(gv, axis=1, keepdims=True)
    mu = total / GMLP_WIDTH
    xc = gv_scr[...] - mu
    var = jnp.sum(xc * xc, axis=1, keepdims=True) / GMLP_WIDTH
    return mu, lax.rsqrt(var + LN_EPS)


def _tril_bf16(ws_ref, g):
    t = lax.broadcasted_iota(jnp.int32, (GMLP_CHUNK, GMLP_CHUNK), 0)
    s = lax.broadcasted_iota(jnp.int32, (GMLP_CHUNK, GMLP_CHUNK), 1)
    return jnp.where(s <= t, ws_ref[g], 0.0).astype(BF16), s <= t


def _gmlp_fwd(proj, ws, bs_t, lg, lb, comm=None):
    S = proj.shape[0]
    nb = S // GMLP_CHUNK
    c_arrays, c_shapes, c_sems, c_build, c_alias = _comm_fields(comm)
    n_cin, n_cout = len(c_arrays), len(c_shapes)

    def body(*refs):
        u_refs, vg_refs, gb_refs = refs[0:4], refs[4:8], refs[8:12]
        ws_ref, bst_ref, lg_ref, lb_ref = refs[12:16]
        cin, out_ref, cout = refs[16:16 + n_cin], refs[16 + n_cin], refs[17 + n_cin:17 + n_cin + n_cout]
        gv_scr, sems = refs[17 + n_cin + n_cout], refs[18 + n_cin + n_cout:]
        if comm is not None:
            _start_when(pl.program_id(0) == 0, c_build(cin, cout, sems))
        mu, rstd = _gmlp_norm_stats(vg_refs, gv_scr)
        for g in range(GMLP_GROUPS):
            cols = slice(g * GMLP_GROUP_DIM, (g + 1) * GMLP_GROUP_DIM)
            vn = (gv_scr[:, cols] - mu) * rstd * lg_ref[:, cols] + lb_ref[:, cols]
            w, _ = _tril_bf16(ws_ref, g)
            mixed = jnp.dot(w, vn.astype(BF16), preferred_element_type=F32) + bst_ref[:, g:g + 1]
            gb = _group(gb_refs, g)
            out_ref[:, cols] = (_gelu(_group(u_refs, g)) * mixed * (gb * _sigmoid(gb))).astype(BF16)
        if comm is not None:
            _wait_when(pl.program_id(0) == nb - 1, c_build(cin, cout, sems))

    full = lambda shape: pl.BlockSpec(shape, lambda i: tuple(0 for _ in shape))
    aliases = {16 + t: 1 + t for t in range(n_cin)} if c_alias else {}
    out = pl.pallas_call(
        body, name="gmlp_fwd", out_shape=(jax.ShapeDtypeStruct((S, GMLP_WIDTH), BF16),) + tuple(c_shapes), grid=(nb,),
        in_specs=_part_specs(OFF_U) + _part_specs(OFF_VG) + _part_specs(OFF_GB)
        + [full(ws.shape), full(bs_t.shape), full(lg.shape), full(lb.shape)] + [HBM_SPEC] * n_cin,
        out_specs=(pl.BlockSpec((GMLP_CHUNK, GMLP_WIDTH), lambda i: (i, 0)),) + (HBM_SPEC,) * n_cout,
        scratch_shapes=[pltpu.VMEM((GMLP_CHUNK, GMLP_WIDTH), F32)] + _comm_scratch(c_sems), input_output_aliases=aliases,
        compiler_params=_params("arbitrary" if comm is not None else "parallel"),
    )(*([proj] * 12), ws, bs_t, lg, lb, *c_arrays)
    return out if comm is not None else out[0]


def _gmlp_bwd(proj, ws, bs_t, lg, lb, db_in):
    S = proj.shape[0]
    nb = S // GMLP_CHUNK
    W = GMLP_WIDTH

    def body(*refs):
        u_refs, vg_refs, gb_refs = refs[0:4], refs[4:8], refs[8:12]
        ws_ref, bst_ref, lg_ref, lb_ref, db_ref = refs[12:17]
        out_ref, dws_ref, dbs_ref, dlg_ref, dlb_ref = refs[17:22]
        gv_scr, dvh_scr = refs[22:]
        i = pl.program_id(0)

        @pl.when(i == 0)
        def _():
            dws_ref[...] = jnp.zeros_like(dws_ref)
            dbs_ref[...] = jnp.zeros_like(dbs_ref)
            dlg_ref[...] = jnp.zeros_like(dlg_ref)
            dlb_ref[...] = jnp.zeros_like(dlb_ref)

        mu, rstd = _gmlp_norm_stats(vg_refs, gv_scr)
        sum_dvh = jnp.zeros((GMLP_CHUNK, 1), F32)
        sum_dvh_vh = jnp.zeros((GMLP_CHUNK, 1), F32)
        for g in range(GMLP_GROUPS):
            cols = slice(g * GMLP_GROUP_DIM, (g + 1) * GMLP_GROUP_DIM)
            vhat = (gv_scr[:, cols] - mu) * rstd
            vn = (vhat * lg_ref[:, cols] + lb_ref[:, cols]).astype(BF16)
            w, tril = _tril_bf16(ws_ref, g)
            mixed = jnp.dot(w, vn, preferred_element_type=F32) + bst_ref[:, g:g + 1]
            u, gb, db = _group(u_refs, g), _group(gb_refs, g), db_ref[:, cols]
            gu, sgb = _gelu(u), _sigmoid(gb)
            dsg = db * (gb * sgb)
            out_ref[:, 2 * W + g * GMLP_GROUP_DIM: 2 * W + (g + 1) * GMLP_GROUP_DIM] = (
                db * (gu * mixed) * (sgb * (1.0 + gb * (1.0 - sgb)))).astype(BF16)
            out_ref[:, cols] = (dsg * mixed * _gelu_grad(u)).astype(BF16)
            dmixed = dsg * gu
            dmixed_b = dmixed.astype(BF16)
            dvn = lax.dot_general(w, dmixed_b, (((0,), (0,)), ((), ())), preferred_element_type=F32)
            dw = lax.dot_general(dmixed_b, vn, (((1,), (1,)), ((), ())), preferred_element_type=F32)
            dws_ref[g] += jnp.where(tril, dw, 0.0)
            dbs_ref[g] += jnp.sum(dmixed, axis=1, keepdims=True)
            dlg_ref[:, cols] += jnp.sum(dvn * vhat, axis=0, keepdims=True)
            dlb_ref[:, cols] += jnp.sum(dvn, axis=0, keepdims=True)
            dvh = dvn * lg_ref[:, cols]
            dvh_scr[:, cols] = dvh
            sum_dvh = sum_dvh + jnp.sum(dvh, axis=1, keepdims=True)
            sum_dvh_vh = sum_dvh_vh + jnp.sum(dvh * vhat, axis=1, keepdims=True)
        m1, m2 = sum_dvh / W, sum_dvh_vh / W
        for k in range(N_PARTS):
            cols = slice(k * COL_BLK, (k + 1) * COL_BLK)
            vhat = (gv_scr[:, cols] - mu) * rstd
            dgv = rstd * (dvh_scr[:, cols] - m1 - vhat * m2)
            out_ref[:, W + k * COL_BLK: W + (k + 1) * COL_BLK] = (
                dgv * _gelu_grad(vg_refs[k][...].astype(F32))).astype(BF16)

    full = lambda shape: pl.BlockSpec(shape, lambda i: tuple(0 for _ in shape))
    row = pl.BlockSpec((GMLP_CHUNK, W), lambda i: (i, 0))
    return pl.pallas_call(
        body, name="gmlp_bwd",
        out_shape=(jax.ShapeDtypeStruct((S, 3 * W), BF16), jax.ShapeDtypeStruct(ws.shape, F32),
                   jax.ShapeDtypeStruct((GMLP_GROUPS, GMLP_CHUNK, 1), F32), jax.ShapeDtypeStruct((1, W), F32),
                   jax.ShapeDtypeStruct((1, W), F32)),
        grid=(nb,),
        in_specs=_part_specs(OFF_U) + _part_specs(OFF_VG) + _part_specs(OFF_GB)
        + [full(ws.shape), full(bs_t.shape), full(lg.shape), full(lb.shape), row],
        out_specs=(pl.BlockSpec((GMLP_CHUNK, 3 * W), lambda i: (i, 0)), full(ws.shape),
                   full((GMLP_GROUPS, GMLP_CHUNK, 1)), full((1, W)), full((1, W))),
        scratch_shapes=[pltpu.VMEM((GMLP_CHUNK, W), F32), pltpu.VMEM((GMLP_CHUNK, W), F32)],
        compiler_params=_params("arbitrary"),
    )(*([proj] * 12), ws, bs_t, lg, lb, db_in)


def _merge_fwd(proj, y_a, y_b, rows=256):
    S = proj.shape[0]
    rows = min(rows, S)

    def body(ma_ref, mb_ref, ya_ref, yb_ref, out_ref):
        out_ref[...] = (_sigmoid(ma_ref[...].astype(F32)) * ya_ref[...]
                        + _sigmoid(mb_ref[...].astype(F32)) * yb_ref[...]).astype(BF16)

    blk = lambda off: pl.BlockSpec((rows, COL_BLK), lambda i, j: (i, off // COL_BLK + j))
    return pl.pallas_call(
        body, name="merge_fwd", out_shape=jax.ShapeDtypeStruct((S, D_MODEL), BF16), grid=(S // rows, D_MODEL // COL_BLK),
        in_specs=[blk(OFF_MA), blk(OFF_MB), blk(0), blk(0)], out_specs=blk(0),
        compiler_params=_params("parallel", "parallel"),
    )(proj, proj, y_a, y_b)


def _merge_bwd(proj, y_a, y_b, dmerged, rows=128):
    S = proj.shape[0]
    rows = min(rows, S)
    nj = D_MODEL // COL_BLK

    def body(*refs):
        ma_refs, mb_refs = refs[0:nj], refs[nj:2 * nj]
        ya_ref, yb_ref, dm_ref, dya_ref, dyb_ref, dg_ref = refs[2 * nj:]
        for k in range(nj):
            cols = slice(k * COL_BLK, (k + 1) * COL_BLK)
            dm = dm_ref[:, cols]
            sa, sb = _sigmoid(ma_refs[k][...].astype(F32)), _sigmoid(mb_refs[k][...].astype(F32))
            dya_ref[:, cols] = (dm * sa).astype(BF16)
            dyb_ref[:, cols] = (dm * sb).astype(BF16)
            dg_ref[:, cols] = (dm * ya_ref[:, cols] * sa * (1.0 - sa)).astype(BF16)
            dg_ref[:, D_MODEL + k * COL_BLK: D_MODEL + (k + 1) * COL_BLK] = (
                dm * yb_ref[:, cols] * sb * (1.0 - sb)).astype(BF16)

    part = lambda off: [pl.BlockSpec((rows, COL_BLK), functools.partial(lambda j, i: (i, j), off // COL_BLK + k))
                        for k in range(nj)]
    row = pl.BlockSpec((rows, D_MODEL), lambda i: (i, 0))
    return pl.pallas_call(
        body, name="merge_bwd",
        out_shape=(jax.ShapeDtypeStruct((S, D_MODEL), BF16), jax.ShapeDtypeStruct((S, D_MODEL), BF16),
                   jax.ShapeDtypeStruct((S, 2 * D_MODEL), BF16)),
        grid=(S // rows,), in_specs=part(OFF_MA) + part(OFF_MB) + [row, row, row],
        out_specs=(row, row, pl.BlockSpec((rows, 2 * D_MODEL), lambda i: (i, 0))),
        compiler_params=_params("parallel"),
    )(*([proj] * (2 * nj)), y_a, y_b, dmerged)


def _loss_head(x2, target, fg, rows=128):
    S, D = x2.shape
    rows = min(rows, S)

    def body(x_ref, t_ref, g_ref, dx_ref, dxb_ref, dg_ref, loss_ref):
        i = pl.program_id(0)

        @pl.when(i == 0)
        def _():
            dg_ref[...] = jnp.zeros_like(dg_ref)
            loss_ref[...] = jnp.zeros_like(loss_ref)

        xv, g = x_ref[...], g_ref[...]
        rstd = lax.rsqrt(jnp.mean(xv * xv, axis=-1, keepdims=True) + NORM_EPS)
        xhat = xv * rstd
        err = xhat * g - t_ref[...]
        loss_ref[...] += (0.5 / D) * jnp.sum(err * err)
        dy = err * (1.0 / D)
        dg_ref[...] += jnp.sum(dy * xhat, axis=0, keepdims=True)
        dxh = dy * g
        dx = rstd * (dxh - xhat * jnp.mean(dxh * xhat, axis=-1, keepdims=True))
        dx_ref[...] = dx
        dxb_ref[...] = dx.astype(BF16)

    row = pl.BlockSpec((rows, D), lambda i: (i, 0))
    vec = pl.BlockSpec((1, D), lambda i: (0, 0))
    return pl.pallas_call(
        body, name="loss_head",
        out_shape=(jax.ShapeDtypeStruct((S, D), F32), jax.ShapeDtypeStruct((S, D), BF16),
                   jax.ShapeDtypeStruct((1, D), F32), jax.ShapeDtypeStruct((8, LANES), F32)),
        grid=(S // rows,), in_specs=[row, row, vec],
        out_specs=(row, row, vec, pl.BlockSpec((8, LANES), lambda i: (0, 0))),
        compiler_params=_params("arbitrary"),
    )(x2, target, fg)


def _rms_bwd(x, dh, dx2, g, rows=128):
    S, D = x.shape
    rows = min(rows, S)

    def body(x_ref, dh_ref, dx2_ref, g_ref, gx_ref, dg_ref):
        i = pl.program_id(0)

        @pl.when(i == 0)
        def _():
            dg_ref[...] = jnp.zeros_like(dg_ref)

        xv, dh_v = x_ref[...], dh_ref[...]
        rstd = lax.rsqrt(jnp.mean(xv * xv, axis=-1, keepdims=True) + NORM_EPS)
        xhat = xv * rstd
        dg_ref[...] += jnp.sum(dh_v * xhat, axis=0, keepdims=True)
        dxh = dh_v * g_ref[...]
        gx_ref[...] = dx2_ref[...] + rstd * (dxh - xhat * jnp.mean(dxh * xhat, axis=-1, keepdims=True))

    row = pl.BlockSpec((rows, D), lambda i: (i, 0))
    vec = pl.BlockSpec((1, D), lambda i: (0, 0))
    return pl.pallas_call(
        body, name="rms_bwd", out_shape=(jax.ShapeDtypeStruct((S, D), F32), jax.ShapeDtypeStruct((1, D), F32)),
        grid=(S // rows,), in_specs=[row, row, row, vec], out_specs=(row, vec),
        compiler_params=_params("arbitrary"),
    )(x, dh, dx2, g)


def _adamw(w, g, m, v, name, rows=64):
    R, C = w.shape
    rows = min(rows, R)
    c1 = 1.0 - ADAM_B1 ** ADAM_STEP
    c2 = 1.0 - ADAM_B2 ** ADAM_STEP

    def body(w_ref, g_ref, m_ref, v_ref, go_ref, d_ref, mo_ref, vo_ref):
        gv = g_ref[...]
        mn = ADAM_B1 * m_ref[...] + (1.0 - ADAM_B1) * gv
        vn = ADAM_B2 * v_ref[...] + (1.0 - ADAM_B2) * (gv * gv)
        go_ref[...] = gv
        mo_ref[...] = mn
        vo_ref[...] = vn
        d_ref[...] = -ADAM_LR * ((mn / c1) / (jnp.sqrt(vn / c2) + ADAM_EPS) + ADAM_WD * w_ref[...])

    spec = pl.BlockSpec((rows, C), lambda i: (i, 0))
    shape = jax.ShapeDtypeStruct((R, C), F32)
    return pl.pallas_call(body, name=name, out_shape=(shape,) * 4, grid=(R // rows,), in_specs=[spec] * 4,
                          out_specs=(spec,) * 4, compiler_params=_params("parallel"))(w, g, m, v)


def _place():
    x, y, c = lax.axis_index("x"), lax.axis_index("y"), lax.axis_index("c")
    others = [(1 - x, y), (x, 1 - y), (1 - x, 1 - y)]
    return x, y, c, others


def _chip_index(chip):
    return 2 * chip[0] + chip[1]


def _remote(src, dst, sems, k, to):
    send_sems, recv_sems = sems
    return pltpu.make_async_remote_copy(src_ref=src, dst_ref=dst, send_sem=send_sems.at[k], recv_sem=recv_sems.at[k],
                                        device_id=to, device_id_type=MESH)


def _comm_fields(comm):
    return comm if comm is not None else ((), (), 0, None, False)


def _comm_scratch(n_sems):
    return [pltpu.SemaphoreType.DMA((n_sems,)), pltpu.SemaphoreType.DMA((n_sems,))] if n_sems else []


def _start_when(cond, copies):
    @pl.when(cond)
    def _():
        for cp in copies:
            cp.start()


def _wait_when(cond, copies):
    @pl.when(cond)
    def _():
        for cp in copies:
            cp.wait()


def _proj_gather(tile_cols, h, w_full, tm=1024, tk=512):
    S, Dm = h.shape
    tm, tk = min(tm, S), min(tk, Dm)
    tn = SLAB // 2
    ni, nj, nk = S // tm, 2 * N_CHIPS, Dm // tk
    total = nj * ni * nk
    half_in = Dm // 2

    def body(cols_ref, h_ref, _wi, proj_ref, fi_ref, acc, bbuf, bsem, send_sems, recv_sems):
        j, i, k = pl.program_id(0), pl.program_id(1), pl.program_id(2)
        t = (j * ni + i) * nk + k
        x, y, c, others = _place()
        me_chip = 2 * x + y
        sems = (send_sems, recv_sems)

        def piece(chip, half, tile):
            return fi_ref.at[pl.ds(half * half_in, half_in), pl.ds(chip * SLAB + tile * tn, tn)]

        def ici_send(r, tile):
            mine = piece(me_chip, c, tile)
            return _remote(mine, mine, sems, 2 * r + tile, (*others[r], c))

        def forward(tile):
            src_chip = (x + (1 - c) * (1 - 2 * x), y + c * (1 - 2 * y))
            dst_chip = (x + c * (1 - 2 * x), y + (1 - c) * (1 - 2 * y))
            got = piece(_chip_index(src_chip), c, tile)
            return _remote(got, got, sems, 4 + tile, (*dst_chip, c))

        def ici_recv(r, tile):
            got = piece(_chip_index(others[r]), c, tile)
            return _remote(got, got, sems, 2 * r + tile, (x, y, c))

        def pass_on(r, tile):
            got = piece(_chip_index(others[r]), c, tile)
            return _remote(got, got, sems, 6 + 2 * r + tile, (x, y, 1 - c))

        def passed_recv(r, tile):
            got = piece(_chip_index(others[r]), 1 - c, tile)
            return _remote(got, got, sems, 6 + 2 * r + tile, (x, y, c))

        def b_copy(jj, kk, slot):
            return pltpu.make_async_copy(fi_ref.at[pl.ds(kk * tk, tk), pl.ds(cols_ref[jj] * tn, tn)], bbuf.at[slot],
                                         bsem.at[slot])

        @pl.when(t == 0)
        def _():
            for tile in range(2):
                ici_send(0, tile).start()
                ici_send(1, tile).start()
            b_copy(0, 0, 0).start()

        nxt = t + 1
        kn, i_n, jn = nxt % nk, (nxt // nk) % ni, nxt // (nk * ni)

        def opens(jj):
            return (jn == jj) & (i_n == 0) & (kn == 0)

        @pl.when(nxt < total)
        def _():
            for tile in range(2):
                @pl.when(opens(2 + 2 * tile))
                def _():
                    ici_recv(0, tile).wait_recv()
                    ici_recv(1, tile).wait_recv()
                    forward(tile).start()
                    pass_on(0, tile).start()
                    pass_on(1, tile).start()
                    passed_recv(0, tile).wait_recv()

                @pl.when(opens(3 + 2 * tile))
                def _():
                    passed_recv(1, tile).wait_recv()

                @pl.when(opens(6 + tile))
                def _():
                    ici_recv(2, tile).wait_recv()
                    pass_on(2, tile).start()
                    passed_recv(2, tile).wait_recv()

            b_copy(jn, kn, nxt % 2).start()

        b_copy(j, k, t % 2).wait()

        def product():
            return jnp.dot(h_ref[...], bbuf[t % 2], preferred_element_type=F32)

        @pl.when(k == 0)
        def _():
            acc[...] = product()

        @pl.when((k > 0) & (k < nk - 1))
        def _():
            acc[...] += product()

        @pl.when(k == nk - 1)
        def _():
            proj_ref[...] = (acc[...] + product()).astype(BF16)

        @pl.when(t == total - 1)
        def _():
            for tile in range(2):
                ici_send(0, tile).wait_send()
                ici_send(1, tile).wait_send()
                forward(tile).wait_send()
                for r in range(3):
                    pass_on(r, tile).wait_send()

    grid_spec = pltpu.PrefetchScalarGridSpec(
        num_scalar_prefetch=1, grid=(nj, ni, nk),
        in_specs=[pl.BlockSpec((tm, tk), lambda j, i, k, cols: (i, k)), HBM_SPEC],
        out_specs=(pl.BlockSpec((tm, tn), lambda j, i, k, cols: (i, cols[j])), HBM_SPEC),
        scratch_shapes=[pltpu.VMEM((tm, tn), F32), pltpu.VMEM((2, tk, tn), BF16), pltpu.SemaphoreType.DMA((2,)),
                        pltpu.SemaphoreType.DMA((12,)), pltpu.SemaphoreType.DMA((12,))])
    return pl.pallas_call(
        body, name="proj_gather",
        out_shape=(jax.ShapeDtypeStruct((S, PROJ_WIDTH), BF16), jax.ShapeDtypeStruct(w_full.shape, BF16)),
        grid_spec=grid_spec, input_output_aliases={2: 1},
        compiler_params=_params("arbitrary", "arbitrary", "arbitrary"),
    )(tile_cols, h, w_full)


def _exchange(comm, name):
    arrays, shapes, n_sems, build, aliased = comm
    n_in, n_out = len(arrays), len(shapes)

    def body(*refs):
        copies = build(refs[:n_in], refs[n_in:n_in + n_out], refs[n_in + n_out:])
        for cp in copies:
            cp.start()
        for cp in copies:
            cp.wait()

    return pl.pallas_call(
        body, name=name, out_shape=tuple(shapes), in_specs=[HBM_SPEC] * n_in, out_specs=(HBM_SPEC,) * n_out,
        scratch_shapes=_comm_scratch(n_sems), input_output_aliases={t: t for t in range(n_in)} if aliased else {},
    )(*arrays)


def _pair_halves(parts):
    def half_of(ref, which):
        if len(ref.shape) == 2:
            rows = ref.shape[0] // 2
            return ref.at[pl.ds(which * rows, rows), :]
        return ref.at[:, pl.ds(which, 1)]

    def out_shape(p):
        if p.ndim == 2:
            return jax.ShapeDtypeStruct((p.shape[0] // 2, p.shape[1]), BF16)
        return jax.ShapeDtypeStruct((p.shape[0], 1) + p.shape[2:], BF16)

    def build(p_refs, r_refs, sems):
        x, y, c, _ = _place()
        return [_remote(half_of(p, 1 - c), r, sems, t, (x, y, 1 - c)) for t, (p, r) in enumerate(zip(p_refs, r_refs))]

    return (tuple(parts), tuple(out_shape(p) for p in parts), len(parts), build, False)


def _gather_squares(fulls, over_ici):
    half_up = UP_ROWS // 2

    def build(_, f_refs, sems):
        x, y, c, others = _place()
        copies = []
        for t, f in enumerate(f_refs):
            for r, chip in enumerate(others):
                src_chip = 2 * x + y if over_ici else _chip_index(chip)
                rows = f.at[pl.ds(src_chip * UP_ROWS + c * half_up, half_up), :]
                copies.append(_remote(rows, rows, sems, 3 * t + r, (*chip, c) if over_ici else (x, y, 1 - c)))
        return copies

    return (tuple(fulls), tuple(jax.ShapeDtypeStruct(f.shape, f.dtype) for f in fulls), 3 * len(fulls), build, True)


def _send_whole(parts):
    def build(p_refs, r_refs, sems):
        x, y, c, _ = _place()
        return [_remote(p, r, sems, t, (x, y, 1 - c)) for t, (p, r) in enumerate(zip(p_refs, r_refs))]

    return (tuple(parts), tuple(jax.ShapeDtypeStruct(p.shape, p.dtype) for p in parts), len(parts), build, False)


def _pair_add_in(p_own, r_in, rows=256, cols=SLAB):
    half, P = p_own.shape
    rows = min(rows, half)

    def body(p_ref, r_ref, o_ref):
        o_ref[...] = (p_ref[...].astype(F32) + r_ref[...].astype(F32)).astype(BF16)

    spec = pl.BlockSpec((rows, cols), lambda i, j: (i, j))
    return pl.pallas_call(body, name="pair_add_in", out_shape=jax.ShapeDtypeStruct((half, P), BF16),
                          grid=(half // rows, P // cols), in_specs=[spec, spec], out_specs=spec,
                          compiler_params=_params("parallel", "parallel"))(p_own, r_in)


def _pair_add_up(idx, p_up, r_up, name):
    _, _, R, C = p_up.shape

    def body(idx_ref, p_ref, r_ref, o_ref):
        o_ref[...] = (p_ref[...].astype(F32) + r_ref[...].astype(F32)).astype(BF16)

    grid_spec = pltpu.PrefetchScalarGridSpec(
        num_scalar_prefetch=1, grid=(N_CHIPS,),
        in_specs=[pl.BlockSpec((1, 1, R, C), lambda j, idx: (j, idx[0], 0, 0)),
                  pl.BlockSpec((1, 1, R, C), lambda j, idx: (j, 0, 0, 0))],
        out_specs=pl.BlockSpec((1, 1, R, C), lambda j, idx: (j, 0, 0, 0)))
    return pl.pallas_call(body, name=name, out_shape=jax.ShapeDtypeStruct((N_CHIPS, 1, R, C), BF16), grid_spec=grid_spec,
                          compiler_params=_params("parallel"))(idx, p_up, r_up)


def _slab_exchange(qs):
    n = len(qs)

    def out_shape(q):
        if q.ndim == 2:
            return jax.ShapeDtypeStruct((3, q.shape[0], SLAB), BF16)
        return jax.ShapeDtypeStruct((3,) + q.shape[1:], BF16)

    def build(q_refs, r_refs, sems):
        _, _, c, others = _place()
        copies = []
        for r, chip in enumerate(others):
            ci = _chip_index(chip)
            for t in range(n):
                src = q_refs[t].at[:, pl.ds(ci * SLAB, SLAB)] if len(q_refs[t].shape) == 2 else q_refs[t].at[ci]
                copies.append(_remote(src, r_refs[t].at[r], sems, 3 * t + r, (*chip, c)))
        return copies

    return (tuple(qs), tuple(out_shape(q) for q in qs), 3 * n, build, False)


def _slab_add_in(idx, q_in, r2_in, rows=128):
    half = q_in.shape[0]
    rows = min(rows, half)
    nrb = half // rows

    def body(idx_ref, q_ref, r_ref, o_ref):
        o_ref[...] = ((q_ref[...].astype(F32) + r_ref[0].astype(F32)) + r_ref[1].astype(F32)) + r_ref[2].astype(F32)

    grid_spec = pltpu.PrefetchScalarGridSpec(
        num_scalar_prefetch=1, grid=(nrb,),
        in_specs=[pl.BlockSpec((rows, SLAB), lambda i, idx: (i, idx[1])),
                  pl.BlockSpec((3, rows, SLAB), lambda i, idx: (0, i, 0))],
        out_specs=pl.BlockSpec((rows, SLAB), lambda i, idx: (idx[0] * nrb + i, 0)))
    return pl.pallas_call(body, name="slab_add_in", out_shape=jax.ShapeDtypeStruct((2 * half, SLAB), F32),
                          grid_spec=grid_spec, compiler_params=_params("parallel"))(idx, q_in, r2_in)


def _slab_add_up(idx, q_up, r2_up, name, rows=128):
    _, _, R, C = q_up.shape
    rows = min(rows, R)
    nrb = R // rows

    def body(idx_ref, q_ref, r_ref, o_ref):
        o_ref[...] = ((q_ref[0, 0].astype(F32) + r_ref[0, 0].astype(F32)) + r_ref[1, 0].astype(F32)) + r_ref[2, 0].astype(F32)

    grid_spec = pltpu.PrefetchScalarGridSpec(
        num_scalar_prefetch=1, grid=(nrb,),
        in_specs=[pl.BlockSpec((1, 1, rows, C), lambda i, idx: (idx[1], 0, i, 0)),
                  pl.BlockSpec((3, 1, rows, C), lambda i, idx: (0, 0, i, 0))],
        out_specs=pl.BlockSpec((rows, C), lambda i, idx: (idx[0] * nrb + i, 0)))
    return pl.pallas_call(body, name=name, out_shape=jax.ShapeDtypeStruct((2 * R, C), F32), grid_spec=grid_spec,
                          compiler_params=_params("parallel"))(idx, q_up, r2_up)


def _pair_share(gs):
    def build(_, g_refs, sems):
        x, y, c, _ = _place()
        copies = []
        for t, g in enumerate(g_refs):
            rows = g.shape[0] // 2
            mine = g.at[pl.ds(c * rows, rows), :]
            copies.append(_remote(mine, mine, sems, t, (x, y, 1 - c)))
        return copies

    return (tuple(gs), tuple(jax.ShapeDtypeStruct(g.shape, g.dtype) for g in gs), len(gs), build, True)


def _all_reduce_small(packed):
    R, C = packed.shape
    N_DEV = 2 * N_CHIPS

    def body(x_ref, out_ref, all_ref, send_sems, recv_sems, local_sem):
        x, y, c, others = _place()
        me, sib = (x, y, c), (x, y, 1 - c)
        sems = (send_sems, recv_sems)

        def rows(px, py, pc):
            return all_ref.at[4 * px + 2 * py + pc]

        mine = pltpu.make_async_copy(x_ref, rows(*me), local_sem)
        mine.start()
        first = [_remote(x_ref, rows(*me), sems, 0, sib)]
        first += [_remote(x_ref, rows(*me), sems, 1 + j, (*chip, c)) for j, chip in enumerate(others)]
        for cp in first:
            cp.start()
        passed = [_remote(rows(*chip, c), rows(*chip, c), sems, 4 + j, sib) for j, chip in enumerate(others)]
        for j, chip in enumerate(others):
            _remote(rows(*chip, c), rows(*chip, c), sems, 1 + j, me).wait_recv()
            passed[j].start()
        _remote(rows(*sib), rows(*sib), sems, 0, me).wait_recv()
        for j, chip in enumerate(others):
            _remote(rows(*chip, 1 - c), rows(*chip, 1 - c), sems, 4 + j, me).wait_recv()
        for cp in first + passed:
            cp.wait_send()
        mine.wait()
        total = all_ref[0]
        for d in range(1, N_DEV):
            total = total + all_ref[d]
        out_ref[...] = total

    vmem = pl.BlockSpec(memory_space=pltpu.VMEM)
    return pl.pallas_call(
        body, name="all_reduce_small", out_shape=jax.ShapeDtypeStruct((R, C), F32),
        in_specs=[vmem], out_specs=vmem,
        scratch_shapes=[pltpu.VMEM((N_DEV, R, C), F32), pltpu.SemaphoreType.DMA((7,)), pltpu.SemaphoreType.DMA((7,)),
                        pltpu.SemaphoreType.DMA],
        compiler_params=pltpu.CompilerParams(vmem_limit_bytes=VMEM_LIMIT),
    )(packed)


SMALL_NAMES = ("w_spatial", "b_spatial", "norm_g", "gmlp_ln_g", "gmlp_ln_b", "final_norm_g", "attn_sink")


def _pack_small(parts, extra=None):
    blocks = []
    for n in SMALL_NAMES:
        flat = parts[n].reshape(-1).astype(F32)
        rows = -(-flat.shape[0] // (8 * LANES)) * 8
        flat = jnp.pad(flat, (0, rows * LANES - flat.shape[0]))
        blocks.append(flat.reshape(rows, LANES))
    blocks.append(jnp.zeros((8, LANES), F32) if extra is None else extra)
    return jnp.concatenate(blocks, axis=0)


def _unpack_small(packed, shapes):
    out, r = {}, 0
    for n in SMALL_NAMES:
        size = 1
        for s in shapes[n]:
            size *= s
        rows = -(-size // (8 * LANES)) * 8
        out[n] = packed[r:r + rows].reshape(-1)[:size].reshape(shapes[n])
        r += rows
    return out, packed[r:r + 8]


def kernel(x, positions, norm_g, w_in, attn_sink, gmlp_ln_g, gmlp_ln_b, w_spatial, b_spatial, w_up_attn, w_up_gmlp, w_out, final_norm_g, loss_target, m_norm_g, m_w_in, m_attn_sink, m_gmlp_ln_g, m_gmlp_ln_b, m_w_spatial, m_b_spatial, m_w_up_attn, m_w_up_gmlp, m_w_out, m_final_norm_g, v_norm_g, v_w_in, v_attn_sink, v_gmlp_ln_g, v_gmlp_ln_b, v_w_spatial, v_b_spatial, v_w_up_attn, v_w_up_gmlp, v_w_out, v_final_norm_g):
    xs, tgt = x[0], loss_target[0]
    mx, my, mc = lax.axis_index("x"), lax.axis_index("y"), lax.axis_index("c")
    idx = jnp.stack([mc, 2 * mx + my]).astype(jnp.int32)
    own, x_nbr, y_nbr, diag = 2 * mx + my, 2 * (1 - mx) + my, 2 * mx + (1 - my), 2 * (1 - mx) + (1 - my)
    tile_cols = jnp.stack([2 * own, 2 * own + 1, 2 * x_nbr, 2 * y_nbr, 2 * x_nbr + 1, 2 * y_nbr + 1, 2 * diag,
                           2 * diag + 1]).astype(jnp.int32)

    square = (D_MODEL, D_MODEL)
    w_full = _cast_into(idx, w_in[0], (D_MODEL, PROJ_WIDTH), "cast_w_in")
    ups = [_cast_into(idx, w_up_attn[0], square, "cast_w_up_attn"), _cast_into(idx, w_up_gmlp[0], square, "cast_w_up_gmlp"),
           _cast_into(idx, w_out[0], square, "cast_w_out")]

    tables = _rope_tables(positions[0])
    bias = _band_bias()
    sink = attn_sink[0]
    ws, bs_t = w_spatial[0], b_spatial[0].T
    h = _rms_fwd(xs, norm_g)
    proj, w_full = _proj_gather(tile_cols, h, w_full)
    attn, a_in, *got = _attn_fwd(proj, tables, bias, sink, comm=_gather_squares(ups[:2], over_ici=True))
    b_in, wua, wug = _gmlp_fwd(proj, ws, bs_t, gmlp_ln_g, gmlp_ln_b, comm=_gather_squares(got, over_ici=False))
    y_a, wo = _matmul(a_in, wua, mode="nn", out_dtype=F32, name="up_attn", comm=_gather_squares(ups[2:], over_ici=True))
    y_b, wo = _matmul(b_in, wug, mode="nn", out_dtype=F32, name="up_gmlp", comm=_gather_squares([wo], over_ici=False))
    merged = _merge_fwd(proj, y_a, y_b)
    x2 = _matmul(merged, wo, mode="nn", out_dtype=F32, name="out_proj", residual=xs, tk=2048)
    dx2, dx2_b, d_fg, loss_part = _loss_head(x2, tgt, final_norm_g.reshape(1, D_MODEL))

    dmerged = _matmul(dx2_b, wo, mode="nt", out_dtype=F32, name="d_merged")
    dy_a, dy_b, d_gates = _merge_bwd(proj, y_a, y_b, dmerged)

    p_ups = [_matmul(a_in, dy_a, mode="tn", out_dtype=BF16, name="dw_up_attn"),
             _matmul(b_in, dy_b, mode="tn", out_dtype=BF16, name="dw_up_gmlp"),
             _matmul(merged, dx2_b, mode="tn", out_dtype=BF16, name="dw_out")]
    p_ups = [p.reshape(N_CHIPS, 2, UP_ROWS // 2, D_MODEL) for p in p_ups]
    da_in, *r1_ups = _matmul(dy_a, wua, mode="nt", out_dtype=F32, name="d_a_in", comm=_pair_halves(p_ups))
    q_ups = [_pair_add_up(idx, p, r, "pair_add_up%d" % t) for t, (p, r) in enumerate(zip(p_ups, r1_ups))]
    db_in = _matmul(dy_b, wug, mode="nt", out_dtype=F32, name="d_b_in")
    dq, dk, dv, dga, d_sink = _attn_bwd(proj, tables, bias, sink, attn, da_in)
    d_gmlp, d_ws, d_bs, d_lg, d_lb = _gmlp_bwd(proj, ws, bs_t, gmlp_ln_g, gmlp_ln_b, db_in)
    dproj = jnp.concatenate([dq, dk, dv, dga, d_gmlp, d_gates], axis=1)

    half = D_MODEL // 2
    h_sib = lax.dynamic_slice(h, (0, (1 - mc) * half), (h.shape[0], half))
    h_own = lax.dynamic_slice(h, (0, mc * half), (h.shape[0], half))
    p_sib, *r2_ups = _matmul(h_sib, dproj, mode="tn", out_dtype=BF16, name="dw_in_sibling", comm=_slab_exchange(q_ups))
    p_own, r1_in = _matmul(h_own, dproj, mode="tn", out_dtype=BF16, name="dw_in_own", comm=_send_whole([p_sib]))
    q_in = _pair_add_in(p_own, r1_in)
    dh, r2_in = _matmul(dproj, w_full, mode="nt", out_dtype=F32, name="d_h", comm=_slab_exchange([q_in]), tk=SLAB // 2)
    grad_x, d_ng = _rms_bwd(xs, dh, dx2, norm_g)

    g_half = [_slab_add_in(idx, q_in, r2_in)] + [_slab_add_up(idx, q, r, "slab_add_up%d" % t)
                                                 for t, (q, r) in enumerate(zip(q_ups, r2_ups))]
    g_big = _exchange(_pair_share(g_half), "pair_share")

    small_shapes = {"w_spatial": w_spatial.shape, "b_spatial": b_spatial.shape, "norm_g": norm_g.shape,
                    "gmlp_ln_g": gmlp_ln_g.shape, "gmlp_ln_b": gmlp_ln_b.shape, "final_norm_g": final_norm_g.shape,
                    "attn_sink": attn_sink.shape}
    d_small = {"w_spatial": d_ws, "b_spatial": d_bs, "norm_g": d_ng, "gmlp_ln_g": d_lg, "gmlp_ln_b": d_lb,
               "final_norm_g": d_fg, "attn_sink": d_sink[:, 0, :HEADS_PER_STEP]}
    g_small = _all_reduce_small(_pack_small(d_small, loss_part))

    w_small = _pack_small(dict(w_spatial=w_spatial, b_spatial=b_spatial, norm_g=norm_g, gmlp_ln_g=gmlp_ln_g,
                               gmlp_ln_b=gmlp_ln_b, final_norm_g=final_norm_g, attn_sink=attn_sink))
    m_small = _pack_small(dict(w_spatial=m_w_spatial, b_spatial=m_b_spatial, norm_g=m_norm_g, gmlp_ln_g=m_gmlp_ln_g,
                               gmlp_ln_b=m_gmlp_ln_b, final_norm_g=m_final_norm_g, attn_sink=m_attn_sink))
    v_small = _pack_small(dict(w_spatial=v_w_spatial, b_spatial=v_b_spatial, norm_g=v_norm_g, gmlp_ln_g=v_gmlp_ln_g,
                               gmlp_ln_b=v_gmlp_ln_b, final_norm_g=v_final_norm_g, attn_sink=v_attn_sink))
    small_out = _adamw(w_small, g_small, m_small, v_small, "adamw_small", rows=w_small.shape[0])
    big = {
        "w_in": _adamw(w_in[0], g_big[0], m_w_in[0], v_w_in[0], "adamw_w_in"),
        "w_up_attn": _adamw(w_up_attn[0], g_big[1], m_w_up_attn[0], v_w_up_attn[0], "adamw_w_up_attn"),
        "w_up_gmlp": _adamw(w_up_gmlp[0], g_big[2], m_w_up_gmlp[0], v_w_up_gmlp[0], "adamw_w_up_gmlp"),
        "w_out": _adamw(w_out[0], g_big[3], m_w_out[0], v_w_out[0], "adamw_w_out"),
    }

    order = ("norm_g", "w_in", "attn_sink", "gmlp_ln_g", "gmlp_ln_b", "w_spatial", "b_spatial", "w_up_attn", "w_up_gmlp",
             "w_out", "final_norm_g")
    outs = []
    loss = None
    for kind in range(4):
        small, extra = _unpack_small(small_out[kind], small_shapes)
        if kind == 0:
            loss = extra[0, 0]
        for n in order:
            outs.append(big[n][kind][None] if n in big else small[n])
    return (loss, grad_x[None], *outs)
```

```python
import functools

import jax
import jax.numpy as jnp
from jax import lax
from jax.experimental import pallas as pl
from jax.experimental.pallas import tpu as pltpu

F32 = jnp.float32
BF16 = jnp.bfloat16

D_MODEL = 4096
N_Q_HEADS = 64
N_KV_HEADS = 8
HEAD_DIM = 64
Q_PER_KV = N_Q_HEADS // N_KV_HEADS
ATTN_WIDTH = N_Q_HEADS * HEAD_DIM
KV_WIDTH = N_KV_HEADS * HEAD_DIM
WINDOW = 128
BLOCK = 128
ROPE_THETA = 500000.0
ROPE_DIM = HEAD_DIM // 4
ROPE_HALF = ROPE_DIM // 2
GMLP_WIDTH = D_MODEL
GMLP_GROUPS = 8
GMLP_GROUP_DIM = GMLP_WIDTH // GMLP_GROUPS
GMLP_CHUNK = 128
NORM_EPS = 1e-5
LN_EPS = 1e-5

PROJ_SIZES = (ATTN_WIDTH, KV_WIDTH, KV_WIDTH, ATTN_WIDTH, GMLP_WIDTH, GMLP_WIDTH, GMLP_WIDTH, D_MODEL, D_MODEL)
PROJ_WIDTH = sum(PROJ_SIZES)
OFF_Q, OFF_K, OFF_V, OFF_GA, OFF_U, OFF_VG, OFF_GB, OFF_MA, OFF_MB = (
    sum(PROJ_SIZES[:i]) for i in range(len(PROJ_SIZES)))

ADAM_LR = 0.001
ADAM_B1 = 0.9
ADAM_B2 = 0.999
ADAM_EPS = 1e-08
ADAM_WD = 0.01
ADAM_STEP = 10

N_CHIPS = 4
SLAB = PROJ_WIDTH // N_CHIPS
UP_ROWS = D_MODEL // N_CHIPS
LANES = 128
COL_BLK = 1024
HEADS_PER_STEP = 2 * Q_PER_KV
VMEM_LIMIT = 56 * 1024 * 1024

MESH = pl.DeviceIdType.MESH
HBM_SPEC = pl.BlockSpec(memory_space=pltpu.HBM)
SMEM_SPEC = pl.BlockSpec(memory_space=pltpu.SMEM)


def _params(*sem):
    return pltpu.CompilerParams(dimension_semantics=sem, vmem_limit_bytes=VMEM_LIMIT)


def _sigmoid(x):
    return jax.nn.sigmoid(x)


def _gelu(x):
    return jax.nn.gelu(x, approximate=True)


def _gelu_both(x):
    c = 0.7978845608028654
    x2 = x * x
    t = jnp.tanh(c * (x + 0.044715 * x * x2))
    cdf = 0.5 * (1.0 + t)
    return x * cdf, cdf + 0.5 * x * (1.0 - t * t) * c * (1.0 + 3 * 0.044715 * x2)


def _matmul(a, b, *, mode, out_dtype, name, residual=None, comm=None, tm=1024, tn=1024, tk=4096):
    if mode == "nn":
        (M, K), N = a.shape, b.shape[1]
    elif mode == "nt":
        (M, K), N = a.shape, b.shape[0]
    else:
        (K, M), N = a.shape, b.shape[1]
    tm, tn, tk = min(tm, M), min(tn, N), min(tk, K)
    assert M % tm == 0 and N % tn == 0 and K % tk == 0
    ni, nj, nk = M // tm, N // tn, K // tk
    if mode == "nn":
        a_spec = pl.BlockSpec((tm, tk), lambda i, j, k: (i, k))
        b_spec = pl.BlockSpec((tk, tn), lambda i, j, k: (k, j))
        dims = (((1,), (0,)), ((), ()))
    elif mode == "nt":
        a_spec = pl.BlockSpec((tm, tk), lambda i, j, k: (i, k))
        b_spec = pl.BlockSpec((tn, tk), lambda i, j, k: (j, k))
        dims = (((1,), (1,)), ((), ()))
    else:
        a_spec = pl.BlockSpec((tk, tm), lambda i, j, k: (k, i))
        b_spec = pl.BlockSpec((tk, tn), lambda i, j, k: (k, j))
        dims = (((0,), (0,)), ((), ()))
    o_spec = pl.BlockSpec((tm, tn), lambda i, j, k: (i, j))
    n_res = 1 if residual is not None else 0
    c_arrays, c_shapes, c_sems, c_build, c_alias = _comm_fields(comm)
    n_cin, n_cout = len(c_arrays), len(c_shapes)

    def body(*refs):
        a_ref, b_ref = refs[0], refs[1]
        r_ref = refs[2] if n_res else None
        cin = refs[2 + n_res:2 + n_res + n_cin]
        o_ref = refs[2 + n_res + n_cin]
        cout = refs[3 + n_res + n_cin:3 + n_res + n_cin + n_cout]
        n_acc = 1 if nk > 1 else 0
        acc = refs[3 + n_res + n_cin + n_cout] if n_acc else None
        sems = refs[3 + n_acc + n_res + n_cin + n_cout:]
        i, j, k = pl.program_id(0), pl.program_id(1), pl.program_id(2)
        if comm is not None:
            _start_when((i == 0) & (j == 0) & (k == 0), c_build(cin, cout, sems))

        def product():
            return lax.dot_general(a_ref[...], b_ref[...], dims, preferred_element_type=F32)

        def finish(r):
            if n_res:
                r = r + r_ref[...]
            o_ref[...] = r.astype(out_dtype)

        if nk == 1:
            finish(product())
        else:
            @pl.when(k == 0)
            def _():
                acc[...] = product()

            @pl.when((k > 0) & (k < nk - 1))
            def _():
                acc[...] += product()

            @pl.when(k == nk - 1)
            def _():
                finish(acc[...] + product())

        if comm is not None:
            _wait_when((i == ni - 1) & (j == nj - 1) & (k == nk - 1), c_build(cin, cout, sems))

    args = (a, b) + ((residual,) if n_res else ()) + tuple(c_arrays)
    in_specs = [a_spec, b_spec] + ([o_spec] if n_res else []) + [HBM_SPEC] * n_cin
    scratch = ([pltpu.VMEM((tm, tn), F32)] if nk > 1 else []) + _comm_scratch(c_sems)
    sem = ("arbitrary",) * 3 if comm is not None else ("parallel", "parallel", "arbitrary")
    aliases = {2 + n_res + t: 1 + t for t in range(n_cin)} if c_alias else {}
    out = pl.pallas_call(
        body, name=name, out_shape=(jax.ShapeDtypeStruct((M, N), out_dtype),) + tuple(c_shapes),
        grid=(ni, nj, nk), in_specs=in_specs, out_specs=(o_spec,) + (HBM_SPEC,) * n_cout,
        scratch_shapes=scratch, input_output_aliases=aliases, compiler_params=_params(*sem),
    )(*args)
    return out if comm is not None else out[0]


def _cast_into(idx, w, full_shape, name, rows=64):
    R, C = w.shape
    rows = min(rows, R)
    nrb = R // rows
    by_cols = C != full_shape[1]

    def body(idx_ref, w_ref, o_ref):
        o_ref[...] = w_ref[...].astype(BF16)

    out_map = (lambda i, idx: (i, idx[1])) if by_cols else (lambda i, idx: (idx[1] * nrb + i, 0))
    grid_spec = pltpu.PrefetchScalarGridSpec(
        num_scalar_prefetch=1, grid=(nrb,), in_specs=[pl.BlockSpec((rows, C), lambda i, idx: (i, 0))],
        out_specs=pl.BlockSpec((rows, C), out_map))
    return pl.pallas_call(body, name=name, out_shape=jax.ShapeDtypeStruct(full_shape, BF16), grid_spec=grid_spec,
                          compiler_params=_params("parallel"))(idx, w)


def _rms_fwd(x, g, rows=256):
    S, D = x.shape
    rows = min(rows, S)

    def body(x_ref, g_ref, h_ref):
        xv = x_ref[...]
        ms = jnp.mean(xv * xv, axis=-1, keepdims=True)
        h_ref[...] = (xv * lax.rsqrt(ms + NORM_EPS) * g_ref[...]).astype(BF16)

    spec = pl.BlockSpec((rows, D), lambda i: (i, 0))
    return pl.pallas_call(body, name="rms_fwd", out_shape=jax.ShapeDtypeStruct((S, D), BF16), grid=(S // rows,),
                          in_specs=[spec, pl.BlockSpec((1, D), lambda i: (0, 0))], out_specs=spec,
                          compiler_params=_params("parallel"))(x, g)


def _rope_tables(positions):
    inv_freq = ROPE_THETA ** (-jnp.arange(ROPE_HALF, dtype=F32) * 2.0 / ROPE_DIM)
    ang = positions.astype(F32)[:, None] * inv_freq
    cos, sin = jnp.cos(ang), jnp.sin(ang)
    S = positions.shape[0]
    rest = HEAD_DIM - ROPE_DIM
    zeros_h, zeros_r = jnp.zeros((S, ROPE_HALF), F32), jnp.zeros((S, rest), F32)
    c = jnp.concatenate([cos, cos, jnp.ones((S, rest), F32)], axis=1)
    s1 = jnp.concatenate([-sin, zeros_h, zeros_r], axis=1)
    s2 = jnp.concatenate([zeros_h, sin, zeros_r], axis=1)
    reps = LANES // HEAD_DIM
    return jnp.tile(c, (1, reps)), jnp.tile(s1, (1, reps)), jnp.tile(s2, (1, reps))


def _rope(t, c, s1, s2, sign):
    n = t.shape[1]
    reps = n // c.shape[1]
    if reps > 1:
        c, s1, s2 = jnp.tile(c, (1, reps)), jnp.tile(s1, (1, reps)), jnp.tile(s2, (1, reps))
    up = pltpu.roll(t, n - ROPE_HALF, 1)
    down = pltpu.roll(t, ROPE_HALF, 1)
    return t * c + sign * (up * s1 + down * s2)


def _attn_specs(nb):
    last = nb - 1

    def cur(i):
        return jnp.minimum(i, last)

    def prev(i):
        return jnp.maximum(jnp.minimum(i, last) - 1, 0)

    kq, kk, kv, kg = OFF_Q // COL_BLK, OFF_K // LANES, OFF_V // LANES, OFF_GA // COL_BLK
    wide = lambda off: pl.BlockSpec((BLOCK, COL_BLK), lambda p, i: (cur(i), off + p))
    kv_cur = lambda off: pl.BlockSpec((BLOCK, LANES), lambda p, i: (cur(i), off + p))
    kv_prev = lambda off: pl.BlockSpec((BLOCK, LANES), lambda p, i: (prev(i), off + p))
    tab_cur = pl.BlockSpec((BLOCK, LANES), lambda p, i: (cur(i), 0))
    tab_prev = pl.BlockSpec((BLOCK, LANES), lambda p, i: (prev(i), 0))
    proj_specs = [wide(kq), kv_cur(kk), kv_prev(kk), kv_cur(kv), kv_prev(kv), wide(kg)]
    table_specs = [tab_cur] * 3 + [tab_prev] * 3
    bias_spec = pl.BlockSpec((1, BLOCK, QCOLS), lambda p, i: (jnp.minimum(i, 1), 0, 0))
    return proj_specs, table_specs + [bias_spec], cur, prev


PAIRS = Q_PER_KV // 2
QCOLS = PAIRS * BLOCK
NT = (((1,), (1,)), ((), ()))
TN = (((0,), (0,)), ((), ()))


def _band_bias():
    first = jnp.where(jnp.arange(BLOCK)[:, None] > (jnp.arange(QCOLS) % BLOCK)[None, :], -1e30, 0.0)
    return jnp.stack([first, jnp.zeros_like(first)]).astype(F32)


def _from_prev():
    a = lax.broadcasted_iota(jnp.int32, (BLOCK, QCOLS), 1) & (BLOCK - 1)
    return lax.broadcasted_iota(jnp.int32, (BLOCK, QCOLS), 0) > a


def _pair_rows(ref, kvh):
    base = kvh * Q_PER_KV * HEAD_DIM
    return jnp.concatenate([ref[:, base + b * LANES: base + (b + 1) * LANES] for b in range(PAIRS)], axis=0).astype(BF16)


def _unpair_rows(ref, kvh, val_t):
    base = kvh * Q_PER_KV * HEAD_DIM
    for b in range(PAIRS):
        ref[:, base + b * LANES: base + (b + 1) * LANES] = val_t[:, b * BLOCK:(b + 1) * BLOCK].T.astype(ref.dtype)


def _split_heads(t, kvh):
    lane = lax.broadcasted_iota(jnp.int32, t.shape, 1)
    other = pltpu.roll(t, HEAD_DIM, 1)
    lo = jnp.where(lane < HEAD_DIM, t if kvh == 0 else other, 0.0)
    hi = jnp.where(lane >= HEAD_DIM, other if kvh == 0 else t, 0.0)
    return lo.astype(BF16), hi.astype(BF16)


def _join_heads(even, odd, kvh):
    lane = lax.broadcasted_iota(jnp.int32, even.shape, 1)
    lo = jnp.where(lane < HEAD_DIM, even, 0.0) + pltpu.roll(jnp.where(lane >= HEAD_DIM, odd, 0.0), HEAD_DIM, 1)
    return lo if kvh == 0 else pltpu.roll(lo, HEAD_DIM, 1)


def _sink_row(sink_ref, pair, kvh, odd):
    first = (pair * 2 + kvh) * Q_PER_KV + odd
    return jnp.concatenate([jnp.full((1, BLOCK), sink_ref[first + 2 * b], F32) for b in range(PAIRS)], axis=1)


def _softmax_t(kp, kc, qp, sink_row, bias, from_prev):
    st = jnp.where(from_prev, lax.dot_general(kp, qp, NT, preferred_element_type=F32),
                   lax.dot_general(kc, qp, NT, preferred_element_type=F32)) + bias
    m = jnp.maximum(jnp.max(st, axis=0, keepdims=True), sink_row)
    e = jnp.exp(st - m)
    es = jnp.exp(sink_row - m)
    inv = 1.0 / (jnp.sum(e, axis=0, keepdims=True) + es)
    return e * inv, es * inv


def _roped_keys(k_ref, kp_ref, tabs):
    c, s1, s2, cp, s1p, s2p = (t[...] for t in tabs)
    k_cur = _rope(k_ref[...].astype(F32), c, s1, s2, 1.0)
    k_prev = _rope(kp_ref[...].astype(F32), cp, s1p, s2p, 1.0)
    return k_prev, k_cur


SCALE = HEAD_DIM ** -0.5


def _attn_fwd(proj, tables, bias, sink, comm=None):
    S = proj.shape[0]
    nb = S // BLOCK
    npairs = N_KV_HEADS // 2
    proj_specs, table_specs, _, _ = _attn_specs(nb)
    c_arrays, c_shapes, c_sems, c_build, c_alias = _comm_fields(comm)
    n_in = len(proj_specs) + len(table_specs) + 1

    def body(*refs):
        (q_ref, k_ref, kp_ref, v_ref, vp_ref, ga_ref, c_ref, s1_ref, s2_ref, cp_ref, s1p_ref, s2p_ref, bias_ref,
         sink_ref) = refs[:n_in]
        cin = refs[n_in:n_in + len(c_arrays)]
        attn_ref, ain_ref = refs[n_in + len(c_arrays):n_in + len(c_arrays) + 2]
        cout = refs[n_in + len(c_arrays) + 2:n_in + len(c_arrays) + 2 + len(c_shapes)]
        qr_scr = refs[n_in + len(c_arrays) + 2 + len(c_shapes)]
        sems = refs[n_in + len(c_arrays) + 3 + len(c_shapes):]
        pair, i = pl.program_id(0), pl.program_id(1)
        if comm is not None:
            _start_when((pair == 0) & (i == 0), c_build(cin, cout, sems))
        tabs = (c_ref, s1_ref, s2_ref, cp_ref, s1p_ref, s2p_ref)
        qr_scr[...] = _rope(q_ref[...].astype(F32), c_ref[...], s1_ref[...], s2_ref[...], 1.0) * SCALE
        k_prev, k_cur = _roped_keys(k_ref, kp_ref, tabs)
        v_prev, v_cur = vp_ref[...].astype(F32), v_ref[...].astype(F32)
        from_prev = _from_prev()
        for kvh in range(2):
            qp = _pair_rows(qr_scr, kvh)
            kps, kcs = _split_heads(k_prev, kvh), _split_heads(k_cur, kvh)
            vps, vcs = _split_heads(v_prev, kvh), _split_heads(v_cur, kvh)
            o_t = None
            for odd in range(2):
                p_t, _ = _softmax_t(kps[odd], kcs[odd], qp, _sink_row(sink_ref, pair, kvh, odd), bias_ref[0], from_prev)
                pb = p_t.astype(BF16)
                part = (lax.dot_general(vps[odd], jnp.where(from_prev, pb, 0), TN, preferred_element_type=F32)
                        + lax.dot_general(vcs[odd], jnp.where(from_prev, 0, pb), TN, preferred_element_type=F32))
                o_t = part if o_t is None else o_t + part
            _unpair_rows(attn_ref, kvh, o_t)
        ga = ga_ref[...].astype(F32)
        ain_ref[...] = (attn_ref[...] * (ga * _sigmoid(ga))).astype(BF16)
        if comm is not None:
            _wait_when((pair == npairs - 1) & (i == nb - 1), c_build(cin, cout, sems))

    out_spec = pl.BlockSpec((BLOCK, COL_BLK), lambda p, i: (i, p))
    aliases = {n_in + t: 2 + t for t in range(len(c_arrays))} if c_alias else {}
    out = pl.pallas_call(
        body, name="attn_fwd",
        out_shape=(jax.ShapeDtypeStruct((S, ATTN_WIDTH), F32), jax.ShapeDtypeStruct((S, ATTN_WIDTH), BF16)) + tuple(c_shapes),
        grid=(npairs, nb), in_specs=proj_specs + table_specs + [SMEM_SPEC] + [HBM_SPEC] * len(c_arrays),
        out_specs=(out_spec, out_spec) + (HBM_SPEC,) * len(c_shapes),
        scratch_shapes=[pltpu.VMEM((BLOCK, COL_BLK), F32)] + _comm_scratch(c_sems), input_output_aliases=aliases,
        compiler_params=_params("arbitrary", "arbitrary"),
    )(*([proj] * 6), *tables, *tables, bias, sink, *c_arrays)
    return out


def _attn_bwd(proj, tables, bias, sink, attn, da_in):
    S = proj.shape[0]
    nb = S // BLOCK
    proj_specs, table_specs, cur, prev = _attn_specs(nb)
    wide_cur = pl.BlockSpec((BLOCK, COL_BLK), lambda p, i: (cur(i), p))
    kv_out = pl.BlockSpec((BLOCK, LANES), lambda p, i: (jnp.maximum(i - 1, 0), p))

    def body(q_ref, k_ref, kp_ref, v_ref, vp_ref, ga_ref, c_ref, s1_ref, s2_ref, cp_ref, s1p_ref, s2p_ref, bias_ref,
             sink_ref, attn_ref, da_ref, dq_ref, dk_ref, dv_ref, dga_ref, dsink_ref,
             qr_scr, do_scr, dq_scr, dk_scr, dv_scr, carry_k, carry_v, sink_acc):
        pair, i = pl.program_id(0), pl.program_id(1)
        tabs = (c_ref, s1_ref, s2_ref, cp_ref, s1p_ref, s2p_ref)

        @pl.when(i < nb)
        def _():
            ga = ga_ref[...].astype(F32)
            sg = _sigmoid(ga)
            da = da_ref[...]
            dga_ref[...] = (da * attn_ref[...] * (sg * (1.0 + ga * (1.0 - sg)))).astype(BF16)
            do_scr[...] = da * (ga * sg)
            qr_scr[...] = _rope(q_ref[...].astype(F32), c_ref[...], s1_ref[...], s2_ref[...], 1.0) * SCALE
            k_prev, k_cur = _roped_keys(k_ref, kp_ref, tabs)
            v_prev, v_cur = vp_ref[...].astype(F32), v_ref[...].astype(F32)
            from_prev = _from_prev()
            dkv = [None] * 4
            for kvh in range(2):
                qp, dop = _pair_rows(qr_scr, kvh), _pair_rows(do_scr, kvh)
                kps, kcs = _split_heads(k_prev, kvh), _split_heads(k_cur, kvh)
                vps, vcs = _split_heads(v_prev, kvh), _split_heads(v_cur, kvh)
                dq_t, halves = None, []
                for odd in range(2):
                    p_t, p_sink = _softmax_t(kps[odd], kcs[odd], qp, _sink_row(sink_ref, pair, kvh, odd), bias_ref[0],
                                             from_prev)
                    dp_t = jnp.where(from_prev, lax.dot_general(vps[odd], dop, NT, preferred_element_type=F32),
                                     lax.dot_general(vcs[odd], dop, NT, preferred_element_type=F32))
                    delta = jnp.sum(p_t * dp_t, axis=0, keepdims=True)
                    ds = (p_t * (dp_t - delta)).astype(BF16)
                    pb = p_t.astype(BF16)
                    ds_prev, ds_cur = jnp.where(from_prev, ds, 0), jnp.where(from_prev, 0, ds)
                    p_prev, p_cur = jnp.where(from_prev, pb, 0), jnp.where(from_prev, 0, pb)
                    part = (lax.dot_general(kps[odd], ds_prev, TN, preferred_element_type=F32)
                            + lax.dot_general(kcs[odd], ds_cur, TN, preferred_element_type=F32))
                    dq_t = part if dq_t is None else dq_t + part
                    halves.append([jnp.dot(ds_prev, qp, preferred_element_type=F32),
                                   jnp.dot(ds_cur, qp, preferred_element_type=F32),
                                   jnp.dot(p_prev, dop, preferred_element_type=F32),
                                   jnp.dot(p_cur, dop, preferred_element_type=F32)])
                    contrib = -p_sink * delta

                    @pl.when(i == 0)
                    def _():
                        sink_acc[2 * kvh + odd] = contrib

                    @pl.when(i > 0)
                    def _():
                        sink_acc[2 * kvh + odd] += contrib

                _unpair_rows(dq_scr, kvh, dq_t)
                for n in range(4):
                    joined = _join_heads(halves[0][n], halves[1][n], kvh)
                    dkv[n] = joined if dkv[n] is None else dkv[n] + joined
            dk_scr[0:BLOCK, :], dk_scr[BLOCK:2 * BLOCK, :] = dkv[0], dkv[1]
            dv_scr[0:BLOCK, :], dv_scr[BLOCK:2 * BLOCK, :] = dkv[2], dkv[3]

            dq_ref[...] = _rope(dq_scr[...] * SCALE, c_ref[...], s1_ref[...], s2_ref[...], -1.0).astype(BF16)
            dk_prev = _rope(dk_scr[0:BLOCK, :], cp_ref[...], s1p_ref[...], s2p_ref[...], -1.0)
            dk_cur = _rope(dk_scr[BLOCK:2 * BLOCK, :], c_ref[...], s1_ref[...], s2_ref[...], -1.0)
            dv_prev, dv_cur = dv_scr[0:BLOCK, :], dv_scr[BLOCK:2 * BLOCK, :]

            @pl.when(i > 0)
            def _():
                dk_ref[...] = (carry_k[...] + dk_prev).astype(BF16)
                dv_ref[...] = (carry_v[...] + dv_prev).astype(BF16)

            carry_k[...] = dk_cur
            carry_v[...] = dv_cur

        @pl.when(i == nb)
        def _():
            dk_ref[...] = carry_k[...].astype(BF16)
            dv_ref[...] = carry_v[...].astype(BF16)
            lane = lax.broadcasted_iota(jnp.int32, (8, LANES), 1)
            acc = jnp.zeros((8, LANES), F32)
            for kvh in range(2):
                for odd in range(2):
                    for b in range(PAIRS):
                        val = jnp.sum(sink_acc[2 * kvh + odd, :, b * BLOCK:(b + 1) * BLOCK], axis=1, keepdims=True)
                        acc = jnp.where(lane == kvh * Q_PER_KV + 2 * b + odd, val, acc)
            dsink_ref[0] = acc

    return pl.pallas_call(
        body, name="attn_bwd",
        out_shape=(jax.ShapeDtypeStruct((S, ATTN_WIDTH), BF16), jax.ShapeDtypeStruct((S, KV_WIDTH), BF16),
                   jax.ShapeDtypeStruct((S, KV_WIDTH), BF16), jax.ShapeDtypeStruct((S, ATTN_WIDTH), BF16),
                   jax.ShapeDtypeStruct((N_KV_HEADS // 2, 8, LANES), F32)),
        grid=(N_KV_HEADS // 2, nb + 1),
        in_specs=proj_specs + table_specs + [SMEM_SPEC, wide_cur, wide_cur],
        out_specs=(wide_cur, kv_out, kv_out, wide_cur, pl.BlockSpec((1, 8, LANES), lambda p, i: (p, 0, 0))),
        scratch_shapes=[pltpu.VMEM((BLOCK, COL_BLK), F32), pltpu.VMEM((BLOCK, COL_BLK), F32),
                        pltpu.VMEM((BLOCK, COL_BLK), F32), pltpu.VMEM((2 * BLOCK, LANES), F32),
                        pltpu.VMEM((2 * BLOCK, LANES), F32), pltpu.VMEM((BLOCK, LANES), F32),
                        pltpu.VMEM((BLOCK, LANES), F32), pltpu.VMEM((4, 1, QCOLS), F32)],
        compiler_params=_params("arbitrary", "arbitrary"),
    )(*([proj] * 6), *tables, *tables, bias, sink, attn, da_in)


N_PARTS = GMLP_WIDTH // COL_BLK
GROUPS_PER_PART = COL_BLK // GMLP_GROUP_DIM


def _part_specs(off):
    return [pl.BlockSpec((GMLP_CHUNK, COL_BLK), functools.partial(lambda j, i: (i, j), off // COL_BLK + k))
            for k in range(N_PARTS)]


def _group(refs, g):
    lo = (g % GROUPS_PER_PART) * GMLP_GROUP_DIM
    return refs[g // GROUPS_PER_PART][:, lo:lo + GMLP_GROUP_DIM].astype(F32)


def _gmlp_norm_stats(vg_refs, gv_scr, grad_scr=None):
    total = jnp.zeros((GMLP_CHUNK, 1), F32)
    for k in range(N_PARTS):
        vg = vg_refs[k][...].astype(F32)
        if grad_scr is None:
            gv = _gelu(vg)
        else:
            gv, grad_scr[:, k * COL_BLK:(k + 1) * COL_BLK] = _gelu_both(vg)
        gv_scr[:, k * COL_BLK:(k + 1) * COL_BLK] = gv
        total = total + jnp.sum(gv, axis=1, keepdims=True)
    mu = total / GMLP_WIDTH
    xc = gv_scr[...] - mu
    var = jnp.sum(xc * xc, axis=1, keepdims=True) / GMLP_WIDTH
    return mu, lax.rsqrt(var + LN_EPS)


def _tril_bf16(ws_ref, g):
    t = lax.broadcasted_iota(jnp.int32, (GMLP_CHUNK, GMLP_CHUNK), 0)
    s = lax.broadcasted_iota(jnp.int32, (GMLP_CHUNK, GMLP_CHUNK), 1)
    return jnp.where(s <= t, ws_ref[g], 0.0).astype(BF16), s <= t


def _gmlp_fwd(proj, ws, bs_t, lg, lb, comm=None):
    S = proj.shape[0]
    nb = S // GMLP_CHUNK
    c_arrays, c_shapes, c_sems, c_build, c_alias = _comm_fields(comm)
    n_cin, n_cout = len(c_arrays), len(c_shapes)

    def body(*refs):
        u_refs, vg_refs, gb_refs = refs[0:4], refs[4:8], refs[8:12]
        ws_ref, bst_ref, lg_ref, lb_ref = refs[12:16]
        cin, out_ref, cout = refs[16:16 + n_cin], refs[16 + n_cin], refs[17 + n_cin:17 + n_cin + n_cout]
        gv_scr, sems = refs[17 + n_cin + n_cout], refs[18 + n_cin + n_cout:]
        if comm is not None:
            _start_when(pl.program_id(0) == 0, c_build(cin, cout, sems))
        mu, rstd = _gmlp_norm_stats(vg_refs, gv_scr)
        for g in range(GMLP_GROUPS):
            cols = slice(g * GMLP_GROUP_DIM, (g + 1) * GMLP_GROUP_DIM)
            vn = (gv_scr[:, cols] - mu) * rstd * lg_ref[:, cols] + lb_ref[:, cols]
            w, _ = _tril_bf16(ws_ref, g)
            mixed = jnp.dot(w, vn.astype(BF16), preferred_element_type=F32) + bst_ref[:, g:g + 1]
            gb = _group(gb_refs, g)
            out_ref[:, cols] = (_gelu(_group(u_refs, g)) * mixed * (gb * _sigmoid(gb))).astype(BF16)
        if comm is not None:
            _wait_when(pl.program_id(0) == nb - 1, c_build(cin, cout, sems))

    full = lambda shape: pl.BlockSpec(shape, lambda i: tuple(0 for _ in shape))
    aliases = {16 + t: 1 + t for t in range(n_cin)} if c_alias else {}
    out = pl.pallas_call(
        body, name="gmlp_fwd", out_shape=(jax.ShapeDtypeStruct((S, GMLP_WIDTH), BF16),) + tuple(c_shapes), grid=(nb,),
        in_specs=_part_specs(OFF_U) + _part_specs(OFF_VG) + _part_specs(OFF_GB)
        + [full(ws.shape), full(bs_t.shape), full(lg.shape), full(lb.shape)] + [HBM_SPEC] * n_cin,
        out_specs=(pl.BlockSpec((GMLP_CHUNK, GMLP_WIDTH), lambda i: (i, 0)),) + (HBM_SPEC,) * n_cout,
        scratch_shapes=[pltpu.VMEM((GMLP_CHUNK, GMLP_WIDTH), F32)] + _comm_scratch(c_sems), input_output_aliases=aliases,
        compiler_params=_params("arbitrary" if comm is not None else "parallel"),
    )(*([proj] * 12), ws, bs_t, lg, lb, *c_arrays)
    return out if comm is not None else out[0]


def _gmlp_bwd(proj, ws, bs_t, lg, lb, db_in):
    S = proj.shape[0]
    nb = S // GMLP_CHUNK
    W = GMLP_WIDTH

    def body(*refs):
        u_refs, vg_refs, gb_refs = refs[0:4], refs[4:8], refs[8:12]
        ws_ref, bst_ref, lg_ref, lb_ref, db_ref = refs[12:17]
        out_ref, dws_ref, dbs_ref, dlg_ref, dlb_ref = refs[17:22]
        gv_scr, dvh_scr, vgrad_scr = refs[22:]
        i = pl.program_id(0)

        @pl.when(i == 0)
        def _():
            dws_ref[...] = jnp.zeros_like(dws_ref)
            dbs_ref[...] = jnp.zeros_like(dbs_ref)
            dlg_ref[...] = jnp.zeros_like(dlg_ref)
            dlb_ref[...] = jnp.zeros_like(dlb_ref)

        mu, rstd = _gmlp_norm_stats(vg_refs, gv_scr, vgrad_scr)
        sum_dvh = jnp.zeros((GMLP_CHUNK, 1), F32)
        sum_dvh_vh = jnp.zeros((GMLP_CHUNK, 1), F32)
        for g in range(GMLP_GROUPS):
            cols = slice(g * GMLP_GROUP_DIM, (g + 1) * GMLP_GROUP_DIM)
            vhat = (gv_scr[:, cols] - mu) * rstd
            vn = (vhat * lg_ref[:, cols] + lb_ref[:, cols]).astype(BF16)
            w, tril = _tril_bf16(ws_ref, g)
            mixed = jnp.dot(w, vn, preferred_element_type=F32) + bst_ref[:, g:g + 1]
            u, gb, db = _group(u_refs, g), _group(gb_refs, g), db_ref[:, cols]
            (gu, gu_grad), sgb = _gelu_both(u), _sigmoid(gb)
            dsg = db * (gb * sgb)
            out_ref[:, 2 * W + g * GMLP_GROUP_DIM: 2 * W + (g + 1) * GMLP_GROUP_DIM] = (
                db * (gu * mixed) * (sgb * (1.0 + gb * (1.0 - sgb)))).astype(BF16)
            out_ref[:, cols] = (dsg * mixed * gu_grad).astype(BF16)
            dmixed = dsg * gu
            dmixed_b = dmixed.astype(BF16)
            dvn = lax.dot_general(w, dmixed_b, (((0,), (0,)), ((), ())), preferred_element_type=F32)
            dw = lax.dot_general(dmixed_b, vn, (((1,), (1,)), ((), ())), preferred_element_type=F32)
            dws_ref[g] += jnp.where(tril, dw, 0.0)
            dbs_ref[g] += jnp.sum(dmixed, axis=1, keepdims=True)
            dlg_ref[:, cols] += jnp.sum(dvn * vhat, axis=0, keepdims=True)
            dlb_ref[:, cols] += jnp.sum(dvn, axis=0, keepdims=True)
            dvh = dvn * lg_ref[:, cols]
            dvh_scr[:, cols] = dvh
            sum_dvh = sum_dvh + jnp.sum(dvh, axis=1, keepdims=True)
            sum_dvh_vh = sum_dvh_vh + jnp.sum(dvh * vhat, axis=1, keepdims=True)
        m1, m2 = sum_dvh / W, sum_dvh_vh / W
        for k in range(N_PARTS):
            cols = slice(k * COL_BLK, (k + 1) * COL_BLK)
            vhat = (gv_scr[:, cols] - mu) * rstd
            dgv = rstd * (dvh_scr[:, cols] - m1 - vhat * m2)
            out_ref[:, W + k * COL_BLK: W + (k + 1) * COL_BLK] = (dgv * vgrad_scr[:, cols]).astype(BF16)

    full = lambda shape: pl.BlockSpec(shape, lambda i: tuple(0 for _ in shape))
    row = pl.BlockSpec((GMLP_CHUNK, W), lambda i: (i, 0))
    return pl.pallas_call(
        body, name="gmlp_bwd",
        out_shape=(jax.ShapeDtypeStruct((S, 3 * W), BF16), jax.ShapeDtypeStruct(ws.shape, F32),
                   jax.ShapeDtypeStruct((GMLP_GROUPS, GMLP_CHUNK, 1), F32), jax.ShapeDtypeStruct((1, W), F32),
                   jax.ShapeDtypeStruct((1, W), F32)),
        grid=(nb,),
        in_specs=_part_specs(OFF_U) + _part_specs(OFF_VG) + _part_specs(OFF_GB)
        + [full(ws.shape), full(bs_t.shape), full(lg.shape), full(lb.shape), row],
        out_specs=(pl.BlockSpec((GMLP_CHUNK, 3 * W), lambda i: (i, 0)), full(ws.shape),
                   full((GMLP_GROUPS, GMLP_CHUNK, 1)), full((1, W)), full((1, W))),
        scratch_shapes=[pltpu.VMEM((GMLP_CHUNK, W), F32)] * 3,
        compiler_params=_params("arbitrary"),
    )(*([proj] * 12), ws, bs_t, lg, lb, db_in)


def _merge_fwd(proj, y_a, y_b, rows=256):
    S = proj.shape[0]
    rows = min(rows, S)

    def body(ma_ref, mb_ref, ya_ref, yb_ref, out_ref):
        out_ref[...] = (_sigmoid(ma_ref[...].astype(F32)) * ya_ref[...]
                        + _sigmoid(mb_ref[...].astype(F32)) * yb_ref[...]).astype(BF16)

    blk = lambda off: pl.BlockSpec((rows, COL_BLK), lambda i, j: (i, off // COL_BLK + j))
    return pl.pallas_call(
        body, name="merge_fwd", out_shape=jax.ShapeDtypeStruct((S, D_MODEL), BF16), grid=(S // rows, D_MODEL // COL_BLK),
        in_specs=[blk(OFF_MA), blk(OFF_MB), blk(0), blk(0)], out_specs=blk(0),
        compiler_params=_params("parallel", "parallel"),
    )(proj, proj, y_a, y_b)


def _merge_bwd(proj, y_a, y_b, dmerged, rows=128):
    S = proj.shape[0]
    rows = min(rows, S)
    nj = D_MODEL // COL_BLK

    def body(*refs):
        ma_refs, mb_refs = refs[0:nj], refs[nj:2 * nj]
        ya_ref, yb_ref, dm_ref, dya_ref, dyb_ref, dg_ref = refs[2 * nj:]
        for k in range(nj):
            cols = slice(k * COL_BLK, (k + 1) * COL_BLK)
            dm = dm_ref[:, cols]
            sa, sb = _sigmoid(ma_refs[k][...].astype(F32)), _sigmoid(mb_refs[k][...].astype(F32))
            dya_ref[:, cols] = (dm * sa).astype(BF16)
            dyb_ref[:, cols] = (dm * sb).astype(BF16)
            dg_ref[:, cols] = (dm * ya_ref[:, cols] * sa * (1.0 - sa)).astype(BF16)
            dg_ref[:, D_MODEL + k * COL_BLK: D_MODEL + (k + 1) * COL_BLK] = (
                dm * yb_ref[:, cols] * sb * (1.0 - sb)).astype(BF16)

    part = lambda off: [pl.BlockSpec((rows, COL_BLK), functools.partial(lambda j, i: (i, j), off // COL_BLK + k))
                        for k in range(nj)]
    row = pl.BlockSpec((rows, D_MODEL), lambda i: (i, 0))
    return pl.pallas_call(
        body, name="merge_bwd",
        out_shape=(jax.ShapeDtypeStruct((S, D_MODEL), BF16), jax.ShapeDtypeStruct((S, D_MODEL), BF16),
                   jax.ShapeDtypeStruct((S, 2 * D_MODEL), BF16)),
        grid=(S // rows,), in_specs=part(OFF_MA) + part(OFF_MB) + [row, row, row],
        out_specs=(row, row, pl.BlockSpec((rows, 2 * D_MODEL), lambda i: (i, 0))),
        compiler_params=_params("parallel"),
    )(*([proj] * (2 * nj)), y_a, y_b, dmerged)


def _loss_head(x2, target, fg, rows=128):
    S, D = x2.shape
    rows = min(rows, S)

    def body(x_ref, t_ref, g_ref, dx_ref, dxb_ref, dg_ref, loss_ref):
        i = pl.program_id(0)

        @pl.when(i == 0)
        def _():
            dg_ref[...] = jnp.zeros_like(dg_ref)
            loss_ref[...] = jnp.zeros_like(loss_ref)

        xv, g = x_ref[...], g_ref[...]
        rstd = lax.rsqrt(jnp.mean(xv * xv, axis=-1, keepdims=True) + NORM_EPS)
        xhat = xv * rstd
        err = xhat * g - t_ref[...]
        loss_ref[...] += (0.5 / D) * jnp.sum(err * err)
        dy = err * (1.0 / D)
        dg_ref[...] += jnp.sum(dy * xhat, axis=0, keepdims=True)
        dxh = dy * g
        dx = rstd * (dxh - xhat * jnp.mean(dxh * xhat, axis=-1, keepdims=True))
        dx_ref[...] = dx
        dxb_ref[...] = dx.astype(BF16)

    row = pl.BlockSpec((rows, D), lambda i: (i, 0))
    vec = pl.BlockSpec((1, D), lambda i: (0, 0))
    return pl.pallas_call(
        body, name="loss_head",
        out_shape=(jax.ShapeDtypeStruct((S, D), F32), jax.ShapeDtypeStruct((S, D), BF16),
                   jax.ShapeDtypeStruct((1, D), F32), jax.ShapeDtypeStruct((8, LANES), F32)),
        grid=(S // rows,), in_specs=[row, row, vec],
        out_specs=(row, row, vec, pl.BlockSpec((8, LANES), lambda i: (0, 0))),
        compiler_params=_params("arbitrary"),
    )(x2, target, fg)


def _rms_bwd(x, dh, dx2, g, comm=None, rows=128):
    S, D = x.shape
    rows = min(rows, S)
    nsteps = S // rows
    c_arrays, c_shapes, c_sems, c_build, c_alias = _comm_fields(comm)
    n_cin, n_cout = len(c_arrays), len(c_shapes)

    def body(*refs):
        x_ref, dh_ref, dx2_ref, g_ref = refs[:4]
        cin = refs[4:4 + n_cin]
        gx_ref, dg_ref = refs[4 + n_cin], refs[5 + n_cin]
        cout, sems = refs[6 + n_cin:6 + n_cin + n_cout], refs[6 + n_cin + n_cout:]
        i = pl.program_id(0)
        if comm is not None:
            _start_when(i == 0, c_build(cin, cout, sems))

        @pl.when(i == 0)
        def _():
            dg_ref[...] = jnp.zeros_like(dg_ref)

        xv, dh_v = x_ref[...], dh_ref[...]
        rstd = lax.rsqrt(jnp.mean(xv * xv, axis=-1, keepdims=True) + NORM_EPS)
        xhat = xv * rstd
        dg_ref[...] += jnp.sum(dh_v * xhat, axis=0, keepdims=True)
        dxh = dh_v * g_ref[...]
        gx_ref[...] = dx2_ref[...] + rstd * (dxh - xhat * jnp.mean(dxh * xhat, axis=-1, keepdims=True))
        if comm is not None:
            _wait_when(i == nsteps - 1, c_build(cin, cout, sems))

    row = pl.BlockSpec((rows, D), lambda i: (i, 0))
    vec = pl.BlockSpec((1, D), lambda i: (0, 0))
    aliases = {4 + t: 2 + t for t in range(n_cin)} if c_alias else {}
    return pl.pallas_call(
        body, name="rms_bwd",
        out_shape=(jax.ShapeDtypeStruct((S, D), F32), jax.ShapeDtypeStruct((1, D), F32)) + tuple(c_shapes),
        grid=(nsteps,), in_specs=[row, row, row, vec] + [HBM_SPEC] * n_cin, out_specs=(row, vec) + (HBM_SPEC,) * n_cout,
        scratch_shapes=_comm_scratch(c_sems), input_output_aliases=aliases, compiler_params=_params("arbitrary"),
    )(x, dh, dx2, g, *c_arrays)


def _adamw(w, g, m, v, name, rows=64):
    R, C = w.shape
    rows = min(rows, R)
    c1 = 1.0 - ADAM_B1 ** ADAM_STEP
    c2 = 1.0 - ADAM_B2 ** ADAM_STEP

    def body(w_ref, g_ref, m_ref, v_ref, go_ref, d_ref, mo_ref, vo_ref):
        gv = g_ref[...]
        mn = ADAM_B1 * m_ref[...] + (1.0 - ADAM_B1) * gv
        vn = ADAM_B2 * v_ref[...] + (1.0 - ADAM_B2) * (gv * gv)
        go_ref[...] = gv
        mo_ref[...] = mn
        vo_ref[...] = vn
        d_ref[...] = -ADAM_LR * ((mn / c1) / (jnp.sqrt(vn / c2) + ADAM_EPS) + ADAM_WD * w_ref[...])

    spec = pl.BlockSpec((rows, C), lambda i: (i, 0))
    shape = jax.ShapeDtypeStruct((R, C), F32)
    return pl.pallas_call(body, name=name, out_shape=(shape,) * 4, grid=(R // rows,), in_specs=[spec] * 4,
                          out_specs=(spec,) * 4, compiler_params=_params("parallel"))(w, g, m, v)


def _place():
    x, y, c = lax.axis_index("x"), lax.axis_index("y"), lax.axis_index("c")
    others = [(1 - x, y), (x, 1 - y), (1 - x, 1 - y)]
    return x, y, c, others


def _chip_index(chip):
    return 2 * chip[0] + chip[1]


def _remote(src, dst, sems, k, to):
    send_sems, recv_sems = sems
    return pltpu.make_async_remote_copy(src_ref=src, dst_ref=dst, send_sem=send_sems.at[k], recv_sem=recv_sems.at[k],
                                        device_id=to, device_id_type=MESH)


def _comm_fields(comm):
    return comm if comm is not None else ((), (), 0, None, False)


def _comm_scratch(n_sems):
    return [pltpu.SemaphoreType.DMA((n_sems,)), pltpu.SemaphoreType.DMA((n_sems,))] if n_sems else []


def _start_when(cond, copies):
    @pl.when(cond)
    def _():
        for cp in copies:
            cp.start()


def _wait_when(cond, copies):
    @pl.when(cond)
    def _():
        for cp in copies:
            cp.wait()


def _proj_gather(tile_cols, h, w_full, comm=None, tm=1024, tk=512):
    c_arrays, c_shapes, c_sems, c_build, c_alias = _comm_fields(comm)
    n_cin, n_cout = len(c_arrays), len(c_shapes)
    S, Dm = h.shape
    tm, tk = min(tm, S), min(tk, Dm)
    tn = SLAB // 2
    ni, nj, nk = S // tm, 2 * N_CHIPS, Dm // tk
    total = nj * ni * nk
    half_in = Dm // 2

    def body(*refs):
        cols_ref, h_ref = refs[0], refs[1]
        cin = refs[3:3 + n_cin]
        proj_ref, fi_ref = refs[3 + n_cin], refs[4 + n_cin]
        cout = refs[5 + n_cin:5 + n_cin + n_cout]
        acc, bbuf, bsem, send_sems, recv_sems = refs[5 + n_cin + n_cout:10 + n_cin + n_cout]
        c_sem_refs = refs[10 + n_cin + n_cout:]
        j, i, k = pl.program_id(0), pl.program_id(1), pl.program_id(2)
        t = (j * ni + i) * nk + k
        x, y, c, others = _place()
        me_chip = 2 * x + y
        sems = (send_sems, recv_sems)

        def piece(chip, half, tile):
            return fi_ref.at[pl.ds(half * half_in, half_in), pl.ds(chip * SLAB + tile * tn, tn)]

        def ici_send(r, tile):
            mine = piece(me_chip, c, tile)
            return _remote(mine, mine, sems, 2 * r + tile, (*others[r], c))

        def forward(tile):
            src_chip = (x + (1 - c) * (1 - 2 * x), y + c * (1 - 2 * y))
            dst_chip = (x + c * (1 - 2 * x), y + (1 - c) * (1 - 2 * y))
            got = piece(_chip_index(src_chip), c, tile)
            return _remote(got, got, sems, 4 + tile, (*dst_chip, c))

        def ici_recv(r, tile):
            got = piece(_chip_index(others[r]), c, tile)
            return _remote(got, got, sems, 2 * r + tile, (x, y, c))

        def pass_on(r, tile):
            got = piece(_chip_index(others[r]), c, tile)
            return _remote(got, got, sems, 6 + 2 * r + tile, (x, y, 1 - c))

        def passed_recv(r, tile):
            got = piece(_chip_index(others[r]), 1 - c, tile)
            return _remote(got, got, sems, 6 + 2 * r + tile, (x, y, c))

        def b_copy(jj, kk, slot):
            return pltpu.make_async_copy(fi_ref.at[pl.ds(kk * tk, tk), pl.ds(cols_ref[jj] * tn, tn)], bbuf.at[slot],
                                         bsem.at[slot])

        @pl.when(t == 0)
        def _():
            for tile in range(2):
                ici_send(0, tile).start()
                ici_send(1, tile).start()
            b_copy(0, 0, 0).start()
            if comm is not None:
                for cp in c_build(cin, cout, c_sem_refs):
                    cp.start()

        nxt = t + 1
        kn, i_n, jn = nxt % nk, (nxt // nk) % ni, nxt // (nk * ni)

        def opens(jj):
            return (jn == jj) & (i_n == 0) & (kn == 0)

        @pl.when(nxt < total)
        def _():
            for tile in range(2):
                @pl.when(opens(2 + 2 * tile))
                def _():
                    ici_recv(0, tile).wait_recv()
                    ici_recv(1, tile).wait_recv()
                    forward(tile).start()
                    pass_on(0, tile).start()
                    pass_on(1, tile).start()
                    passed_recv(0, tile).wait_recv()

                @pl.when(opens(3 + 2 * tile))
                def _():
                    passed_recv(1, tile).wait_recv()

                @pl.when(opens(6 + tile))
                def _():
                    ici_recv(2, tile).wait_recv()
                    pass_on(2, tile).start()
                    passed_recv(2, tile).wait_recv()

            b_copy(jn, kn, nxt % 2).start()

        b_copy(j, k, t % 2).wait()

        def product():
            return jnp.dot(h_ref[...], bbuf[t % 2], preferred_element_type=F32)

        @pl.when(k == 0)
        def _():
            acc[...] = product()

        @pl.when((k > 0) & (k < nk - 1))
        def _():
            acc[...] += product()

        @pl.when(k == nk - 1)
        def _():
            proj_ref[...] = (acc[...] + product()).astype(BF16)

        @pl.when(t == total - 1)
        def _():
            for tile in range(2):
                ici_send(0, tile).wait_send()
                ici_send(1, tile).wait_send()
                forward(tile).wait_send()
                for r in range(3):
                    pass_on(r, tile).wait_send()
            if comm is not None:
                for cp in c_build(cin, cout, c_sem_refs):
                    cp.wait()

    grid_spec = pltpu.PrefetchScalarGridSpec(
        num_scalar_prefetch=1, grid=(nj, ni, nk),
        in_specs=[pl.BlockSpec((tm, tk), lambda j, i, k, cols: (i, k)), HBM_SPEC] + [HBM_SPEC] * n_cin,
        out_specs=(pl.BlockSpec((tm, tn), lambda j, i, k, cols: (i, cols[j])), HBM_SPEC) + (HBM_SPEC,) * n_cout,
        scratch_shapes=[pltpu.VMEM((tm, tn), F32), pltpu.VMEM((2, tk, tn), BF16), pltpu.SemaphoreType.DMA((2,)),
                        pltpu.SemaphoreType.DMA((12,)), pltpu.SemaphoreType.DMA((12,))] + _comm_scratch(c_sems))
    aliases = {2: 1}
    if c_alias:
        aliases.update({3 + t: 2 + t for t in range(n_cin)})
    return pl.pallas_call(
        body, name="proj_gather",
        out_shape=(jax.ShapeDtypeStruct((S, PROJ_WIDTH), BF16), jax.ShapeDtypeStruct(w_full.shape, BF16)) + tuple(c_shapes),
        grid_spec=grid_spec, input_output_aliases=aliases,
        compiler_params=_params("arbitrary", "arbitrary", "arbitrary"),
    )(tile_cols, h, w_full, *c_arrays)


def _exchange(comm, name):
    arrays, shapes, n_sems, build, aliased = comm
    n_in, n_out = len(arrays), len(shapes)

    def body(*refs):
        copies = build(refs[:n_in], refs[n_in:n_in + n_out], refs[n_in + n_out:])
        for cp in copies:
            cp.start()
        for cp in copies:
            cp.wait()

    return pl.pallas_call(
        body, name=name, out_shape=tuple(shapes), in_specs=[HBM_SPEC] * n_in, out_specs=(HBM_SPEC,) * n_out,
        scratch_shapes=_comm_scratch(n_sems), input_output_aliases={t: t for t in range(n_in)} if aliased else {},
    )(*arrays)


def _pair_halves(parts):
    def half_of(ref, which):
        if len(ref.shape) == 2:
            rows = ref.shape[0] // 2
            return ref.at[pl.ds(which * rows, rows), :]
        return ref.at[:, pl.ds(which, 1)]

    def out_shape(p):
        if p.ndim == 2:
            return jax.ShapeDtypeStruct((p.shape[0] // 2, p.shape[1]), BF16)
        return jax.ShapeDtypeStruct((p.shape[0], 1) + p.shape[2:], BF16)

    def build(p_refs, r_refs, sems):
        x, y, c, _ = _place()
        return [_remote(half_of(p, 1 - c), r, sems, t, (x, y, 1 - c)) for t, (p, r) in enumerate(zip(p_refs, r_refs))]

    return (tuple(parts), tuple(out_shape(p) for p in parts), len(parts), build, False)


def _gather_squares(items):
    half_up = UP_ROWS // 2
    fulls = [f for f, _ in items]

    def build(_, f_refs, sems):
        x, y, c, others = _place()
        copies = []
        for t, (f, (_, over_ici)) in enumerate(zip(f_refs, items)):
            for r, chip in enumerate(others):
                src_chip = 2 * x + y if over_ici else _chip_index(chip)
                rows = f.at[pl.ds(src_chip * UP_ROWS + c * half_up, half_up), :]
                copies.append(_remote(rows, rows, sems, 3 * t + r, (*chip, c) if over_ici else (x, y, 1 - c)))
        return copies

    return (tuple(fulls), tuple(jax.ShapeDtypeStruct(f.shape, f.dtype) for f in fulls), 3 * len(fulls), build, True)


def _send_whole(parts):
    def build(p_refs, r_refs, sems):
        x, y, c, _ = _place()
        return [_remote(p, r, sems, t, (x, y, 1 - c)) for t, (p, r) in enumerate(zip(p_refs, r_refs))]

    return (tuple(parts), tuple(jax.ShapeDtypeStruct(p.shape, p.dtype) for p in parts), len(parts), build, False)


def _pair_add_in(p_own, r_in, rows=256, cols=SLAB):
    half, P = p_own.shape
    rows = min(rows, half)

    def body(p_ref, r_ref, o_ref):
        o_ref[...] = (p_ref[...].astype(F32) + r_ref[...].astype(F32)).astype(BF16)

    spec = pl.BlockSpec((rows, cols), lambda i, j: (i, j))
    return pl.pallas_call(body, name="pair_add_in", out_shape=jax.ShapeDtypeStruct((half, P), BF16),
                          grid=(half // rows, P // cols), in_specs=[spec, spec], out_specs=spec,
                          compiler_params=_params("parallel", "parallel"))(p_own, r_in)


def _pair_add_up(idx, p_up, r_up, name):
    _, _, R, C = p_up.shape

    def body(idx_ref, p_ref, r_ref, o_ref):
        o_ref[...] = (p_ref[...].astype(F32) + r_ref[...].astype(F32)).astype(BF16)

    grid_spec = pltpu.PrefetchScalarGridSpec(
        num_scalar_prefetch=1, grid=(N_CHIPS,),
        in_specs=[pl.BlockSpec((1, 1, R, C), lambda j, idx: (j, idx[0], 0, 0)),
                  pl.BlockSpec((1, 1, R, C), lambda j, idx: (j, 0, 0, 0))],
        out_specs=pl.BlockSpec((1, 1, R, C), lambda j, idx: (j, 0, 0, 0)))
    return pl.pallas_call(body, name=name, out_shape=jax.ShapeDtypeStruct((N_CHIPS, 1, R, C), BF16), grid_spec=grid_spec,
                          compiler_params=_params("parallel"))(idx, p_up, r_up)


def _slab_exchange(qs):
    n = len(qs)

    def out_shape(q):
        if q.ndim == 2:
            return jax.ShapeDtypeStruct((3, q.shape[0], SLAB), BF16)
        return jax.ShapeDtypeStruct((3,) + q.shape[1:], BF16)

    def build(q_refs, r_refs, sems):
        _, _, c, others = _place()
        copies = []
        for r, chip in enumerate(others):
            ci = _chip_index(chip)
            for t in range(n):
                src = q_refs[t].at[:, pl.ds(ci * SLAB, SLAB)] if len(q_refs[t].shape) == 2 else q_refs[t].at[ci]
                copies.append(_remote(src, r_refs[t].at[r], sems, 3 * t + r, (*chip, c)))
        return copies

    return (tuple(qs), tuple(out_shape(q) for q in qs), 3 * n, build, False)


def _slab_add_in(idx, q_in, r2_in, rows=128):
    half = q_in.shape[0]
    rows = min(rows, half)
    nrb = half // rows

    def body(idx_ref, q_ref, r_ref, o_ref):
        o_ref[...] = ((q_ref[...].astype(F32) + r_ref[0].astype(F32)) + r_ref[1].astype(F32)) + r_ref[2].astype(F32)

    grid_spec = pltpu.PrefetchScalarGridSpec(
        num_scalar_prefetch=1, grid=(nrb,),
        in_specs=[pl.BlockSpec((rows, SLAB), lambda i, idx: (i, idx[1])),
                  pl.BlockSpec((3, rows, SLAB), lambda i, idx: (0, i, 0))],
        out_specs=pl.BlockSpec((rows, SLAB), lambda i, idx: (idx[0] * nrb + i, 0)))
    return pl.pallas_call(body, name="slab_add_in", out_shape=jax.ShapeDtypeStruct((2 * half, SLAB), F32),
                          grid_spec=grid_spec, compiler_params=_params("parallel"))(idx, q_in, r2_in)


def _slab_add_up(idx, q_up, r2_up, name, rows=128):
    _, _, R, C = q_up.shape
    rows = min(rows, R)
    nrb = R // rows

    def body(idx_ref, q_ref, r_ref, o_ref):
        o_ref[...] = ((q_ref[0, 0].astype(F32) + r_ref[0, 0].astype(F32)) + r_ref[1, 0].astype(F32)) + r_ref[2, 0].astype(F32)

    grid_spec = pltpu.PrefetchScalarGridSpec(
        num_scalar_prefetch=1, grid=(nrb,),
        in_specs=[pl.BlockSpec((1, 1, rows, C), lambda i, idx: (idx[1], 0, i, 0)),
                  pl.BlockSpec((3, 1, rows, C), lambda i, idx: (0, 0, i, 0))],
        out_specs=pl.BlockSpec((rows, C), lambda i, idx: (idx[0] * nrb + i, 0)))
    return pl.pallas_call(body, name=name, out_shape=jax.ShapeDtypeStruct((2 * R, C), F32), grid_spec=grid_spec,
                          compiler_params=_params("parallel"))(idx, q_up, r2_up)


def _pair_share(gs):
    def build(_, g_refs, sems):
        x, y, c, _ = _place()
        copies = []
        for t, g in enumerate(g_refs):
            rows = g.shape[0] // 2
            mine = g.at[pl.ds(c * rows, rows), :]
            copies.append(_remote(mine, mine, sems, t, (x, y, 1 - c)))
        return copies

    return (tuple(gs), tuple(jax.ShapeDtypeStruct(g.shape, g.dtype) for g in gs), len(gs), build, True)


def _all_reduce_small(packed):
    R, C = packed.shape
    N_DEV = 2 * N_CHIPS

    def body(x_ref, out_ref, all_ref, send_sems, recv_sems, local_sem):
        x, y, c, others = _place()
        me, sib = (x, y, c), (x, y, 1 - c)
        sems = (send_sems, recv_sems)

        def rows(px, py, pc):
            return all_ref.at[4 * px + 2 * py + pc]

        mine = pltpu.make_async_copy(x_ref, rows(*me), local_sem)
        mine.start()
        first = [_remote(x_ref, rows(*me), sems, 0, sib)]
        first += [_remote(x_ref, rows(*me), sems, 1 + j, (*chip, c)) for j, chip in enumerate(others)]
        for cp in first:
            cp.start()
        passed = [_remote(rows(*chip, c), rows(*chip, c), sems, 4 + j, sib) for j, chip in enumerate(others)]
        for j, chip in enumerate(others):
            _remote(rows(*chip, c), rows(*chip, c), sems, 1 + j, me).wait_recv()
            passed[j].start()
        _remote(rows(*sib), rows(*sib), sems, 0, me).wait_recv()
        for j, chip in enumerate(others):
            _remote(rows(*chip, 1 - c), rows(*chip, 1 - c), sems, 4 + j, me).wait_recv()
        for cp in first + passed:
            cp.wait_send()
        mine.wait()
        total = all_ref[0]
        for d in range(1, N_DEV):
            total = total + all_ref[d]
        out_ref[...] = total

    vmem = pl.BlockSpec(memory_space=pltpu.VMEM)
    return pl.pallas_call(
        body, name="all_reduce_small", out_shape=jax.ShapeDtypeStruct((R, C), F32),
        in_specs=[vmem], out_specs=vmem,
        scratch_shapes=[pltpu.VMEM((N_DEV, R, C), F32), pltpu.SemaphoreType.DMA((7,)), pltpu.SemaphoreType.DMA((7,)),
                        pltpu.SemaphoreType.DMA],
        compiler_params=pltpu.CompilerParams(vmem_limit_bytes=VMEM_LIMIT),
    )(packed)


SMALL_NAMES = ("w_spatial", "b_spatial", "norm_g", "gmlp_ln_g", "gmlp_ln_b", "final_norm_g", "attn_sink")


def _pack_small(parts, extra=None):
    blocks = []
    for n in SMALL_NAMES:
        flat = parts[n].reshape(-1).astype(F32)
        rows = -(-flat.shape[0] // (8 * LANES)) * 8
        flat = jnp.pad(flat, (0, rows * LANES - flat.shape[0]))
        blocks.append(flat.reshape(rows, LANES))
    blocks.append(jnp.zeros((8, LANES), F32) if extra is None else extra)
    return jnp.concatenate(blocks, axis=0)


def _unpack_small(packed, shapes):
    out, r = {}, 0
    for n in SMALL_NAMES:
        size = 1
        for s in shapes[n]:
            size *= s
        rows = -(-size // (8 * LANES)) * 8
        out[n] = packed[r:r + rows].reshape(-1)[:size].reshape(shapes[n])
        r += rows
    return out, packed[r:r + 8]


def kernel(x, positions, norm_g, w_in, attn_sink, gmlp_ln_g, gmlp_ln_b, w_spatial, b_spatial, w_up_attn, w_up_gmlp, w_out, final_norm_g, loss_target, m_norm_g, m_w_in, m_attn_sink, m_gmlp_ln_g, m_gmlp_ln_b, m_w_spatial, m_b_spatial, m_w_up_attn, m_w_up_gmlp, m_w_out, m_final_norm_g, v_norm_g, v_w_in, v_attn_sink, v_gmlp_ln_g, v_gmlp_ln_b, v_w_spatial, v_b_spatial, v_w_up_attn, v_w_up_gmlp, v_w_out, v_final_norm_g):
    xs, tgt = x[0], loss_target[0]
    mx, my, mc = lax.axis_index("x"), lax.axis_index("y"), lax.axis_index("c")
    idx = jnp.stack([mc, 2 * mx + my]).astype(jnp.int32)
    own, x_nbr, y_nbr, diag = 2 * mx + my, 2 * (1 - mx) + my, 2 * mx + (1 - my), 2 * (1 - mx) + (1 - my)
    tile_cols = jnp.stack([2 * own, 2 * own + 1, 2 * x_nbr, 2 * y_nbr, 2 * x_nbr + 1, 2 * y_nbr + 1, 2 * diag,
                           2 * diag + 1]).astype(jnp.int32)

    square = (D_MODEL, D_MODEL)
    w_full = _cast_into(idx, w_in[0], (D_MODEL, PROJ_WIDTH), "cast_w_in")
    ups = [_cast_into(idx, w_up_attn[0], square, "cast_w_up_attn"), _cast_into(idx, w_up_gmlp[0], square, "cast_w_up_gmlp"),
           _cast_into(idx, w_out[0], square, "cast_w_out")]

    tables = _rope_tables(positions[0])
    bias = _band_bias()
    sink = attn_sink[0]
    ws, bs_t = w_spatial[0], b_spatial[0].T
    h = _rms_fwd(xs, norm_g)
    proj, w_full, wua = _proj_gather(tile_cols, h, w_full, comm=_gather_squares([(ups[0], True)]))
    attn, a_in, wua, wug = _attn_fwd(proj, tables, bias, sink, comm=_gather_squares([(wua, False), (ups[1], True)]))
    b_in, wug = _gmlp_fwd(proj, ws, bs_t, gmlp_ln_g, gmlp_ln_b, comm=_gather_squares([(wug, False)]))
    y_a, wo = _matmul(a_in, wua, mode="nn", out_dtype=F32, name="up_attn", comm=_gather_squares([(ups[2], True)]))
    y_b, wo = _matmul(b_in, wug, mode="nn", out_dtype=F32, name="up_gmlp", comm=_gather_squares([(wo, False)]))
    merged = _merge_fwd(proj, y_a, y_b)
    x2 = _matmul(merged, wo, mode="nn", out_dtype=F32, name="out_proj", residual=xs, tk=2048)
    dx2, dx2_b, d_fg, loss_part = _loss_head(x2, tgt, final_norm_g.reshape(1, D_MODEL))

    dmerged = _matmul(dx2_b, wo, mode="nt", out_dtype=F32, name="d_merged")
    dy_a, dy_b, d_gates = _merge_bwd(proj, y_a, y_b, dmerged)

    p_ups = [_matmul(a_in, dy_a, mode="tn", out_dtype=BF16, name="dw_up_attn"),
             _matmul(b_in, dy_b, mode="tn", out_dtype=BF16, name="dw_up_gmlp"),
             _matmul(merged, dx2_b, mode="tn", out_dtype=BF16, name="dw_out")]
    p_ups = [p.reshape(N_CHIPS, 2, UP_ROWS // 2, D_MODEL) for p in p_ups]
    da_in, *r1_ups = _matmul(dy_a, wua, mode="nt", out_dtype=F32, name="d_a_in", comm=_pair_halves(p_ups))
    q_ups = [_pair_add_up(idx, p, r, "pair_add_up%d" % t) for t, (p, r) in enumerate(zip(p_ups, r1_ups))]
    db_in = _matmul(dy_b, wug, mode="nt", out_dtype=F32, name="d_b_in")
    dq, dk, dv, dga, d_sink = _attn_bwd(proj, tables, bias, sink, attn, da_in)
    d_gmlp, d_ws, d_bs, d_lg, d_lb = _gmlp_bwd(proj, ws, bs_t, gmlp_ln_g, gmlp_ln_b, db_in)
    dproj = jnp.concatenate([dq, dk, dv, dga, d_gmlp, d_gates], axis=1)

    half = D_MODEL // 2
    h_sib = lax.dynamic_slice(h, (0, (1 - mc) * half), (h.shape[0], half))
    h_own = lax.dynamic_slice(h, (0, mc * half), (h.shape[0], half))
    p_sib, *r2_ups = _matmul(h_sib, dproj, mode="tn", out_dtype=BF16, name="dw_in_sibling", comm=_slab_exchange(q_ups))
    p_own, r1_in = _matmul(h_own, dproj, mode="tn", out_dtype=BF16, name="dw_in_own", comm=_send_whole([p_sib]))
    q_in = _pair_add_in(p_own, r1_in)
    dh, r2_in = _matmul(dproj, w_full, mode="nt", out_dtype=F32, name="d_h", comm=_slab_exchange([q_in]), tk=SLAB // 2)

    g_half = [_slab_add_in(idx, q_in, r2_in)] + [_slab_add_up(idx, q, r, "slab_add_up%d" % t)
                                                 for t, (q, r) in enumerate(zip(q_ups, r2_ups))]
    grad_x, d_ng, *g_big = _rms_bwd(xs, dh, dx2, norm_g, comm=_pair_share(g_half))

    small_shapes = {"w_spatial": w_spatial.shape, "b_spatial": b_spatial.shape, "norm_g": norm_g.shape,
                    "gmlp_ln_g": gmlp_ln_g.shape, "gmlp_ln_b": gmlp_ln_b.shape, "final_norm_g": final_norm_g.shape,
                    "attn_sink": attn_sink.shape}
    d_small = {"w_spatial": d_ws, "b_spatial": d_bs, "norm_g": d_ng, "gmlp_ln_g": d_lg, "gmlp_ln_b": d_lb,
               "final_norm_g": d_fg, "attn_sink": d_sink[:, 0, :HEADS_PER_STEP]}
    g_small = _all_reduce_small(_pack_small(d_small, loss_part))

    w_small = _pack_small(dict(w_spatial=w_spatial, b_spatial=b_spatial, norm_g=norm_g, gmlp_ln_g=gmlp_ln_g,
                               gmlp_ln_b=gmlp_ln_b, final_norm_g=final_norm_g, attn_sink=attn_sink))
    m_small = _pack_small(dict(w_spatial=m_w_spatial, b_spatial=m_b_spatial, norm_g=m_norm_g, gmlp_ln_g=m_gmlp_ln_g,
                               gmlp_ln_b=m_gmlp_ln_b, final_norm_g=m_final_norm_g, attn_sink=m_attn_sink))
    v_small = _pack_small(dict(w_spatial=v_w_spatial, b_spatial=v_b_spatial, norm_g=v_norm_g, gmlp_ln_g=v_gmlp_ln_g,
                               gmlp_ln_b=v_gmlp_ln_b, final_norm_g=v_final_norm_g, attn_sink=v_attn_sink))
    small_out = _adamw(w_small, g_small, m_small, v_small, "adamw_small", rows=w_small.shape[0])
    big = {
        "w_in": _adamw(w_in[0], g_big[0], m_w_in[0], v_w_in[0], "adamw_w_in"),
        "w_up_attn": _adamw(w_up_attn[0], g_big[1], m_w_up_attn[0], v_w_up_attn[0], "adamw_w_up_attn"),
        "w_up_gmlp": _adamw(w_up_gmlp[0], g_big[2], m_w_up_gmlp[0], v_w_up_gmlp[0], "adamw_w_up_gmlp"),
        "w_out": _adamw(w_out[0], g_big[3], m_w_out[0], v_w_out[0], "adamw_w_out"),
    }

    order = ("norm_g", "w_in", "attn_sink", "gmlp_ln_g", "gmlp_ln_b", "w_spatial", "b_spatial", "w_up_attn", "w_up_gmlp",
             "w_out", "final_norm_g")
    outs = []
    loss = None
    for kind in range(4):
        small, extra = _unpack_small(small_out[kind], small_shapes)
        if kind == 0:
            loss = extra[0, 0]
        for n in order:
            outs.append(big[n][kind][None] if n in big else small[n])
    return (loss, grad_x[None], *outs)
```

```python
import functools

import jax
import jax.numpy as jnp
from jax import lax
from jax.experimental import pallas as pl
from jax.experimental.pallas import tpu as pltpu

F32 = jnp.float32
BF16 = jnp.bfloat16

D_MODEL = 4096
N_Q_HEADS = 64
N_KV_HEADS = 8
HEAD_DIM = 64
Q_PER_KV = N_Q_HEADS // N_KV_HEADS
ATTN_WIDTH = N_Q_HEADS * HEAD_DIM
KV_WIDTH = N_KV_HEADS * HEAD_DIM
WINDOW = 128
BLOCK = 128
ROPE_THETA = 500000.0
ROPE_DIM = HEAD_DIM // 4
ROPE_HALF = ROPE_DIM // 2
GMLP_WIDTH = D_MODEL
GMLP_GROUPS = 8
GMLP_GROUP_DIM = GMLP_WIDTH // GMLP_GROUPS
GMLP_CHUNK = 128
NORM_EPS = 1e-5
LN_EPS = 1e-5

PROJ_SIZES = (ATTN_WIDTH, KV_WIDTH, KV_WIDTH, ATTN_WIDTH, GMLP_WIDTH, GMLP_WIDTH, GMLP_WIDTH, D_MODEL, D_MODEL)
PROJ_WIDTH = sum(PROJ_SIZES)
OFF_Q, OFF_K, OFF_V, OFF_GA, OFF_U, OFF_VG, OFF_GB, OFF_MA, OFF_MB = (
    sum(PROJ_SIZES[:i]) for i in range(len(PROJ_SIZES)))

ADAM_LR = 0.001
ADAM_B1 = 0.9
ADAM_B2 = 0.999
ADAM_EPS = 1e-08
ADAM_WD = 0.01
ADAM_STEP = 10

N_CHIPS = 4
SLAB = PROJ_WIDTH // N_CHIPS
UP_ROWS = D_MODEL // N_CHIPS
LANES = 128
COL_BLK = 1024
HEADS_PER_STEP = 2 * Q_PER_KV
VMEM_LIMIT = 56 * 1024 * 1024

MESH = pl.DeviceIdType.MESH
HBM_SPEC = pl.BlockSpec(memory_space=pltpu.HBM)
SMEM_SPEC = pl.BlockSpec(memory_space=pltpu.SMEM)


def _params(*sem):
    return pltpu.CompilerParams(dimension_semantics=sem, vmem_limit_bytes=VMEM_LIMIT)


def _sigmoid(x):
    return jax.nn.sigmoid(x)


def _gelu(x):
    return jax.nn.gelu(x, approximate=True)


def _gelu_both(x):
    c = 0.7978845608028654
    x2 = x * x
    t = jnp.tanh(c * (x + 0.044715 * x * x2))
    cdf = 0.5 * (1.0 + t)
    return x * cdf, cdf + 0.5 * x * (1.0 - t * t) * c * (1.0 + 3 * 0.044715 * x2)


def _matmul(a, b, *, mode, out_dtype, name, residual=None, comm=None, tm=1024, tn=1024, tk=4096):
    if mode == "nn":
        (M, K), N = a.shape, b.shape[1]
    elif mode == "nt":
        (M, K), N = a.shape, b.shape[0]
    else:
        (K, M), N = a.shape, b.shape[1]
    tm, tn, tk = min(tm, M), min(tn, N), min(tk, K)
    assert M % tm == 0 and N % tn == 0 and K % tk == 0
    ni, nj, nk = M // tm, N // tn, K // tk
    if mode == "nn":
        a_spec = pl.BlockSpec((tm, tk), lambda i, j, k: (i, k))
        b_spec = pl.BlockSpec((tk, tn), lambda i, j, k: (k, j))
        dims = (((1,), (0,)), ((), ()))
    elif mode == "nt":
        a_spec = pl.BlockSpec((tm, tk), lambda i, j, k: (i, k))
        b_spec = pl.BlockSpec((tn, tk), lambda i, j, k: (j, k))
        dims = (((1,), (1,)), ((), ()))
    else:
        a_spec = pl.BlockSpec((tk, tm), lambda i, j, k: (k, i))
        b_spec = pl.BlockSpec((tk, tn), lambda i, j, k: (k, j))
        dims = (((0,), (0,)), ((), ()))
    o_spec = pl.BlockSpec((tm, tn), lambda i, j, k: (i, j))
    n_res = 1 if residual is not None else 0
    c_arrays, c_shapes, c_sems, c_build, c_alias = _comm_fields(comm)
    n_cin, n_cout = len(c_arrays), len(c_shapes)

    def body(*refs):
        a_ref, b_ref = refs[0], refs[1]
        r_ref = refs[2] if n_res else None
        cin = refs[2 + n_res:2 + n_res + n_cin]
        o_ref = refs[2 + n_res + n_cin]
        cout = refs[3 + n_res + n_cin:3 + n_res + n_cin + n_cout]
        n_acc = 1 if nk > 1 else 0
        acc = refs[3 + n_res + n_cin + n_cout] if n_acc else None
        sems = refs[3 + n_acc + n_res + n_cin + n_cout:]
        i, j, k = pl.program_id(0), pl.program_id(1), pl.program_id(2)
        if comm is not None:
            _start_when((i == 0) & (j == 0) & (k == 0), c_build(cin, cout, sems))

        def product():
            return lax.dot_general(a_ref[...], b_ref[...], dims, preferred_element_type=F32)

        def finish(r):
            if n_res:
                r = r + r_ref[...]
            o_ref[...] = r.astype(out_dtype)

        if nk == 1:
            finish(product())
        else:
            @pl.when(k == 0)
            def _():
                acc[...] = product()

            @pl.when((k > 0) & (k < nk - 1))
            def _():
                acc[...] += product()

            @pl.when(k == nk - 1)
            def _():
                finish(acc[...] + product())

        if comm is not None:
            _wait_when((i == ni - 1) & (j == nj - 1) & (k == nk - 1), c_build(cin, cout, sems))

    args = (a, b) + ((residual,) if n_res else ()) + tuple(c_arrays)
    in_specs = [a_spec, b_spec] + ([o_spec] if n_res else []) + [HBM_SPEC] * n_cin
    scratch = ([pltpu.VMEM((tm, tn), F32)] if nk > 1 else []) + _comm_scratch(c_sems)
    sem = ("arbitrary",) * 3 if comm is not None else ("parallel", "parallel", "arbitrary")
    aliases = {2 + n_res + t: 1 + t for t in range(n_cin)} if c_alias else {}
    out = pl.pallas_call(
        body, name=name, out_shape=(jax.ShapeDtypeStruct((M, N), out_dtype),) + tuple(c_shapes),
        grid=(ni, nj, nk), in_specs=in_specs, out_specs=(o_spec,) + (HBM_SPEC,) * n_cout,
        scratch_shapes=scratch, input_output_aliases=aliases, compiler_params=_params(*sem),
    )(*args)
    return out if comm is not None else out[0]


def _cast_into(idx, w, full_shape, name, rows=64):
    R, C = w.shape
    rows = min(rows, R)
    nrb = R // rows
    by_cols = C != full_shape[1]

    def body(idx_ref, w_ref, o_ref):
        o_ref[...] = w_ref[...].astype(BF16)

    out_map = (lambda i, idx: (i, idx[1])) if by_cols else (lambda i, idx: (idx[1] * nrb + i, 0))
    grid_spec = pltpu.PrefetchScalarGridSpec(
        num_scalar_prefetch=1, grid=(nrb,), in_specs=[pl.BlockSpec((rows, C), lambda i, idx: (i, 0))],
        out_specs=pl.BlockSpec((rows, C), out_map))
    return pl.pallas_call(body, name=name, out_shape=jax.ShapeDtypeStruct(full_shape, BF16), grid_spec=grid_spec,
                          compiler_params=_params("parallel"))(idx, w)


def _rms_fwd(x, g, rows=256):
    S, D = x.shape
    rows = min(rows, S)

    def body(x_ref, g_ref, h_ref):
        xv = x_ref[...]
        ms = jnp.mean(xv * xv, axis=-1, keepdims=True)
        h_ref[...] = (xv * lax.rsqrt(ms + NORM_EPS) * g_ref[...]).astype(BF16)

    spec = pl.BlockSpec((rows, D), lambda i: (i, 0))
    return pl.pallas_call(body, name="rms_fwd", out_shape=jax.ShapeDtypeStruct((S, D), BF16), grid=(S // rows,),
                          in_specs=[spec, pl.BlockSpec((1, D), lambda i: (0, 0))], out_specs=spec,
                          compiler_params=_params("parallel"))(x, g)


def _rope_tables(positions):
    inv_freq = ROPE_THETA ** (-jnp.arange(ROPE_HALF, dtype=F32) * 2.0 / ROPE_DIM)
    ang = positions.astype(F32)[:, None] * inv_freq
    cos, sin = jnp.cos(ang), jnp.sin(ang)
    S = positions.shape[0]
    rest = HEAD_DIM - ROPE_DIM
    zeros_h, zeros_r = jnp.zeros((S, ROPE_HALF), F32), jnp.zeros((S, rest), F32)
    c = jnp.concatenate([cos, cos, jnp.ones((S, rest), F32)], axis=1)
    s1 = jnp.concatenate([-sin, zeros_h, zeros_r], axis=1)
    s2 = jnp.concatenate([zeros_h, sin, zeros_r], axis=1)
    reps = LANES // HEAD_DIM
    return jnp.tile(c, (1, reps)), jnp.tile(s1, (1, reps)), jnp.tile(s2, (1, reps))


def _rope(t, c, s1, s2, sign):
    if sign != 1.0:
        s1, s2 = sign * s1, sign * s2
    blocks = []
    for b in range(t.shape[1] // LANES):
        blk = t[:, b * LANES:(b + 1) * LANES]
        up = pltpu.roll(blk, LANES - ROPE_HALF, 1)
        down = pltpu.roll(blk, ROPE_HALF, 1)
        blocks.append(blk * c + up * s1 + down * s2)
    return blocks[0] if len(blocks) == 1 else jnp.concatenate(blocks, axis=1)


def _attn_specs(nb):
    last = nb - 1

    def cur(i):
        return jnp.minimum(i, last)

    def prev(i):
        return jnp.maximum(jnp.minimum(i, last) - 1, 0)

    kq, kk, kv, kg = OFF_Q // COL_BLK, OFF_K // LANES, OFF_V // LANES, OFF_GA // COL_BLK
    wide = lambda off: pl.BlockSpec((BLOCK, COL_BLK), lambda p, i: (cur(i), off + p))
    kv_cur = lambda off: pl.BlockSpec((BLOCK, LANES), lambda p, i: (cur(i), off + p))
    kv_prev = lambda off: pl.BlockSpec((BLOCK, LANES), lambda p, i: (prev(i), off + p))
    tab_cur = pl.BlockSpec((BLOCK, LANES), lambda p, i: (cur(i), 0))
    tab_prev = pl.BlockSpec((BLOCK, LANES), lambda p, i: (prev(i), 0))
    proj_specs = [wide(kq), kv_cur(kk), kv_prev(kk), kv_cur(kv), kv_prev(kv), wide(kg)]
    table_specs = [tab_cur] * 3 + [tab_prev] * 3
    bias_spec = pl.BlockSpec((1, BLOCK, QCOLS), lambda p, i: (jnp.minimum(i, 1), 0, 0))
    return proj_specs, table_specs + [bias_spec], cur, prev


PAIRS = Q_PER_KV // 2
QCOLS = PAIRS * BLOCK
NT = (((1,), (1,)), ((), ()))
TN = (((0,), (0,)), ((), ()))


def _band_bias():
    first = jnp.where(jnp.arange(BLOCK)[:, None] > (jnp.arange(QCOLS) % BLOCK)[None, :], -1e30, 0.0)
    return jnp.stack([first, jnp.zeros_like(first)]).astype(F32)


def _from_prev():
    a = lax.broadcasted_iota(jnp.int32, (BLOCK, QCOLS), 1) & (BLOCK - 1)
    return lax.broadcasted_iota(jnp.int32, (BLOCK, QCOLS), 0) > a


def _pair_rows(ref, kvh):
    base = kvh * Q_PER_KV * HEAD_DIM
    return jnp.concatenate([ref[:, base + b * LANES: base + (b + 1) * LANES] for b in range(PAIRS)], axis=0).astype(BF16)


def _unpair_rows(ref, kvh, val_t):
    base = kvh * Q_PER_KV * HEAD_DIM
    for b in range(PAIRS):
        ref[:, base + b * LANES: base + (b + 1) * LANES] = val_t[:, b * BLOCK:(b + 1) * BLOCK].T.astype(ref.dtype)


def _split_heads(t, kvh):
    lane = lax.broadcasted_iota(jnp.int32, t.shape, 1)
    other = pltpu.roll(t, HEAD_DIM, 1)
    lo = jnp.where(lane < HEAD_DIM, t if kvh == 0 else other, 0.0)
    hi = jnp.where(lane >= HEAD_DIM, other if kvh == 0 else t, 0.0)
    return lo.astype(BF16), hi.astype(BF16)


def _join_heads(even, odd, kvh):
    lane = lax.broadcasted_iota(jnp.int32, even.shape, 1)
    lo = jnp.where(lane < HEAD_DIM, even, 0.0) + pltpu.roll(jnp.where(lane >= HEAD_DIM, odd, 0.0), HEAD_DIM, 1)
    return lo if kvh == 0 else pltpu.roll(lo, HEAD_DIM, 1)


def _sink_row(sink_ref, pair, kvh, odd):
    first = (pair * 2 + kvh) * Q_PER_KV + odd
    return jnp.concatenate([jnp.full((1, BLOCK), sink_ref[first + 2 * b], F32) for b in range(PAIRS)], axis=1)


def _band_t(ap, ac, b, from_prev):
    return jnp.where(from_prev, lax.dot_general(ap, b, NT, preferred_element_type=F32),
                     lax.dot_general(ac, b, NT, preferred_element_type=F32))


def _softmax_t(st, sink_row):
    m = jnp.maximum(jnp.max(st, axis=0, keepdims=True), sink_row)
    e = jnp.exp(st - m)
    es = jnp.exp(sink_row - m)
    inv = 1.0 / (jnp.sum(e, axis=0, keepdims=True) + es)
    return e * inv, es * inv


def _roped_keys(k_ref, kp_ref, tabs):
    c, s1, s2, cp, s1p, s2p = (t[...] for t in tabs)
    k_cur = _rope(k_ref[...].astype(F32), c, s1, s2, 1.0)
    k_prev = _rope(kp_ref[...].astype(F32), cp, s1p, s2p, 1.0)
    return k_prev, k_cur


SCALE = HEAD_DIM ** -0.5


def _attn_fwd(proj, tables, bias, sink, comm=None):
    S = proj.shape[0]
    nb = S // BLOCK
    npairs = N_KV_HEADS // 2
    proj_specs, table_specs, _, _ = _attn_specs(nb)
    c_arrays, c_shapes, c_sems, c_build, c_alias = _comm_fields(comm)
    n_in = len(proj_specs) + len(table_specs) + 1

    def body(*refs):
        (q_ref, k_ref, kp_ref, v_ref, vp_ref, ga_ref, c_ref, s1_ref, s2_ref, cp_ref, s1p_ref, s2p_ref, bias_ref,
         sink_ref) = refs[:n_in]
        cin = refs[n_in:n_in + len(c_arrays)]
        attn_ref, ain_ref = refs[n_in + len(c_arrays):n_in + len(c_arrays) + 2]
        cout = refs[n_in + len(c_arrays) + 2:n_in + len(c_arrays) + 2 + len(c_shapes)]
        qr_scr = refs[n_in + len(c_arrays) + 2 + len(c_shapes)]
        sems = refs[n_in + len(c_arrays) + 3 + len(c_shapes):]
        pair, i = pl.program_id(0), pl.program_id(1)
        if comm is not None:
            _start_when((pair == 0) & (i == 0), c_build(cin, cout, sems))
        tabs = (c_ref, s1_ref, s2_ref, cp_ref, s1p_ref, s2p_ref)
        qr_scr[...] = _rope(q_ref[...].astype(F32), c_ref[...], s1_ref[...], s2_ref[...], 1.0) * SCALE
        k_prev, k_cur = _roped_keys(k_ref, kp_ref, tabs)
        v_prev, v_cur = vp_ref[...].astype(F32), v_ref[...].astype(F32)
        from_prev = _from_prev()
        heads = [(kvh, odd) for kvh in range(2) for odd in range(2)]
        qps = [_pair_rows(qr_scr, kvh) for kvh in range(2)]
        kpss, kcss = [_split_heads(k_prev, kvh) for kvh in range(2)], [_split_heads(k_cur, kvh) for kvh in range(2)]
        scores = [_band_t(kpss[kvh][odd], kcss[kvh][odd], qps[kvh], from_prev) + bias_ref[0] for kvh, odd in heads]
        probs = []
        for n, (kvh, odd) in enumerate(heads):
            pb = _softmax_t(scores[n], _sink_row(sink_ref, pair, kvh, odd))[0].astype(BF16)
            probs.append((jnp.where(from_prev, pb, 0), jnp.where(from_prev, 0, pb)))
        for kvh in range(2):
            vps, vcs = _split_heads(v_prev, kvh), _split_heads(v_cur, kvh)
            o_t = None
            for odd in range(2):
                p_prev, p_cur = probs[2 * kvh + odd]
                part = (lax.dot_general(vps[odd], p_prev, TN, preferred_element_type=F32)
                        + lax.dot_general(vcs[odd], p_cur, TN, preferred_element_type=F32))
                o_t = part if o_t is None else o_t + part
            _unpair_rows(attn_ref, kvh, o_t)
        ga = ga_ref[...].astype(F32)
        ain_ref[...] = (attn_ref[...] * (ga * _sigmoid(ga))).astype(BF16)
        if comm is not None:
            _wait_when((pair == npairs - 1) & (i == nb - 1), c_build(cin, cout, sems))

    out_spec = pl.BlockSpec((BLOCK, COL_BLK), lambda p, i: (i, p))
    aliases = {n_in + t: 2 + t for t in range(len(c_arrays))} if c_alias else {}
    out = pl.pallas_call(
        body, name="attn_fwd",
        out_shape=(jax.ShapeDtypeStruct((S, ATTN_WIDTH), F32), jax.ShapeDtypeStruct((S, ATTN_WIDTH), BF16)) + tuple(c_shapes),
        grid=(npairs, nb), in_specs=proj_specs + table_specs + [SMEM_SPEC] + [HBM_SPEC] * len(c_arrays),
        out_specs=(out_spec, out_spec) + (HBM_SPEC,) * len(c_shapes),
        scratch_shapes=[pltpu.VMEM((BLOCK, COL_BLK), F32)] + _comm_scratch(c_sems), input_output_aliases=aliases,
        compiler_params=_params("arbitrary", "arbitrary"),
    )(*([proj] * 6), *tables, *tables, bias, sink, *c_arrays)
    return out


def _attn_bwd(proj, tables, bias, sink, attn, da_in):
    S = proj.shape[0]
    nb = S // BLOCK
    proj_specs, table_specs, cur, prev = _attn_specs(nb)
    wide_cur = pl.BlockSpec((BLOCK, COL_BLK), lambda p, i: (cur(i), p))
    kv_out = pl.BlockSpec((BLOCK, LANES), lambda p, i: (jnp.maximum(i - 1, 0), p))

    def body(q_ref, k_ref, kp_ref, v_ref, vp_ref, ga_ref, c_ref, s1_ref, s2_ref, cp_ref, s1p_ref, s2p_ref, bias_ref,
             sink_ref, attn_ref, da_ref, dq_ref, dk_ref, dv_ref, dga_ref, dsink_ref,
             qr_scr, do_scr, dq_scr, dk_scr, dv_scr, carry_k, carry_v, sink_acc):
        pair, i = pl.program_id(0), pl.program_id(1)
        tabs = (c_ref, s1_ref, s2_ref, cp_ref, s1p_ref, s2p_ref)

        @pl.when(i < nb)
        def _():
            ga = ga_ref[...].astype(F32)
            sg = _sigmoid(ga)
            da = da_ref[...]
            dga_ref[...] = (da * attn_ref[...] * (sg * (1.0 + ga * (1.0 - sg)))).astype(BF16)
            do_scr[...] = da * (ga * sg)
            qr_scr[...] = _rope(q_ref[...].astype(F32), c_ref[...], s1_ref[...], s2_ref[...], 1.0) * SCALE
            k_prev, k_cur = _roped_keys(k_ref, kp_ref, tabs)
            v_prev, v_cur = vp_ref[...].astype(F32), v_ref[...].astype(F32)
            from_prev = _from_prev()
            dkv = [None] * 4
            heads = [(kvh, odd) for kvh in range(2) for odd in range(2)]
            qps, dops = [_pair_rows(qr_scr, kvh) for kvh in range(2)], [_pair_rows(do_scr, kvh) for kvh in range(2)]
            kpss, kcss = [_split_heads(k_prev, kvh) for kvh in range(2)], [_split_heads(k_cur, kvh) for kvh in range(2)]
            vpss, vcss = [_split_heads(v_prev, kvh) for kvh in range(2)], [_split_heads(v_cur, kvh) for kvh in range(2)]
            scores = [_band_t(kpss[kvh][odd], kcss[kvh][odd], qps[kvh], from_prev) + bias_ref[0] for kvh, odd in heads]
            dps = [_band_t(vpss[kvh][odd], vcss[kvh][odd], dops[kvh], from_prev) for kvh, odd in heads]
            masked = []
            for n, (kvh, odd) in enumerate(heads):
                p_t, p_sink = _softmax_t(scores[n], _sink_row(sink_ref, pair, kvh, odd))
                delta = jnp.sum(p_t * dps[n], axis=0, keepdims=True)
                ds = (p_t * (dps[n] - delta)).astype(BF16)
                pb = p_t.astype(BF16)
                masked.append((jnp.where(from_prev, ds, 0), jnp.where(from_prev, 0, ds),
                               jnp.where(from_prev, pb, 0), jnp.where(from_prev, 0, pb)))
                contrib = -p_sink * delta

                @pl.when(i == 0)
                def _():
                    sink_acc[n] = contrib

                @pl.when(i > 0)
                def _():
                    sink_acc[n] += contrib

            for kvh in range(2):
                qp, dop = qps[kvh], dops[kvh]
                dq_t, halves = None, []
                for odd in range(2):
                    ds_prev, ds_cur, p_prev, p_cur = masked[2 * kvh + odd]
                    part = (lax.dot_general(kpss[kvh][odd], ds_prev, TN, preferred_element_type=F32)
                            + lax.dot_general(kcss[kvh][odd], ds_cur, TN, preferred_element_type=F32))
                    dq_t = part if dq_t is None else dq_t + part
                    halves.append([jnp.dot(ds_prev, qp, preferred_element_type=F32),
                                   jnp.dot(ds_cur, qp, preferred_element_type=F32),
                                   jnp.dot(p_prev, dop, preferred_element_type=F32),
                                   jnp.dot(p_cur, dop, preferred_element_type=F32)])
                _unpair_rows(dq_scr, kvh, dq_t)
                for n in range(4):
                    joined = _join_heads(halves[0][n], halves[1][n], kvh)
                    dkv[n] = joined if dkv[n] is None else dkv[n] + joined
            dk_scr[0:BLOCK, :], dk_scr[BLOCK:2 * BLOCK, :] = dkv[0], dkv[1]
            dv_scr[0:BLOCK, :], dv_scr[BLOCK:2 * BLOCK, :] = dkv[2], dkv[3]

            dq_ref[...] = _rope(dq_scr[...] * SCALE, c_ref[...], s1_ref[...], s2_ref[...], -1.0).astype(BF16)
            dk_prev = _rope(dk_scr[0:BLOCK, :], cp_ref[...], s1p_ref[...], s2p_ref[...], -1.0)
            dk_cur = _rope(dk_scr[BLOCK:2 * BLOCK, :], c_ref[...], s1_ref[...], s2_ref[...], -1.0)
            dv_prev, dv_cur = dv_scr[0:BLOCK, :], dv_scr[BLOCK:2 * BLOCK, :]

            @pl.when(i > 0)
            def _():
                dk_ref[...] = (carry_k[...] + dk_prev).astype(BF16)
                dv_ref[...] = (carry_v[...] + dv_prev).astype(BF16)

            carry_k[...] = dk_cur
            carry_v[...] = dv_cur

        @pl.when(i == nb)
        def _():
            dk_ref[...] = carry_k[...].astype(BF16)
            dv_ref[...] = carry_v[...].astype(BF16)
            lane = lax.broadcasted_iota(jnp.int32, (8, LANES), 1)
            acc = jnp.zeros((8, LANES), F32)
            for kvh in range(2):
                for odd in range(2):
                    for b in range(PAIRS):
                        val = jnp.sum(sink_acc[2 * kvh + odd, :, b * BLOCK:(b + 1) * BLOCK], axis=1, keepdims=True)
                        acc = jnp.where(lane == kvh * Q_PER_KV + 2 * b + odd, val, acc)
            dsink_ref[0] = acc

    return pl.pallas_call(
        body, name="attn_bwd",
        out_shape=(jax.ShapeDtypeStruct((S, ATTN_WIDTH), BF16), jax.ShapeDtypeStruct((S, KV_WIDTH), BF16),
                   jax.ShapeDtypeStruct((S, KV_WIDTH), BF16), jax.ShapeDtypeStruct((S, ATTN_WIDTH), BF16),
                   jax.ShapeDtypeStruct((N_KV_HEADS // 2, 8, LANES), F32)),
        grid=(N_KV_HEADS // 2, nb + 1),
        in_specs=proj_specs + table_specs + [SMEM_SPEC, wide_cur, wide_cur],
        out_specs=(wide_cur, kv_out, kv_out, wide_cur, pl.BlockSpec((1, 8, LANES), lambda p, i: (p, 0, 0))),
        scratch_shapes=[pltpu.VMEM((BLOCK, COL_BLK), F32), pltpu.VMEM((BLOCK, COL_BLK), F32),
                        pltpu.VMEM((BLOCK, COL_BLK), F32), pltpu.VMEM((2 * BLOCK, LANES), F32),
                        pltpu.VMEM((2 * BLOCK, LANES), F32), pltpu.VMEM((BLOCK, LANES), F32),
                        pltpu.VMEM((BLOCK, LANES), F32), pltpu.VMEM((4, 1, QCOLS), F32)],
        compiler_params=_params("arbitrary", "arbitrary"),
    )(*([proj] * 6), *tables, *tables, bias, sink, attn, da_in)


N_PARTS = GMLP_WIDTH // COL_BLK
GROUPS_PER_PART = COL_BLK // GMLP_GROUP_DIM


def _part_specs(off):
    return [pl.BlockSpec((GMLP_CHUNK, COL_BLK), functools.partial(lambda j, i: (i, j), off // COL_BLK + k))
            for k in range(N_PARTS)]


def _group(refs, g):
    lo = (g % GROUPS_PER_PART) * GMLP_GROUP_DIM
    return refs[g // GROUPS_PER_PART][:, lo:lo + GMLP_GROUP_DIM].astype(F32)


def _gmlp_norm_stats(vg_refs, gv_scr, grad_scr=None):
    total = jnp.zeros((GMLP_CHUNK, 1), F32)
    for k in range(N_PARTS):
        vg = vg_refs[k][...].astype(F32)
        if grad_scr is None:
            gv = _gelu(vg)
        else:
            gv, grad_scr[:, k * COL_BLK:(k + 1) * COL_BLK] = _gelu_both(vg)
        gv_scr[:, k * COL_BLK:(k + 1) * COL_BLK] = gv
        total = total + jnp.sum(gv, axis=1, keepdims=True)
    mu = total / GMLP_WIDTH
    xc = gv_scr[...] - mu
    var = jnp.sum(xc * xc, axis=1, keepdims=True) / GMLP_WIDTH
    return mu, lax.rsqrt(var + LN_EPS)


def _tril_bf16(ws_ref, g):
    t = lax.broadcasted_iota(jnp.int32, (GMLP_CHUNK, GMLP_CHUNK), 0)
    s = lax.broadcasted_iota(jnp.int32, (GMLP_CHUNK, GMLP_CHUNK), 1)
    return jnp.where(s <= t, ws_ref[g], 0.0).astype(BF16), s <= t


def _gmlp_fwd(proj, ws, bs_t, lg, lb, comm=None):
    S = proj.shape[0]
    nb = S // GMLP_CHUNK
    c_arrays, c_shapes, c_sems, c_build, c_alias = _comm_fields(comm)
    n_cin, n_cout = len(c_arrays), len(c_shapes)

    def body(*refs):
        u_refs, vg_refs, gb_refs = refs[0:4], refs[4:8], refs[8:12]
        ws_ref, bst_ref, lg_ref, lb_ref = refs[12:16]
        cin, out_ref, cout = refs[16:16 + n_cin], refs[16 + n_cin], refs[17 + n_cin:17 + n_cin + n_cout]
        gv_scr, sems = refs[17 + n_cin + n_cout], refs[18 + n_cin + n_cout:]
        if comm is not None:
            _start_when(pl.program_id(0) == 0, c_build(cin, cout, sems))
        mu, rstd = _gmlp_norm_stats(vg_refs, gv_scr)
        for g in range(GMLP_GROUPS):
            cols = slice(g * GMLP_GROUP_DIM, (g + 1) * GMLP_GROUP_DIM)
            vn = (gv_scr[:, cols] - mu) * rstd * lg_ref[:, cols] + lb_ref[:, cols]
            w, _ = _tril_bf16(ws_ref, g)
            mixed = jnp.dot(w, vn.astype(BF16), preferred_element_type=F32) + bst_ref[:, g:g + 1]
            gb = _group(gb_refs, g)
            out_ref[:, cols] = (_gelu(_group(u_refs, g)) * mixed * (gb * _sigmoid(gb))).astype(BF16)
        if comm is not None:
            _wait_when(pl.program_id(0) == nb - 1, c_build(cin, cout, sems))

    full = lambda shape: pl.BlockSpec(shape, lambda i: tuple(0 for _ in shape))
    aliases = {16 + t: 1 + t for t in range(n_cin)} if c_alias else {}
    out = pl.pallas_call(
        body, name="gmlp_fwd", out_shape=(jax.ShapeDtypeStruct((S, GMLP_WIDTH), BF16),) + tuple(c_shapes), grid=(nb,),
        in_specs=_part_specs(OFF_U) + _part_specs(OFF_VG) + _part_specs(OFF_GB)
        + [full(ws.shape), full(bs_t.shape), full(lg.shape), full(lb.shape)] + [HBM_SPEC] * n_cin,
        out_specs=(pl.BlockSpec((GMLP_CHUNK, GMLP_WIDTH), lambda i: (i, 0)),) + (HBM_SPEC,) * n_cout,
        scratch_shapes=[pltpu.VMEM((GMLP_CHUNK, GMLP_WIDTH), F32)] + _comm_scratch(c_sems), input_output_aliases=aliases,
        compiler_params=_params("arbitrary" if comm is not None else "parallel"),
    )(*([proj] * 12), ws, bs_t, lg, lb, *c_arrays)
    return out if comm is not None else out[0]


def _gmlp_bwd(proj, ws, bs_t, lg, lb, db_in):
    S = proj.shape[0]
    nb = S // GMLP_CHUNK
    W = GMLP_WIDTH

    def body(*refs):
        u_refs, vg_refs, gb_refs = refs[0:4], refs[4:8], refs[8:12]
        ws_ref, bst_ref, lg_ref, lb_ref, db_ref = refs[12:17]
        out_ref, dws_ref, dbs_ref, dlg_ref, dlb_ref = refs[17:22]
        gv_scr, dvh_scr, vgrad_scr = refs[22:]
        i = pl.program_id(0)

        @pl.when(i == 0)
        def _():
            dws_ref[...] = jnp.zeros_like(dws_ref)
            dbs_ref[...] = jnp.zeros_like(dbs_ref)
            dlg_ref[...] = jnp.zeros_like(dlg_ref)
            dlb_ref[...] = jnp.zeros_like(dlb_ref)

        mu, rstd = _gmlp_norm_stats(vg_refs, gv_scr, vgrad_scr)
        sum_dvh = jnp.zeros((GMLP_CHUNK, 1), F32)
        sum_dvh_vh = jnp.zeros((GMLP_CHUNK, 1), F32)
        for g in range(GMLP_GROUPS):
            cols = slice(g * GMLP_GROUP_DIM, (g + 1) * GMLP_GROUP_DIM)
            vhat = (gv_scr[:, cols] - mu) * rstd
            vn = (vhat * lg_ref[:, cols] + lb_ref[:, cols]).astype(BF16)
            w, tril = _tril_bf16(ws_ref, g)
            mixed = jnp.dot(w, vn, preferred_element_type=F32) + bst_ref[:, g:g + 1]
            u, gb, db = _group(u_refs, g), _group(gb_refs, g), db_ref[:, cols]
            (gu, gu_grad), sgb = _gelu_both(u), _sigmoid(gb)
            dsg = db * (gb * sgb)
            out_ref[:, 2 * W + g * GMLP_GROUP_DIM: 2 * W + (g + 1) * GMLP_GROUP_DIM] = (
                db * (gu * mixed) * (sgb * (1.0 + gb * (1.0 - sgb)))).astype(BF16)
            out_ref[:, cols] = (dsg * mixed * gu_grad).astype(BF16)
            dmixed = dsg * gu
            dmixed_b = dmixed.astype(BF16)
            dvn = lax.dot_general(w, dmixed_b, (((0,), (0,)), ((), ())), preferred_element_type=F32)
            dw = lax.dot_general(dmixed_b, vn, (((1,), (1,)), ((), ())), preferred_element_type=F32)
            dws_ref[g] += jnp.where(tril, dw, 0.0)
            dbs_ref[g] += jnp.sum(dmixed, axis=1, keepdims=True)
            dlg_ref[:, cols] += jnp.sum(dvn * vhat, axis=0, keepdims=True)
            dlb_ref[:, cols] += jnp.sum(dvn, axis=0, keepdims=True)
            dvh = dvn * lg_ref[:, cols]
            dvh_scr[:, cols] = dvh
            sum_dvh = sum_dvh + jnp.sum(dvh, axis=1, keepdims=True)
            sum_dvh_vh = sum_dvh_vh + jnp.sum(dvh * vhat, axis=1, keepdims=True)
        m1, m2 = sum_dvh / W, sum_dvh_vh / W
        for k in range(N_PARTS):
            cols = slice(k * COL_BLK, (k + 1) * COL_BLK)
            vhat = (gv_scr[:, cols] - mu) * rstd
            dgv = rstd * (dvh_scr[:, cols] - m1 - vhat * m2)
            out_ref[:, W + k * COL_BLK: W + (k + 1) * COL_BLK] = (dgv * vgrad_scr[:, cols]).astype(BF16)

    full = lambda shape: pl.BlockSpec(shape, lambda i: tuple(0 for _ in shape))
    row = pl.BlockSpec((GMLP_CHUNK, W), lambda i: (i, 0))
    return pl.pallas_call(
        body, name="gmlp_bwd",
        out_shape=(jax.ShapeDtypeStruct((S, 3 * W), BF16), jax.ShapeDtypeStruct(ws.shape, F32),
                   jax.ShapeDtypeStruct((GMLP_GROUPS, GMLP_CHUNK, 1), F32), jax.ShapeDtypeStruct((1, W), F32),
                   jax.ShapeDtypeStruct((1, W), F32)),
        grid=(nb,),
        in_specs=_part_specs(OFF_U) + _part_specs(OFF_VG) + _part_specs(OFF_GB)
        + [full(ws.shape), full(bs_t.shape), full(lg.shape), full(lb.shape), row],
        out_specs=(pl.BlockSpec((GMLP_CHUNK, 3 * W), lambda i: (i, 0)), full(ws.shape),
                   full((GMLP_GROUPS, GMLP_CHUNK, 1)), full((1, W)), full((1, W))),
        scratch_shapes=[pltpu.VMEM((GMLP_CHUNK, W), F32)] * 3,
        compiler_params=_params("arbitrary"),
    )(*([proj] * 12), ws, bs_t, lg, lb, db_in)


def _merge_fwd(proj, y_a, y_b, rows=256):
    S = proj.shape[0]
    rows = min(rows, S)

    def body(ma_ref, mb_ref, ya_ref, yb_ref, out_ref):
        out_ref[...] = (_sigmoid(ma_ref[...].astype(F32)) * ya_ref[...]
                        + _sigmoid(mb_ref[...].astype(F32)) * yb_ref[...]).astype(BF16)

    blk = lambda off: pl.BlockSpec((rows, COL_BLK), lambda i, j: (i, off // COL_BLK + j))
    return pl.pallas_call(
        body, name="merge_fwd", out_shape=jax.ShapeDtypeStruct((S, D_MODEL), BF16), grid=(S // rows, D_MODEL // COL_BLK),
        in_specs=[blk(OFF_MA), blk(OFF_MB), blk(0), blk(0)], out_specs=blk(0),
        compiler_params=_params("parallel", "parallel"),
    )(proj, proj, y_a, y_b)


def _merge_bwd(proj, y_a, y_b, dmerged, rows=128):
    S = proj.shape[0]
    rows = min(rows, S)
    nj = D_MODEL // COL_BLK

    def body(*refs):
        ma_refs, mb_refs = refs[0:nj], refs[nj:2 * nj]
        ya_ref, yb_ref, dm_ref, dya_ref, dyb_ref, dg_ref = refs[2 * nj:]
        for k in range(nj):
            cols = slice(k * COL_BLK, (k + 1) * COL_BLK)
            dm = dm_ref[:, cols]
            sa, sb = _sigmoid(ma_refs[k][...].astype(F32)), _sigmoid(mb_refs[k][...].astype(F32))
            dya_ref[:, cols] = (dm * sa).astype(BF16)
            dyb_ref[:, cols] = (dm * sb).astype(BF16)
            dg_ref[:, cols] = (dm * ya_ref[:, cols] * sa * (1.0 - sa)).astype(BF16)
            dg_ref[:, D_MODEL + k * COL_BLK: D_MODEL + (k + 1) * COL_BLK] = (
                dm * yb_ref[:, cols] * sb * (1.0 - sb)).astype(BF16)

    part = lambda off: [pl.BlockSpec((rows, COL_BLK), functools.partial(lambda j, i: (i, j), off // COL_BLK + k))
                        for k in range(nj)]
    row = pl.BlockSpec((rows, D_MODEL), lambda i: (i, 0))
    return pl.pallas_call(
        body, name="merge_bwd",
        out_shape=(jax.ShapeDtypeStruct((S, D_MODEL), BF16), jax.ShapeDtypeStruct((S, D_MODEL), BF16),
                   jax.ShapeDtypeStruct((S, 2 * D_MODEL), BF16)),
        grid=(S // rows,), in_specs=part(OFF_MA) + part(OFF_MB) + [row, row, row],
        out_specs=(row, row, pl.BlockSpec((rows, 2 * D_MODEL), lambda i: (i, 0))),
        compiler_params=_params("parallel"),
    )(*([proj] * (2 * nj)), y_a, y_b, dmerged)


def _loss_head(x2, target, fg, rows=128):
    S, D = x2.shape
    rows = min(rows, S)

    def body(x_ref, t_ref, g_ref, dx_ref, dxb_ref, dg_ref, loss_ref):
        i = pl.program_id(0)

        @pl.when(i == 0)
        def _():
            dg_ref[...] = jnp.zeros_like(dg_ref)
            loss_ref[...] = jnp.zeros_like(loss_ref)

        xv, g = x_ref[...], g_ref[...]
        rstd = lax.rsqrt(jnp.mean(xv * xv, axis=-1, keepdims=True) + NORM_EPS)
        xhat = xv * rstd
        err = xhat * g - t_ref[...]
        loss_ref[...] += (0.5 / D) * jnp.sum(err * err)
        dy = err * (1.0 / D)
        dg_ref[...] += jnp.sum(dy * xhat, axis=0, keepdims=True)
        dxh = dy * g
        dx = rstd * (dxh - xhat * jnp.mean(dxh * xhat, axis=-1, keepdims=True))
        dx_ref[...] = dx
        dxb_ref[...] = dx.astype(BF16)

    row = pl.BlockSpec((rows, D), lambda i: (i, 0))
    vec = pl.BlockSpec((1, D), lambda i: (0, 0))
    return pl.pallas_call(
        body, name="loss_head",
        out_shape=(jax.ShapeDtypeStruct((S, D), F32), jax.ShapeDtypeStruct((S, D), BF16),
                   jax.ShapeDtypeStruct((1, D), F32), jax.ShapeDtypeStruct((8, LANES), F32)),
        grid=(S // rows,), in_specs=[row, row, vec],
        out_specs=(row, row, vec, pl.BlockSpec((8, LANES), lambda i: (0, 0))),
        compiler_params=_params("arbitrary"),
    )(x2, target, fg)


def _rms_bwd(x, dh, dx2, g, comm=None, rows=128):
    S, D = x.shape
    rows = min(rows, S)
    nsteps = S // rows
    c_arrays, c_shapes, c_sems, c_build, c_alias = _comm_fields(comm)
    n_cin, n_cout = len(c_arrays), len(c_shapes)

    def body(*refs):
        x_ref, dh_ref, dx2_ref, g_ref = refs[:4]
        cin = refs[4:4 + n_cin]
        gx_ref, dg_ref = refs[4 + n_cin], refs[5 + n_cin]
        cout, sems = refs[6 + n_cin:6 + n_cin + n_cout], refs[6 + n_cin + n_cout:]
        i = pl.program_id(0)
        if comm is not None:
            _start_when(i == 0, c_build(cin, cout, sems))

        @pl.when(i == 0)
        def _():
            dg_ref[...] = jnp.zeros_like(dg_ref)

        xv, dh_v = x_ref[...], dh_ref[...]
        rstd = lax.rsqrt(jnp.mean(xv * xv, axis=-1, keepdims=True) + NORM_EPS)
        xhat = xv * rstd
        dg_ref[...] += jnp.sum(dh_v * xhat, axis=0, keepdims=True)
        dxh = dh_v * g_ref[...]
        gx_ref[...] = dx2_ref[...] + rstd * (dxh - xhat * jnp.mean(dxh * xhat, axis=-1, keepdims=True))
        if comm is not None:
            _wait_when(i == nsteps - 1, c_build(cin, cout, sems))

    row = pl.BlockSpec((rows, D), lambda i: (i, 0))
    vec = pl.BlockSpec((1, D), lambda i: (0, 0))
    aliases = {4 + t: 2 + t for t in range(n_cin)} if c_alias else {}
    return pl.pallas_call(
        body, name="rms_bwd",
        out_shape=(jax.ShapeDtypeStruct((S, D), F32), jax.ShapeDtypeStruct((1, D), F32)) + tuple(c_shapes),
        grid=(nsteps,), in_specs=[row, row, row, vec] + [HBM_SPEC] * n_cin, out_specs=(row, vec) + (HBM_SPEC,) * n_cout,
        scratch_shapes=_comm_scratch(c_sems), input_output_aliases=aliases, compiler_params=_params("arbitrary"),
    )(x, dh, dx2, g, *c_arrays)


def _adamw(w, g, m, v, name, rows=64):
    R, C = w.shape
    rows = min(rows, R)
    c1 = 1.0 - ADAM_B1 ** ADAM_STEP
    c2 = 1.0 - ADAM_B2 ** ADAM_STEP

    def body(w_ref, g_ref, m_ref, v_ref, go_ref, d_ref, mo_ref, vo_ref):
        gv = g_ref[...]
        mn = ADAM_B1 * m_ref[...] + (1.0 - ADAM_B1) * gv
        vn = ADAM_B2 * v_ref[...] + (1.0 - ADAM_B2) * (gv * gv)
        go_ref[...] = gv
        mo_ref[...] = mn
        vo_ref[...] = vn
        d_ref[...] = -ADAM_LR * ((mn / c1) / (jnp.sqrt(vn / c2) + ADAM_EPS) + ADAM_WD * w_ref[...])

    spec = pl.BlockSpec((rows, C), lambda i: (i, 0))
    shape = jax.ShapeDtypeStruct((R, C), F32)
    return pl.pallas_call(body, name=name, out_shape=(shape,) * 4, grid=(R // rows,), in_specs=[spec] * 4,
                          out_specs=(spec,) * 4, compiler_params=_params("parallel"))(w, g, m, v)


def _place():
    x, y, c = lax.axis_index("x"), lax.axis_index("y"), lax.axis_index("c")
    others = [(1 - x, y), (x, 1 - y), (1 - x, 1 - y)]
    return x, y, c, others


def _chip_index(chip):
    return 2 * chip[0] + chip[1]


def _remote(src, dst, sems, k, to):
    send_sems, recv_sems = sems
    return pltpu.make_async_remote_copy(src_ref=src, dst_ref=dst, send_sem=send_sems.at[k], recv_sem=recv_sems.at[k],
                                        device_id=to, device_id_type=MESH)


def _comm_fields(comm):
    return comm if comm is not None else ((), (), 0, None, False)


def _comm_scratch(n_sems):
    return [pltpu.SemaphoreType.DMA((n_sems,)), pltpu.SemaphoreType.DMA((n_sems,))] if n_sems else []


def _start_when(cond, copies):
    @pl.when(cond)
    def _():
        for cp in copies:
            cp.start()


def _wait_when(cond, copies):
    @pl.when(cond)
    def _():
        for cp in copies:
            cp.wait()


def _proj_gather(tile_cols, h, w_full, comm=None, tm=1024, tk=512):
    c_arrays, c_shapes, c_sems, c_build, c_alias = _comm_fields(comm)
    n_cin, n_cout = len(c_arrays), len(c_shapes)
    S, Dm = h.shape
    tm, tk = min(tm, S), min(tk, Dm)
    tn = SLAB // 2
    ni, nj, nk = S // tm, 2 * N_CHIPS, Dm // tk
    total = nj * ni * nk
    half_in = Dm // 2

    def body(*refs):
        cols_ref, h_ref = refs[0], refs[1]
        cin = refs[3:3 + n_cin]
        proj_ref, fi_ref = refs[3 + n_cin], refs[4 + n_cin]
        cout = refs[5 + n_cin:5 + n_cin + n_cout]
        acc, bbuf, bsem, send_sems, recv_sems = refs[5 + n_cin + n_cout:10 + n_cin + n_cout]
        c_sem_refs = refs[10 + n_cin + n_cout:]
        j, i, k = pl.program_id(0), pl.program_id(1), pl.program_id(2)
        t = (j * ni + i) * nk + k
        x, y, c, others = _place()
        me_chip = 2 * x + y
        sems = (send_sems, recv_sems)

        def piece(chip, half, tile):
            return fi_ref.at[pl.ds(half * half_in, half_in), pl.ds(chip * SLAB + tile * tn, tn)]

        def ici_send(r, tile):
            mine = piece(me_chip, c, tile)
            return _remote(mine, mine, sems, 2 * r + tile, (*others[r], c))

        def forward(tile):
            src_chip = (x + (1 - c) * (1 - 2 * x), y + c * (1 - 2 * y))
            dst_chip = (x + c * (1 - 2 * x), y + (1 - c) * (1 - 2 * y))
            got = piece(_chip_index(src_chip), c, tile)
            return _remote(got, got, sems, 4 + tile, (*dst_chip, c))

        def ici_recv(r, tile):
            got = piece(_chip_index(others[r]), c, tile)
            return _remote(got, got, sems, 2 * r + tile, (x, y, c))

        def pass_on(r, tile):
            got = piece(_chip_index(others[r]), c, tile)
            return _remote(got, got, sems, 6 + 2 * r + tile, (x, y, 1 - c))

        def passed_recv(r, tile):
            got = piece(_chip_index(others[r]), 1 - c, tile)
            return _remote(got, got, sems, 6 + 2 * r + tile, (x, y, c))

        def b_copy(jj, kk, slot):
            return pltpu.make_async_copy(fi_ref.at[pl.ds(kk * tk, tk), pl.ds(cols_ref[jj] * tn, tn)], bbuf.at[slot],
                                         bsem.at[slot])

        @pl.when(t == 0)
        def _():
            for tile in range(2):
                ici_send(0, tile).start()
                ici_send(1, tile).start()
            b_copy(0, 0, 0).start()
            if comm is not None:
                for cp in c_build(cin, cout, c_sem_refs):
                    cp.start()

        nxt = t + 1
        kn, i_n, jn = nxt % nk, (nxt // nk) % ni, nxt // (nk * ni)

        def opens(jj):
            return (jn == jj) & (i_n == 0) & (kn == 0)

        @pl.when(nxt < total)
        def _():
            for tile in range(2):
                @pl.when(opens(2 + 2 * tile))
                def _():
                    ici_recv(0, tile).wait_recv()
                    ici_recv(1, tile).wait_recv()
                    forward(tile).start()
                    pass_on(0, tile).start()
                    pass_on(1, tile).start()
                    passed_recv(0, tile).wait_recv()

                @pl.when(opens(3 + 2 * tile))
                def _():
                    passed_recv(1, tile).wait_recv()

                @pl.when(opens(6 + tile))
                def _():
                    ici_recv(2, tile).wait_recv()
                    pass_on(2, tile).start()
                    passed_recv(2, tile).wait_recv()

            b_copy(jn, kn, nxt % 2).start()

        b_copy(j, k, t % 2).wait()

        def product():
            return jnp.dot(h_ref[...], bbuf[t % 2], preferred_element_type=F32)

        @pl.when(k == 0)
        def _():
            acc[...] = product()

        @pl.when((k > 0) & (k < nk - 1))
        def _():
            acc[...] += product()

        @pl.when(k == nk - 1)
        def _():
            proj_ref[...] = (acc[...] + product()).astype(BF16)

        @pl.when(t == total - 1)
        def _():
            for tile in range(2):
                ici_send(0, tile).wait_send()
                ici_send(1, tile).wait_send()
                forward(tile).wait_send()
                for r in range(3):
                    pass_on(r, tile).wait_send()
            if comm is not None:
                for cp in c_build(cin, cout, c_sem_refs):
                    cp.wait()

    grid_spec = pltpu.PrefetchScalarGridSpec(
        num_scalar_prefetch=1, grid=(nj, ni, nk),
        in_specs=[pl.BlockSpec((tm, tk), lambda j, i, k, cols: (i, k)), HBM_SPEC] + [HBM_SPEC] * n_cin,
        out_specs=(pl.BlockSpec((tm, tn), lambda j, i, k, cols: (i, cols[j])), HBM_SPEC) + (HBM_SPEC,) * n_cout,
        scratch_shapes=[pltpu.VMEM((tm, tn), F32), pltpu.VMEM((2, tk, tn), BF16), pltpu.SemaphoreType.DMA((2,)),
                        pltpu.SemaphoreType.DMA((12,)), pltpu.SemaphoreType.DMA((12,))] + _comm_scratch(c_sems))
    aliases = {2: 1}
    if c_alias:
        aliases.update({3 + t: 2 + t for t in range(n_cin)})
    return pl.pallas_call(
        body, name="proj_gather",
        out_shape=(jax.ShapeDtypeStruct((S, PROJ_WIDTH), BF16), jax.ShapeDtypeStruct(w_full.shape, BF16)) + tuple(c_shapes),
        grid_spec=grid_spec, input_output_aliases=aliases,
        compiler_params=_params("arbitrary", "arbitrary", "arbitrary"),
    )(tile_cols, h, w_full, *c_arrays)


def _exchange(comm, name):
    arrays, shapes, n_sems, build, aliased = comm
    n_in, n_out = len(arrays), len(shapes)

    def body(*refs):
        copies = build(refs[:n_in], refs[n_in:n_in + n_out], refs[n_in + n_out:])
        for cp in copies:
            cp.start()
        for cp in copies:
            cp.wait()

    return pl.pallas_call(
        body, name=name, out_shape=tuple(shapes), in_specs=[HBM_SPEC] * n_in, out_specs=(HBM_SPEC,) * n_out,
        scratch_shapes=_comm_scratch(n_sems), input_output_aliases={t: t for t in range(n_in)} if aliased else {},
    )(*arrays)


def _pair_halves(parts):
    def half_of(ref, which):
        if len(ref.shape) == 2:
            rows = ref.shape[0] // 2
            return ref.at[pl.ds(which * rows, rows), :]
        return ref.at[:, pl.ds(which, 1)]

    def out_shape(p):
        if p.ndim == 2:
            return jax.ShapeDtypeStruct((p.shape[0] // 2, p.shape[1]), BF16)
        return jax.ShapeDtypeStruct((p.shape[0], 1) + p.shape[2:], BF16)

    def build(p_refs, r_refs, sems):
        x, y, c, _ = _place()
        return [_remote(half_of(p, 1 - c), r, sems, t, (x, y, 1 - c)) for t, (p, r) in enumerate(zip(p_refs, r_refs))]

    return (tuple(parts), tuple(out_shape(p) for p in parts), len(parts), build, False)


def _gather_squares(items):
    half_up = UP_ROWS // 2
    fulls = [f for f, _ in items]

    def build(_, f_refs, sems):
        x, y, c, others = _place()
        copies = []
        for t, (f, (_, over_ici)) in enumerate(zip(f_refs, items)):
            for r, chip in enumerate(others):
                src_chip = 2 * x + y if over_ici else _chip_index(chip)
                rows = f.at[pl.ds(src_chip * UP_ROWS + c * half_up, half_up), :]
                copies.append(_remote(rows, rows, sems, 3 * t + r, (*chip, c) if over_ici else (x, y, 1 - c)))
        return copies

    return (tuple(fulls), tuple(jax.ShapeDtypeStruct(f.shape, f.dtype) for f in fulls), 3 * len(fulls), build, True)


def _send_whole(parts):
    def build(p_refs, r_refs, sems):
        x, y, c, _ = _place()
        return [_remote(p, r, sems, t, (x, y, 1 - c)) for t, (p, r) in enumerate(zip(p_refs, r_refs))]

    return (tuple(parts), tuple(jax.ShapeDtypeStruct(p.shape, p.dtype) for p in parts), len(parts), build, False)


def _pair_add_in(p_own, r_in, rows=256, cols=SLAB):
    half, P = p_own.shape
    rows = min(rows, half)

    def body(p_ref, r_ref, o_ref):
        o_ref[...] = (p_ref[...].astype(F32) + r_ref[...].astype(F32)).astype(BF16)

    spec = pl.BlockSpec((rows, cols), lambda i, j: (i, j))
    return pl.pallas_call(body, name="pair_add_in", out_shape=jax.ShapeDtypeStruct((half, P), BF16),
                          grid=(half // rows, P // cols), in_specs=[spec, spec], out_specs=spec,
                          compiler_params=_params("parallel", "parallel"))(p_own, r_in)


def _pair_add_up(idx, p_up, r_up, name):
    _, _, R, C = p_up.shape

    def body(idx_ref, p_ref, r_ref, o_ref):
        o_ref[...] = (p_ref[...].astype(F32) + r_ref[...].astype(F32)).astype(BF16)

    grid_spec = pltpu.PrefetchScalarGridSpec(
        num_scalar_prefetch=1, grid=(N_CHIPS,),
        in_specs=[pl.BlockSpec((1, 1, R, C), lambda j, idx: (j, idx[0], 0, 0)),
                  pl.BlockSpec((1, 1, R, C), lambda j, idx: (j, 0, 0, 0))],
        out_specs=pl.BlockSpec((1, 1, R, C), lambda j, idx: (j, 0, 0, 0)))
    return pl.pallas_call(body, name=name, out_shape=jax.ShapeDtypeStruct((N_CHIPS, 1, R, C), BF16), grid_spec=grid_spec,
                          compiler_params=_params("parallel"))(idx, p_up, r_up)


def _slab_exchange(qs):
    n = len(qs)

    def out_shape(q):
        if q.ndim == 2:
            return jax.ShapeDtypeStruct((3, q.shape[0], SLAB), BF16)
        return jax.ShapeDtypeStruct((3,) + q.shape[1:], BF16)

    def build(q_refs, r_refs, sems):
        _, _, c, others = _place()
        copies = []
        for r, chip in enumerate(others):
            ci = _chip_index(chip)
            for t in range(n):
                src = q_refs[t].at[:, pl.ds(ci * SLAB, SLAB)] if len(q_refs[t].shape) == 2 else q_refs[t].at[ci]
                copies.append(_remote(src, r_refs[t].at[r], sems, 3 * t + r, (*chip, c)))
        return copies

    return (tuple(qs), tuple(out_shape(q) for q in qs), 3 * n, build, False)


def _slab_add_in(idx, q_in, r2_in, rows=128):
    half = q_in.shape[0]
    rows = min(rows, half)
    nrb = half // rows

    def body(idx_ref, q_ref, r_ref, o_ref):
        o_ref[...] = ((q_ref[...].astype(F32) + r_ref[0].astype(F32)) + r_ref[1].astype(F32)) + r_ref[2].astype(F32)

    grid_spec = pltpu.PrefetchScalarGridSpec(
        num_scalar_prefetch=1, grid=(nrb,),
        in_specs=[pl.BlockSpec((rows, SLAB), lambda i, idx: (i, idx[1])),
                  pl.BlockSpec((3, rows, SLAB), lambda i, idx: (0, i, 0))],
        out_specs=pl.BlockSpec((rows, SLAB), lambda i, idx: (idx[0] * nrb + i, 0)))
    return pl.pallas_call(body, name="slab_add_in", out_shape=jax.ShapeDtypeStruct((2 * half, SLAB), F32),
                          grid_spec=grid_spec, compiler_params=_params("parallel"))(idx, q_in, r2_in)


def _slab_add_up(idx, q_up, r2_up, name, rows=128):
    _, _, R, C = q_up.shape
    rows = min(rows, R)
    nrb = R // rows

    def body(idx_ref, q_ref, r_ref, o_ref):
        o_ref[...] = ((q_ref[0, 0].astype(F32) + r_ref[0, 0].astype(F32)) + r_ref[1, 0].astype(F32)) + r_ref[2, 0].astype(F32)

    grid_spec = pltpu.PrefetchScalarGridSpec(
        num_scalar_prefetch=1, grid=(nrb,),
        in_specs=[pl.BlockSpec((1, 1, rows, C), lambda i, idx: (idx[1], 0, i, 0)),
                  pl.BlockSpec((3, 1, rows, C), lambda i, idx: (0, 0, i, 0))],
        out_specs=pl.BlockSpec((rows, C), lambda i, idx: (idx[0] * nrb + i, 0)))
    return pl.pallas_call(body, name=name, out_shape=jax.ShapeDtypeStruct((2 * R, C), F32), grid_spec=grid_spec,
                          compiler_params=_params("parallel"))(idx, q_up, r2_up)


def _pair_share(gs):
    def build(_, g_refs, sems):
        x, y, c, _ = _place()
        copies = []
        for t, g in enumerate(g_refs):
            rows = g.shape[0] // 2
            mine = g.at[pl.ds(c * rows, rows), :]
            copies.append(_remote(mine, mine, sems, t, (x, y, 1 - c)))
        return copies

    return (tuple(gs), tuple(jax.ShapeDtypeStruct(g.shape, g.dtype) for g in gs), len(gs), build, True)


def _all_reduce_small(packed):
    R, C = packed.shape
    N_DEV = 2 * N_CHIPS

    def body(x_ref, out_ref, all_ref, send_sems, recv_sems, local_sem):
        x, y, c, others = _place()
        me, sib = (x, y, c), (x, y, 1 - c)
        sems = (send_sems, recv_sems)

        def rows(px, py, pc):
            return all_ref.at[4 * px + 2 * py + pc]

        mine = pltpu.make_async_copy(x_ref, rows(*me), local_sem)
        mine.start()
        first = [_remote(x_ref, rows(*me), sems, 0, sib)]
        first += [_remote(x_ref, rows(*me), sems, 1 + j, (*chip, c)) for j, chip in enumerate(others)]
        for cp in first:
            cp.start()
        passed = [_remote(rows(*chip, c), rows(*chip, c), sems, 4 + j, sib) for j, chip in enumerate(others)]
        for j, chip in enumerate(others):
            _remote(rows(*chip, c), rows(*chip, c), sems, 1 + j, me).wait_recv()
            passed[j].start()
        _remote(rows(*sib), rows(*sib), sems, 0, me).wait_recv()
        for j, chip in enumerate(others):
            _remote(rows(*chip, 1 - c), rows(*chip, 1 - c), sems, 4 + j, me).wait_recv()
        for cp in first + passed:
            cp.wait_send()
        mine.wait()
        total = all_ref[0]
        for d in range(1, N_DEV):
            total = total + all_ref[d]
        out_ref[...] = total

    vmem = pl.BlockSpec(memory_space=pltpu.VMEM)
    return pl.pallas_call(
        body, name="all_reduce_small", out_shape=jax.ShapeDtypeStruct((R, C), F32),
        in_specs=[vmem], out_specs=vmem,
        scratch_shapes=[pltpu.VMEM((N_DEV, R, C), F32), pltpu.SemaphoreType.DMA((7,)), pltpu.SemaphoreType.DMA((7,)),
                        pltpu.SemaphoreType.DMA],
        compiler_params=pltpu.CompilerParams(vmem_limit_bytes=VMEM_LIMIT),
    )(packed)


SMALL_NAMES = ("w_spatial", "b_spatial", "norm_g", "gmlp_ln_g", "gmlp_ln_b", "final_norm_g", "attn_sink")


def _pack_small(parts, extra=None):
    blocks = []
    for n in SMALL_NAMES:
        flat = parts[n].reshape(-1).astype(F32)
        rows = -(-flat.shape[0] // (8 * LANES)) * 8
        flat = jnp.pad(flat, (0, rows * LANES - flat.shape[0]))
        blocks.append(flat.reshape(rows, LANES))
    blocks.append(jnp.zeros((8, LANES), F32) if extra is None else extra)
    return jnp.concatenate(blocks, axis=0)


def _unpack_small(packed, shapes):
    out, r = {}, 0
    for n in SMALL_NAMES:
        size = 1
        for s in shapes[n]:
            size *= s
        rows = -(-size // (8 * LANES)) * 8
        out[n] = packed[r:r + rows].reshape(-1)[:size].reshape(shapes[n])
        r += rows
    return out, packed[r:r + 8]


def kernel(x, positions, norm_g, w_in, attn_sink, gmlp_ln_g, gmlp_ln_b, w_spatial, b_spatial, w_up_attn, w_up_gmlp, w_out, final_norm_g, loss_target, m_norm_g, m_w_in, m_attn_sink, m_gmlp_ln_g, m_gmlp_ln_b, m_w_spatial, m_b_spatial, m_w_up_attn, m_w_up_gmlp, m_w_out, m_final_norm_g, v_norm_g, v_w_in, v_attn_sink, v_gmlp_ln_g, v_gmlp_ln_b, v_w_spatial, v_b_spatial, v_w_up_attn, v_w_up_gmlp, v_w_out, v_final_norm_g):
    xs, tgt = x[0], loss_target[0]
    mx, my, mc = lax.axis_index("x"), lax.axis_index("y"), lax.axis_index("c")
    idx = jnp.stack([mc, 2 * mx + my]).astype(jnp.int32)
    own, x_nbr, y_nbr, diag = 2 * mx + my, 2 * (1 - mx) + my, 2 * mx + (1 - my), 2 * (1 - mx) + (1 - my)
    tile_cols = jnp.stack([2 * own, 2 * own + 1, 2 * x_nbr, 2 * y_nbr, 2 * x_nbr + 1, 2 * y_nbr + 1, 2 * diag,
                           2 * diag + 1]).astype(jnp.int32)

    square = (D_MODEL, D_MODEL)
    w_full = _cast_into(idx, w_in[0], (D_MODEL, PROJ_WIDTH), "cast_w_in")
    ups = [_cast_into(idx, w_up_attn[0], square, "cast_w_up_attn"), _cast_into(idx, w_up_gmlp[0], square, "cast_w_up_gmlp"),
           _cast_into(idx, w_out[0], square, "cast_w_out")]

    tables = _rope_tables(positions[0])
    bias = _band_bias()
    sink = attn_sink[0]
    ws, bs_t = w_spatial[0], b_spatial[0].T
    h = _rms_fwd(xs, norm_g)
    proj, w_full, wua = _proj_gather(tile_cols, h, w_full, comm=_gather_squares([(ups[0], True)]))
    attn, a_in, wua, wug = _attn_fwd(proj, tables, bias, sink, comm=_gather_squares([(wua, False), (ups[1], True)]))
    b_in, wug = _gmlp_fwd(proj, ws, bs_t, gmlp_ln_g, gmlp_ln_b, comm=_gather_squares([(wug, False)]))
    y_a, wo = _matmul(a_in, wua, mode="nn", out_dtype=F32, name="up_attn", comm=_gather_squares([(ups[2], True)]))
    y_b, wo = _matmul(b_in, wug, mode="nn", out_dtype=F32, name="up_gmlp", comm=_gather_squares([(wo, False)]))
    merged = _merge_fwd(proj, y_a, y_b)
    x2 = _matmul(merged, wo, mode="nn", out_dtype=F32, name="out_proj", residual=xs, tk=2048)
    dx2, dx2_b, d_fg, loss_part = _loss_head(x2, tgt, final_norm_g.reshape(1, D_MODEL))

    dmerged = _matmul(dx2_b, wo, mode="nt", out_dtype=F32, name="d_merged")
    dy_a, dy_b, d_gates = _merge_bwd(proj, y_a, y_b, dmerged)

    p_ups = [_matmul(a_in, dy_a, mode="tn", out_dtype=BF16, name="dw_up_attn"),
             _matmul(b_in, dy_b, mode="tn", out_dtype=BF16, name="dw_up_gmlp"),
             _matmul(merged, dx2_b, mode="tn", out_dtype=BF16, name="dw_out")]
    p_ups = [p.reshape(N_CHIPS, 2, UP_ROWS // 2, D_MODEL) for p in p_ups]
    da_in, *r1_ups = _matmul(dy_a, wua, mode="nt", out_dtype=F32, name="d_a_in", comm=_pair_halves(p_ups))
    q_ups = [_pair_add_up(idx, p, r, "pair_add_up%d" % t) for t, (p, r) in enumerate(zip(p_ups, r1_ups))]
    db_in = _matmul(dy_b, wug, mode="nt", out_dtype=F32, name="d_b_in")
    dq, dk, dv, dga, d_sink = _attn_bwd(proj, tables, bias, sink, attn, da_in)
    d_gmlp, d_ws, d_bs, d_lg, d_lb = _gmlp_bwd(proj, ws, bs_t, gmlp_ln_g, gmlp_ln_b, db_in)
    dproj = jnp.concatenate([dq, dk, dv, dga, d_gmlp, d_gates], axis=1)

    half = D_MODEL // 2
    h_sib = lax.dynamic_slice(h, (0, (1 - mc) * half), (h.shape[0], half))
    h_own = lax.dynamic_slice(h, (0, mc * half), (h.shape[0], half))
    p_sib, *r2_ups = _matmul(h_sib, dproj, mode="tn", out_dtype=BF16, name="dw_in_sibling", comm=_slab_exchange(q_ups))
    p_own, r1_in = _matmul(h_own, dproj, mode="tn", out_dtype=BF16, name="dw_in_own", comm=_send_whole([p_sib]))
    q_in = _pair_add_in(p_own, r1_in)
    dh, r2_in = _matmul(dproj, w_full, mode="nt", out_dtype=F32, name="d_h", comm=_slab_exchange([q_in]), tk=SLAB // 2)

    g_half = [_slab_add_in(idx, q_in, r2_in)] + [_slab_add_up(idx, q, r, "slab_add_up%d" % t)
                                                 for t, (q, r) in enumerate(zip(q_ups, r2_ups))]
    grad_x, d_ng, *g_big = _rms_bwd(xs, dh, dx2, norm_g, comm=_pair_share(g_half))

    small_shapes = {"w_spatial": w_spatial.shape, "b_spatial": b_spatial.shape, "norm_g": norm_g.shape,
                    "gmlp_ln_g": gmlp_ln_g.shape, "gmlp_ln_b": gmlp_ln_b.shape, "final_norm_g": final_norm_g.shape,
                    "attn_sink": attn_sink.shape}
    d_small = {"w_spatial": d_ws, "b_spatial": d_bs, "norm_g": d_ng, "gmlp_ln_g": d_lg, "gmlp_ln_b": d_lb,
               "final_norm_g": d_fg, "attn_sink": d_sink[:, 0, :HEADS_PER_STEP]}
    g_small = _all_reduce_small(_pack_small(d_small, loss_part))

    w_small = _pack_small(dict(w_spatial=w_spatial, b_spatial=b_spatial, norm_g=norm_g, gmlp_ln_g=gmlp_ln_g,
                               gmlp_ln_b=gmlp_ln_b, final_norm_g=final_norm_g, attn_sink=attn_sink))
    m_small = _pack_small(dict(w_spatial=m_w_spatial, b_spatial=m_b_spatial, norm_g=m_norm_g, gmlp_ln_g=m_gmlp_ln_g,
                               gmlp_ln_b=m_gmlp_ln_b, final_norm_g=m_final_norm_g, attn_sink=m_attn_sink))
    v_small = _pack_small(dict(w_spatial=v_w_spatial, b_spatial=v_b_spatial, norm_g=v_norm_g, gmlp_ln_g=v_gmlp_ln_g,
                               gmlp_ln_b=v_gmlp_ln_b, final_norm_g=v_final_norm_g, attn_sink=v_attn_sink))
    small_out = _adamw(w_small, g_small, m_small, v_small, "adamw_small", rows=w_small.shape[0])
    big = {
        "w_in": _adamw(w_in[0], g_big[0], m_w_in[0], v_w_in[0], "adamw_w_in"),
        "w_up_attn": _adamw(w_up_attn[0], g_big[1], m_w_up_attn[0], v_w_up_attn[0], "adamw_w_up_attn"),
        "w_up_gmlp": _adamw(w_up_gmlp[0], g_big[2], m_w_up_gmlp[0], v_w_up_gmlp[0], "adamw_w_up_gmlp"),
        "w_out": _adamw(w_out[0], g_big[3], m_w_out[0], v_w_out[0], "adamw_w_out"),
    }

    order = ("norm_g", "w_in", "attn_sink", "gmlp_ln_g", "gmlp_ln_b", "w_spatial", "b_spatial", "w_up_attn", "w_up_gmlp",
             "w_out", "final_norm_g")
    outs = []
    loss = None
    for kind in range(4):
        small, extra = _unpack_small(small_out[kind], small_shapes)
        if kind == 0:
            loss = extra[0, 0]
        for n in order:
            outs.append(big[n][kind][None] if n in big else small[n])
    return (loss, grad_x[None], *outs)
```

```python
import functools

import jax
import jax.numpy as jnp
from jax import lax
from jax.experimental import pallas as pl
from jax.experimental.pallas import tpu as pltpu

F32 = jnp.float32
BF16 = jnp.bfloat16

D_MODEL = 4096
N_Q_HEADS = 64
N_KV_HEADS = 8
HEAD_DIM = 64
Q_PER_KV = N_Q_HEADS // N_KV_HEADS
ATTN_WIDTH = N_Q_HEADS * HEAD_DIM
KV_WIDTH = N_KV_HEADS * HEAD_DIM
WINDOW = 128
BLOCK = 128
ROPE_THETA = 500000.0
ROPE_DIM = HEAD_DIM // 4
ROPE_HALF = ROPE_DIM // 2
GMLP_WIDTH = D_MODEL
GMLP_GROUPS = 8
GMLP_GROUP_DIM = GMLP_WIDTH // GMLP_GROUPS
GMLP_CHUNK = 128
NORM_EPS = 1e-5
LN_EPS = 1e-5

PROJ_SIZES = (ATTN_WIDTH, KV_WIDTH, KV_WIDTH, ATTN_WIDTH, GMLP_WIDTH, GMLP_WIDTH, GMLP_WIDTH, D_MODEL, D_MODEL)
PROJ_WIDTH = sum(PROJ_SIZES)
OFF_Q, OFF_K, OFF_V, OFF_GA, OFF_U, OFF_VG, OFF_GB, OFF_MA, OFF_MB = (
    sum(PROJ_SIZES[:i]) for i in range(len(PROJ_SIZES)))

ADAM_LR = 0.001
ADAM_B1 = 0.9
ADAM_B2 = 0.999
ADAM_EPS = 1e-08
ADAM_WD = 0.01
ADAM_STEP = 10

N_CHIPS = 4
SLAB = PROJ_WIDTH // N_CHIPS
UP_ROWS = D_MODEL // N_CHIPS
LANES = 128
COL_BLK = 1024
HEADS_PER_STEP = 2 * Q_PER_KV
VMEM_LIMIT = 56 * 1024 * 1024

MESH = pl.DeviceIdType.MESH
HBM_SPEC = pl.BlockSpec(memory_space=pltpu.HBM)
SMEM_SPEC = pl.BlockSpec(memory_space=pltpu.SMEM)


def _params(*sem):
    return pltpu.CompilerParams(dimension_semantics=sem, vmem_limit_bytes=VMEM_LIMIT)


def _sigmoid(x):
    return jax.nn.sigmoid(x)


def _gelu(x):
    return jax.nn.gelu(x, approximate=True)


def _gelu_both(x):
    c = 0.7978845608028654
    x2 = x * x
    t = jnp.tanh(c * (x + 0.044715 * x * x2))
    cdf = 0.5 * (1.0 + t)
    return x * cdf, cdf + 0.5 * x * (1.0 - t * t) * c * (1.0 + 3 * 0.044715 * x2)


def _matmul(a, b, *, mode, out_dtype, name, residual=None, comm=None, tm=1024, tn=1024, tk=4096):
    if mode == "nn":
        (M, K), N = a.shape, b.shape[1]
    elif mode == "nt":
        (M, K), N = a.shape, b.shape[0]
    else:
        (K, M), N = a.shape, b.shape[1]
    tm, tn, tk = min(tm, M), min(tn, N), min(tk, K)
    assert M % tm == 0 and N % tn == 0 and K % tk == 0
    ni, nj, nk = M // tm, N // tn, K // tk
    if mode == "nn":
        a_spec = pl.BlockSpec((tm, tk), lambda i, j, k: (i, k))
        b_spec = pl.BlockSpec((tk, tn), lambda i, j, k: (k, j))
        dims = (((1,), (0,)), ((), ()))
    elif mode == "nt":
        a_spec = pl.BlockSpec((tm, tk), lambda i, j, k: (i, k))
        b_spec = pl.BlockSpec((tn, tk), lambda i, j, k: (j, k))
        dims = (((1,), (1,)), ((), ()))
    else:
        a_spec = pl.BlockSpec((tk, tm), lambda i, j, k: (k, i))
        b_spec = pl.BlockSpec((tk, tn), lambda i, j, k: (k, j))
        dims = (((0,), (0,)), ((), ()))
    o_spec = pl.BlockSpec((tm, tn), lambda i, j, k: (i, j))
    n_res = 1 if residual is not None else 0
    c_arrays, c_shapes, c_sems, c_build, c_alias = _comm_fields(comm)
    n_cin, n_cout = len(c_arrays), len(c_shapes)

    def body(*refs):
        a_ref, b_ref = refs[0], refs[1]
        r_ref = refs[2] if n_res else None
        cin = refs[2 + n_res:2 + n_res + n_cin]
        o_ref = refs[2 + n_res + n_cin]
        cout = refs[3 + n_res + n_cin:3 + n_res + n_cin + n_cout]
        n_acc = 1 if nk > 1 else 0
        acc = refs[3 + n_res + n_cin + n_cout] if n_acc else None
        sems = refs[3 + n_acc + n_res + n_cin + n_cout:]
        i, j, k = pl.program_id(0), pl.program_id(1), pl.program_id(2)
        if comm is not None:
            _start_when((i == 0) & (j == 0) & (k == 0), c_build(cin, cout, sems))

        def product():
            return lax.dot_general(a_ref[...], b_ref[...], dims, preferred_element_type=F32)

        def finish(r):
            if n_res:
                r = r + r_ref[...]
            o_ref[...] = r.astype(out_dtype)

        if nk == 1:
            finish(product())
        else:
            @pl.when(k == 0)
            def _():
                acc[...] = product()

            @pl.when((k > 0) & (k < nk - 1))
            def _():
                acc[...] += product()

            @pl.when(k == nk - 1)
            def _():
                finish(acc[...] + product())

        if comm is not None:
            _wait_when((i == ni - 1) & (j == nj - 1) & (k == nk - 1), c_build(cin, cout, sems))

    args = (a, b) + ((residual,) if n_res else ()) + tuple(c_arrays)
    in_specs = [a_spec, b_spec] + ([o_spec] if n_res else []) + [HBM_SPEC] * n_cin
    scratch = ([pltpu.VMEM((tm, tn), F32)] if nk > 1 else []) + _comm_scratch(c_sems)
    sem = ("arbitrary",) * 3 if comm is not None else ("parallel", "parallel", "arbitrary")
    aliases = {2 + n_res + t: 1 + t for t in range(n_cin)} if c_alias else {}
    out = pl.pallas_call(
        body, name=name, out_shape=(jax.ShapeDtypeStruct((M, N), out_dtype),) + tuple(c_shapes),
        grid=(ni, nj, nk), in_specs=in_specs, out_specs=(o_spec,) + (HBM_SPEC,) * n_cout,
        scratch_shapes=scratch, input_output_aliases=aliases, compiler_params=_params(*sem),
    )(*args)
    return out if comm is not None else out[0]


def _cast_into(idx, w, full_shape, name, rows=64):
    R, C = w.shape
    rows = min(rows, R)
    nrb = R // rows
    by_cols = C != full_shape[1]

    def body(idx_ref, w_ref, o_ref):
        o_ref[...] = w_ref[...].astype(BF16)

    out_map = (lambda i, idx: (i, idx[1])) if by_cols else (lambda i, idx: (idx[1] * nrb + i, 0))
    grid_spec = pltpu.PrefetchScalarGridSpec(
        num_scalar_prefetch=1, grid=(nrb,), in_specs=[pl.BlockSpec((rows, C), lambda i, idx: (i, 0))],
        out_specs=pl.BlockSpec((rows, C), out_map))
    return pl.pallas_call(body, name=name, out_shape=jax.ShapeDtypeStruct(full_shape, BF16), grid_spec=grid_spec,
                          compiler_params=_params("parallel"))(idx, w)


def _rms_fwd(x, g, rows=256):
    S, D = x.shape
    rows = min(rows, S)

    def body(x_ref, g_ref, h_ref):
        xv = x_ref[...]
        ms = jnp.mean(xv * xv, axis=-1, keepdims=True)
        h_ref[...] = (xv * lax.rsqrt(ms + NORM_EPS) * g_ref[...]).astype(BF16)

    spec = pl.BlockSpec((rows, D), lambda i: (i, 0))
    return pl.pallas_call(body, name="rms_fwd", out_shape=jax.ShapeDtypeStruct((S, D), BF16), grid=(S // rows,),
                          in_specs=[spec, pl.BlockSpec((1, D), lambda i: (0, 0))], out_specs=spec,
                          compiler_params=_params("parallel"))(x, g)


def _rope_tables(positions):
    inv_freq = ROPE_THETA ** (-jnp.arange(ROPE_HALF, dtype=F32) * 2.0 / ROPE_DIM)
    ang = positions.astype(F32)[:, None] * inv_freq
    cos, sin = jnp.cos(ang), jnp.sin(ang)
    S = positions.shape[0]
    rest = HEAD_DIM - ROPE_DIM
    zeros_h, zeros_r = jnp.zeros((S, ROPE_HALF), F32), jnp.zeros((S, rest), F32)
    c = jnp.concatenate([cos, cos, jnp.ones((S, rest), F32)], axis=1)
    s1 = jnp.concatenate([-sin, zeros_h, zeros_r], axis=1)
    s2 = jnp.concatenate([zeros_h, sin, zeros_r], axis=1)
    reps = LANES // HEAD_DIM
    return jnp.tile(c, (1, reps)), jnp.tile(s1, (1, reps)), jnp.tile(s2, (1, reps))


def _rope(t, c, s1, s2, sign):
    if sign != 1.0:
        s1, s2 = sign * s1, sign * s2
    blocks = []
    for b in range(t.shape[1] // LANES):
        blk = t[:, b * LANES:(b + 1) * LANES]
        up = pltpu.roll(blk, LANES - ROPE_HALF, 1)
        down = pltpu.roll(blk, ROPE_HALF, 1)
        blocks.append(blk * c + up * s1 + down * s2)
    return blocks[0] if len(blocks) == 1 else jnp.concatenate(blocks, axis=1)


def _attn_specs(nb):
    last = nb - 1

    def cur(i):
        return jnp.minimum(i, last)

    def prev(i):
        return jnp.maximum(jnp.minimum(i, last) - 1, 0)

    kq, kk, kv, kg = OFF_Q // COL_BLK, OFF_K // LANES, OFF_V // LANES, OFF_GA // COL_BLK
    wide = lambda off: pl.BlockSpec((BLOCK, COL_BLK), lambda p, i: (cur(i), off + p))
    kv_cur = lambda off: pl.BlockSpec((BLOCK, LANES), lambda p, i: (cur(i), off + p))
    kv_prev = lambda off: pl.BlockSpec((BLOCK, LANES), lambda p, i: (prev(i), off + p))
    tab_cur = pl.BlockSpec((BLOCK, LANES), lambda p, i: (cur(i), 0))
    tab_prev = pl.BlockSpec((BLOCK, LANES), lambda p, i: (prev(i), 0))
    proj_specs = [wide(kq), kv_cur(kk), kv_prev(kk), kv_cur(kv), kv_prev(kv), wide(kg)]
    table_specs = [tab_cur] * 3 + [tab_prev] * 3
    bias_spec = pl.BlockSpec((1, BLOCK, QCOLS), lambda p, i: (jnp.minimum(i, 1), 0, 0))
    return proj_specs, table_specs + [bias_spec], cur, prev


PAIRS = Q_PER_KV // 2
QCOLS = PAIRS * BLOCK
NT = (((1,), (1,)), ((), ()))
TN = (((0,), (0,)), ((), ()))


def _band_bias():
    first = jnp.where(jnp.arange(BLOCK)[:, None] > (jnp.arange(QCOLS) % BLOCK)[None, :], -1e30, 0.0)
    return jnp.stack([first, jnp.zeros_like(first)]).astype(F32)


def _from_prev():
    a = lax.broadcasted_iota(jnp.int32, (BLOCK, QCOLS), 1) & (BLOCK - 1)
    return lax.broadcasted_iota(jnp.int32, (BLOCK, QCOLS), 0) > a


def _pair_rows(ref, kvh):
    base = kvh * Q_PER_KV * HEAD_DIM
    return jnp.concatenate([ref[:, base + b * LANES: base + (b + 1) * LANES] for b in range(PAIRS)], axis=0).astype(BF16)


def _unpair_rows(ref, kvh, val_t):
    base = kvh * Q_PER_KV * HEAD_DIM
    for b in range(PAIRS):
        ref[:, base + b * LANES: base + (b + 1) * LANES] = val_t[:, b * BLOCK:(b + 1) * BLOCK].T.astype(ref.dtype)


def _split_heads(t, kvh):
    lane = lax.broadcasted_iota(jnp.int32, t.shape, 1)
    other = pltpu.roll(t, HEAD_DIM, 1)
    lo = jnp.where(lane < HEAD_DIM, t if kvh == 0 else other, 0.0)
    hi = jnp.where(lane >= HEAD_DIM, other if kvh == 0 else t, 0.0)
    return lo.astype(BF16), hi.astype(BF16)


def _join_heads(even, odd, kvh):
    lane = lax.broadcasted_iota(jnp.int32, even.shape, 1)
    lo = jnp.where(lane < HEAD_DIM, even, 0.0) + pltpu.roll(jnp.where(lane >= HEAD_DIM, odd, 0.0), HEAD_DIM, 1)
    return lo if kvh == 0 else pltpu.roll(lo, HEAD_DIM, 1)


def _sink_row(sink_ref, pair, kvh, odd):
    first = (pair * 2 + kvh) * Q_PER_KV + odd
    return jnp.concatenate([jnp.full((1, BLOCK), sink_ref[first + 2 * b], F32) for b in range(PAIRS)], axis=1)


def _band_t(ap, ac, b, from_prev):
    return jnp.where(from_prev, lax.dot_general(ap, b, NT, preferred_element_type=F32),
                     lax.dot_general(ac, b, NT, preferred_element_type=F32))


def _softmax_t(st, sink_row):
    m = jnp.maximum(jnp.max(st, axis=0, keepdims=True), sink_row)
    e = jnp.exp(st - m)
    es = jnp.exp(sink_row - m)
    inv = 1.0 / (jnp.sum(e, axis=0, keepdims=True) + es)
    return e * inv, es * inv


def _roped_keys(k_ref, kp_ref, tabs):
    c, s1, s2, cp, s1p, s2p = (t[...] for t in tabs)
    k_cur = _rope(k_ref[...].astype(F32), c, s1, s2, 1.0)
    k_prev = _rope(kp_ref[...].astype(F32), cp, s1p, s2p, 1.0)
    return k_prev, k_cur


SCALE = HEAD_DIM ** -0.5


def _attn_fwd(proj, tables, bias, sink, comm=None):
    S = proj.shape[0]
    nb = S // BLOCK
    npairs = N_KV_HEADS // 2
    proj_specs, table_specs, _, _ = _attn_specs(nb)
    c_arrays, c_shapes, c_sems, c_build, c_alias = _comm_fields(comm)
    n_in = len(proj_specs) + len(table_specs) + 1

    def body(*refs):
        (q_ref, k_ref, kp_ref, v_ref, vp_ref, ga_ref, c_ref, s1_ref, s2_ref, cp_ref, s1p_ref, s2p_ref, bias_ref,
         sink_ref) = refs[:n_in]
        cin = refs[n_in:n_in + len(c_arrays)]
        attn_ref, ain_ref = refs[n_in + len(c_arrays):n_in + len(c_arrays) + 2]
        cout = refs[n_in + len(c_arrays) + 2:n_in + len(c_arrays) + 2 + len(c_shapes)]
        qr_scr = refs[n_in + len(c_arrays) + 2 + len(c_shapes)]
        sems = refs[n_in + len(c_arrays) + 3 + len(c_shapes):]
        pair, i = pl.program_id(0), pl.program_id(1)
        if comm is not None:
            _start_when((pair == 0) & (i == 0), c_build(cin, cout, sems))
        tabs = (c_ref, s1_ref, s2_ref, cp_ref, s1p_ref, s2p_ref)
        qr_scr[...] = _rope(q_ref[...].astype(F32), c_ref[...], s1_ref[...], s2_ref[...], 1.0) * SCALE
        k_prev, k_cur = _roped_keys(k_ref, kp_ref, tabs)
        v_prev, v_cur = vp_ref[...].astype(F32), v_ref[...].astype(F32)
        from_prev = _from_prev()
        heads = [(kvh, odd) for kvh in range(2) for odd in range(2)]
        qps = [_pair_rows(qr_scr, kvh) for kvh in range(2)]
        kpss, kcss = [_split_heads(k_prev, kvh) for kvh in range(2)], [_split_heads(k_cur, kvh) for kvh in range(2)]
        scores = [_band_t(kpss[kvh][odd], kcss[kvh][odd], qps[kvh], from_prev) + bias_ref[0] for kvh, odd in heads]
        probs = []
        for n, (kvh, odd) in enumerate(heads):
            pb = _softmax_t(scores[n], _sink_row(sink_ref, pair, kvh, odd))[0].astype(BF16)
            probs.append((jnp.where(from_prev, pb, 0), jnp.where(from_prev, 0, pb)))
        for kvh in range(2):
            vps, vcs = _split_heads(v_prev, kvh), _split_heads(v_cur, kvh)
            o_t = None
            for odd in range(2):
                p_prev, p_cur = probs[2 * kvh + odd]
                part = (lax.dot_general(vps[odd], p_prev, TN, preferred_element_type=F32)
                        + lax.dot_general(vcs[odd], p_cur, TN, preferred_element_type=F32))
                o_t = part if o_t is None else o_t + part
            _unpair_rows(attn_ref, kvh, o_t)
        ga = ga_ref[...].astype(F32)
        ain_ref[...] = (attn_ref[...] * (ga * _sigmoid(ga))).astype(BF16)
        if comm is not None:
            _wait_when((pair == npairs - 1) & (i == nb - 1), c_build(cin, cout, sems))

    out_spec = pl.BlockSpec((BLOCK, COL_BLK), lambda p, i: (i, p))
    aliases = {n_in + t: 2 + t for t in range(len(c_arrays))} if c_alias else {}
    out = pl.pallas_call(
        body, name="attn_fwd",
        out_shape=(jax.ShapeDtypeStruct((S, ATTN_WIDTH), F32), jax.ShapeDtypeStruct((S, ATTN_WIDTH), BF16)) + tuple(c_shapes),
        grid=(npairs, nb), in_specs=proj_specs + table_specs + [SMEM_SPEC] + [HBM_SPEC] * len(c_arrays),
        out_specs=(out_spec, out_spec) + (HBM_SPEC,) * len(c_shapes),
        scratch_shapes=[pltpu.VMEM((BLOCK, COL_BLK), F32)] + _comm_scratch(c_sems), input_output_aliases=aliases,
        compiler_params=_params("arbitrary", "arbitrary"),
    )(*([proj] * 6), *tables, *tables, bias, sink, *c_arrays)
    return out


def _attn_bwd(proj, tables, bias, sink, attn, da_in):
    S = proj.shape[0]
    nb = S // BLOCK
    proj_specs, table_specs, cur, prev = _attn_specs(nb)
    wide_cur = pl.BlockSpec((BLOCK, COL_BLK), lambda p, i: (cur(i), p))
    kv_out = pl.BlockSpec((BLOCK, LANES), lambda p, i: (jnp.maximum(i - 1, 0), p))

    def body(q_ref, k_ref, kp_ref, v_ref, vp_ref, ga_ref, c_ref, s1_ref, s2_ref, cp_ref, s1p_ref, s2p_ref, bias_ref,
             sink_ref, attn_ref, da_ref, dq_ref, dk_ref, dv_ref, dga_ref, dsink_ref,
             qr_scr, do_scr, dq_scr, dk_scr, dv_scr, carry_k, carry_v, sink_acc):
        pair, i = pl.program_id(0), pl.program_id(1)
        tabs = (c_ref, s1_ref, s2_ref, cp_ref, s1p_ref, s2p_ref)

        @pl.when(i < nb)
        def _():
            ga = ga_ref[...].astype(F32)
            sg = _sigmoid(ga)
            da = da_ref[...]
            dga_ref[...] = (da * attn_ref[...] * (sg * (1.0 + ga * (1.0 - sg)))).astype(BF16)
            do_scr[...] = da * (ga * sg)
            qr_scr[...] = _rope(q_ref[...].astype(F32), c_ref[...], s1_ref[...], s2_ref[...], 1.0) * SCALE
            k_prev, k_cur = _roped_keys(k_ref, kp_ref, tabs)
            v_prev, v_cur = vp_ref[...].astype(F32), v_ref[...].astype(F32)
            from_prev = _from_prev()
            dkv = [None] * 4
            heads = [(kvh, odd) for kvh in range(2) for odd in range(2)]
            qps, dops = [_pair_rows(qr_scr, kvh) for kvh in range(2)], [_pair_rows(do_scr, kvh) for kvh in range(2)]
            kpss, kcss = [_split_heads(k_prev, kvh) for kvh in range(2)], [_split_heads(k_cur, kvh) for kvh in range(2)]
            vpss, vcss = [_split_heads(v_prev, kvh) for kvh in range(2)], [_split_heads(v_cur, kvh) for kvh in range(2)]
            scores = [_band_t(kpss[kvh][odd], kcss[kvh][odd], qps[kvh], from_prev) + bias_ref[0] for kvh, odd in heads]
            dps = [_band_t(vpss[kvh][odd], vcss[kvh][odd], dops[kvh], from_prev) for kvh, odd in heads]
            masked = []
            for n, (kvh, odd) in enumerate(heads):
                p_t, p_sink = _softmax_t(scores[n], _sink_row(sink_ref, pair, kvh, odd))
                delta = jnp.sum(p_t * dps[n], axis=0, keepdims=True)
                ds = (p_t * (dps[n] - delta)).astype(BF16)
                pb = p_t.astype(BF16)
                masked.append((jnp.where(from_prev, ds, 0), jnp.where(from_prev, 0, ds),
                               jnp.where(from_prev, pb, 0), jnp.where(from_prev, 0, pb)))
                contrib = -p_sink * delta

                @pl.when(i == 0)
                def _():
                    sink_acc[n] = contrib

                @pl.when(i > 0)
                def _():
                    sink_acc[n] += contrib

            for kvh in range(2):
                qp, dop = qps[kvh], dops[kvh]
                dq_t, halves = None, []
                for odd in range(2):
                    ds_prev, ds_cur, p_prev, p_cur = masked[2 * kvh + odd]
                    part = (lax.dot_general(kpss[kvh][odd], ds_prev, TN, preferred_element_type=F32)
                            + lax.dot_general(kcss[kvh][odd], ds_cur, TN, preferred_element_type=F32))
                    dq_t = part if dq_t is None else dq_t + part
                    halves.append([jnp.dot(ds_prev, qp, preferred_element_type=F32),
                                   jnp.dot(ds_cur, qp, preferred_element_type=F32),
                                   jnp.dot(p_prev, dop, preferred_element_type=F32),
                                   jnp.dot(p_cur, dop, preferred_element_type=F32)])
                _unpair_rows(dq_scr, kvh, dq_t)
                for n in range(4):
                    joined = _join_heads(halves[0][n], halves[1][n], kvh)
                    dkv[n] = joined if dkv[n] is None else dkv[n] + joined
            dk_scr[0:BLOCK, :], dk_scr[BLOCK:2 * BLOCK, :] = dkv[0], dkv[1]
            dv_scr[0:BLOCK, :], dv_scr[BLOCK:2 * BLOCK, :] = dkv[2], dkv[3]

            dq_ref[...] = _rope(dq_scr[...] * SCALE, c_ref[...], s1_ref[...], s2_ref[...], -1.0).astype(BF16)
            dk_prev = _rope(dk_scr[0:BLOCK, :], cp_ref[...], s1p_ref[...], s2p_ref[...], -1.0)
            dk_cur = _rope(dk_scr[BLOCK:2 * BLOCK, :], c_ref[...], s1_ref[...], s2_ref[...], -1.0)
            dv_prev, dv_cur = dv_scr[0:BLOCK, :], dv_scr[BLOCK:2 * BLOCK, :]

            @pl.when(i > 0)
            def _():
                dk_ref[...] = (carry_k[...] + dk_prev).astype(BF16)
                dv_ref[...] = (carry_v[...] + dv_prev).astype(BF16)

            carry_k[...] = dk_cur
            carry_v[...] = dv_cur

        @pl.when(i == nb)
        def _():
            dk_ref[...] = carry_k[...].astype(BF16)
            dv_ref[...] = carry_v[...].astype(BF16)
            lane = lax.broadcasted_iota(jnp.int32, (8, LANES), 1)
            acc = jnp.zeros((8, LANES), F32)
            for kvh in range(2):
                for odd in range(2):
                    for b in range(PAIRS):
                        val = jnp.sum(sink_acc[2 * kvh + odd, :, b * BLOCK:(b + 1) * BLOCK], axis=1, keepdims=True)
                        acc = jnp.where(lane == kvh * Q_PER_KV + 2 * b + odd, val, acc)
            dsink_ref[0] = acc

    return pl.pallas_call(
        body, name="attn_bwd",
        out_shape=(jax.ShapeDtypeStruct((S, ATTN_WIDTH), BF16), jax.ShapeDtypeStruct((S, KV_WIDTH), BF16),
                   jax.ShapeDtypeStruct((S, KV_WIDTH), BF16), jax.ShapeDtypeStruct((S, ATTN_WIDTH), BF16),
                   jax.ShapeDtypeStruct((N_KV_HEADS // 2, 8, LANES), F32)),
        grid=(N_KV_HEADS // 2, nb + 1),
        in_specs=proj_specs + table_specs + [SMEM_SPEC, wide_cur, wide_cur],
        out_specs=(wide_cur, kv_out, kv_out, wide_cur, pl.BlockSpec((1, 8, LANES), lambda p, i: (p, 0, 0))),
        scratch_shapes=[pltpu.VMEM((BLOCK, COL_BLK), F32), pltpu.VMEM((BLOCK, COL_BLK), F32),
                        pltpu.VMEM((BLOCK, COL_BLK), F32), pltpu.VMEM((2 * BLOCK, LANES), F32),
                        pltpu.VMEM((2 * BLOCK, LANES), F32), pltpu.VMEM((BLOCK, LANES), F32),
                        pltpu.VMEM((BLOCK, LANES), F32), pltpu.VMEM((4, 1, QCOLS), F32)],
        compiler_params=_params("arbitrary", "arbitrary"),
    )(*([proj] * 6), *tables, *tables, bias, sink, attn, da_in)


N_PARTS = GMLP_WIDTH // COL_BLK
GROUPS_PER_PART = COL_BLK // GMLP_GROUP_DIM


def _part_specs(off):
    return [pl.BlockSpec((GMLP_CHUNK, COL_BLK), functools.partial(lambda j, i: (i, j), off // COL_BLK + k))
            for k in range(N_PARTS)]


def _group(refs, g):
    lo = (g % GROUPS_PER_PART) * GMLP_GROUP_DIM
    return refs[g // GROUPS_PER_PART][:, lo:lo + GMLP_GROUP_DIM].astype(F32)


def _gmlp_norm_stats(vg_refs, gv_scr, grad_scr=None):
    total = jnp.zeros((GMLP_CHUNK, 1), F32)
    for k in range(N_PARTS):
        vg = vg_refs[k][...].astype(F32)
        if grad_scr is None:
            gv = _gelu(vg)
        else:
            gv, grad_scr[:, k * COL_BLK:(k + 1) * COL_BLK] = _gelu_both(vg)
        gv_scr[:, k * COL_BLK:(k + 1) * COL_BLK] = gv
        total = total + jnp.sum(gv, axis=1, keepdims=True)
    mu = total / GMLP_WIDTH
    xc = gv_scr[...] - mu
    var = jnp.sum(xc * xc, axis=1, keepdims=True) / GMLP_WIDTH
    return mu, lax.rsqrt(var + LN_EPS)


def _tril_bf16(ws_ref, g):
    t = lax.broadcasted_iota(jnp.int32, (GMLP_CHUNK, GMLP_CHUNK), 0)
    s = lax.broadcasted_iota(jnp.int32, (GMLP_CHUNK, GMLP_CHUNK), 1)
    return jnp.where(s <= t, ws_ref[g], 0.0).astype(BF16), s <= t


def _gmlp_fwd(proj, ws, bs_t, lg, lb, comm=None):
    S = proj.shape[0]
    nb = S // GMLP_CHUNK
    c_arrays, c_shapes, c_sems, c_build, c_alias = _comm_fields(comm)
    n_cin, n_cout = len(c_arrays), len(c_shapes)

    def body(*refs):
        u_refs, vg_refs, gb_refs = refs[0:4], refs[4:8], refs[8:12]
        ws_ref, bst_ref, lg_ref, lb_ref = refs[12:16]
        cin, out_ref, cout = refs[16:16 + n_cin], refs[16 + n_cin], refs[17 + n_cin:17 + n_cin + n_cout]
        gv_scr, sems = refs[17 + n_cin + n_cout], refs[18 + n_cin + n_cout:]
        if comm is not None:
            _start_when(pl.program_id(0) == 0, c_build(cin, cout, sems))
        mu, rstd = _gmlp_norm_stats(vg_refs, gv_scr)
        for g in range(GMLP_GROUPS):
            cols = slice(g * GMLP_GROUP_DIM, (g + 1) * GMLP_GROUP_DIM)
            vn = (gv_scr[:, cols] - mu) * rstd * lg_ref[:, cols] + lb_ref[:, cols]
            w, _ = _tril_bf16(ws_ref, g)
            mixed = jnp.dot(w, vn.astype(BF16), preferred_element_type=F32) + bst_ref[:, g:g + 1]
            gb = _group(gb_refs, g)
            out_ref[:, cols] = (_gelu(_group(u_refs, g)) * mixed * (gb * _sigmoid(gb))).astype(BF16)
        if comm is not None:
            _wait_when(pl.program_id(0) == nb - 1, c_build(cin, cout, sems))

    full = lambda shape: pl.BlockSpec(shape, lambda i: tuple(0 for _ in shape))
    aliases = {16 + t: 1 + t for t in range(n_cin)} if c_alias else {}
    out = pl.pallas_call(
        body, name="gmlp_fwd", out_shape=(jax.ShapeDtypeStruct((S, GMLP_WIDTH), BF16),) + tuple(c_shapes), grid=(nb,),
        in_specs=_part_specs(OFF_U) + _part_specs(OFF_VG) + _part_specs(OFF_GB)
        + [full(ws.shape), full(bs_t.shape), full(lg.shape), full(lb.shape)] + [HBM_SPEC] * n_cin,
        out_specs=(pl.BlockSpec((GMLP_CHUNK, GMLP_WIDTH), lambda i: (i, 0)),) + (HBM_SPEC,) * n_cout,
        scratch_shapes=[pltpu.VMEM((GMLP_CHUNK, GMLP_WIDTH), F32)] + _comm_scratch(c_sems), input_output_aliases=aliases,
        compiler_params=_params("arbitrary" if comm is not None else "parallel"),
    )(*([proj] * 12), ws, bs_t, lg, lb, *c_arrays)
    return out if comm is not None else out[0]


def _gmlp_bwd(proj, ws, bs_t, lg, lb, db_in):
    S = proj.shape[0]
    nb = S // GMLP_CHUNK
    W = GMLP_WIDTH

    def body(*refs):
        u_refs, vg_refs, gb_refs = refs[0:4], refs[4:8], refs[8:12]
        ws_ref, bst_ref, lg_ref, lb_ref, db_ref = refs[12:17]
        out_ref, dws_ref, dbs_ref, dlg_ref, dlb_ref = refs[17:22]
        gv_scr, dvh_scr, vgrad_scr = refs[22:]
        i = pl.program_id(0)

        @pl.when(i == 0)
        def _():
            dws_ref[...] = jnp.zeros_like(dws_ref)
            dbs_ref[...] = jnp.zeros_like(dbs_ref)
            dlg_ref[...] = jnp.zeros_like(dlg_ref)
            dlb_ref[...] = jnp.zeros_like(dlb_ref)

        mu, rstd = _gmlp_norm_stats(vg_refs, gv_scr, vgrad_scr)
        sum_dvh = jnp.zeros((GMLP_CHUNK, 1), F32)
        sum_dvh_vh = jnp.zeros((GMLP_CHUNK, 1), F32)
        for g in range(GMLP_GROUPS):
            cols = slice(g * GMLP_GROUP_DIM, (g + 1) * GMLP_GROUP_DIM)
            vhat = (gv_scr[:, cols] - mu) * rstd
            vn = (vhat * lg_ref[:, cols] + lb_ref[:, cols]).astype(BF16)
            w, tril = _tril_bf16(ws_ref, g)
            mixed = jnp.dot(w, vn, preferred_element_type=F32) + bst_ref[:, g:g + 1]
            u, gb, db = _group(u_refs, g), _group(gb_refs, g), db_ref[:, cols]
            (gu, gu_grad), sgb = _gelu_both(u), _sigmoid(gb)
            dsg = db * (gb * sgb)
            out_ref[:, 2 * W + g * GMLP_GROUP_DIM: 2 * W + (g + 1) * GMLP_GROUP_DIM] = (
                db * (gu * mixed) * (sgb * (1.0 + gb * (1.0 - sgb)))).astype(BF16)
            out_ref[:, cols] = (dsg * mixed * gu_grad).astype(BF16)
            dmixed = dsg * gu
            dmixed_b = dmixed.astype(BF16)
            dvn = lax.dot_general(w, dmixed_b, (((0,), (0,)), ((), ())), preferred_element_type=F32)
            dw = lax.dot_general(dmixed_b, vn, (((1,), (1,)), ((), ())), preferred_element_type=F32)
            dws_ref[g] += jnp.where(tril, dw, 0.0)
            dbs_ref[g] += jnp.sum(dmixed, axis=1, keepdims=True)
            dlg_ref[:, cols] += jnp.sum(dvn * vhat, axis=0, keepdims=True)
            dlb_ref[:, cols] += jnp.sum(dvn, axis=0, keepdims=True)
            dvh = dvn * lg_ref[:, cols]
            dvh_scr[:, cols] = dvh
            sum_dvh = sum_dvh + jnp.sum(dvh, axis=1, keepdims=True)
            sum_dvh_vh = sum_dvh_vh + jnp.sum(dvh * vhat, axis=1, keepdims=True)
        m1, m2 = sum_dvh / W, sum_dvh_vh / W
        for k in range(N_PARTS):
            cols = slice(k * COL_BLK, (k + 1) * COL_BLK)
            vhat = (gv_scr[:, cols] - mu) * rstd
            dgv = rstd * (dvh_scr[:, cols] - m1 - vhat * m2)
            out_ref[:, W + k * COL_BLK: W + (k + 1) * COL_BLK] = (dgv * vgrad_scr[:, cols]).astype(BF16)

    full = lambda shape: pl.BlockSpec(shape, lambda i: tuple(0 for _ in shape))
    row = pl.BlockSpec((GMLP_CHUNK, W), lambda i: (i, 0))
    return pl.pallas_call(
        body, name="gmlp_bwd",
        out_shape=(jax.ShapeDtypeStruct((S, 3 * W), BF16), jax.ShapeDtypeStruct(ws.shape, F32),
                   jax.ShapeDtypeStruct((GMLP_GROUPS, GMLP_CHUNK, 1), F32), jax.ShapeDtypeStruct((1, W), F32),
                   jax.ShapeDtypeStruct((1, W), F32)),
        grid=(nb,),
        in_specs=_part_specs(OFF_U) + _part_specs(OFF_VG) + _part_specs(OFF_GB)
        + [full(ws.shape), full(bs_t.shape), full(lg.shape), full(lb.shape), row],
        out_specs=(pl.BlockSpec((GMLP_CHUNK, 3 * W), lambda i: (i, 0)), full(ws.shape),
                   full((GMLP_GROUPS, GMLP_CHUNK, 1)), full((1, W)), full((1, W))),
        scratch_shapes=[pltpu.VMEM((GMLP_CHUNK, W), F32)] * 3,
        compiler_params=_params("arbitrary"),
    )(*([proj] * 12), ws, bs_t, lg, lb, db_in)


def _merge_fwd(proj, y_a, y_b, rows=256):
    S = proj.shape[0]
    rows = min(rows, S)

    def body(ma_ref, mb_ref, ya_ref, yb_ref, out_ref):
        out_ref[...] = (_sigmoid(ma_ref[...].astype(F32)) * ya_ref[...]
                        + _sigmoid(mb_ref[...].astype(F32)) * yb_ref[...]).astype(BF16)

    blk = lambda off: pl.BlockSpec((rows, COL_BLK), lambda i, j: (i, off // COL_BLK + j))
    return pl.pallas_call(
        body, name="merge_fwd", out_shape=jax.ShapeDtypeStruct((S, D_MODEL), BF16), grid=(S // rows, D_MODEL // COL_BLK),
        in_specs=[blk(OFF_MA), blk(OFF_MB), blk(0), blk(0)], out_specs=blk(0),
        compiler_params=_params("parallel", "parallel"),
    )(proj, proj, y_a, y_b)


def _merge_bwd(proj, y_a, y_b, dmerged, rows=128):
    S = proj.shape[0]
    rows = min(rows, S)
    nj = D_MODEL // COL_BLK

    def body(*refs):
        ma_refs, mb_refs = refs[0:nj], refs[nj:2 * nj]
        ya_ref, yb_ref, dm_ref, dya_ref, dyb_ref, dg_ref = refs[2 * nj:]
        for k in range(nj):
            cols = slice(k * COL_BLK, (k + 1) * COL_BLK)
            dm = dm_ref[:, cols]
            sa, sb = _sigmoid(ma_refs[k][...].astype(F32)), _sigmoid(mb_refs[k][...].astype(F32))
            dya_ref[:, cols] = (dm * sa).astype(BF16)
            dyb_ref[:, cols] = (dm * sb).astype(BF16)
            dg_ref[:, cols] = (dm * ya_ref[:, cols] * sa * (1.0 - sa)).astype(BF16)
            dg_ref[:, D_MODEL + k * COL_BLK: D_MODEL + (k + 1) * COL_BLK] = (
                dm * yb_ref[:, cols] * sb * (1.0 - sb)).astype(BF16)

    part = lambda off: [pl.BlockSpec((rows, COL_BLK), functools.partial(lambda j, i: (i, j), off // COL_BLK + k))
                        for k in range(nj)]
    row = pl.BlockSpec((rows, D_MODEL), lambda i: (i, 0))
    return pl.pallas_call(
        body, name="merge_bwd",
        out_shape=(jax.ShapeDtypeStruct((S, D_MODEL), BF16), jax.ShapeDtypeStruct((S, D_MODEL), BF16),
                   jax.ShapeDtypeStruct((S, 2 * D_MODEL), BF16)),
        grid=(S // rows,), in_specs=part(OFF_MA) + part(OFF_MB) + [row, row, row],
        out_specs=(row, row, pl.BlockSpec((rows, 2 * D_MODEL), lambda i: (i, 0))),
        compiler_params=_params("parallel"),
    )(*([proj] * (2 * nj)), y_a, y_b, dmerged)


def _loss_head(x2, target, fg, rows=128):
    S, D = x2.shape
    rows = min(rows, S)

    def body(x_ref, t_ref, g_ref, dx_ref, dxb_ref, dg_ref, loss_ref):
        i = pl.program_id(0)

        @pl.when(i == 0)
        def _():
            dg_ref[...] = jnp.zeros_like(dg_ref)
            loss_ref[...] = jnp.zeros_like(loss_ref)

        xv, g = x_ref[...], g_ref[...]
        rstd = lax.rsqrt(jnp.mean(xv * xv, axis=-1, keepdims=True) + NORM_EPS)
        xhat = xv * rstd
        err = xhat * g - t_ref[...]
        loss_ref[...] += (0.5 / D) * jnp.sum(err * err)
        dy = err * (1.0 / D)
        dg_ref[...] += jnp.sum(dy * xhat, axis=0, keepdims=True)
        dxh = dy * g
        dx = rstd * (dxh - xhat * jnp.mean(dxh * xhat, axis=-1, keepdims=True))
        dx_ref[...] = dx
        dxb_ref[...] = dx.astype(BF16)

    row = pl.BlockSpec((rows, D), lambda i: (i, 0))
    vec = pl.BlockSpec((1, D), lambda i: (0, 0))
    return pl.pallas_call(
        body, name="loss_head",
        out_shape=(jax.ShapeDtypeStruct((S, D), F32), jax.ShapeDtypeStruct((S, D), BF16),
                   jax.ShapeDtypeStruct((1, D), F32), jax.ShapeDtypeStruct((8, LANES), F32)),
        grid=(S // rows,), in_specs=[row, row, vec],
        out_specs=(row, row, vec, pl.BlockSpec((8, LANES), lambda i: (0, 0))),
        compiler_params=_params("arbitrary"),
    )(x2, target, fg)


def _rms_bwd(x, dh, dx2, g, comm=None, rows=128):
    S, D = x.shape
    rows = min(rows, S)
    nsteps = S // rows
    c_arrays, c_shapes, c_sems, c_build, c_alias = _comm_fields(comm)
    n_cin, n_cout = len(c_arrays), len(c_shapes)

    def body(*refs):
        x_ref, dh_ref, dx2_ref, g_ref = refs[:4]
        cin = refs[4:4 + n_cin]
        gx_ref, dg_ref = refs[4 + n_cin], refs[5 + n_cin]
        cout, sems = refs[6 + n_cin:6 + n_cin + n_cout], refs[6 + n_cin + n_cout:]
        i = pl.program_id(0)
        if comm is not None:
            _start_when(i == 0, c_build(cin, cout, sems))

        @pl.when(i == 0)
        def _():
            dg_ref[...] = jnp.zeros_like(dg_ref)

        xv, dh_v = x_ref[...], dh_ref[...]
        rstd = lax.rsqrt(jnp.mean(xv * xv, axis=-1, keepdims=True) + NORM_EPS)
        xhat = xv * rstd
        dg_ref[...] += jnp.sum(dh_v * xhat, axis=0, keepdims=True)
        dxh = dh_v * g_ref[...]
        gx_ref[...] = dx2_ref[...] + rstd * (dxh - xhat * jnp.mean(dxh * xhat, axis=-1, keepdims=True))
        if comm is not None:
            _wait_when(i == nsteps - 1, c_build(cin, cout, sems))

    row = pl.BlockSpec((rows, D), lambda i: (i, 0))
    vec = pl.BlockSpec((1, D), lambda i: (0, 0))
    aliases = {4 + t: 2 + t for t in range(n_cin)} if c_alias else {}
    return pl.pallas_call(
        body, name="rms_bwd",
        out_shape=(jax.ShapeDtypeStruct((S, D), F32), jax.ShapeDtypeStruct((1, D), F32)) + tuple(c_shapes),
        grid=(nsteps,), in_specs=[row, row, row, vec] + [HBM_SPEC] * n_cin, out_specs=(row, vec) + (HBM_SPEC,) * n_cout,
        scratch_shapes=_comm_scratch(c_sems), input_output_aliases=aliases, compiler_params=_params("arbitrary"),
    )(x, dh, dx2, g, *c_arrays)


def _adamw(w, g, m, v, name, rows=64):
    R, C = w.shape
    rows = min(rows, R)
    c1 = 1.0 - ADAM_B1 ** ADAM_STEP
    c2 = 1.0 - ADAM_B2 ** ADAM_STEP

    def body(w_ref, g_ref, m_ref, v_ref, go_ref, d_ref, mo_ref, vo_ref):
        gv = g_ref[...]
        mn = ADAM_B1 * m_ref[...] + (1.0 - ADAM_B1) * gv
        vn = ADAM_B2 * v_ref[...] + (1.0 - ADAM_B2) * (gv * gv)
        go_ref[...] = gv
        mo_ref[...] = mn
        vo_ref[...] = vn
        d_ref[...] = -ADAM_LR * ((mn / c1) / (jnp.sqrt(vn / c2) + ADAM_EPS) + ADAM_WD * w_ref[...])

    spec = pl.BlockSpec((rows, C), lambda i: (i, 0))
    shape = jax.ShapeDtypeStruct((R, C), F32)
    return pl.pallas_call(body, name=name, out_shape=(shape,) * 4, grid=(R // rows,), in_specs=[spec] * 4,
                          out_specs=(spec,) * 4, compiler_params=_params("parallel"))(w, g, m, v)


def _place():
    x, y, c = lax.axis_index("x"), lax.axis_index("y"), lax.axis_index("c")
    others = [(1 - x, y), (x, 1 - y), (1 - x, 1 - y)]
    return x, y, c, others


def _chip_index(chip):
    return 2 * chip[0] + chip[1]


def _remote(src, dst, sems, k, to):
    send_sems, recv_sems = sems
    return pltpu.make_async_remote_copy(src_ref=src, dst_ref=dst, send_sem=send_sems.at[k], recv_sem=recv_sems.at[k],
                                        device_id=to, device_id_type=MESH)


def _comm_fields(comm):
    return comm if comm is not None else ((), (), 0, None, False)


def _comm_scratch(n_sems):
    return [pltpu.SemaphoreType.DMA((n_sems,)), pltpu.SemaphoreType.DMA((n_sems,))] if n_sems else []


def _start_when(cond, copies):
    @pl.when(cond)
    def _():
        for cp in copies:
            cp.start()


def _wait_when(cond, copies):
    @pl.when(cond)
    def _():
        for cp in copies:
            cp.wait()


def _proj_gather(tile_cols, h, w_full, comm=None, tm=1024, tk=512):
    c_arrays, c_shapes, c_sems, c_build, c_alias = _comm_fields(comm)
    n_cin, n_cout = len(c_arrays), len(c_shapes)
    S, Dm = h.shape
    tm, tk = min(tm, S), min(tk, Dm)
    tn = SLAB // 2
    ni, nj, nk = S // tm, 2 * N_CHIPS, Dm // tk
    total = nj * ni * nk
    half_in = Dm // 2

    def body(*refs):
        cols_ref, h_ref = refs[0], refs[1]
        cin = refs[3:3 + n_cin]
        proj_ref, fi_ref = refs[3 + n_cin], refs[4 + n_cin]
        cout = refs[5 + n_cin:5 + n_cin + n_cout]
        acc, bbuf, bsem, send_sems, recv_sems = refs[5 + n_cin + n_cout:10 + n_cin + n_cout]
        c_sem_refs = refs[10 + n_cin + n_cout:]
        j, i, k = pl.program_id(0), pl.program_id(1), pl.program_id(2)
        t = (j * ni + i) * nk + k
        x, y, c, others = _place()
        me_chip = 2 * x + y
        sems = (send_sems, recv_sems)

        def piece(chip, half, tile):
            return fi_ref.at[pl.ds(half * half_in, half_in), pl.ds(chip * SLAB + tile * tn, tn)]

        def ici_send(r, tile):
            mine = piece(me_chip, c, tile)
            return _remote(mine, mine, sems, 2 * r + tile, (*others[r], c))

        def forward(tile):
            src_chip = (x + (1 - c) * (1 - 2 * x), y + c * (1 - 2 * y))
            dst_chip = (x + c * (1 - 2 * x), y + (1 - c) * (1 - 2 * y))
            got = piece(_chip_index(src_chip), c, tile)
            return _remote(got, got, sems, 4 + tile, (*dst_chip, c))

        def ici_recv(r, tile):
            got = piece(_chip_index(others[r]), c, tile)
            return _remote(got, got, sems, 2 * r + tile, (x, y, c))

        def pass_on(r, tile):
            got = piece(_chip_index(others[r]), c, tile)
            return _remote(got, got, sems, 6 + 2 * r + tile, (x, y, 1 - c))

        def passed_recv(r, tile):
            got = piece(_chip_index(others[r]), 1 - c, tile)
            return _remote(got, got, sems, 6 + 2 * r + tile, (x, y, c))

        def b_copy(jj, kk, slot):
            return pltpu.make_async_copy(fi_ref.at[pl.ds(kk * tk, tk), pl.ds(cols_ref[jj] * tn, tn)], bbuf.at[slot],
                                         bsem.at[slot])

        @pl.when(t == 0)
        def _():
            for tile in range(2):
                ici_send(0, tile).start()
                ici_send(1, tile).start()
            b_copy(0, 0, 0).start()
            if comm is not None:
                for cp in c_build(cin, cout, c_sem_refs):
                    cp.start()

        nxt = t + 1
        kn, i_n, jn = nxt % nk, (nxt // nk) % ni, nxt // (nk * ni)

        def opens(jj):
            return (jn == jj) & (i_n == 0) & (kn == 0)

        @pl.when(nxt < total)
        def _():
            for tile in range(2):
                @pl.when(opens(2 + 2 * tile))
                def _():
                    ici_recv(0, tile).wait_recv()
                    ici_recv(1, tile).wait_recv()
                    forward(tile).start()
                    pass_on(0, tile).start()
                    pass_on(1, tile).start()
                    passed_recv(0, tile).wait_recv()

                @pl.when(opens(3 + 2 * tile))
                def _():
                    passed_recv(1, tile).wait_recv()

                @pl.when((jn == 5 + tile) & (i_n == ni - 1) & (kn == 0))
                def _():
                    ici_recv(2, tile).wait_recv()
                    pass_on(2, tile).start()

                @pl.when(opens(6 + tile))
                def _():
                    passed_recv(2, tile).wait_recv()

            b_copy(jn, kn, nxt % 2).start()

        b_copy(j, k, t % 2).wait()

        def product():
            return jnp.dot(h_ref[...], bbuf[t % 2], preferred_element_type=F32)

        @pl.when(k == 0)
        def _():
            acc[...] = product()

        @pl.when((k > 0) & (k < nk - 1))
        def _():
            acc[...] += product()

        @pl.when(k == nk - 1)
        def _():
            proj_ref[...] = (acc[...] + product()).astype(BF16)

        @pl.when(t == total - 1)
        def _():
            for tile in range(2):
                ici_send(0, tile).wait_send()
                ici_send(1, tile).wait_send()
                forward(tile).wait_send()
                for r in range(3):
                    pass_on(r, tile).wait_send()
            if comm is not None:
                for cp in c_build(cin, cout, c_sem_refs):
                    cp.wait()

    grid_spec = pltpu.PrefetchScalarGridSpec(
        num_scalar_prefetch=1, grid=(nj, ni, nk),
        in_specs=[pl.BlockSpec((tm, tk), lambda j, i, k, cols: (i, k)), HBM_SPEC] + [HBM_SPEC] * n_cin,
        out_specs=(pl.BlockSpec((tm, tn), lambda j, i, k, cols: (i, cols[j])), HBM_SPEC) + (HBM_SPEC,) * n_cout,
        scratch_shapes=[pltpu.VMEM((tm, tn), F32), pltpu.VMEM((2, tk, tn), BF16), pltpu.SemaphoreType.DMA((2,)),
                        pltpu.SemaphoreType.DMA((12,)), pltpu.SemaphoreType.DMA((12,))] + _comm_scratch(c_sems))
    aliases = {2: 1}
    if c_alias:
        aliases.update({3 + t: 2 + t for t in range(n_cin)})
    return pl.pallas_call(
        body, name="proj_gather",
        out_shape=(jax.ShapeDtypeStruct((S, PROJ_WIDTH), BF16), jax.ShapeDtypeStruct(w_full.shape, BF16)) + tuple(c_shapes),
        grid_spec=grid_spec, input_output_aliases=aliases,
        compiler_params=_params("arbitrary", "arbitrary", "arbitrary"),
    )(tile_cols, h, w_full, *c_arrays)


def _pair_halves(parts):
    def half_of(ref, which):
        if len(ref.shape) == 2:
            rows = ref.shape[0] // 2
            return ref.at[pl.ds(which * rows, rows), :]
        return ref.at[:, pl.ds(which, 1)]

    def out_shape(p):
        if p.ndim == 2:
            return jax.ShapeDtypeStruct((p.shape[0] // 2, p.shape[1]), BF16)
        return jax.ShapeDtypeStruct((p.shape[0], 1) + p.shape[2:], BF16)

    def build(p_refs, r_refs, sems):
        x, y, c, _ = _place()
        return [_remote(half_of(p, 1 - c), r, sems, t, (x, y, 1 - c)) for t, (p, r) in enumerate(zip(p_refs, r_refs))]

    return (tuple(parts), tuple(out_shape(p) for p in parts), len(parts), build, False)


def _gather_squares(items):
    half_up = UP_ROWS // 2
    fulls = [f for f, _ in items]

    def build(_, f_refs, sems):
        x, y, c, others = _place()
        copies = []
        for t, (f, (_, over_ici)) in enumerate(zip(f_refs, items)):
            for r, chip in enumerate(others):
                src_chip = 2 * x + y if over_ici else _chip_index(chip)
                rows = f.at[pl.ds(src_chip * UP_ROWS + c * half_up, half_up), :]
                copies.append(_remote(rows, rows, sems, 3 * t + r, (*chip, c) if over_ici else (x, y, 1 - c)))
        return copies

    return (tuple(fulls), tuple(jax.ShapeDtypeStruct(f.shape, f.dtype) for f in fulls), 3 * len(fulls), build, True)


def _send_whole(parts):
    def build(p_refs, r_refs, sems):
        x, y, c, _ = _place()
        return [_remote(p, r, sems, t, (x, y, 1 - c)) for t, (p, r) in enumerate(zip(p_refs, r_refs))]

    return (tuple(parts), tuple(jax.ShapeDtypeStruct(p.shape, p.dtype) for p in parts), len(parts), build, False)


def _pair_add_in(p_own, r_in, rows=256, cols=SLAB):
    half, P = p_own.shape
    rows = min(rows, half)

    def body(p_ref, r_ref, o_ref):
        o_ref[...] = (p_ref[...].astype(F32) + r_ref[...].astype(F32)).astype(BF16)

    spec = pl.BlockSpec((rows, cols), lambda i, j: (i, j))
    return pl.pallas_call(body, name="pair_add_in", out_shape=jax.ShapeDtypeStruct((half, P), BF16),
                          grid=(half // rows, P // cols), in_specs=[spec, spec], out_specs=spec,
                          compiler_params=_params("parallel", "parallel"))(p_own, r_in)


def _pair_add_up(idx, p_up, r_up, name):
    _, _, R, C = p_up.shape

    def body(idx_ref, p_ref, r_ref, o_ref):
        o_ref[...] = (p_ref[...].astype(F32) + r_ref[...].astype(F32)).astype(BF16)

    grid_spec = pltpu.PrefetchScalarGridSpec(
        num_scalar_prefetch=1, grid=(N_CHIPS,),
        in_specs=[pl.BlockSpec((1, 1, R, C), lambda j, idx: (j, idx[0], 0, 0)),
                  pl.BlockSpec((1, 1, R, C), lambda j, idx: (j, 0, 0, 0))],
        out_specs=pl.BlockSpec((1, 1, R, C), lambda j, idx: (j, 0, 0, 0)))
    return pl.pallas_call(body, name=name, out_shape=jax.ShapeDtypeStruct((N_CHIPS, 1, R, C), BF16), grid_spec=grid_spec,
                          compiler_params=_params("parallel"))(idx, p_up, r_up)


def _slab_exchange(qs):
    n = len(qs)

    def out_shape(q):
        if q.ndim == 2:
            return jax.ShapeDtypeStruct((3, q.shape[0], SLAB), BF16)
        return jax.ShapeDtypeStruct((3,) + q.shape[1:], BF16)

    def build(q_refs, r_refs, sems):
        _, _, c, others = _place()
        copies = []
        for r, chip in enumerate(others):
            ci = _chip_index(chip)
            for t in range(n):
                src = q_refs[t].at[:, pl.ds(ci * SLAB, SLAB)] if len(q_refs[t].shape) == 2 else q_refs[t].at[ci]
                copies.append(_remote(src, r_refs[t].at[r], sems, 3 * t + r, (*chip, c)))
        return copies

    return (tuple(qs), tuple(out_shape(q) for q in qs), 3 * n, build, False)


def _slab_add_in(idx, q_in, r2_in, rows=128):
    half = q_in.shape[0]
    rows = min(rows, half)
    nrb = half // rows

    def body(idx_ref, q_ref, r_ref, o_ref):
        o_ref[...] = ((q_ref[...].astype(F32) + r_ref[0].astype(F32)) + r_ref[1].astype(F32)) + r_ref[2].astype(F32)

    grid_spec = pltpu.PrefetchScalarGridSpec(
        num_scalar_prefetch=1, grid=(nrb,),
        in_specs=[pl.BlockSpec((rows, SLAB), lambda i, idx: (i, idx[1])),
                  pl.BlockSpec((3, rows, SLAB), lambda i, idx: (0, i, 0))],
        out_specs=pl.BlockSpec((rows, SLAB), lambda i, idx: (idx[0] * nrb + i, 0)))
    return pl.pallas_call(body, name="slab_add_in", out_shape=jax.ShapeDtypeStruct((2 * half, SLAB), F32),
                          grid_spec=grid_spec, compiler_params=_params("parallel"))(idx, q_in, r2_in)


def _slab_add_up(idx, q_up, r2_up, name, rows=128):
    _, _, R, C = q_up.shape
    rows = min(rows, R)
    nrb = R // rows

    def body(idx_ref, q_ref, r_ref, o_ref):
        o_ref[...] = ((q_ref[0, 0].astype(F32) + r_ref[0, 0].astype(F32)) + r_ref[1, 0].astype(F32)) + r_ref[2, 0].astype(F32)

    grid_spec = pltpu.PrefetchScalarGridSpec(
        num_scalar_prefetch=1, grid=(nrb,),
        in_specs=[pl.BlockSpec((1, 1, rows, C), lambda i, idx: (idx[1], 0, i, 0)),
                  pl.BlockSpec((3, 1, rows, C), lambda i, idx: (0, 0, i, 0))],
        out_specs=pl.BlockSpec((rows, C), lambda i, idx: (idx[0] * nrb + i, 0)))
    return pl.pallas_call(body, name=name, out_shape=jax.ShapeDtypeStruct((2 * R, C), F32), grid_spec=grid_spec,
                          compiler_params=_params("parallel"))(idx, q_up, r2_up)


def _pair_share(gs):
    def build(_, g_refs, sems):
        x, y, c, _ = _place()
        copies = []
        for t, g in enumerate(g_refs):
            rows = g.shape[0] // 2
            mine = g.at[pl.ds(c * rows, rows), :]
            copies.append(_remote(mine, mine, sems, t, (x, y, 1 - c)))
        return copies

    return (tuple(gs), tuple(jax.ShapeDtypeStruct(g.shape, g.dtype) for g in gs), len(gs), build, True)


def _all_reduce_small(packed):
    R, C = packed.shape
    N_DEV = 2 * N_CHIPS

    def body(x_ref, out_ref, all_ref, send_sems, recv_sems, local_sem):
        x, y, c, others = _place()
        me, sib = (x, y, c), (x, y, 1 - c)
        sems = (send_sems, recv_sems)

        def rows(px, py, pc):
            return all_ref.at[4 * px + 2 * py + pc]

        mine = pltpu.make_async_copy(x_ref, rows(*me), local_sem)
        mine.start()
        first = [_remote(x_ref, rows(*me), sems, 0, sib)]
        first += [_remote(x_ref, rows(*me), sems, 1 + j, (*chip, c)) for j, chip in enumerate(others)]
        for cp in first:
            cp.start()
        passed = [_remote(rows(*chip, c), rows(*chip, c), sems, 4 + j, sib) for j, chip in enumerate(others)]
        for j, chip in enumerate(others):
            _remote(rows(*chip, c), rows(*chip, c), sems, 1 + j, me).wait_recv()
            passed[j].start()
        _remote(rows(*sib), rows(*sib), sems, 0, me).wait_recv()
        for j, chip in enumerate(others):
            _remote(rows(*chip, 1 - c), rows(*chip, 1 - c), sems, 4 + j, me).wait_recv()
        for cp in first + passed:
            cp.wait_send()
        mine.wait()
        total = all_ref[0]
        for d in range(1, N_DEV):
            total = total + all_ref[d]
        out_ref[...] = total

    vmem = pl.BlockSpec(memory_space=pltpu.VMEM)
    return pl.pallas_call(
        body, name="all_reduce_small", out_shape=jax.ShapeDtypeStruct((R, C), F32),
        in_specs=[vmem], out_specs=vmem,
        scratch_shapes=[pltpu.VMEM((N_DEV, R, C), F32), pltpu.SemaphoreType.DMA((7,)), pltpu.SemaphoreType.DMA((7,)),
                        pltpu.SemaphoreType.DMA],
        compiler_params=pltpu.CompilerParams(vmem_limit_bytes=VMEM_LIMIT),
    )(packed)


SMALL_NAMES = ("w_spatial", "b_spatial", "norm_g", "gmlp_ln_g", "gmlp_ln_b", "final_norm_g", "attn_sink")


def _pack_small(parts, extra=None):
    blocks = []
    for n in SMALL_NAMES:
        flat = parts[n].reshape(-1).astype(F32)
        rows = -(-flat.shape[0] // (8 * LANES)) * 8
        flat = jnp.pad(flat, (0, rows * LANES - flat.shape[0]))
        blocks.append(flat.reshape(rows, LANES))
    blocks.append(jnp.zeros((8, LANES), F32) if extra is None else extra)
    return jnp.concatenate(blocks, axis=0)


def _unpack_small(packed, shapes):
    out, r = {}, 0
    for n in SMALL_NAMES:
        size = 1
        for s in shapes[n]:
            size *= s
        rows = -(-size // (8 * LANES)) * 8
        out[n] = packed[r:r + rows].reshape(-1)[:size].reshape(shapes[n])
        r += rows
    return out, packed[r:r + 8]


def kernel(x, positions, norm_g, w_in, attn_sink, gmlp_ln_g, gmlp_ln_b, w_spatial, b_spatial, w_up_attn, w_up_gmlp, w_out, final_norm_g, loss_target, m_norm_g, m_w_in, m_attn_sink, m_gmlp_ln_g, m_gmlp_ln_b, m_w_spatial, m_b_spatial, m_w_up_attn, m_w_up_gmlp, m_w_out, m_final_norm_g, v_norm_g, v_w_in, v_attn_sink, v_gmlp_ln_g, v_gmlp_ln_b, v_w_spatial, v_b_spatial, v_w_up_attn, v_w_up_gmlp, v_w_out, v_final_norm_g):
    xs, tgt = x[0], loss_target[0]
    mx, my, mc = lax.axis_index("x"), lax.axis_index("y"), lax.axis_index("c")
    idx = jnp.stack([mc, 2 * mx + my]).astype(jnp.int32)
    own, x_nbr, y_nbr, diag = 2 * mx + my, 2 * (1 - mx) + my, 2 * mx + (1 - my), 2 * (1 - mx) + (1 - my)
    tile_cols = jnp.stack([2 * own, 2 * own + 1, 2 * x_nbr, 2 * y_nbr, 2 * x_nbr + 1, 2 * y_nbr + 1, 2 * diag,
                           2 * diag + 1]).astype(jnp.int32)

    square = (D_MODEL, D_MODEL)
    w_full = _cast_into(idx, w_in[0], (D_MODEL, PROJ_WIDTH), "cast_w_in")
    ups = [_cast_into(idx, w_up_attn[0], square, "cast_w_up_attn"), _cast_into(idx, w_up_gmlp[0], square, "cast_w_up_gmlp"),
           _cast_into(idx, w_out[0], square, "cast_w_out")]

    tables = _rope_tables(positions[0])
    bias = _band_bias()
    sink = attn_sink[0]
    ws, bs_t = w_spatial[0], b_spatial[0].T
    h = _rms_fwd(xs, norm_g)
    proj, w_full, wua = _proj_gather(tile_cols, h, w_full, comm=_gather_squares([(ups[0], True)]))
    attn, a_in, wua, wug = _attn_fwd(proj, tables, bias, sink, comm=_gather_squares([(wua, False), (ups[1], True)]))
    b_in, wug = _gmlp_fwd(proj, ws, bs_t, gmlp_ln_g, gmlp_ln_b, comm=_gather_squares([(wug, False)]))
    y_a, wo = _matmul(a_in, wua, mode="nn", out_dtype=F32, name="up_attn", comm=_gather_squares([(ups[2], True)]))
    y_b, wo = _matmul(b_in, wug, mode="nn", out_dtype=F32, name="up_gmlp", comm=_gather_squares([(wo, False)]))
    merged = _merge_fwd(proj, y_a, y_b)
    x2 = _matmul(merged, wo, mode="nn", out_dtype=F32, name="out_proj", residual=xs, tk=2048)
    dx2, dx2_b, d_fg, loss_part = _loss_head(x2, tgt, final_norm_g.reshape(1, D_MODEL))

    dmerged = _matmul(dx2_b, wo, mode="nt", out_dtype=F32, name="d_merged")
    dy_a, dy_b, d_gates = _merge_bwd(proj, y_a, y_b, dmerged)

    p_ups = [_matmul(a_in, dy_a, mode="tn", out_dtype=BF16, name="dw_up_attn"),
             _matmul(b_in, dy_b, mode="tn", out_dtype=BF16, name="dw_up_gmlp"),
             _matmul(merged, dx2_b, mode="tn", out_dtype=BF16, name="dw_out")]
    p_ups = [p.reshape(N_CHIPS, 2, UP_ROWS // 2, D_MODEL) for p in p_ups]
    da_in, *r1_ups = _matmul(dy_a, wua, mode="nt", out_dtype=F32, name="d_a_in", comm=_pair_halves(p_ups))
    q_ups = [_pair_add_up(idx, p, r, "pair_add_up%d" % t) for t, (p, r) in enumerate(zip(p_ups, r1_ups))]
    db_in = _matmul(dy_b, wug, mode="nt", out_dtype=F32, name="d_b_in")
    dq, dk, dv, dga, d_sink = _attn_bwd(proj, tables, bias, sink, attn, da_in)
    d_gmlp, d_ws, d_bs, d_lg, d_lb = _gmlp_bwd(proj, ws, bs_t, gmlp_ln_g, gmlp_ln_b, db_in)
    dproj = jnp.concatenate([dq, dk, dv, dga, d_gmlp, d_gates], axis=1)

    half = D_MODEL // 2
    h_sib = lax.dynamic_slice(h, (0, (1 - mc) * half), (h.shape[0], half))
    h_own = lax.dynamic_slice(h, (0, mc * half), (h.shape[0], half))
    p_sib, *r2_ups = _matmul(h_sib, dproj, mode="tn", out_dtype=BF16, name="dw_in_sibling", comm=_slab_exchange(q_ups))
    p_own, r1_in = _matmul(h_own, dproj, mode="tn", out_dtype=BF16, name="dw_in_own", comm=_send_whole([p_sib]))
    q_in = _pair_add_in(p_own, r1_in)
    dh, r2_in = _matmul(dproj, w_full, mode="nt", out_dtype=F32, name="d_h", comm=_slab_exchange([q_in]), tk=SLAB // 2)

    g_half = [_slab_add_in(idx, q_in, r2_in)] + [_slab_add_up(idx, q, r, "slab_add_up%d" % t)
                                                 for t, (q, r) in enumerate(zip(q_ups, r2_ups))]
    grad_x, d_ng, *g_big = _rms_bwd(xs, dh, dx2, norm_g, comm=_pair_share(g_half))

    small_shapes = {"w_spatial": w_spatial.shape, "b_spatial": b_spatial.shape, "norm_g": norm_g.shape,
                    "gmlp_ln_g": gmlp_ln_g.shape, "gmlp_ln_b": gmlp_ln_b.shape, "final_norm_g": final_norm_g.shape,
                    "attn_sink": attn_sink.shape}
    d_small = {"w_spatial": d_ws, "b_spatial": d_bs, "norm_g": d_ng, "gmlp_ln_g": d_lg, "gmlp_ln_b": d_lb,
               "final_norm_g": d_fg, "attn_sink": d_sink[:, 0, :HEADS_PER_STEP]}
    g_small = _all_reduce_small(_pack_small(d_small, loss_part))

    w_small = _pack_small(dict(w_spatial=w_spatial, b_spatial=b_spatial, norm_g=norm_g, gmlp_ln_g=gmlp_ln_g,
                               gmlp_ln_b=gmlp_ln_b, final_norm_g=final_norm_g, attn_sink=attn_sink))
    m_small = _pack_small(dict(w_spatial=m_w_spatial, b_spatial=m_b_spatial, norm_g=m_norm_g, gmlp_ln_g=m_gmlp_ln_g,
                               gmlp_ln_b=m_gmlp_ln_b, final_norm_g=m_final_norm_g, attn_sink=m_attn_sink))
    v_small = _pack_small(dict(w_spatial=v_w_spatial, b_spatial=v_b_spatial, norm_g=v_norm_g, gmlp_ln_g=v_gmlp_ln_g,
                               gmlp_ln_b=v_gmlp_ln_b, final_norm_g=v_final_norm_g, attn_sink=v_attn_sink))
    small_out = _adamw(w_small, g_small, m_small, v_small, "adamw_small", rows=w_small.shape[0])
    big = {
        "w_in": _adamw(w_in[0], g_big[0], m_w_in[0], v_w_in[0], "adamw_w_in"),
        "w_up_attn": _adamw(w_up_attn[0], g_big[1], m_w_up_attn[0], v_w_up_attn[0], "adamw_w_up_attn"),
        "w_up_gmlp": _adamw(w_up_gmlp[0], g_big[2], m_w_up_gmlp[0], v_w_up_gmlp[0], "adamw_w_up_gmlp"),
        "w_out": _adamw(w_out[0], g_big[3], m_w_out[0], v_w_out[0], "adamw_w_out"),
    }

    order = ("norm_g", "w_in", "attn_sink", "gmlp_ln_g", "gmlp_ln_b", "w_spatial", "b_spatial", "w_up_attn", "w_up_gmlp",
             "w_out", "final_norm_g")
    outs = []
    loss = None
    for kind in range(4):
        small, extra = _unpack_small(small_out[kind], small_shapes)
        if kind == 0:
            loss = extra[0, 0]
        for n in order:
            outs.append(big[n][kind][None] if n in big else small[n])
    return (loss, grad_x[None], *outs)
```

```python
import functools

import jax
import jax.numpy as jnp
from jax import lax
from jax.experimental import pallas as pl
from jax.experimental.pallas import tpu as pltpu

F32 = jnp.float32
BF16 = jnp.bfloat16

D_MODEL = 4096
N_Q_HEADS = 64
N_KV_HEADS = 8
HEAD_DIM = 64
Q_PER_KV = N_Q_HEADS // N_KV_HEADS
ATTN_WIDTH = N_Q_HEADS * HEAD_DIM
KV_WIDTH = N_KV_HEADS * HEAD_DIM
WINDOW = 128
BLOCK = 128
ROPE_THETA = 500000.0
ROPE_DIM = HEAD_DIM // 4
ROPE_HALF = ROPE_DIM // 2
GMLP_WIDTH = D_MODEL
GMLP_GROUPS = 8
GMLP_GROUP_DIM = GMLP_WIDTH // GMLP_GROUPS
GMLP_CHUNK = 128
NORM_EPS = 1e-5
LN_EPS = 1e-5

PROJ_SIZES = (ATTN_WIDTH, KV_WIDTH, KV_WIDTH, ATTN_WIDTH, GMLP_WIDTH, GMLP_WIDTH, GMLP_WIDTH, D_MODEL, D_MODEL)
PROJ_WIDTH = sum(PROJ_SIZES)
OFF_Q, OFF_K, OFF_V, OFF_GA, OFF_U, OFF_VG, OFF_GB, OFF_MA, OFF_MB = (
    sum(PROJ_SIZES[:i]) for i in range(len(PROJ_SIZES)))

ADAM_LR = 0.001
ADAM_B1 = 0.9
ADAM_B2 = 0.999
ADAM_EPS = 1e-08
ADAM_WD = 0.01
ADAM_STEP = 10

N_CHIPS = 4
SLAB = PROJ_WIDTH // N_CHIPS
UP_ROWS = D_MODEL // N_CHIPS
LANES = 128
COL_BLK = 1024
HEADS_PER_STEP = 2 * Q_PER_KV
VMEM_LIMIT = 56 * 1024 * 1024

MESH = pl.DeviceIdType.MESH
HBM_SPEC = pl.BlockSpec(memory_space=pltpu.HBM)
SMEM_SPEC = pl.BlockSpec(memory_space=pltpu.SMEM)


def _params(*sem):
    return pltpu.CompilerParams(dimension_semantics=sem, vmem_limit_bytes=VMEM_LIMIT)


def _sigmoid(x):
    return jax.nn.sigmoid(x)


def _gelu(x):
    return jax.nn.gelu(x, approximate=True)


def _gelu_both(x):
    c = 0.7978845608028654
    x2 = x * x
    t = jnp.tanh(c * (x + 0.044715 * x * x2))
    cdf = 0.5 * (1.0 + t)
    return x * cdf, cdf + 0.5 * x * (1.0 - t * t) * c * (1.0 + 3 * 0.044715 * x2)


def _matmul(a, b, *, mode, out_dtype, name, residual=None, comm=None, tm=1024, tn=1024, tk=4096):
    if mode == "nn":
        (M, K), N = a.shape, b.shape[1]
    elif mode == "nt":
        (M, K), N = a.shape, b.shape[0]
    else:
        (K, M), N = a.shape, b.shape[1]
    tm, tn, tk = min(tm, M), min(tn, N), min(tk, K)
    assert M % tm == 0 and N % tn == 0 and K % tk == 0
    ni, nj, nk = M // tm, N // tn, K // tk
    if mode == "nn":
        a_spec = pl.BlockSpec((tm, tk), lambda i, j, k: (i, k))
        b_spec = pl.BlockSpec((tk, tn), lambda i, j, k: (k, j))
        dims = (((1,), (0,)), ((), ()))
    elif mode == "nt":
        a_spec = pl.BlockSpec((tm, tk), lambda i, j, k: (i, k))
        b_spec = pl.BlockSpec((tn, tk), lambda i, j, k: (j, k))
        dims = (((1,), (1,)), ((), ()))
    else:
        a_spec = pl.BlockSpec((tk, tm), lambda i, j, k: (k, i))
        b_spec = pl.BlockSpec((tk, tn), lambda i, j, k: (k, j))
        dims = (((0,), (0,)), ((), ()))
    o_spec = pl.BlockSpec((tm, tn), lambda i, j, k: (i, j))
    n_res = 1 if residual is not None else 0
    c_arrays, c_shapes, c_sems, c_build, c_alias = _comm_fields(comm)
    n_cin, n_cout = len(c_arrays), len(c_shapes)

    def body(*refs):
        a_ref, b_ref = refs[0], refs[1]
        r_ref = refs[2] if n_res else None
        cin = refs[2 + n_res:2 + n_res + n_cin]
        o_ref = refs[2 + n_res + n_cin]
        cout = refs[3 + n_res + n_cin:3 + n_res + n_cin + n_cout]
        n_acc = 1 if nk > 1 else 0
        acc = refs[3 + n_res + n_cin + n_cout] if n_acc else None
        sems = refs[3 + n_acc + n_res + n_cin + n_cout:]
        i, j, k = pl.program_id(0), pl.program_id(1), pl.program_id(2)
        if comm is not None:
            _start_when((i == 0) & (j == 0) & (k == 0), c_build(cin, cout, sems))

        def product():
            return lax.dot_general(a_ref[...], b_ref[...], dims, preferred_element_type=F32)

        def finish(r):
            if n_res:
                r = r + r_ref[...]
            o_ref[...] = r.astype(out_dtype)

        if nk == 1:
            finish(product())
        else:
            @pl.when(k == 0)
            def _():
                acc[...] = product()

            @pl.when((k > 0) & (k < nk - 1))
            def _():
                acc[...] += product()

            @pl.when(k == nk - 1)
            def _():
                finish(acc[...] + product())

        if comm is not None:
            _wait_when((i == ni - 1) & (j == nj - 1) & (k == nk - 1), c_build(cin, cout, sems))

    args = (a, b) + ((residual,) if n_res else ()) + tuple(c_arrays)
    in_specs = [a_spec, b_spec] + ([o_spec] if n_res else []) + [HBM_SPEC] * n_cin
    scratch = ([pltpu.VMEM((tm, tn), F32)] if nk > 1 else []) + _comm_scratch(c_sems)
    sem = ("arbitrary",) * 3 if comm is not None else ("parallel", "parallel", "arbitrary")
    per_array = c_alias if isinstance(c_alias, tuple) else (c_alias,) * n_cin
    aliases = {2 + n_res + t: 1 + t for t in range(n_cin) if per_array[t]}
    out = pl.pallas_call(
        body, name=name, out_shape=(jax.ShapeDtypeStruct((M, N), out_dtype),) + tuple(c_shapes),
        grid=(ni, nj, nk), in_specs=in_specs, out_specs=(o_spec,) + (HBM_SPEC,) * n_cout,
        scratch_shapes=scratch, input_output_aliases=aliases, compiler_params=_params(*sem),
    )(*args)
    return out if comm is not None else out[0]


def _cast_into(idx, w, full_shape, name, rows=64):
    R, C = w.shape
    rows = min(rows, R)
    nrb = R // rows
    by_cols = C != full_shape[1]

    def body(idx_ref, w_ref, o_ref):
        o_ref[...] = w_ref[...].astype(BF16)

    out_map = (lambda i, idx: (i, idx[1])) if by_cols else (lambda i, idx: (idx[1] * nrb + i, 0))
    grid_spec = pltpu.PrefetchScalarGridSpec(
        num_scalar_prefetch=1, grid=(nrb,), in_specs=[pl.BlockSpec((rows, C), lambda i, idx: (i, 0))],
        out_specs=pl.BlockSpec((rows, C), out_map))
    return pl.pallas_call(body, name=name, out_shape=jax.ShapeDtypeStruct(full_shape, BF16), grid_spec=grid_spec,
                          compiler_params=_params("parallel"))(idx, w)


def _rms_fwd(x, g, rows=256):
    S, D = x.shape
    rows = min(rows, S)

    def body(x_ref, g_ref, h_ref):
        xv = x_ref[...]
        ms = jnp.mean(xv * xv, axis=-1, keepdims=True)
        h_ref[...] = (xv * lax.rsqrt(ms + NORM_EPS) * g_ref[...]).astype(BF16)

    spec = pl.BlockSpec((rows, D), lambda i: (i, 0))
    return pl.pallas_call(body, name="rms_fwd", out_shape=jax.ShapeDtypeStruct((S, D), BF16), grid=(S // rows,),
                          in_specs=[spec, pl.BlockSpec((1, D), lambda i: (0, 0))], out_specs=spec,
                          compiler_params=_params("parallel"))(x, g)


def _rope_tables(positions):
    inv_freq = ROPE_THETA ** (-jnp.arange(ROPE_HALF, dtype=F32) * 2.0 / ROPE_DIM)
    ang = positions.astype(F32)[:, None] * inv_freq
    cos, sin = jnp.cos(ang), jnp.sin(ang)
    S = positions.shape[0]
    rest = HEAD_DIM - ROPE_DIM
    zeros_h, zeros_r = jnp.zeros((S, ROPE_HALF), F32), jnp.zeros((S, rest), F32)
    c = jnp.concatenate([cos, cos, jnp.ones((S, rest), F32)], axis=1)
    s1 = jnp.concatenate([-sin, zeros_h, zeros_r], axis=1)
    s2 = jnp.concatenate([zeros_h, sin, zeros_r], axis=1)
    reps = LANES // HEAD_DIM
    return jnp.tile(c, (1, reps)), jnp.tile(s1, (1, reps)), jnp.tile(s2, (1, reps))


def _rope(t, c, s1, s2, sign):
    if sign != 1.0:
        s1, s2 = sign * s1, sign * s2
    blocks = []
    for b in range(t.shape[1] // LANES):
        blk = t[:, b * LANES:(b + 1) * LANES]
        up = pltpu.roll(blk, LANES - ROPE_HALF, 1)
        down = pltpu.roll(blk, ROPE_HALF, 1)
        blocks.append(blk * c + up * s1 + down * s2)
    return blocks[0] if len(blocks) == 1 else jnp.concatenate(blocks, axis=1)


def _attn_specs(nb):
    last = nb - 1

    def cur(i):
        return jnp.minimum(i, last)

    def prev(i):
        return jnp.maximum(jnp.minimum(i, last) - 1, 0)

    kq, kk, kv, kg = OFF_Q // COL_BLK, OFF_K // LANES, OFF_V // LANES, OFF_GA // COL_BLK
    wide = lambda off: pl.BlockSpec((BLOCK, COL_BLK), lambda p, i: (cur(i), off + p))
    kv_cur = lambda off: pl.BlockSpec((BLOCK, LANES), lambda p, i: (cur(i), off + p))
    kv_prev = lambda off: pl.BlockSpec((BLOCK, LANES), lambda p, i: (prev(i), off + p))
    tab_cur = pl.BlockSpec((BLOCK, LANES), lambda p, i: (cur(i), 0))
    tab_prev = pl.BlockSpec((BLOCK, LANES), lambda p, i: (prev(i), 0))
    proj_specs = [wide(kq), kv_cur(kk), kv_prev(kk), kv_cur(kv), kv_prev(kv), wide(kg)]
    table_specs = [tab_cur] * 3 + [tab_prev] * 3
    bias_spec = pl.BlockSpec((1, BLOCK, QCOLS), lambda p, i: (jnp.minimum(i, 1), 0, 0))
    return proj_specs, table_specs + [bias_spec], cur, prev


PAIRS = Q_PER_KV // 2
QCOLS = PAIRS * BLOCK
NT = (((1,), (1,)), ((), ()))
TN = (((0,), (0,)), ((), ()))


def _band_bias():
    first = jnp.where(jnp.arange(BLOCK)[:, None] > (jnp.arange(QCOLS) % BLOCK)[None, :], -1e30, 0.0)
    return jnp.stack([first, jnp.zeros_like(first)]).astype(F32)


def _from_prev():
    a = lax.broadcasted_iota(jnp.int32, (BLOCK, QCOLS), 1) & (BLOCK - 1)
    return lax.broadcasted_iota(jnp.int32, (BLOCK, QCOLS), 0) > a


def _pair_rows(ref, kvh):
    base = kvh * Q_PER_KV * HEAD_DIM
    return jnp.concatenate([ref[:, base + b * LANES: base + (b + 1) * LANES] for b in range(PAIRS)], axis=0).astype(BF16)


def _unpair_rows(ref, kvh, val_t):
    base = kvh * Q_PER_KV * HEAD_DIM
    for b in range(PAIRS):
        ref[:, base + b * LANES: base + (b + 1) * LANES] = val_t[:, b * BLOCK:(b + 1) * BLOCK].T.astype(ref.dtype)


def _split_heads(t, kvh):
    lane = lax.broadcasted_iota(jnp.int32, t.shape, 1)
    other = pltpu.roll(t, HEAD_DIM, 1)
    lo = jnp.where(lane < HEAD_DIM, t if kvh == 0 else other, 0.0)
    hi = jnp.where(lane >= HEAD_DIM, other if kvh == 0 else t, 0.0)
    return lo.astype(BF16), hi.astype(BF16)


def _join_heads(even, odd, kvh):
    lane = lax.broadcasted_iota(jnp.int32, even.shape, 1)
    lo = jnp.where(lane < HEAD_DIM, even, 0.0) + pltpu.roll(jnp.where(lane >= HEAD_DIM, odd, 0.0), HEAD_DIM, 1)
    return lo if kvh == 0 else pltpu.roll(lo, HEAD_DIM, 1)


def _sink_row(sink_ref, pair, kvh, odd):
    first = (pair * 2 + kvh) * Q_PER_KV + odd
    return jnp.concatenate([jnp.full((1, BLOCK), sink_ref[first + 2 * b], F32) for b in range(PAIRS)], axis=1)


def _band_t(ap, ac, b, from_prev):
    return jnp.where(from_prev, lax.dot_general(ap, b, NT, preferred_element_type=F32),
                     lax.dot_general(ac, b, NT, preferred_element_type=F32))


def _softmax_t(st, sink_row):
    m = jnp.maximum(jnp.max(st, axis=0, keepdims=True), sink_row)
    e = jnp.exp(st - m)
    es = jnp.exp(sink_row - m)
    inv = 1.0 / (jnp.sum(e, axis=0, keepdims=True) + es)
    return e * inv, es * inv


def _roped_keys(k_ref, kp_ref, tabs):
    c, s1, s2, cp, s1p, s2p = (t[...] for t in tabs)
    k_cur = _rope(k_ref[...].astype(F32), c, s1, s2, 1.0)
    k_prev = _rope(kp_ref[...].astype(F32), cp, s1p, s2p, 1.0)
    return k_prev, k_cur


SCALE = HEAD_DIM ** -0.5


def _attn_fwd(proj, tables, bias, sink, comm=None):
    S = proj.shape[0]
    nb = S // BLOCK
    npairs = N_KV_HEADS // 2
    proj_specs, table_specs, _, _ = _attn_specs(nb)
    c_arrays, c_shapes, c_sems, c_build, c_alias = _comm_fields(comm)
    n_in = len(proj_specs) + len(table_specs) + 1

    def body(*refs):
        (q_ref, k_ref, kp_ref, v_ref, vp_ref, ga_ref, c_ref, s1_ref, s2_ref, cp_ref, s1p_ref, s2p_ref, bias_ref,
         sink_ref) = refs[:n_in]
        cin = refs[n_in:n_in + len(c_arrays)]
        attn_ref, ain_ref = refs[n_in + len(c_arrays):n_in + len(c_arrays) + 2]
        cout = refs[n_in + len(c_arrays) + 2:n_in + len(c_arrays) + 2 + len(c_shapes)]
        qr_scr = refs[n_in + len(c_arrays) + 2 + len(c_shapes)]
        sems = refs[n_in + len(c_arrays) + 3 + len(c_shapes):]
        pair, i = pl.program_id(0), pl.program_id(1)
        if comm is not None:
            _start_when((pair == 0) & (i == 0), c_build(cin, cout, sems))
        tabs = (c_ref, s1_ref, s2_ref, cp_ref, s1p_ref, s2p_ref)
        qr_scr[...] = _rope(q_ref[...].astype(F32), c_ref[...], s1_ref[...], s2_ref[...], 1.0) * SCALE
        k_prev, k_cur = _roped_keys(k_ref, kp_ref, tabs)
        v_prev, v_cur = vp_ref[...].astype(F32), v_ref[...].astype(F32)
        from_prev = _from_prev()
        heads = [(kvh, odd) for kvh in range(2) for odd in range(2)]
        qps = [_pair_rows(qr_scr, kvh) for kvh in range(2)]
        kpss, kcss = [_split_heads(k_prev, kvh) for kvh in range(2)], [_split_heads(k_cur, kvh) for kvh in range(2)]
        scores = [_band_t(kpss[kvh][odd], kcss[kvh][odd], qps[kvh], from_prev) + bias_ref[0] for kvh, odd in heads]
        probs = []
        for n, (kvh, odd) in enumerate(heads):
            pb = _softmax_t(scores[n], _sink_row(sink_ref, pair, kvh, odd))[0].astype(BF16)
            probs.append((jnp.where(from_prev, pb, 0), jnp.where(from_prev, 0, pb)))
        for kvh in range(2):
            vps, vcs = _split_heads(v_prev, kvh), _split_heads(v_cur, kvh)
            o_t = None
            for odd in range(2):
                p_prev, p_cur = probs[2 * kvh + odd]
                part = (lax.dot_general(vps[odd], p_prev, TN, preferred_element_type=F32)
                        + lax.dot_general(vcs[odd], p_cur, TN, preferred_element_type=F32))
                o_t = part if o_t is None else o_t + part
            _unpair_rows(attn_ref, kvh, o_t)
        ga = ga_ref[...].astype(F32)
        ain_ref[...] = (attn_ref[...] * (ga * _sigmoid(ga))).astype(BF16)
        if comm is not None:
            _wait_when((pair == npairs - 1) & (i == nb - 1), c_build(cin, cout, sems))

    out_spec = pl.BlockSpec((BLOCK, COL_BLK), lambda p, i: (i, p))
    aliases = {n_in + t: 2 + t for t in range(len(c_arrays))} if c_alias else {}
    out = pl.pallas_call(
        body, name="attn_fwd",
        out_shape=(jax.ShapeDtypeStruct((S, ATTN_WIDTH), F32), jax.ShapeDtypeStruct((S, ATTN_WIDTH), BF16)) + tuple(c_shapes),
        grid=(npairs, nb), in_specs=proj_specs + table_specs + [SMEM_SPEC] + [HBM_SPEC] * len(c_arrays),
        out_specs=(out_spec, out_spec) + (HBM_SPEC,) * len(c_shapes),
        scratch_shapes=[pltpu.VMEM((BLOCK, COL_BLK), F32)] + _comm_scratch(c_sems), input_output_aliases=aliases,
        compiler_params=_params("arbitrary", "arbitrary"),
    )(*([proj] * 6), *tables, *tables, bias, sink, *c_arrays)
    return out


def _attn_bwd(proj, tables, bias, sink, attn, da_in):
    S = proj.shape[0]
    nb = S // BLOCK
    proj_specs, table_specs, cur, prev = _attn_specs(nb)
    wide_cur = pl.BlockSpec((BLOCK, COL_BLK), lambda p, i: (cur(i), p))
    kv_out = pl.BlockSpec((BLOCK, LANES), lambda p, i: (jnp.maximum(i - 1, 0), p))

    def body(q_ref, k_ref, kp_ref, v_ref, vp_ref, ga_ref, c_ref, s1_ref, s2_ref, cp_ref, s1p_ref, s2p_ref, bias_ref,
             sink_ref, attn_ref, da_ref, dq_ref, dk_ref, dv_ref, dga_ref, dsink_ref,
             qr_scr, do_scr, dq_scr, dk_scr, dv_scr, carry_k, carry_v, sink_acc):
        pair, i = pl.program_id(0), pl.program_id(1)
        tabs = (c_ref, s1_ref, s2_ref, cp_ref, s1p_ref, s2p_ref)

        @pl.when(i < nb)
        def _():
            ga = ga_ref[...].astype(F32)
            sg = _sigmoid(ga)
            da = da_ref[...]
            dga_ref[...] = (da * attn_ref[...] * (sg * (1.0 + ga * (1.0 - sg)))).astype(BF16)
            do_scr[...] = da * (ga * sg)
            qr_scr[...] = _rope(q_ref[...].astype(F32), c_ref[...], s1_ref[...], s2_ref[...], 1.0) * SCALE
            k_prev, k_cur = _roped_keys(k_ref, kp_ref, tabs)
            v_prev, v_cur = vp_ref[...].astype(F32), v_ref[...].astype(F32)
            from_prev = _from_prev()
            dkv = [None] * 4
            heads = [(kvh, odd) for kvh in range(2) for odd in range(2)]
            qps, dops = [_pair_rows(qr_scr, kvh) for kvh in range(2)], [_pair_rows(do_scr, kvh) for kvh in range(2)]
            kpss, kcss = [_split_heads(k_prev, kvh) for kvh in range(2)], [_split_heads(k_cur, kvh) for kvh in range(2)]
            vpss, vcss = [_split_heads(v_prev, kvh) for kvh in range(2)], [_split_heads(v_cur, kvh) for kvh in range(2)]
            scores = [_band_t(kpss[kvh][odd], kcss[kvh][odd], qps[kvh], from_prev) + bias_ref[0] for kvh, odd in heads]
            dps = [_band_t(vpss[kvh][odd], vcss[kvh][odd], dops[kvh], from_prev) for kvh, odd in heads]
            masked = []
            for n, (kvh, odd) in enumerate(heads):
                p_t, p_sink = _softmax_t(scores[n], _sink_row(sink_ref, pair, kvh, odd))
                delta = jnp.sum(p_t * dps[n], axis=0, keepdims=True)
                ds = (p_t * (dps[n] - delta)).astype(BF16)
                pb = p_t.astype(BF16)
                masked.append((jnp.where(from_prev, ds, 0), jnp.where(from_prev, 0, ds),
                               jnp.where(from_prev, pb, 0), jnp.where(from_prev, 0, pb)))
                contrib = -p_sink * delta

                @pl.when(i == 0)
                def _():
                    sink_acc[n] = contrib

                @pl.when(i > 0)
                def _():
                    sink_acc[n] += contrib

            for kvh in range(2):
                qp, dop = qps[kvh], dops[kvh]
                dq_t, halves = None, []
                for odd in range(2):
                    ds_prev, ds_cur, p_prev, p_cur = masked[2 * kvh + odd]
                    part = (lax.dot_general(kpss[kvh][odd], ds_prev, TN, preferred_element_type=F32)
                            + lax.dot_general(kcss[kvh][odd], ds_cur, TN, preferred_element_type=F32))
                    dq_t = part if dq_t is None else dq_t + part
                    halves.append([jnp.dot(ds_prev, qp, preferred_element_type=F32),
                                   jnp.dot(ds_cur, qp, preferred_element_type=F32),
                                   jnp.dot(p_prev, dop, preferred_element_type=F32),
                                   jnp.dot(p_cur, dop, preferred_element_type=F32)])
                _unpair_rows(dq_scr, kvh, dq_t)
                for n in range(4):
                    joined = _join_heads(halves[0][n], halves[1][n], kvh)
                    dkv[n] = joined if dkv[n] is None else dkv[n] + joined
            dk_scr[0:BLOCK, :], dk_scr[BLOCK:2 * BLOCK, :] = dkv[0], dkv[1]
            dv_scr[0:BLOCK, :], dv_scr[BLOCK:2 * BLOCK, :] = dkv[2], dkv[3]

            dq_ref[...] = _rope(dq_scr[...] * SCALE, c_ref[...], s1_ref[...], s2_ref[...], -1.0).astype(BF16)
            dk_prev = _rope(dk_scr[0:BLOCK, :], cp_ref[...], s1p_ref[...], s2p_ref[...], -1.0)
            dk_cur = _rope(dk_scr[BLOCK:2 * BLOCK, :], c_ref[...], s1_ref[...], s2_ref[...], -1.0)
            dv_prev, dv_cur = dv_scr[0:BLOCK, :], dv_scr[BLOCK:2 * BLOCK, :]

            @pl.when(i > 0)
            def _():
                dk_ref[...] = (carry_k[...] + dk_prev).astype(BF16)
                dv_ref[...] = (carry_v[...] + dv_prev).astype(BF16)

            carry_k[...] = dk_cur
            carry_v[...] = dv_cur

        @pl.when(i == nb)
        def _():
            dk_ref[...] = carry_k[...].astype(BF16)
            dv_ref[...] = carry_v[...].astype(BF16)
            lane = lax.broadcasted_iota(jnp.int32, (8, LANES), 1)
            acc = jnp.zeros((8, LANES), F32)
            for kvh in range(2):
                for odd in range(2):
                    for b in range(PAIRS):
                        val = jnp.sum(sink_acc[2 * kvh + odd, :, b * BLOCK:(b + 1) * BLOCK], axis=1, keepdims=True)
                        acc = jnp.where(lane == kvh * Q_PER_KV + 2 * b + odd, val, acc)
            dsink_ref[0] = acc

    return pl.pallas_call(
        body, name="attn_bwd",
        out_shape=(jax.ShapeDtypeStruct((S, ATTN_WIDTH), BF16), jax.ShapeDtypeStruct((S, KV_WIDTH), BF16),
                   jax.ShapeDtypeStruct((S, KV_WIDTH), BF16), jax.ShapeDtypeStruct((S, ATTN_WIDTH), BF16),
                   jax.ShapeDtypeStruct((N_KV_HEADS // 2, 8, LANES), F32)),
        grid=(N_KV_HEADS // 2, nb + 1),
        in_specs=proj_specs + table_specs + [SMEM_SPEC, wide_cur, wide_cur],
        out_specs=(wide_cur, kv_out, kv_out, wide_cur, pl.BlockSpec((1, 8, LANES), lambda p, i: (p, 0, 0))),
        scratch_shapes=[pltpu.VMEM((BLOCK, COL_BLK), F32), pltpu.VMEM((BLOCK, COL_BLK), F32),
                        pltpu.VMEM((BLOCK, COL_BLK), F32), pltpu.VMEM((2 * BLOCK, LANES), F32),
                        pltpu.VMEM((2 * BLOCK, LANES), F32), pltpu.VMEM((BLOCK, LANES), F32),
                        pltpu.VMEM((BLOCK, LANES), F32), pltpu.VMEM((4, 1, QCOLS), F32)],
        compiler_params=_params("arbitrary", "arbitrary"),
    )(*([proj] * 6), *tables, *tables, bias, sink, attn, da_in)


N_PARTS = GMLP_WIDTH // COL_BLK
GROUPS_PER_PART = COL_BLK // GMLP_GROUP_DIM


def _part_specs(off):
    return [pl.BlockSpec((GMLP_CHUNK, COL_BLK), functools.partial(lambda j, i: (i, j), off // COL_BLK + k))
            for k in range(N_PARTS)]


def _group(refs, g):
    lo = (g % GROUPS_PER_PART) * GMLP_GROUP_DIM
    return refs[g // GROUPS_PER_PART][:, lo:lo + GMLP_GROUP_DIM].astype(F32)


def _gmlp_norm_stats(vg_refs, gv_scr, grad_scr=None):
    total = jnp.zeros((GMLP_CHUNK, 1), F32)
    for k in range(N_PARTS):
        vg = vg_refs[k][...].astype(F32)
        if grad_scr is None:
            gv = _gelu(vg)
        else:
            gv, grad_scr[:, k * COL_BLK:(k + 1) * COL_BLK] = _gelu_both(vg)
        gv_scr[:, k * COL_BLK:(k + 1) * COL_BLK] = gv
        total = total + jnp.sum(gv, axis=1, keepdims=True)
    mu = total / GMLP_WIDTH
    xc = gv_scr[...] - mu
    var = jnp.sum(xc * xc, axis=1, keepdims=True) / GMLP_WIDTH
    return mu, lax.rsqrt(var + LN_EPS)


def _tril_bf16(ws_ref, g):
    t = lax.broadcasted_iota(jnp.int32, (GMLP_CHUNK, GMLP_CHUNK), 0)
    s = lax.broadcasted_iota(jnp.int32, (GMLP_CHUNK, GMLP_CHUNK), 1)
    return jnp.where(s <= t, ws_ref[g], 0.0).astype(BF16), s <= t


def _gmlp_fwd(proj, ws, bs_t, lg, lb, comm=None):
    S = proj.shape[0]
    nb = S // GMLP_CHUNK
    c_arrays, c_shapes, c_sems, c_build, c_alias = _comm_fields(comm)
    n_cin, n_cout = len(c_arrays), len(c_shapes)

    def body(*refs):
        u_refs, vg_refs, gb_refs = refs[0:4], refs[4:8], refs[8:12]
        ws_ref, bst_ref, lg_ref, lb_ref = refs[12:16]
        cin, out_ref, cout = refs[16:16 + n_cin], refs[16 + n_cin], refs[17 + n_cin:17 + n_cin + n_cout]
        gv_scr, sems = refs[17 + n_cin + n_cout], refs[18 + n_cin + n_cout:]
        if comm is not None:
            _start_when(pl.program_id(0) == 0, c_build(cin, cout, sems))
        mu, rstd = _gmlp_norm_stats(vg_refs, gv_scr)
        for g in range(GMLP_GROUPS):
            cols = slice(g * GMLP_GROUP_DIM, (g + 1) * GMLP_GROUP_DIM)
            vn = (gv_scr[:, cols] - mu) * rstd * lg_ref[:, cols] + lb_ref[:, cols]
            w, _ = _tril_bf16(ws_ref, g)
            mixed = jnp.dot(w, vn.astype(BF16), preferred_element_type=F32) + bst_ref[:, g:g + 1]
            gb = _group(gb_refs, g)
            out_ref[:, cols] = (_gelu(_group(u_refs, g)) * mixed * (gb * _sigmoid(gb))).astype(BF16)
        if comm is not None:
            _wait_when(pl.program_id(0) == nb - 1, c_build(cin, cout, sems))

    full = lambda shape: pl.BlockSpec(shape, lambda i: tuple(0 for _ in shape))
    aliases = {16 + t: 1 + t for t in range(n_cin)} if c_alias else {}
    out = pl.pallas_call(
        body, name="gmlp_fwd", out_shape=(jax.ShapeDtypeStruct((S, GMLP_WIDTH), BF16),) + tuple(c_shapes), grid=(nb,),
        in_specs=_part_specs(OFF_U) + _part_specs(OFF_VG) + _part_specs(OFF_GB)
        + [full(ws.shape), full(bs_t.shape), full(lg.shape), full(lb.shape)] + [HBM_SPEC] * n_cin,
        out_specs=(pl.BlockSpec((GMLP_CHUNK, GMLP_WIDTH), lambda i: (i, 0)),) + (HBM_SPEC,) * n_cout,
        scratch_shapes=[pltpu.VMEM((GMLP_CHUNK, GMLP_WIDTH), F32)] + _comm_scratch(c_sems), input_output_aliases=aliases,
        compiler_params=_params("arbitrary" if comm is not None else "parallel"),
    )(*([proj] * 12), ws, bs_t, lg, lb, *c_arrays)
    return out if comm is not None else out[0]


def _gmlp_bwd(proj, ws, bs_t, lg, lb, db_in):
    S = proj.shape[0]
    nb = S // GMLP_CHUNK
    W = GMLP_WIDTH

    def body(*refs):
        u_refs, vg_refs, gb_refs = refs[0:4], refs[4:8], refs[8:12]
        ws_ref, bst_ref, lg_ref, lb_ref, db_ref = refs[12:17]
        out_ref, dws_ref, dbs_ref, dlg_ref, dlb_ref = refs[17:22]
        gv_scr, dvh_scr, vgrad_scr = refs[22:]
        i = pl.program_id(0)

        @pl.when(i == 0)
        def _():
            dws_ref[...] = jnp.zeros_like(dws_ref)
            dbs_ref[...] = jnp.zeros_like(dbs_ref)
            dlg_ref[...] = jnp.zeros_like(dlg_ref)
            dlb_ref[...] = jnp.zeros_like(dlb_ref)

        mu, rstd = _gmlp_norm_stats(vg_refs, gv_scr, vgrad_scr)
        sum_dvh = jnp.zeros((GMLP_CHUNK, 1), F32)
        sum_dvh_vh = jnp.zeros((GMLP_CHUNK, 1), F32)
        for g in range(GMLP_GROUPS):
            cols = slice(g * GMLP_GROUP_DIM, (g + 1) * GMLP_GROUP_DIM)
            vhat = (gv_scr[:, cols] - mu) * rstd
            vn = (vhat * lg_ref[:, cols] + lb_ref[:, cols]).astype(BF16)
            w, tril = _tril_bf16(ws_ref, g)
            mixed = jnp.dot(w, vn, preferred_element_type=F32) + bst_ref[:, g:g + 1]
            u, gb, db = _group(u_refs, g), _group(gb_refs, g), db_ref[:, cols]
            (gu, gu_grad), sgb = _gelu_both(u), _sigmoid(gb)
            dsg = db * (gb * sgb)
            out_ref[:, 2 * W + g * GMLP_GROUP_DIM: 2 * W + (g + 1) * GMLP_GROUP_DIM] = (
                db * (gu * mixed) * (sgb * (1.0 + gb * (1.0 - sgb)))).astype(BF16)
            out_ref[:, cols] = (dsg * mixed * gu_grad).astype(BF16)
            dmixed = dsg * gu
            dmixed_b = dmixed.astype(BF16)
            dvn = lax.dot_general(w, dmixed_b, (((0,), (0,)), ((), ())), preferred_element_type=F32)
            dw = lax.dot_general(dmixed_b, vn, (((1,), (1,)), ((), ())), preferred_element_type=F32)
            dws_ref[g] += jnp.where(tril, dw, 0.0)
            dbs_ref[g] += jnp.sum(dmixed, axis=1, keepdims=True)
            dlg_ref[:, cols] += jnp.sum(dvn * vhat, axis=0, keepdims=True)
            dlb_ref[:, cols] += jnp.sum(dvn, axis=0, keepdims=True)
            dvh = dvn * lg_ref[:, cols]
            dvh_scr[:, cols] = dvh
            sum_dvh = sum_dvh + jnp.sum(dvh, axis=1, keepdims=True)
            sum_dvh_vh = sum_dvh_vh + jnp.sum(dvh * vhat, axis=1, keepdims=True)
        m1, m2 = sum_dvh / W, sum_dvh_vh / W
        for k in range(N_PARTS):
            cols = slice(k * COL_BLK, (k + 1) * COL_BLK)
            vhat = (gv_scr[:, cols] - mu) * rstd
            dgv = rstd * (dvh_scr[:, cols] - m1 - vhat * m2)
            out_ref[:, W + k * COL_BLK: W + (k + 1) * COL_BLK] = (dgv * vgrad_scr[:, cols]).astype(BF16)

    full = lambda shape: pl.BlockSpec(shape, lambda i: tuple(0 for _ in shape))
    row = pl.BlockSpec((GMLP_CHUNK, W), lambda i: (i, 0))
    return pl.pallas_call(
        body, name="gmlp_bwd",
        out_shape=(jax.ShapeDtypeStruct((S, 3 * W), BF16), jax.ShapeDtypeStruct(ws.shape, F32),
                   jax.ShapeDtypeStruct((GMLP_GROUPS, GMLP_CHUNK, 1), F32), jax.ShapeDtypeStruct((1, W), F32),
                   jax.ShapeDtypeStruct((1, W), F32)),
        grid=(nb,),
        in_specs=_part_specs(OFF_U) + _part_specs(OFF_VG) + _part_specs(OFF_GB)
        + [full(ws.shape), full(bs_t.shape), full(lg.shape), full(lb.shape), row],
        out_specs=(pl.BlockSpec((GMLP_CHUNK, 3 * W), lambda i: (i, 0)), full(ws.shape),
                   full((GMLP_GROUPS, GMLP_CHUNK, 1)), full((1, W)), full((1, W))),
        scratch_shapes=[pltpu.VMEM((GMLP_CHUNK, W), F32)] * 3,
        compiler_params=_params("arbitrary"),
    )(*([proj] * 12), ws, bs_t, lg, lb, db_in)


def _merge_fwd(proj, y_a, y_b, rows=256):
    S = proj.shape[0]
    rows = min(rows, S)

    def body(ma_ref, mb_ref, ya_ref, yb_ref, out_ref):
        out_ref[...] = (_sigmoid(ma_ref[...].astype(F32)) * ya_ref[...]
                        + _sigmoid(mb_ref[...].astype(F32)) * yb_ref[...]).astype(BF16)

    blk = lambda off: pl.BlockSpec((rows, COL_BLK), lambda i, j: (i, off // COL_BLK + j))
    return pl.pallas_call(
        body, name="merge_fwd", out_shape=jax.ShapeDtypeStruct((S, D_MODEL), BF16), grid=(S // rows, D_MODEL // COL_BLK),
        in_specs=[blk(OFF_MA), blk(OFF_MB), blk(0), blk(0)], out_specs=blk(0),
        compiler_params=_params("parallel", "parallel"),
    )(proj, proj, y_a, y_b)


def _merge_bwd(proj, y_a, y_b, dmerged, rows=128):
    S = proj.shape[0]
    rows = min(rows, S)
    nj = D_MODEL // COL_BLK

    def body(*refs):
        ma_refs, mb_refs = refs[0:nj], refs[nj:2 * nj]
        ya_ref, yb_ref, dm_ref, dya_ref, dyb_ref, dg_ref = refs[2 * nj:]
        for k in range(nj):
            cols = slice(k * COL_BLK, (k + 1) * COL_BLK)
            dm = dm_ref[:, cols]
            sa, sb = _sigmoid(ma_refs[k][...].astype(F32)), _sigmoid(mb_refs[k][...].astype(F32))
            dya_ref[:, cols] = (dm * sa).astype(BF16)
            dyb_ref[:, cols] = (dm * sb).astype(BF16)
            dg_ref[:, cols] = (dm * ya_ref[:, cols] * sa * (1.0 - sa)).astype(BF16)
            dg_ref[:, D_MODEL + k * COL_BLK: D_MODEL + (k + 1) * COL_BLK] = (
                dm * yb_ref[:, cols] * sb * (1.0 - sb)).astype(BF16)

    part = lambda off: [pl.BlockSpec((rows, COL_BLK), functools.partial(lambda j, i: (i, j), off // COL_BLK + k))
                        for k in range(nj)]
    row = pl.BlockSpec((rows, D_MODEL), lambda i: (i, 0))
    return pl.pallas_call(
        body, name="merge_bwd",
        out_shape=(jax.ShapeDtypeStruct((S, D_MODEL), BF16), jax.ShapeDtypeStruct((S, D_MODEL), BF16),
                   jax.ShapeDtypeStruct((S, 2 * D_MODEL), BF16)),
        grid=(S // rows,), in_specs=part(OFF_MA) + part(OFF_MB) + [row, row, row],
        out_specs=(row, row, pl.BlockSpec((rows, 2 * D_MODEL), lambda i: (i, 0))),
        compiler_params=_params("parallel"),
    )(*([proj] * (2 * nj)), y_a, y_b, dmerged)


def _loss_head(x2, target, fg, rows=128):
    S, D = x2.shape
    rows = min(rows, S)

    def body(x_ref, t_ref, g_ref, dx_ref, dxb_ref, dg_ref, loss_ref):
        i = pl.program_id(0)

        @pl.when(i == 0)
        def _():
            dg_ref[...] = jnp.zeros_like(dg_ref)
            loss_ref[...] = jnp.zeros_like(loss_ref)

        xv, g = x_ref[...], g_ref[...]
        rstd = lax.rsqrt(jnp.mean(xv * xv, axis=-1, keepdims=True) + NORM_EPS)
        xhat = xv * rstd
        err = xhat * g - t_ref[...]
        loss_ref[...] += (0.5 / D) * jnp.sum(err * err)
        dy = err * (1.0 / D)
        dg_ref[...] += jnp.sum(dy * xhat, axis=0, keepdims=True)
        dxh = dy * g
        dx = rstd * (dxh - xhat * jnp.mean(dxh * xhat, axis=-1, keepdims=True))
        dx_ref[...] = dx
        dxb_ref[...] = dx.astype(BF16)

    row = pl.BlockSpec((rows, D), lambda i: (i, 0))
    vec = pl.BlockSpec((1, D), lambda i: (0, 0))
    return pl.pallas_call(
        body, name="loss_head",
        out_shape=(jax.ShapeDtypeStruct((S, D), F32), jax.ShapeDtypeStruct((S, D), BF16),
                   jax.ShapeDtypeStruct((1, D), F32), jax.ShapeDtypeStruct((8, LANES), F32)),
        grid=(S // rows,), in_specs=[row, row, vec],
        out_specs=(row, row, vec, pl.BlockSpec((8, LANES), lambda i: (0, 0))),
        compiler_params=_params("arbitrary"),
    )(x2, target, fg)


def _rms_bwd(x, dh, dx2, g, comm=None, rows=128):
    S, D = x.shape
    rows = min(rows, S)
    nsteps = S // rows
    c_arrays, c_shapes, c_sems, c_build, c_alias = _comm_fields(comm)
    n_cin, n_cout = len(c_arrays), len(c_shapes)

    def body(*refs):
        x_ref, dh_ref, dx2_ref, g_ref = refs[:4]
        cin = refs[4:4 + n_cin]
        gx_ref, dg_ref = refs[4 + n_cin], refs[5 + n_cin]
        cout, sems = refs[6 + n_cin:6 + n_cin + n_cout], refs[6 + n_cin + n_cout:]
        i = pl.program_id(0)
        if comm is not None:
            _start_when(i == 0, c_build(cin, cout, sems))

        @pl.when(i == 0)
        def _():
            dg_ref[...] = jnp.zeros_like(dg_ref)

        xv, dh_v = x_ref[...], dh_ref[...]
        rstd = lax.rsqrt(jnp.mean(xv * xv, axis=-1, keepdims=True) + NORM_EPS)
        xhat = xv * rstd
        dg_ref[...] += jnp.sum(dh_v * xhat, axis=0, keepdims=True)
        dxh = dh_v * g_ref[...]
        gx_ref[...] = dx2_ref[...] + rstd * (dxh - xhat * jnp.mean(dxh * xhat, axis=-1, keepdims=True))
        if comm is not None:
            _wait_when(i == nsteps - 1, c_build(cin, cout, sems))

    row = pl.BlockSpec((rows, D), lambda i: (i, 0))
    vec = pl.BlockSpec((1, D), lambda i: (0, 0))
    aliases = {4 + t: 2 + t for t in range(n_cin)} if c_alias else {}
    return pl.pallas_call(
        body, name="rms_bwd",
        out_shape=(jax.ShapeDtypeStruct((S, D), F32), jax.ShapeDtypeStruct((1, D), F32)) + tuple(c_shapes),
        grid=(nsteps,), in_specs=[row, row, row, vec] + [HBM_SPEC] * n_cin, out_specs=(row, vec) + (HBM_SPEC,) * n_cout,
        scratch_shapes=_comm_scratch(c_sems), input_output_aliases=aliases, compiler_params=_params("arbitrary"),
    )(x, dh, dx2, g, *c_arrays)


def _adamw(w, g, m, v, name, rows=64):
    R, C = w.shape
    rows = min(rows, R)
    c1 = 1.0 - ADAM_B1 ** ADAM_STEP
    c2 = 1.0 - ADAM_B2 ** ADAM_STEP

    def body(w_ref, g_ref, m_ref, v_ref, go_ref, d_ref, mo_ref, vo_ref):
        gv = g_ref[...]
        mn = ADAM_B1 * m_ref[...] + (1.0 - ADAM_B1) * gv
        vn = ADAM_B2 * v_ref[...] + (1.0 - ADAM_B2) * (gv * gv)
        go_ref[...] = gv
        mo_ref[...] = mn
        vo_ref[...] = vn
        d_ref[...] = -ADAM_LR * ((mn / c1) / (jnp.sqrt(vn / c2) + ADAM_EPS) + ADAM_WD * w_ref[...])

    spec = pl.BlockSpec((rows, C), lambda i: (i, 0))
    shape = jax.ShapeDtypeStruct((R, C), F32)
    return pl.pallas_call(body, name=name, out_shape=(shape,) * 4, grid=(R // rows,), in_specs=[spec] * 4,
                          out_specs=(spec,) * 4, compiler_params=_params("parallel"))(w, g, m, v)


def _place():
    x, y, c = lax.axis_index("x"), lax.axis_index("y"), lax.axis_index("c")
    others = [(1 - x, y), (x, 1 - y), (1 - x, 1 - y)]
    return x, y, c, others


def _chip_index(chip):
    return 2 * chip[0] + chip[1]


def _remote(src, dst, sems, k, to):
    send_sems, recv_sems = sems
    return pltpu.make_async_remote_copy(src_ref=src, dst_ref=dst, send_sem=send_sems.at[k], recv_sem=recv_sems.at[k],
                                        device_id=to, device_id_type=MESH)


def _comm_fields(comm):
    return comm if comm is not None else ((), (), 0, None, False)


def _comm_scratch(n_sems):
    return [pltpu.SemaphoreType.DMA((n_sems,)), pltpu.SemaphoreType.DMA((n_sems,))] if n_sems else []


def _start_when(cond, copies):
    @pl.when(cond)
    def _():
        for cp in copies:
            cp.start()


def _wait_when(cond, copies):
    @pl.when(cond)
    def _():
        for cp in copies:
            cp.wait()


def _proj_gather(tile_cols, h, w_full, comm=None, tm=1024, tk=512):
    c_arrays, c_shapes, c_sems, c_build, c_alias = _comm_fields(comm)
    n_cin, n_cout = len(c_arrays), len(c_shapes)
    S, Dm = h.shape
    tm, tk = min(tm, S), min(tk, Dm)
    tn = SLAB // 2
    ni, nj, nk = S // tm, 2 * N_CHIPS, Dm // tk
    total = nj * ni * nk
    half_in = Dm // 2

    def body(*refs):
        cols_ref, h_ref = refs[0], refs[1]
        cin = refs[3:3 + n_cin]
        proj_ref, fi_ref = refs[3 + n_cin], refs[4 + n_cin]
        cout = refs[5 + n_cin:5 + n_cin + n_cout]
        acc, bbuf, bsem, send_sems, recv_sems = refs[5 + n_cin + n_cout:10 + n_cin + n_cout]
        c_sem_refs = refs[10 + n_cin + n_cout:]
        j, i, k = pl.program_id(0), pl.program_id(1), pl.program_id(2)
        t = (j * ni + i) * nk + k
        x, y, c, others = _place()
        me_chip = 2 * x + y
        sems = (send_sems, recv_sems)

        def piece(chip, half, tile):
            return fi_ref.at[pl.ds(half * half_in, half_in), pl.ds(chip * SLAB + tile * tn, tn)]

        def ici_send(r, tile):
            mine = piece(me_chip, c, tile)
            return _remote(mine, mine, sems, 2 * r + tile, (*others[r], c))

        def forward(tile):
            src_chip = (x + (1 - c) * (1 - 2 * x), y + c * (1 - 2 * y))
            dst_chip = (x + c * (1 - 2 * x), y + (1 - c) * (1 - 2 * y))
            got = piece(_chip_index(src_chip), c, tile)
            return _remote(got, got, sems, 4 + tile, (*dst_chip, c))

        def ici_recv(r, tile):
            got = piece(_chip_index(others[r]), c, tile)
            return _remote(got, got, sems, 2 * r + tile, (x, y, c))

        def pass_on(r, tile):
            got = piece(_chip_index(others[r]), c, tile)
            return _remote(got, got, sems, 6 + 2 * r + tile, (x, y, 1 - c))

        def passed_recv(r, tile):
            got = piece(_chip_index(others[r]), 1 - c, tile)
            return _remote(got, got, sems, 6 + 2 * r + tile, (x, y, c))

        def b_copy(jj, kk, slot):
            return pltpu.make_async_copy(fi_ref.at[pl.ds(kk * tk, tk), pl.ds(cols_ref[jj] * tn, tn)], bbuf.at[slot],
                                         bsem.at[slot])

        @pl.when(t == 0)
        def _():
            for tile in range(2):
                ici_send(0, tile).start()
                ici_send(1, tile).start()
            b_copy(0, 0, 0).start()
            if comm is not None:
                for cp in c_build(cin, cout, c_sem_refs):
                    cp.start()

        nxt = t + 1
        kn, i_n, jn = nxt % nk, (nxt // nk) % ni, nxt // (nk * ni)

        def opens(jj):
            return (jn == jj) & (i_n == 0) & (kn == 0)

        @pl.when(nxt < total)
        def _():
            for tile in range(2):
                @pl.when(opens(2 + 2 * tile))
                def _():
                    ici_recv(0, tile).wait_recv()
                    ici_recv(1, tile).wait_recv()
                    forward(tile).start()
                    pass_on(0, tile).start()
                    pass_on(1, tile).start()
                    passed_recv(0, tile).wait_recv()

                @pl.when(opens(3 + 2 * tile))
                def _():
                    passed_recv(1, tile).wait_recv()

                @pl.when(opens(6 + tile))
                def _():
                    ici_recv(2, tile).wait_recv()
                    pass_on(2, tile).start()
                    passed_recv(2, tile).wait_recv()

            b_copy(jn, kn, nxt % 2).start()

        b_copy(j, k, t % 2).wait()

        def product():
            return jnp.dot(h_ref[...], bbuf[t % 2], preferred_element_type=F32)

        @pl.when(k == 0)
        def _():
            acc[...] = product()

        @pl.when((k > 0) & (k < nk - 1))
        def _():
            acc[...] += product()

        @pl.when(k == nk - 1)
        def _():
            proj_ref[...] = (acc[...] + product()).astype(BF16)

        @pl.when(t == total - 1)
        def _():
            for tile in range(2):
                ici_send(0, tile).wait_send()
                ici_send(1, tile).wait_send()
                forward(tile).wait_send()
                for r in range(3):
                    pass_on(r, tile).wait_send()
            if comm is not None:
                for cp in c_build(cin, cout, c_sem_refs):
                    cp.wait()

    grid_spec = pltpu.PrefetchScalarGridSpec(
        num_scalar_prefetch=1, grid=(nj, ni, nk),
        in_specs=[pl.BlockSpec((tm, tk), lambda j, i, k, cols: (i, k)), HBM_SPEC] + [HBM_SPEC] * n_cin,
        out_specs=(pl.BlockSpec((tm, tn), lambda j, i, k, cols: (i, cols[j])), HBM_SPEC) + (HBM_SPEC,) * n_cout,
        scratch_shapes=[pltpu.VMEM((tm, tn), F32), pltpu.VMEM((2, tk, tn), BF16), pltpu.SemaphoreType.DMA((2,)),
                        pltpu.SemaphoreType.DMA((12,)), pltpu.SemaphoreType.DMA((12,))] + _comm_scratch(c_sems))
    aliases = {2: 1}
    if c_alias:
        aliases.update({3 + t: 2 + t for t in range(n_cin)})
    return pl.pallas_call(
        body, name="proj_gather",
        out_shape=(jax.ShapeDtypeStruct((S, PROJ_WIDTH), BF16), jax.ShapeDtypeStruct(w_full.shape, BF16)) + tuple(c_shapes),
        grid_spec=grid_spec, input_output_aliases=aliases,
        compiler_params=_params("arbitrary", "arbitrary", "arbitrary"),
    )(tile_cols, h, w_full, *c_arrays)


def _exchange(comm, name):
    arrays, shapes, n_sems, build, aliased = comm
    n_in, n_out = len(arrays), len(shapes)

    def body(*refs):
        copies = build(refs[:n_in], refs[n_in:n_in + n_out], refs[n_in + n_out:])
        for cp in copies:
            cp.start()
        for cp in copies:
            cp.wait()

    return pl.pallas_call(
        body, name=name, out_shape=tuple(shapes), in_specs=[HBM_SPEC] * n_in, out_specs=(HBM_SPEC,) * n_out,
        scratch_shapes=_comm_scratch(n_sems), input_output_aliases={t: t for t in range(n_in)} if aliased else {},
    )(*arrays)


def _pair_halves(parts):
    def half_of(ref, which):
        if len(ref.shape) == 2:
            rows = ref.shape[0] // 2
            return ref.at[pl.ds(which * rows, rows), :]
        return ref.at[:, pl.ds(which, 1)]

    def out_shape(p):
        if p.ndim == 2:
            return jax.ShapeDtypeStruct((p.shape[0] // 2, p.shape[1]), BF16)
        return jax.ShapeDtypeStruct((p.shape[0], 1) + p.shape[2:], BF16)

    def build(p_refs, r_refs, sems):
        x, y, c, _ = _place()
        return [_remote(half_of(p, 1 - c), r, sems, t, (x, y, 1 - c)) for t, (p, r) in enumerate(zip(p_refs, r_refs))]

    return (tuple(parts), tuple(out_shape(p) for p in parts), len(parts), build, False)


def _gather_squares(items):
    half_up = UP_ROWS // 2
    fulls = [f for f, _ in items]

    def build(_, f_refs, sems):
        x, y, c, others = _place()
        copies = []
        for t, (f, (_, over_ici)) in enumerate(zip(f_refs, items)):
            for r, chip in enumerate(others):
                src_chip = 2 * x + y if over_ici else _chip_index(chip)
                rows = f.at[pl.ds(src_chip * UP_ROWS + c * half_up, half_up), :]
                copies.append(_remote(rows, rows, sems, 3 * t + r, (*chip, c) if over_ici else (x, y, 1 - c)))
        return copies

    return (tuple(fulls), tuple(jax.ShapeDtypeStruct(f.shape, f.dtype) for f in fulls), 3 * len(fulls), build, True)


def _send_whole(parts):
    def build(p_refs, r_refs, sems):
        x, y, c, _ = _place()
        return [_remote(p, r, sems, t, (x, y, 1 - c)) for t, (p, r) in enumerate(zip(p_refs, r_refs))]

    return (tuple(parts), tuple(jax.ShapeDtypeStruct(p.shape, p.dtype) for p in parts), len(parts), build, False)


def _comm_join(first, second):
    (a1, s1, n1, b1, al1), (a2, s2, n2, b2, al2) = first, second

    def build(in_refs, out_refs, sems):
        send_sems, recv_sems = sems
        later = (send_sems.at[pl.ds(n1, n2)], recv_sems.at[pl.ds(n1, n2)])
        return (b1(in_refs[:len(a1)], out_refs[:len(s1)], sems)
                + b2(in_refs[len(a1):], out_refs[len(s1):], later))

    flags = lambda al, n: al if isinstance(al, tuple) else (al,) * n
    return (a1 + a2, s1 + s2, n1 + n2, build, flags(al1, len(a1)) + flags(al2, len(a2)))


def _pair_add_in(p_own, r_in, rows=256, cols=SLAB):
    half, P = p_own.shape
    rows = min(rows, half)

    def body(p_ref, r_ref, o_ref):
        o_ref[...] = (p_ref[...].astype(F32) + r_ref[...].astype(F32)).astype(BF16)

    spec = pl.BlockSpec((rows, cols), lambda i, j: (i, j))
    return pl.pallas_call(body, name="pair_add_in", out_shape=jax.ShapeDtypeStruct((half, P), BF16),
                          grid=(half // rows, P // cols), in_specs=[spec, spec], out_specs=spec,
                          compiler_params=_params("parallel", "parallel"))(p_own, r_in)


def _pair_add_up(idx, p_up, r_up, name):
    _, _, R, C = p_up.shape

    def body(idx_ref, p_ref, r_ref, o_ref):
        o_ref[...] = (p_ref[...].astype(F32) + r_ref[...].astype(F32)).astype(BF16)

    grid_spec = pltpu.PrefetchScalarGridSpec(
        num_scalar_prefetch=1, grid=(N_CHIPS,),
        in_specs=[pl.BlockSpec((1, 1, R, C), lambda j, idx: (j, idx[0], 0, 0)),
                  pl.BlockSpec((1, 1, R, C), lambda j, idx: (j, 0, 0, 0))],
        out_specs=pl.BlockSpec((1, 1, R, C), lambda j, idx: (j, 0, 0, 0)))
    return pl.pallas_call(body, name=name, out_shape=jax.ShapeDtypeStruct((N_CHIPS, 1, R, C), BF16), grid_spec=grid_spec,
                          compiler_params=_params("parallel"))(idx, p_up, r_up)


def _slab_exchange(qs):
    n = len(qs)

    def out_shape(q):
        if q.ndim == 2:
            return jax.ShapeDtypeStruct((3, q.shape[0], SLAB), BF16)
        return jax.ShapeDtypeStruct((3,) + q.shape[1:], BF16)

    def build(q_refs, r_refs, sems):
        _, _, c, others = _place()
        copies = []
        for r, chip in enumerate(others):
            ci = _chip_index(chip)
            for t in range(n):
                src = q_refs[t].at[:, pl.ds(ci * SLAB, SLAB)] if len(q_refs[t].shape) == 2 else q_refs[t].at[ci]
                copies.append(_remote(src, r_refs[t].at[r], sems, 3 * t + r, (*chip, c)))
        return copies

    return (tuple(qs), tuple(out_shape(q) for q in qs), 3 * n, build, False)


def _slab_add_in(idx, q_in, r2_in, rows=128):
    half = q_in.shape[0]
    rows = min(rows, half)
    nrb = half // rows

    def body(idx_ref, q_ref, r_ref, o_ref):
        o_ref[...] = ((q_ref[...].astype(F32) + r_ref[0].astype(F32)) + r_ref[1].astype(F32)) + r_ref[2].astype(F32)

    grid_spec = pltpu.PrefetchScalarGridSpec(
        num_scalar_prefetch=1, grid=(nrb,),
        in_specs=[pl.BlockSpec((rows, SLAB), lambda i, idx: (i, idx[1])),
                  pl.BlockSpec((3, rows, SLAB), lambda i, idx: (0, i, 0))],
        out_specs=pl.BlockSpec((rows, SLAB), lambda i, idx: (idx[0] * nrb + i, 0)))
    return pl.pallas_call(body, name="slab_add_in", out_shape=jax.ShapeDtypeStruct((2 * half, SLAB), F32),
                          grid_spec=grid_spec, compiler_params=_params("parallel"))(idx, q_in, r2_in)


def _slab_add_up(idx, q_up, r2_up, name, rows=128):
    _, _, R, C = q_up.shape
    rows = min(rows, R)
    nrb = R // rows

    def body(idx_ref, q_ref, r_ref, o_ref):
        o_ref[...] = ((q_ref[0, 0].astype(F32) + r_ref[0, 0].astype(F32)) + r_ref[1, 0].astype(F32)) + r_ref[2, 0].astype(F32)

    grid_spec = pltpu.PrefetchScalarGridSpec(
        num_scalar_prefetch=1, grid=(nrb,),
        in_specs=[pl.BlockSpec((1, 1, rows, C), lambda i, idx: (idx[1], 0, i, 0)),
                  pl.BlockSpec((3, 1, rows, C), lambda i, idx: (0, 0, i, 0))],
        out_specs=pl.BlockSpec((rows, C), lambda i, idx: (idx[0] * nrb + i, 0)))
    return pl.pallas_call(body, name=name, out_shape=jax.ShapeDtypeStruct((2 * R, C), F32), grid_spec=grid_spec,
                          compiler_params=_params("parallel"))(idx, q_up, r2_up)


def _pair_share(gs):
    def build(_, g_refs, sems):
        x, y, c, _ = _place()
        copies = []
        for t, g in enumerate(g_refs):
            rows = g.shape[0] // 2
            mine = g.at[pl.ds(c * rows, rows), :]
            copies.append(_remote(mine, mine, sems, t, (x, y, 1 - c)))
        return copies

    return (tuple(gs), tuple(jax.ShapeDtypeStruct(g.shape, g.dtype) for g in gs), len(gs), build, True)


def _all_reduce_small(packed):
    R, C = packed.shape
    N_DEV = 2 * N_CHIPS

    def body(x_ref, out_ref, all_ref, send_sems, recv_sems, local_sem):
        x, y, c, others = _place()
        me, sib = (x, y, c), (x, y, 1 - c)
        sems = (send_sems, recv_sems)

        def rows(px, py, pc):
            return all_ref.at[4 * px + 2 * py + pc]

        mine = pltpu.make_async_copy(x_ref, rows(*me), local_sem)
        mine.start()
        first = [_remote(x_ref, rows(*me), sems, 0, sib)]
        first += [_remote(x_ref, rows(*me), sems, 1 + j, (*chip, c)) for j, chip in enumerate(others)]
        for cp in first:
            cp.start()
        passed = [_remote(rows(*chip, c), rows(*chip, c), sems, 4 + j, sib) for j, chip in enumerate(others)]
        for j, chip in enumerate(others):
            _remote(rows(*chip, c), rows(*chip, c), sems, 1 + j, me).wait_recv()
            passed[j].start()
        _remote(rows(*sib), rows(*sib), sems, 0, me).wait_recv()
        for j, chip in enumerate(others):
            _remote(rows(*chip, 1 - c), rows(*chip, 1 - c), sems, 4 + j, me).wait_recv()
        for cp in first + passed:
            cp.wait_send()
        mine.wait()
        total = all_ref[0]
        for d in range(1, N_DEV):
            total = total + all_ref[d]
        out_ref[...] = total

    vmem = pl.BlockSpec(memory_space=pltpu.VMEM)
    return pl.pallas_call(
        body, name="all_reduce_small", out_shape=jax.ShapeDtypeStruct((R, C), F32),
        in_specs=[vmem], out_specs=vmem,
        scratch_shapes=[pltpu.VMEM((N_DEV, R, C), F32), pltpu.SemaphoreType.DMA((7,)), pltpu.SemaphoreType.DMA((7,)),
                        pltpu.SemaphoreType.DMA],
        compiler_params=pltpu.CompilerParams(vmem_limit_bytes=VMEM_LIMIT),
    )(packed)


SMALL_NAMES = ("w_spatial", "b_spatial", "norm_g", "gmlp_ln_g", "gmlp_ln_b", "final_norm_g", "attn_sink")


def _pack_small(parts, extra=None):
    blocks = []
    for n in SMALL_NAMES:
        flat = parts[n].reshape(-1).astype(F32)
        rows = -(-flat.shape[0] // (8 * LANES)) * 8
        flat = jnp.pad(flat, (0, rows * LANES - flat.shape[0]))
        blocks.append(flat.reshape(rows, LANES))
    blocks.append(jnp.zeros((8, LANES), F32) if extra is None else extra)
    return jnp.concatenate(blocks, axis=0)


def _unpack_small(packed, shapes):
    out, r = {}, 0
    for n in SMALL_NAMES:
        size = 1
        for s in shapes[n]:
            size *= s
        rows = -(-size // (8 * LANES)) * 8
        out[n] = packed[r:r + rows].reshape(-1)[:size].reshape(shapes[n])
        r += rows
    return out, packed[r:r + 8]


def kernel(x, positions, norm_g, w_in, attn_sink, gmlp_ln_g, gmlp_ln_b, w_spatial, b_spatial, w_up_attn, w_up_gmlp, w_out, final_norm_g, loss_target, m_norm_g, m_w_in, m_attn_sink, m_gmlp_ln_g, m_gmlp_ln_b, m_w_spatial, m_b_spatial, m_w_up_attn, m_w_up_gmlp, m_w_out, m_final_norm_g, v_norm_g, v_w_in, v_attn_sink, v_gmlp_ln_g, v_gmlp_ln_b, v_w_spatial, v_b_spatial, v_w_up_attn, v_w_up_gmlp, v_w_out, v_final_norm_g):
    xs, tgt = x[0], loss_target[0]
    mx, my, mc = lax.axis_index("x"), lax.axis_index("y"), lax.axis_index("c")
    idx = jnp.stack([mc, 2 * mx + my]).astype(jnp.int32)
    own, x_nbr, y_nbr, diag = 2 * mx + my, 2 * (1 - mx) + my, 2 * mx + (1 - my), 2 * (1 - mx) + (1 - my)
    tile_cols = jnp.stack([2 * own, 2 * own + 1, 2 * x_nbr, 2 * y_nbr, 2 * x_nbr + 1, 2 * y_nbr + 1, 2 * diag,
                           2 * diag + 1]).astype(jnp.int32)

    square = (D_MODEL, D_MODEL)
    w_full = _cast_into(idx, w_in[0], (D_MODEL, PROJ_WIDTH), "cast_w_in")
    ups = [_cast_into(idx, w_up_attn[0], square, "cast_w_up_attn"), _cast_into(idx, w_up_gmlp[0], square, "cast_w_up_gmlp"),
           _cast_into(idx, w_out[0], square, "cast_w_out")]

    tables = _rope_tables(positions[0])
    bias = _band_bias()
    sink = attn_sink[0]
    ws, bs_t = w_spatial[0], b_spatial[0].T
    h = _rms_fwd(xs, norm_g)
    proj, w_full, wua = _proj_gather(tile_cols, h, w_full, comm=_gather_squares([(ups[0], True)]))
    attn, a_in, wua, wug = _attn_fwd(proj, tables, bias, sink, comm=_gather_squares([(wua, False), (ups[1], True)]))
    b_in, wug = _gmlp_fwd(proj, ws, bs_t, gmlp_ln_g, gmlp_ln_b, comm=_gather_squares([(wug, False)]))
    y_a, wo = _matmul(a_in, wua, mode="nn", out_dtype=F32, name="up_attn", comm=_gather_squares([(ups[2], True)]))
    y_b, wo = _matmul(b_in, wug, mode="nn", out_dtype=F32, name="up_gmlp", comm=_gather_squares([(wo, False)]))
    merged = _merge_fwd(proj, y_a, y_b)
    x2 = _matmul(merged, wo, mode="nn", out_dtype=F32, name="out_proj", residual=xs, tk=2048)
    dx2, dx2_b, d_fg, loss_part = _loss_head(x2, tgt, final_norm_g.reshape(1, D_MODEL))

    dmerged = _matmul(dx2_b, wo, mode="nt", out_dtype=F32, name="d_merged")
    dy_a, dy_b, d_gates = _merge_bwd(proj, y_a, y_b, dmerged)

    p_ups = [_matmul(a_in, dy_a, mode="tn", out_dtype=BF16, name="dw_up_attn"),
             _matmul(b_in, dy_b, mode="tn", out_dtype=BF16, name="dw_up_gmlp"),
             _matmul(merged, dx2_b, mode="tn", out_dtype=BF16, name="dw_out")]
    p_ups = [p.reshape(N_CHIPS, 2, UP_ROWS // 2, D_MODEL) for p in p_ups]
    da_in, *r1_ups = _matmul(dy_a, wua, mode="nt", out_dtype=F32, name="d_a_in", comm=_pair_halves(p_ups))
    q_ups = [_pair_add_up(idx, p, r, "pair_add_up%d" % t) for t, (p, r) in enumerate(zip(p_ups, r1_ups))]
    db_in = _matmul(dy_b, wug, mode="nt", out_dtype=F32, name="d_b_in")
    dq, dk, dv, dga, d_sink = _attn_bwd(proj, tables, bias, sink, attn, da_in)
    d_gmlp, d_ws, d_bs, d_lg, d_lb = _gmlp_bwd(proj, ws, bs_t, gmlp_ln_g, gmlp_ln_b, db_in)
    dproj = jnp.concatenate([dq, dk, dv, dga, d_gmlp, d_gates], axis=1)

    half = D_MODEL // 2
    h_sib = lax.dynamic_slice(h, (0, (1 - mc) * half), (h.shape[0], half))
    h_own = lax.dynamic_slice(h, (0, mc * half), (h.shape[0], half))
    p_sib, *r2_ups = _matmul(h_sib, dproj, mode="tn", out_dtype=BF16, name="dw_in_sibling", comm=_slab_exchange(q_ups))
    g_ups_half = [_slab_add_up(idx, q, r, "slab_add_up%d" % t) for t, (q, r) in enumerate(zip(q_ups, r2_ups))]
    p_own, r1_in, *g_ups = _matmul(h_own, dproj, mode="tn", out_dtype=BF16, name="dw_in_own",
                                   comm=_comm_join(_send_whole([p_sib]), _pair_share(g_ups_half)))
    q_in = _pair_add_in(p_own, r1_in)
    dh, r2_in = _matmul(dproj, w_full, mode="nt", out_dtype=F32, name="d_h", comm=_slab_exchange([q_in]), tk=SLAB // 2)

    grad_x, d_ng, g_in = _rms_bwd(xs, dh, dx2, norm_g, comm=_pair_share([_slab_add_in(idx, q_in, r2_in)]))
    g_big = [g_in] + g_ups

    small_shapes = {"w_spatial": w_spatial.shape, "b_spatial": b_spatial.shape, "norm_g": norm_g.shape,
                    "gmlp_ln_g": gmlp_ln_g.shape, "gmlp_ln_b": gmlp_ln_b.shape, "final_norm_g": final_norm_g.shape,
                    "attn_sink": attn_sink.shape}
    d_small = {"w_spatial": d_ws, "b_spatial": d_bs, "norm_g": d_ng, "gmlp_ln_g": d_lg, "gmlp_ln_b": d_lb,
               "final_norm_g": d_fg, "attn_sink": d_sink[:, 0, :HEADS_PER_STEP]}
    g_small = _all_reduce_small(_pack_small(d_small, loss_part))

    w_small = _pack_small(dict(w_spatial=w_spatial, b_spatial=b_spatial, norm_g=norm_g, gmlp_ln_g=gmlp_ln_g,
                               gmlp_ln_b=gmlp_ln_b, final_norm_g=final_norm_g, attn_sink=attn_sink))
    m_small = _pack_small(dict(w_spatial=m_w_spatial, b_spatial=m_b_spatial, norm_g=m_norm_g, gmlp_ln_g=m_gmlp_ln_g,
                               gmlp_ln_b=m_gmlp_ln_b, final_norm_g=m_final_norm_g, attn_sink=m_attn_sink))
    v_small = _pack_small(dict(w_spatial=v_w_spatial, b_spatial=v_b_spatial, norm_g=v_norm_g, gmlp_ln_g=v_gmlp_ln_g,
                               gmlp_ln_b=v_gmlp_ln_b, final_norm_g=v_final_norm_g, attn_sink=v_attn_sink))
    small_out = _adamw(w_small, g_small, m_small, v_small, "adamw_small", rows=w_small.shape[0])
    big = {
        "w_in": _adamw(w_in[0], g_big[0], m_w_in[0], v_w_in[0], "adamw_w_in"),
        "w_up_attn": _adamw(w_up_attn[0], g_big[1], m_w_up_attn[0], v_w_up_attn[0], "adamw_w_up_attn"),
        "w_up_gmlp": _adamw(w_up_gmlp[0], g_big[2], m_w_up_gmlp[0], v_w_up_gmlp[0], "adamw_w_up_gmlp"),
        "w_out": _adamw(w_out[0], g_big[3], m_w_out[0], v_w_out[0], "adamw_w_out"),
    }

    order = ("norm_g", "w_in", "attn_sink", "gmlp_ln_g", "gmlp_ln_b", "w_spatial", "b_spatial", "w_up_attn", "w_up_gmlp",
             "w_out", "final_norm_g")
    outs = []
    loss = None
    for kind in range(4):
        small, extra = _unpack_small(small_out[kind], small_shapes)
        if kind == 0:
            loss = extra[0, 0]
        for n in order:
            outs.append(big[n][kind][None] if n in big else small[n])
    return (loss, grad_x[None], *outs)
```
